```python
import math
import jax, jax.numpy as jnp
from jax import lax
import numpy as np

D_MODEL = 1024
BATCH = 2
SEQ = 16384
DEPTH = 1
DEC_BATCH = 8
DEC_SEQ = 8192
PAST_LEN = 128

N_META = 16
CHUNK = 128
MIX_WIDTH = D_MODEL
RET_HEADS = 4
RET_DK = 128
RET_DV = 128
RET_WIDTH = RET_HEADS * RET_DV
ATT_HEADS = 8
ATT_KV_HEADS = 2
ATT_GROUP = ATT_HEADS // ATT_KV_HEADS
ATT_HD = 64
ATT_WIDTH = ATT_HEADS * ATT_HD
WINDOW = 128
N_BUCKETS = 32
MAX_DISTANCE = 128
ROPE_BASE = 10000.0
PEER_HEADS = 8
PEER_NKEYS = 128
PEER_EXPERTS = PEER_NKEYS * PEER_NKEYS
PEER_QDIM = 256
PEER_TOPK = 16
PEER_TOKEN_BLOCK = 128
RMS_EPS = 1e-6
GN_EPS = 1e-5
NEG = -1e30
IN_WIDTHS = (RET_HEADS * RET_DK, RET_HEADS * RET_DK, RET_WIDTH, RET_WIDTH,
             ATT_WIDTH, ATT_KV_HEADS * ATT_HD, ATT_KV_HEADS * ATT_HD)
IN_WIDTH = sum(IN_WIDTHS)

kernel_name = 'hybrid_retention_swa_peer_encoder'


def _rmsnorm(x, w):
    xf = x.astype(jnp.float32)
    y = xf * lax.rsqrt(jnp.mean(xf * xf, axis=-1, keepdims=True) + RMS_EPS)
    return (y * w.astype(jnp.float32)).astype(x.dtype)


def _rotary(x, pos):
    half = x.shape[-1] // 2
    inv = ROPE_BASE ** (-jnp.arange(half, dtype=jnp.float32) / half)
    ang = pos.astype(jnp.float32)[:, None] * inv[None, :]
    cos = jnp.cos(ang)[None, :, None, :]
    sin = jnp.sin(ang)[None, :, None, :]
    x1 = x[..., :half].astype(jnp.float32)
    x2 = x[..., half:].astype(jnp.float32)
    return jnp.concatenate([x1 * cos - x2 * sin, x1 * sin + x2 * cos], axis=-1)


def _retention_direction(q, k, v, log_g, inclusive):
    B, L, H, dk = q.shape
    dv = v.shape[-1]
    nc = L // CHUNK
    qc = q.reshape(B, nc, CHUNK, H, dk)
    kc = k.reshape(B, nc, CHUNK, H, dk)
    vc = v.reshape(B, nc, CHUNK, H, dv)
    idx = jnp.arange(CHUNK)
    diff = (idx[:, None] - idx[None, :]).astype(jnp.float32)
    allowed = (diff >= 0) if inclusive else (diff > 0)
    decay = jnp.where(allowed[None], jnp.exp(jnp.maximum(diff, 0.0)[None] * log_g[:, None, None]), 0.0)
    s = jnp.einsum('bnihd,bnjhd->bnhij', qc, kc) * decay
    o = jnp.einsum('bnhij,bnjhv->bnihv', s, vc)
    k_w = jnp.exp((CHUNK - 1 - idx).astype(jnp.float32)[:, None] * log_g[None, :])
    kv = jnp.einsum('bnjhd,jh,bnjhv->nbhdv', kc, k_w, vc)
    g_chunk = jnp.exp(CHUNK * log_g)[None, :, None, None]

    def step(state, kv_c):
        return state * g_chunk + kv_c, state

    _, prev = lax.scan(step, jnp.zeros((B, H, dk, dv), jnp.float32), kv)
    q_w = jnp.exp((idx + 1).astype(jnp.float32)[:, None] * log_g[None, :])
    o = o + jnp.einsum('bnihd,ih,nbhdv->bnihv', qc, q_w, prev)
    return o.reshape(B, L, H, dv)


def _bidirectional_retention(q, k, v, log_g_f, log_g_b):
    fwd = _retention_direction(q, k, v, log_g_f, True)
    flip = lambda t: jnp.flip(t, axis=1)
    bwd = flip(_retention_direction(flip(q), flip(k), flip(v), log_g_b, False))
    return fwd + bwd


def _t5_bucket(rel):
    half = N_BUCKETS // 2
    exact = half // 2
    n = jnp.abs(rel)
    large = exact + (jnp.log(jnp.maximum(n, 1).astype(jnp.float32) / exact)
                     / math.log(MAX_DISTANCE / exact) * (half - exact)).astype(jnp.int32)
    large = jnp.minimum(large, half - 1)
    return jnp.where(rel > 0, half, 0) + jnp.where(n < exact, n, large)


def _window_attention(q, k, v, sink, rel_bias):
    L = q.shape[0]
    nb = L // CHUNK
    qb = (q * (ATT_HD ** -0.5)).reshape(nb, CHUNK, ATT_KV_HEADS, ATT_GROUP, ATT_HD)

    def band(t):
        tp = jnp.pad(t, ((CHUNK, CHUNK), (0, 0), (0, 0))).reshape(nb + 2, CHUNK, ATT_KV_HEADS, ATT_HD)
        return jnp.concatenate([tp[:-2], tp[1:-1], tp[2:]], axis=1)

    kb, vb = band(k), band(v)
    i = jnp.arange(CHUNK)
    j = jnp.arange(3 * CHUNK)
    blk = jnp.arange(nb)
    rel = (j[None, :] - CHUNK) - i[:, None]
    kpos = (blk[:, None] - 1) * CHUNK + j[None, :]
    key_ok = (kpos >= CHUNK) & (kpos < L)
    band_ok = (jnp.abs(rel) <= WINDOW)[None] & key_ok[:, None, :]
    rb = rel_bias.astype(jnp.float32)
    band_bias = rb[_t5_bucket(rel)].transpose(2, 0, 1).reshape(ATT_KV_HEADS, ATT_GROUP, CHUNK, 3 * CHUNK)
    s_band = jnp.einsum('nigqd,njgd->ngqij', qb, kb).astype(jnp.float32) + band_bias[None]
    s_band = jnp.where(band_ok[:, None, None], s_band, NEG)
    k_meta = k[CHUNK - N_META:CHUNK]
    v_meta = v[CHUNK - N_META:CHUNK]
    qpos = blk[:, None] * CHUNK + i[None, :]
    mpos = CHUNK - N_META + jnp.arange(N_META)
    rel_m = mpos[None, None, :] - qpos[:, :, None]
    meta_bias = rb[_t5_bucket(rel_m)].transpose(0, 3, 1, 2).reshape(nb, ATT_KV_HEADS, ATT_GROUP, CHUNK, N_META)
    s_meta = jnp.einsum('nigqd,mgd->ngqim', qb, k_meta).astype(jnp.float32) + meta_bias
    s_sink = sink.astype(jnp.float32).reshape(ATT_KV_HEADS, ATT_GROUP)[None, :, :, None, None]
    m = jnp.maximum(jnp.maximum(s_band.max(-1, keepdims=True), s_meta.max(-1, keepdims=True)), s_sink)
    p_band = jnp.exp(s_band - m)
    p_meta = jnp.exp(s_meta - m)
    inv = 1.0 / (p_band.sum(-1, keepdims=True) + p_meta.sum(-1, keepdims=True) + jnp.exp(s_sink - m))
    o = (jnp.einsum('ngqij,njgd->nigqd', (p_band * inv).astype(v.dtype), vb)
         + jnp.einsum('ngqim,mgd->nigqd', (p_meta * inv).astype(v.dtype), v_meta))
    return o.reshape(L, ATT_WIDTH)


def _mixer(xn, pos, valid, w_in, dec_f, dec_b, gn_w, sink, rel_bias, w_out):
    B, L, _ = xn.shape
    cuts = [sum(IN_WIDTHS[:c]) for c in range(1, len(IN_WIDTHS))]
    q_r, k_r, v_r, g_r, q_a, k_a, v_a = jnp.split(xn @ w_in, cuts, axis=-1)
    vmask = valid[None, :, None, None]
    qr = _rotary(q_r.reshape(B, L, RET_HEADS, RET_DK), pos)
    kr = _rotary(k_r.reshape(B, L, RET_HEADS, RET_DK), pos) * (RET_DK ** -0.5) * vmask
    vr = v_r.reshape(B, L, RET_HEADS, RET_DV).astype(jnp.float32) * vmask
    o_r = _bidirectional_retention(qr, kr, vr,
                                   jax.nn.log_sigmoid(dec_f.astype(jnp.float32)),
                                   jax.nn.log_sigmoid(dec_b.astype(jnp.float32)))
    mu = jnp.mean(o_r, axis=-1, keepdims=True)
    var = jnp.mean(jnp.square(o_r - mu), axis=-1, keepdims=True)
    o_r = ((o_r - mu) * lax.rsqrt(var + GN_EPS)).reshape(B, L, RET_WIDTH) * gn_w.astype(jnp.float32)
    o_r = (jax.nn.silu(g_r.astype(jnp.float32)) * o_r).astype(xn.dtype)
    qa = q_a.reshape(B, L, ATT_HEADS, ATT_HD)
    ka = k_a.reshape(B, L, ATT_KV_HEADS, ATT_HD)
    va = v_a.reshape(B, L, ATT_KV_HEADS, ATT_HD)
    o_a = lax.map(lambda t: _window_attention(t[0], t[1], t[2], sink, rel_bias), (qa, ka, va))
    return jnp.concatenate([o_r, o_a.astype(xn.dtype)], axis=-1) @ w_out


def _peer(xn, wq, pkeys, u, v):
    B, L, D = xn.shape
    xt = xn.reshape(B * L // PEER_TOKEN_BLOCK, PEER_TOKEN_BLOCK, D)

    def block(xb):
        T = xb.shape[0]
        q = (xb @ wq).reshape(T, PEER_HEADS, 2, PEER_QDIM // 2)
        s = jnp.einsum('thpd,hpnd->thpn', q, pkeys).astype(jnp.float32)
        s1, i1 = lax.top_k(s[:, :, 0], PEER_TOPK)
        s2, i2 = lax.top_k(s[:, :, 1], PEER_TOPK)
        cand_s = (s1[..., :, None] + s2[..., None, :]).reshape(T, PEER_HEADS, PEER_TOPK * PEER_TOPK)
        cand_i = (i1[..., :, None] * PEER_NKEYS + i2[..., None, :]).reshape(T, PEER_HEADS, PEER_TOPK * PEER_TOPK)
        top_s, top_pos = lax.top_k(cand_s, PEER_TOPK)
        expert = jnp.take_along_axis(cand_i, top_pos, axis=-1).reshape(T, PEER_HEADS * PEER_TOPK)
        gate = jax.nn.softmax(top_s, axis=-1).reshape(T, PEER_HEADS * PEER_TOPK)
        act = jax.nn.gelu(jnp.einsum('td,tkd->tk', xb, u[expert]).astype(jnp.float32), approximate=False)
        return jnp.einsum('tk,tkd->td', (gate * act).astype(v.dtype), v[expert])

    return lax.map(block, xt).reshape(B, L, D)


def _encode(x, meta_tokens, norm_mix_w, w_in, ret_decay_fwd, ret_decay_bwd, ret_gn_w, attn_sink,
            rel_bias, w_out, norm_ffn_w, peer_wq, peer_keys, peer_u, peer_v, norm_final_w):
    B, S, D = x.shape
    L = S + CHUNK
    h = jnp.concatenate([jnp.zeros((B, CHUNK - N_META, D), x.dtype),
                         jnp.broadcast_to(meta_tokens.astype(x.dtype)[None], (B, N_META, D)),
                         x], axis=1)
    pos = jnp.arange(L) - (CHUNK - N_META)
    valid = (pos >= 0).astype(jnp.float32)
    for l in range(DEPTH):
        h = h + _mixer(_rmsnorm(h, norm_mix_w[l]), pos, valid, w_in[l], ret_decay_fwd[l], ret_decay_bwd[l],
                       ret_gn_w[l], attn_sink[l], rel_bias, w_out[l])
        h = h + _peer(_rmsnorm(h, norm_ffn_w[l]), peer_wq[l], peer_keys[l], peer_u[l], peer_v[l])
    return _rmsnorm(h[:, CHUNK:], norm_final_w)


def setup_inputs(seed: int = 0) -> dict:
    key = jax.random.key(seed)
    ks = jax.random.split(key, 18)
    nrm = lambda k, shp: jax.random.normal(k, shp, jnp.float32)
    base_decay = jnp.log(2.0 ** (5.0 + jnp.arange(RET_HEADS, dtype=jnp.float32)) - 1.0)
    return {
        'x_prompt': nrm(ks[0], (BATCH, SEQ, D_MODEL)),
        'x_sample': nrm(ks[1], (DEC_BATCH, DEC_SEQ, D_MODEL)),
        'meta_tokens': nrm(ks[2], (N_META, D_MODEL)),
        'norm_mix_w': 1.0 + 0.02 * nrm(ks[3], (DEPTH, D_MODEL)),
        'w_in': nrm(ks[4], (DEPTH, D_MODEL, IN_WIDTH)) * D_MODEL ** -0.5,
        'ret_decay_fwd': base_decay[None] + 0.1 * nrm(ks[5], (DEPTH, RET_HEADS)),
        'ret_decay_bwd': base_decay[None] + 0.1 * nrm(ks[6], (DEPTH, RET_HEADS)),
        'ret_gn_w': 1.0 + 0.02 * nrm(ks[7], (DEPTH, RET_WIDTH)),
        'attn_sink': 0.5 * nrm(ks[8], (DEPTH, ATT_HEADS)),
        'rel_bias': 0.5 * nrm(ks[9], (N_BUCKETS, ATT_HEADS)),
        'w_out': nrm(ks[10], (DEPTH, MIX_WIDTH, D_MODEL)) * MIX_WIDTH ** -0.5,
        'norm_ffn_w': 1.0 + 0.02 * nrm(ks[11], (DEPTH, D_MODEL)),
        'peer_wq': nrm(ks[12], (DEPTH, D_MODEL, PEER_HEADS * PEER_QDIM)) * D_MODEL ** -0.5,
        'peer_keys': nrm(ks[13], (DEPTH, PEER_HEADS, 2, PEER_NKEYS, PEER_QDIM // 2)) * (PEER_QDIM // 2) ** -0.5,
        'peer_u': nrm(ks[14], (DEPTH, PEER_EXPERTS, D_MODEL)) * D_MODEL ** -0.5,
        'peer_v': nrm(ks[15], (DEPTH, PEER_EXPERTS, D_MODEL)) * PEER_HEADS ** -0.5,
        'norm_final_w': 1.0 + 0.02 * nrm(ks[16], (D_MODEL,)),
    }


def reference(x_prompt, x_sample, meta_tokens, norm_mix_w, w_in, ret_decay_fwd, ret_decay_bwd, ret_gn_w,
              attn_sink, rel_bias, w_out, norm_ffn_w, peer_wq, peer_keys, peer_u, peer_v, norm_final_w):
    y_prompt = _encode(x_prompt, meta_tokens, norm_mix_w, w_in, ret_decay_fwd, ret_decay_bwd, ret_gn_w,
                       attn_sink, rel_bias, w_out, norm_ffn_w, peer_wq, peer_keys, peer_u, peer_v, norm_final_w)
    y_sample = _encode(x_sample, meta_tokens, norm_mix_w, w_in, ret_decay_fwd, ret_decay_bwd, ret_gn_w,
                       attn_sink, rel_bias, w_out, norm_ffn_w, peer_wq, peer_keys, peer_u, peer_v, norm_final_w)
    return (y_prompt, y_sample)
```

```python
import functools
import math

import jax
import jax.numpy as jnp
from jax import lax
from jax.experimental import pallas as pl
from jax.experimental.pallas import tpu as pltpu

F32 = jnp.float32
BF16 = jnp.bfloat16

D_MODEL = 1024
N_META = 16
PREFIX = 128
RET_HEADS = 4
RET_D = 128
ATT_HEADS = 8
ATT_KV = 2
ATT_GROUP = ATT_HEADS // ATT_KV
ATT_HD = 64
ATT_BLOCK = 128
N_BUCKETS = 32
MAX_DISTANCE = 128
ROPE_BASE = 10000.0
PEER_HEADS = 8
PEER_NKEYS = 128
PEER_EXPERTS = PEER_NKEYS * PEER_NKEYS
PEER_TOPK = 16
PEER_SLOTS = PEER_HEADS * PEER_TOPK
RMS_EPS = 1e-6
GN_EPS = 1e-5
NEG = -1e30
IN_COLS = (0, 512, 1024, 1536, 2048, 2560, 2688, 2816)

LANES = 128
SUBLANES = 8
VMEM_LIMIT = 56 * 1024 * 1024
ROWS_PER_EXPERT = D_MODEL // 2 // LANES
TABLE_PAD = ROWS_PER_EXPERT
PEER_TOKENS = 64
ROUTE_TOKENS = 256
SLOT_ORDER = (0, 2, 1, 3)


def _cparams(sem):
    return pltpu.CompilerParams(dimension_semantics=sem, vmem_limit_bytes=VMEM_LIMIT)


def _proj_kernel(x_ref, nw_ref, w_ref, cos_ref, sin_ref,
                 qr_ref, kr_ref, vr_ref, gr_ref, qa_ref, ka_ref, va_ref):
    x = x_ref[0]
    ms = jnp.mean(x * x, axis=-1, keepdims=True)
    xn = (x * lax.rsqrt(ms + RMS_EPS) * nw_ref[...]).astype(BF16)
    cosf = cos_ref[...]
    sinf = sin_ref[...]

    def mm(i):
        return jnp.dot(xn, w_ref[:, IN_COLS[i]:IN_COLS[i + 1]], preferred_element_type=F32)

    def rotary(t, scale):
        for h in range(RET_HEADS):
            th = t[:, h * RET_D:(h + 1) * RET_D]
            yield h, (th * cosf + pltpu.roll(th, RET_D // 2, 1) * sinf) * scale

    for h, r in rotary(mm(0), 1.0):
        qr_ref[0, :, h * RET_D:(h + 1) * RET_D] = r.astype(BF16)
    for h, r in rotary(mm(1), RET_D ** -0.5):
        kr_ref[0, :, h * RET_D:(h + 1) * RET_D] = r.astype(BF16)
    vr_ref[0] = mm(2).astype(BF16)
    gr_ref[0] = mm(3)
    qa_ref[0] = (mm(4) * (ATT_HD ** -0.5)).astype(BF16)
    ka_ref[0] = mm(5).astype(BF16)
    va_ref[0] = mm(6).astype(BF16)


def _proj(x, norm_w, w_in_bf, cos_t, sin_t, tm):
    b, s, d = x.shape
    widths = (512, 512, 512, 512, 512, 128, 128)
    dtypes = (BF16, BF16, BF16, F32, BF16, BF16, BF16)
    return pl.pallas_call(
        _proj_kernel,
        out_shape=[jax.ShapeDtypeStruct((b, s, w), dt) for w, dt in zip(widths, dtypes)],
        grid=(b, s // tm),
        in_specs=[
            pl.BlockSpec((1, tm, d), lambda i, j: (i, j, 0)),
            pl.BlockSpec((1, d), lambda i, j: (0, 0)),
            pl.BlockSpec(w_in_bf.shape, lambda i, j: (0, 0)),
            pl.BlockSpec((tm, LANES), lambda i, j: (j, 0)),
            pl.BlockSpec((tm, LANES), lambda i, j: (j, 0)),
        ],
        out_specs=[pl.BlockSpec((1, tm, w), lambda i, j: (i, j, 0)) for w in widths],
        compiler_params=_cparams(("arbitrary", "arbitrary")),
        name="proj",
    )(x, norm_w, w_in_bf, cos_t, sin_t)


def _ret_kernel(q_ref, k_ref, v_ref, g_ref, kpre_ref, vpre_ref, dmat_ref, qwf_ref, qwb_ref,
                kwf_ref, kwb_ref, kwp_ref, gdec_ref, gnw_ref, o_ref, sf_ref, sb_ref, sball_ref):
    ph = pl.program_id(1)
    c = pl.program_id(2)
    nc = pl.num_programs(2)
    contract0 = (((0,), (0,)), ((), ()))
    contract1 = (((1,), (1,)), ((), ()))

    def head(ref, h):
        return ref[0, :, h * RET_D:(h + 1) * RET_D]

    def kv_update(kh, vh, kw):
        kw = (kh.astype(F32) * kw).astype(BF16)
        return lax.dot_general(kw, vh, contract0, preferred_element_type=F32)

    @pl.when(ph == 0)
    def _backward_states():
        @pl.when(c == 0)
        def _():
            sb_ref[...] = jnp.zeros_like(sb_ref)

        j = nc - 1 - c
        for h in range(RET_HEADS):
            sball_ref[j, h] = sb_ref[h].astype(BF16)
            sb_ref[h] = gdec_ref[RET_HEADS + h] * sb_ref[h] + kv_update(head(k_ref, h), head(v_ref, h), kwb_ref[h])

    @pl.when(ph == 1)
    def _outputs():
        @pl.when(c == 0)
        def _():
            for h in range(RET_HEADS):
                kp = kpre_ref[:, h * RET_D:(h + 1) * RET_D]
                vp = vpre_ref[:, h * RET_D:(h + 1) * RET_D]
                sf_ref[h] = kv_update(kp, vp, kwp_ref[h])

        for h in range(RET_HEADS):
            qh, kh, vh = head(q_ref, h), head(k_ref, h), head(v_ref, h)
            s = lax.dot_general(qh, kh, contract1, preferred_element_type=F32) * dmat_ref[h]
            o = jnp.dot(s.astype(BF16), vh, preferred_element_type=F32)
            o = o + jnp.dot(qh, sf_ref[h].astype(BF16), preferred_element_type=F32) * qwf_ref[h]
            o = o + jnp.dot(qh, sball_ref[c, h], preferred_element_type=F32) * qwb_ref[h]
            mu = jnp.mean(o, axis=-1, keepdims=True)
            oc = o - mu
            var = jnp.mean(oc * oc, axis=-1, keepdims=True)
            on = oc * lax.rsqrt(var + GN_EPS) * gnw_ref[:, h * RET_D:(h + 1) * RET_D]
            g = head(g_ref, h)
            o_ref[0, :, h * RET_D:(h + 1) * RET_D] = (g * jax.nn.sigmoid(g) * on).astype(BF16)
            sf_ref[h] = gdec_ref[h] * sf_ref[h] + kv_update(kh, vh, kwf_ref[h])


def _retention(qr, kr, vr, gr, kpre, vpre, dmat, qwf, qwb, kwf, kwb, kwp, gdec, gnw, chunk):
    b, s, w = qr.shape
    nc = s // chunk

    def seq_map(i, ph, c):
        return (i, jnp.where(ph == 0, nc - 1 - c, c), 0)

    def fwd_only(i, ph, c):
        return (i, c * ph, 0)

    const2 = lambda i, ph, c: (0, 0)
    const3 = lambda i, ph, c: (0, 0, 0)
    return pl.pallas_call(
        _ret_kernel,
        out_shape=jax.ShapeDtypeStruct((b, s, w), BF16),
        grid=(b, 2, nc),
        in_specs=[
            pl.BlockSpec((1, chunk, w), fwd_only),
            pl.BlockSpec((1, chunk, w), seq_map),
            pl.BlockSpec((1, chunk, w), seq_map),
            pl.BlockSpec((1, chunk, w), fwd_only),
            pl.BlockSpec(kpre.shape, const2),
            pl.BlockSpec(vpre.shape, const2),
            pl.BlockSpec(dmat.shape, const3),
            pl.BlockSpec(qwf.shape, const3),
            pl.BlockSpec(qwb.shape, const3),
            pl.BlockSpec(kwf.shape, const3),
            pl.BlockSpec(kwb.shape, const3),
            pl.BlockSpec(kwp.shape, const3),
            pl.BlockSpec(memory_space=pltpu.SMEM),
            pl.BlockSpec(gnw.shape, const2),
        ],
        out_specs=pl.BlockSpec((1, chunk, w), fwd_only),
        scratch_shapes=[
            pltpu.VMEM((RET_HEADS, RET_D, RET_D), F32),
            pltpu.VMEM((RET_HEADS, RET_D, RET_D), F32),
            pltpu.VMEM((nc, RET_HEADS, RET_D, RET_D), BF16),
        ],
        compiler_params=_cparams(("arbitrary", "arbitrary", "arbitrary")),
        name="retention",
    )(qr, kr, vr, gr, kpre, vpre, dmat, qwf, qwb, kwf, kwb, kwp, gdec, gnw)


def _attn_kernel(q_ref, kp_ref, kc_ref, kn_ref, vp_ref, vc_ref, vn_ref, km_ref, vm_ref,
                 bias_ref, sink_ref, o_ref):
    j = pl.program_id(1)
    nb = pl.num_programs(1)
    lane = lax.broadcasted_iota(jnp.int32, (1, 4 * ATT_BLOCK), 1)
    first = jnp.logical_and(lane < ATT_BLOCK, j == 0)
    last = jnp.logical_and(jnp.logical_and(lane >= 2 * ATT_BLOCK, lane < 3 * ATT_BLOCK), j == nb - 1)
    pen = jnp.where(jnp.logical_or(first, last), NEG, 0.0).astype(F32)
    contract1 = (((1,), (1,)), ((), ()))
    outs = []
    for g in range(ATT_KV):
        sl = slice(g * ATT_HD, (g + 1) * ATT_HD)
        kall = jnp.concatenate([kp_ref[0][:, sl], kc_ref[0][:, sl], kn_ref[0][:, sl], km_ref[:, sl]], axis=0)
        vall = jnp.concatenate([vp_ref[0][:, sl], vc_ref[0][:, sl], vn_ref[0][:, sl], vm_ref[:, sl]], axis=0)
        for hh in range(ATT_GROUP):
            h = g * ATT_GROUP + hh
            q = q_ref[0, :, h * ATT_HD:(h + 1) * ATT_HD]
            s = lax.dot_general(q, kall, contract1, preferred_element_type=F32)
            s = s + bias_ref[0, h] + pen
            snk = sink_ref[h]
            m = jnp.maximum(jnp.max(s, axis=-1, keepdims=True), snk)
            p = jnp.exp(s - m)
            den = jnp.sum(p, axis=-1, keepdims=True) + jnp.exp(snk - m)
            o = jnp.dot(p.astype(BF16), vall, preferred_element_type=F32)
            outs.append(o / den)
    for h2 in range(ATT_HEADS // 2):
        o_ref[0, :, h2 * LANES:(h2 + 1) * LANES] = jnp.concatenate(
            [outs[2 * h2], outs[2 * h2 + 1]], axis=-1).astype(BF16)


def _attention(qa, ka, va, kmeta, vmeta, bias, sink):
    b, s, w = qa.shape
    nb = s // ATT_BLOCK
    kvw = ka.shape[-1]
    prev = lambda i, j: (i, jnp.maximum(j - 1, 0), 0)
    cur = lambda i, j: (i, j, 0)
    nxt = lambda i, j: (i, jnp.minimum(j + 1, nb - 1), 0)
    const2 = lambda i, j: (0, 0)
    kv = lambda m: pl.BlockSpec((1, ATT_BLOCK, kvw), m)
    return pl.pallas_call(
        _attn_kernel,
        out_shape=jax.ShapeDtypeStruct((b, s, w), BF16),
        grid=(b, nb),
        in_specs=[
            pl.BlockSpec((1, ATT_BLOCK, w), cur),
            kv(prev), kv(cur), kv(nxt), kv(prev), kv(cur), kv(nxt),
            pl.BlockSpec(kmeta.shape, const2),
            pl.BlockSpec(vmeta.shape, const2),
            pl.BlockSpec((1,) + bias.shape[1:], lambda i, j: (jnp.minimum(j, 1), 0, 0, 0)),
            pl.BlockSpec(memory_space=pltpu.SMEM),
        ],
        out_specs=pl.BlockSpec((1, ATT_BLOCK, w), cur),
        compiler_params=_cparams(("arbitrary", "arbitrary")),
        name="attention",
    )(qa, ka, ka, ka, va, va, va, kmeta, vmeta, bias, sink)


def _outproj_kernel(or_ref, oa_ref, x_ref, w_ref, nw_ref, h_ref, xn_ref):
    half = or_ref.shape[-1]
    h = x_ref[0]
    h = h + jnp.dot(or_ref[0], w_ref[:half], preferred_element_type=F32)
    h = h + jnp.dot(oa_ref[0], w_ref[half:], preferred_element_type=F32)
    h_ref[0] = h
    ms = jnp.mean(h * h, axis=-1, keepdims=True)
    xn_ref[0] = h * lax.rsqrt(ms + RMS_EPS) * nw_ref[...]


def _outproj(o_r, o_a, x, w_out_bf, norm_w, tm):
    b, s, d = x.shape
    half = o_r.shape[-1]
    row = lambda i, j: (i, j, 0)
    const2 = lambda i, j: (0, 0)
    return pl.pallas_call(
        _outproj_kernel,
        out_shape=[jax.ShapeDtypeStruct((b, s, d), F32)] * 2,
        grid=(b, s // tm),
        in_specs=[
            pl.BlockSpec((1, tm, half), row),
            pl.BlockSpec((1, tm, half), row),
            pl.BlockSpec((1, tm, d), row),
            pl.BlockSpec(w_out_bf.shape, const2),
            pl.BlockSpec((1, d), const2),
        ],
        out_specs=[pl.BlockSpec((1, tm, d), row)] * 2,
        compiler_params=_cparams(("arbitrary", "arbitrary")),
        name="outproj",
    )(o_r, o_a, x, w_out_bf, norm_w)


def _top16_rows(s, iota):
    nrows = s.shape[0]
    vals, ids = [], []
    for _ in range(PEER_TOPK):
        m = jnp.max(s, axis=0, keepdims=True)
        am = jnp.min(jnp.where(s == m, iota, nrows), axis=0, keepdims=True)
        vals.append(m)
        ids.append(am)
        s = jnp.where(iota == am, -jnp.inf, s)
    return jnp.concatenate(vals, axis=0), jnp.concatenate(ids, axis=0)


_PAIR_GROUPS = ((0, 0, 8), (0, 8, 8), (1, 0, 8), (2, 0, 5), (3, 0, 4), (4, 0, 3), (5, 0, 2), (6, 0, 2), (7, 0, 2))


def _route_kernel(x_ref, wq_ref, pk_ref, idx_ref, gate_ref, sc_ref):
    tm = x_ref.shape[1]
    q = jnp.dot(x_ref[0].astype(BF16), wq_ref[...], preferred_element_type=F32).astype(BF16)
    contract1 = (((1,), (1,)), ((), ()))
    for hp in range(2 * PEER_HEADS):
        sc_ref[hp] = lax.dot_general(pk_ref[hp], q[:, hp * PEER_NKEYS:(hp + 1) * PEER_NKEYS], contract1,
                                     preferred_element_type=F32)
    iota = lax.broadcasted_iota(jnp.int32, (PEER_NKEYS, LANES), 0)
    sub = lax.broadcasted_iota(jnp.int32, (SUBLANES, LANES), 0)
    far = PEER_TOPK * PEER_TOPK

    def head_body(h, carry):
        for lt in range(tm // LANES):
            lanes = slice(lt * LANES, (lt + 1) * LANES)
            s1, i1 = _top16_rows(sc_ref[2 * h, :, lanes], iota)
            s2, i2 = _top16_rows(sc_ref[2 * h + 1, :, lanes], iota)
            i1 = i1 * PEER_NKEYS
            sc, ex, pos = [], [], []
            for i, j0, cnt in _PAIR_GROUPS:
                ok = sub < cnt
                sc.append(jnp.where(ok, s1[i:i + 1] + s2[j0:j0 + SUBLANES], -jnp.inf))
                ex.append(i1[i:i + 1] + i2[j0:j0 + SUBLANES])
                pos.append(jnp.where(ok, i * PEER_TOPK + j0 + sub, far))
            sc.append(s1[SUBLANES:] + s2[0:1])
            ex.append(i1[SUBLANES:] + i2[0:1])
            pos.append((SUBLANES + sub) * PEER_TOPK)
            sc = jnp.concatenate(sc, axis=0)
            ex = jnp.concatenate(ex, axis=0)
            pos = jnp.concatenate(pos, axis=0)
            top, experts = [], []
            for _ in range(PEER_TOPK):
                m = jnp.max(sc, axis=0, keepdims=True)
                pm = jnp.min(jnp.where(sc == m, pos, far), axis=0, keepdims=True)
                hit = pos == pm
                experts.append(jnp.max(jnp.where(hit, ex, -1), axis=0, keepdims=True))
                top.append(m)
                sc = jnp.where(hit, -jnp.inf, sc)
            top = jnp.concatenate(top, axis=0)
            e = jnp.exp(top - top[0:1])
            rows = pl.ds(pl.multiple_of(h * PEER_TOPK, PEER_TOPK), PEER_TOPK)
            gate_ref[0, rows, lanes] = e / jnp.sum(e, axis=0, keepdims=True)
            idx_ref[0, rows, lanes] = jnp.concatenate(experts, axis=0)
        return carry

    lax.fori_loop(0, PEER_HEADS, head_body, 0)


def _route(xn, wq_bf, pk_bf, tm):
    b, s, d = xn.shape
    const2 = lambda i, j: (0, 0)
    slot = lambda i, j: (i, 0, j)
    return pl.pallas_call(
        _route_kernel,
        out_shape=[jax.ShapeDtypeStruct((b, PEER_SLOTS, s), jnp.int32),
                   jax.ShapeDtypeStruct((b, PEER_SLOTS, s), F32)],
        grid=(b, s // tm),
        in_specs=[
            pl.BlockSpec((1, tm, d), lambda i, j: (i, j, 0)),
            pl.BlockSpec(wq_bf.shape, const2),
            pl.BlockSpec(pk_bf.shape, lambda i, j: (0, 0, 0)),
        ],
        out_specs=[pl.BlockSpec((1, PEER_SLOTS, tm), slot)] * 2,
        scratch_shapes=[pltpu.VMEM((2 * PEER_HEADS, PEER_NKEYS, tm), F32)],
        compiler_params=_cparams(("arbitrary", "arbitrary")),
        name="route",
    )(xn, wq_bf, pk_bf)


def _bitrev4(n):
    return ((n & 1) << 3) | ((n & 2) << 1) | ((n & 4) >> 1) | ((n & 8) >> 3)


def _pair_slots():
    out = []
    for n in range(PEER_SLOTS // SUBLANES):
        g = _bitrev4(n)
        for c in range(4):
            top = SUBLANES * g + SLOT_ORDER[c]
            out.append((top, top + 4))
    return out


def _gather_pair(tbl_ref, offs_ref, t, top, bot, upper):
    ra = tbl_ref[pl.ds(pl.multiple_of(offs_ref[t, top], ROWS_PER_EXPERT), SUBLANES), :]
    rb = tbl_ref[pl.ds(pl.multiple_of(offs_ref[t, bot], ROWS_PER_EXPERT), SUBLANES), :]
    row = jnp.where(upper, ra, rb)
    lo = lax.bitcast_convert_type(row << 16, F32)
    hi = lax.bitcast_convert_type(row & jnp.uint32(0xFFFF0000), F32)
    return lo, hi


def _merge_sublanes(a, b, h, sub):
    keep = (sub & h) == 0
    return jnp.where(keep, a + pltpu.roll(a, SUBLANES - h, 0), b + pltpu.roll(b, h, 0))


def _merge_lanes(a, b, h, lane):
    keep = (lane & h) == 0
    return jnp.where(keep, a + pltpu.roll(a, LANES - h, 1), b + pltpu.roll(b, h, 1))


def _peer_u_kernel(offs_ref, x_ref, tbl_ref, o_ref):
    sub = lax.broadcasted_iota(jnp.int32, (SUBLANES, LANES), 0)
    lane = lax.broadcasted_iota(jnp.int32, (SUBLANES, LANES), 1)
    upper = sub < ROWS_PER_EXPERT
    pairs = _pair_slots()

    def group_body(gi, carry):
        ys = []
        for j in range(SUBLANES):
            t = gi * SUBLANES + j
            x = x_ref[pl.ds(pl.multiple_of(t * SUBLANES, SUBLANES), SUBLANES), :]
            swapped = pltpu.roll(x, ROWS_PER_EXPERT, 0)
            xa = jnp.where(upper, x, swapped)
            xb = jnp.where(upper, swapped, x)
            zs = []
            for n in range(PEER_SLOTS // SUBLANES):
                ps = []
                for c in range(4):
                    top, bot = pairs[4 * n + c]
                    lo, hi = _gather_pair(tbl_ref, offs_ref, t, top, bot, upper)
                    ps.append(lo * xa + hi * xb)
                r0 = _merge_sublanes(ps[0], ps[1], 2, sub)
                r1 = _merge_sublanes(ps[2], ps[3], 2, sub)
                zs.append(_merge_sublanes(r0, r1, 1, sub))
            for h in (64, 32, 16, 8):
                zs = [_merge_lanes(zs[2 * i], zs[2 * i + 1], h, lane) for i in range(len(zs) // 2)]
            ys.append(zs[0])
        w4 = [_merge_lanes(ys[j], ys[j + 4], 4, lane) for j in range(4)]
        w2 = [_merge_lanes(w4[j], w4[j + 2], 2, lane) for j in range(2)]
        o_ref[gi] = _merge_lanes(w2[0], w2[1], 1, lane)
        return carry

    lax.fori_loop(0, o_ref.shape[0], group_body, 0)


def _peer_u(offs, x_rows, tbl, tb):
    n = offs.shape[0]
    return pl.pallas_call(
        _peer_u_kernel,
        out_shape=jax.ShapeDtypeStruct((n // SUBLANES, SUBLANES, LANES), F32),
        grid=(n // tb,),
        in_specs=[
            pl.BlockSpec((tb, PEER_SLOTS), lambda i: (i, 0), memory_space=pltpu.SMEM),
            pl.BlockSpec((tb * SUBLANES, LANES), lambda i: (i, 0)),
            pl.BlockSpec(tbl.shape, lambda i: (0, 0), pipeline_mode=pl.Buffered(1)),
        ],
        out_specs=pl.BlockSpec((tb // SUBLANES, SUBLANES, LANES), lambda i: (i, 0, 0)),
        compiler_params=_cparams(("arbitrary",)),
        name="peer_u",
    )(offs, x_rows, tbl)


def _peer_w_kernel(gate_ref, act_ref, w_ref):
    a = act_ref[...]
    w_ref[...] = gate_ref[...] * (0.5 * a * (1.0 + lax.erf(a * (2.0 ** -0.5))))


def _peer_w(gate, act, tm):
    n = gate.shape[0]
    spec = pl.BlockSpec((tm, PEER_SLOTS), lambda i: (i, 0))
    return pl.pallas_call(
        _peer_w_kernel,
        out_shape=jax.ShapeDtypeStruct(gate.shape, F32),
        grid=(n // tm,),
        in_specs=[spec, spec],
        out_specs=spec,
        compiler_params=_cparams(("arbitrary",)),
        name="peer_w",
    )(gate, act)


def _peer_v_kernel(offs_ref, w_ref, h_ref, tbl_ref, nw_ref, o_ref):
    sub = lax.broadcasted_iota(jnp.int32, (SUBLANES, LANES), 0)
    upper = sub < ROWS_PER_EXPERT
    pairs = _pair_slots()

    def token_body(t, carry):
        acc = [jnp.zeros((SUBLANES, LANES), F32) for _ in range(4)]
        for q, (top, bot) in enumerate(pairs):
            lo, hi = _gather_pair(tbl_ref, offs_ref, t, top, bot, upper)
            w = jnp.where(upper, w_ref[t, top], w_ref[t, bot])
            k = 2 * (q % 2)
            acc[k] = acc[k] + lo * w
            acc[k + 1] = acc[k + 1] + hi * w
        lo = acc[0] + acc[2]
        hi = acc[1] + acc[3]
        lo = lo + pltpu.roll(lo, ROWS_PER_EXPERT, 0)
        hi = hi + pltpu.roll(hi, ROWS_PER_EXPERT, 0)
        rows = pl.ds(pl.multiple_of(t * SUBLANES, SUBLANES), SUBLANES)
        h = h_ref[rows, :] + jnp.where(upper, lo, hi)
        ms = jnp.sum(jnp.sum(h * h, axis=1, keepdims=True), axis=0, keepdims=True) * (1.0 / D_MODEL)
        o_ref[rows, :] = h * lax.rsqrt(ms + RMS_EPS) * nw_ref[...]
        return carry

    lax.fori_loop(0, offs_ref.shape[0], token_body, 0)


def _peer_v(offs, w, h_rows, tbl, norm_rows, tb):
    n = offs.shape[0]
    smem = pl.BlockSpec((tb, PEER_SLOTS), lambda i: (i, 0), memory_space=pltpu.SMEM)
    rows = pl.BlockSpec((tb * SUBLANES, LANES), lambda i: (i, 0))
    return pl.pallas_call(
        _peer_v_kernel,
        out_shape=jax.ShapeDtypeStruct(h_rows.shape, F32),
        grid=(n // tb,),
        in_specs=[
            smem, smem, rows,
            pl.BlockSpec(tbl.shape, lambda i: (0, 0), pipeline_mode=pl.Buffered(1)),
            pl.BlockSpec((SUBLANES, LANES), lambda i: (0, 0)),
        ],
        out_specs=rows,
        compiler_params=_cparams(("arbitrary",)),
        name="peer_v",
    )(offs, w, h_rows, tbl, norm_rows)


def _pack_table(t):
    half = D_MODEL // 2
    bits = lax.bitcast_convert_type(t.astype(BF16), jnp.uint16).astype(jnp.uint32)
    words = bits[:, :half] | (bits[:, half:] << 16)
    words = words.reshape(t.shape[0] * ROWS_PER_EXPERT, LANES)
    return jnp.pad(words, ((TABLE_PAD, TABLE_PAD), (0, 0)))


def _t5_bucket(rel):
    half = N_BUCKETS // 2
    exact = half // 2
    n = jnp.abs(rel)
    large = exact + (jnp.log(jnp.maximum(n, 1).astype(F32) / exact)
                     / math.log(MAX_DISTANCE / exact) * (half - exact)).astype(jnp.int32)
    large = jnp.minimum(large, half - 1)
    return jnp.where(rel > 0, half, 0) + jnp.where(n < exact, n, large)


def _attention_bias(rel_bias):
    rb = rel_bias.astype(F32)
    i = jnp.arange(ATT_BLOCK)
    jb = jnp.arange(3 * ATT_BLOCK)
    rel = (jb[None, :] - ATT_BLOCK) - i[:, None]
    band = jnp.where((jnp.abs(rel) <= ATT_BLOCK)[..., None], rb[_t5_bucket(rel)], NEG)
    mpos = PREFIX - N_META + jnp.arange(N_META)
    variants = []
    for blk in (0, 1):
        qpos = PREFIX + blk * ATT_BLOCK + i
        meta = rb[_t5_bucket(mpos[None, :] - qpos[:, None])]
        fill = jnp.full((ATT_BLOCK, ATT_BLOCK - N_META, ATT_HEADS), NEG, F32)
        variants.append(jnp.concatenate([band, meta, fill], axis=1).transpose(2, 0, 1))
    return jnp.stack(variants)


def _rope_tables(pos):
    half = RET_D // 2
    inv = ROPE_BASE ** (-jnp.arange(half, dtype=F32) / half)
    ang = pos.astype(F32)[:, None] * inv[None, :]
    cos, sin = jnp.cos(ang), jnp.sin(ang)
    return jnp.concatenate([cos, cos], axis=1), jnp.concatenate([-sin, sin], axis=1)


def _decay_tables(dec_f, dec_b, chunk):
    lf = jax.nn.log_sigmoid(dec_f.astype(F32))[:, None]
    lb = jax.nn.log_sigmoid(dec_b.astype(F32))[:, None]
    idx = jnp.arange(chunk, dtype=F32)
    diff = idx[:, None] - idx[None, :]
    dmat = (jnp.where(diff >= 0, jnp.exp(jnp.maximum(diff, 0.0)[None] * lf[:, :, None]), 0.0)
            + jnp.where(diff < 0, jnp.exp(jnp.maximum(-diff, 0.0)[None] * lb[:, :, None]), 0.0))
    bc = lambda v: jnp.broadcast_to(v[:, :, None], v.shape + (RET_D,))
    qwf = bc(jnp.exp((idx + 1.0)[None] * lf))
    kwf = bc(jnp.exp((chunk - 1.0 - idx)[None] * lf))
    qwb = bc(jnp.exp((chunk - idx)[None] * lb))
    kwb = bc(jnp.exp(idx[None] * lb))
    pidx = jnp.arange(PREFIX, dtype=F32)
    kwp = bc(jnp.exp((PREFIX - 1.0 - pidx)[None] * lf))
    gdec = jnp.concatenate([jnp.exp(chunk * lf[:, 0]), jnp.exp(chunk * lb[:, 0])])
    return dmat, qwf, qwb, kwf, kwb, kwp, gdec


def _divisor_tile(n, want):
    t = min(n, want)
    while n % t:
        t //= 2
    return t


def _encode(x, shared):
    b, s, d = x.shape
    n = b * s
    tm = _divisor_tile(s, 512)
    chunk = _divisor_tile(s, 256)
    cos_t, sin_t = _rope_tables(jnp.arange(s) + N_META)
    qr, kr, vr, gr, qa, ka, va = _proj(x, shared["norm_mix"], shared["w_in"], cos_t, sin_t, tm)
    o_r = _retention(qr, kr, vr, gr, shared["kr_pre"], shared["vr_pre"], *_decay_tables(
        shared["dec_f"], shared["dec_b"], chunk), shared["gn_w"], chunk)
    o_a = _attention(qa, ka, va, shared["k_meta"], shared["v_meta"], shared["bias"], shared["sink"])
    h1, xn2 = _outproj(o_r, o_a, x, shared["w_out"], shared["norm_ffn"], tm)
    idx_t, gate_t = _route(xn2, shared["wq"], shared["pkeys"], _divisor_tile(s, ROUTE_TOKENS))
    idx = idx_t.transpose(0, 2, 1).reshape(n, PEER_SLOTS)
    gate = gate_t.transpose(0, 2, 1).reshape(n, PEER_SLOTS)
    top_half = (jnp.arange(PEER_SLOTS) % SUBLANES) < ROWS_PER_EXPERT
    offs = idx * ROWS_PER_EXPERT + jnp.where(top_half, TABLE_PAD, 0)[None, :]
    tb = _divisor_tile(n, PEER_TOKENS)
    act = _peer_u(offs, xn2.reshape(n * SUBLANES, LANES), shared["u_tbl"], tb)
    act = act.reshape(n // SUBLANES, SUBLANES, PEER_SLOTS // SUBLANES, SUBLANES)
    act = act.transpose(0, 3, 2, 1).reshape(n, PEER_SLOTS)
    w = _peer_w(gate, act, _divisor_tile(n, 1024))
    y = _peer_v(offs, w, h1.reshape(n * SUBLANES, LANES), shared["v_tbl"], shared["norm_final"], tb)
    return y.reshape(b, s, d)


def kernel(x_prompt, x_sample, meta_tokens, norm_mix_w, w_in, ret_decay_fwd, ret_decay_bwd, ret_gn_w,
           attn_sink, rel_bias, w_out, norm_ffn_w, peer_wq, peer_keys, peer_u, peer_v, norm_final_w):
    layer = 0
    shared = {
        "norm_mix": norm_mix_w[layer][None, :].astype(F32),
        "w_in": w_in[layer].astype(BF16),
        "dec_f": ret_decay_fwd[layer],
        "dec_b": ret_decay_bwd[layer],
        "gn_w": ret_gn_w[layer][None, :].astype(F32),
        "sink": attn_sink[layer].astype(F32),
        "bias": _attention_bias(rel_bias),
        "w_out": w_out[layer].astype(BF16),
        "norm_ffn": norm_ffn_w[layer][None, :].astype(F32),
        "wq": peer_wq[layer].astype(BF16),
        "pkeys": peer_keys[layer].reshape(2 * PEER_HEADS, PEER_NKEYS, PEER_NKEYS).astype(BF16),
        "u_tbl": _pack_table(peer_u[layer]),
        "v_tbl": _pack_table(peer_v[layer]),
        "norm_final": norm_final_w.reshape(SUBLANES, LANES).astype(F32),
    }
    prefix = jnp.concatenate([jnp.zeros((PREFIX - N_META, D_MODEL), x_prompt.dtype),
                              meta_tokens.astype(x_prompt.dtype)], axis=0)[None]
    cos_p, sin_p = _rope_tables(jnp.arange(PREFIX) - (PREFIX - N_META))
    _, kr_p, vr_p, _, _, ka_p, va_p = _proj(prefix, shared["norm_mix"], shared["w_in"], cos_p, sin_p, PREFIX)
    shared["kr_pre"], shared["vr_pre"] = kr_p[0], vr_p[0]
    pad_meta = lambda t: jnp.pad(t[0, PREFIX - N_META:], ((0, ATT_BLOCK - N_META), (0, 0)))
    shared["k_meta"], shared["v_meta"] = pad_meta(ka_p), pad_meta(va_p)
    return (_encode(x_prompt, shared), _encode(x_sample, shared))
```

```python
import functools
import math

import jax
import jax.numpy as jnp
from jax import lax
from jax.experimental import pallas as pl
from jax.experimental.pallas import tpu as pltpu

F32 = jnp.float32
BF16 = jnp.bfloat16

D_MODEL = 1024
N_META = 16
PREFIX = 128
RET_HEADS = 4
RET_D = 128
ATT_HEADS = 8
ATT_KV = 2
ATT_GROUP = ATT_HEADS // ATT_KV
ATT_HD = 64
ATT_BLOCK = 128
N_BUCKETS = 32
MAX_DISTANCE = 128
ROPE_BASE = 10000.0
PEER_HEADS = 8
PEER_NKEYS = 128
PEER_EXPERTS = PEER_NKEYS * PEER_NKEYS
PEER_TOPK = 16
PEER_SLOTS = PEER_HEADS * PEER_TOPK
RMS_EPS = 1e-6
GN_EPS = 1e-5
NEG = -1e30
IN_COLS = (0, 512, 1024, 1536, 2048, 2560, 2688, 2816)

LANES = 128
SUBLANES = 8
VMEM_LIMIT = 56 * 1024 * 1024
ROWS_PER_EXPERT = D_MODEL // 2 // LANES
TABLE_PAD = ROWS_PER_EXPERT
PEER_TOKENS = 256
ROUTE_TOKENS = 256
SLOT_ORDER = (0, 2, 1, 3)


def _cparams(sem):
    return pltpu.CompilerParams(dimension_semantics=sem, vmem_limit_bytes=VMEM_LIMIT)


def _proj_kernel(x_ref, nw_ref, w_ref, cos_ref, sin_ref,
                 qr_ref, kr_ref, vr_ref, gr_ref, qa_ref, ka_ref, va_ref):
    x = x_ref[0]
    ms = jnp.mean(x * x, axis=-1, keepdims=True)
    xn = (x * lax.rsqrt(ms + RMS_EPS) * nw_ref[...]).astype(BF16)
    cosf = cos_ref[...]
    sinf = sin_ref[...]

    def mm(i):
        return jnp.dot(xn, w_ref[:, IN_COLS[i]:IN_COLS[i + 1]], preferred_element_type=F32)

    def rotary(t, scale):
        for h in range(RET_HEADS):
            th = t[:, h * RET_D:(h + 1) * RET_D]
            yield h, (th * cosf + pltpu.roll(th, RET_D // 2, 1) * sinf) * scale

    for h, r in rotary(mm(0), 1.0):
        qr_ref[0, :, h * RET_D:(h + 1) * RET_D] = r.astype(BF16)
    for h, r in rotary(mm(1), RET_D ** -0.5):
        kr_ref[0, :, h * RET_D:(h + 1) * RET_D] = r.astype(BF16)
    vr_ref[0] = mm(2).astype(BF16)
    gr_ref[0] = mm(3)
    qa_ref[0] = (mm(4) * (ATT_HD ** -0.5)).astype(BF16)
    ka_ref[0] = mm(5).astype(BF16)
    va_ref[0] = mm(6).astype(BF16)


def _proj(x, norm_w, w_in_bf, cos_t, sin_t, tm):
    b, s, d = x.shape
    widths = (512, 512, 512, 512, 512, 128, 128)
    dtypes = (BF16, BF16, BF16, F32, BF16, BF16, BF16)
    return pl.pallas_call(
        _proj_kernel,
        out_shape=[jax.ShapeDtypeStruct((b, s, w), dt) for w, dt in zip(widths, dtypes)],
        grid=(b, s // tm),
        in_specs=[
            pl.BlockSpec((1, tm, d), lambda i, j: (i, j, 0)),
            pl.BlockSpec((1, d), lambda i, j: (0, 0)),
            pl.BlockSpec(w_in_bf.shape, lambda i, j: (0, 0)),
            pl.BlockSpec((tm, LANES), lambda i, j: (j, 0)),
            pl.BlockSpec((tm, LANES), lambda i, j: (j, 0)),
        ],
        out_specs=[pl.BlockSpec((1, tm, w), lambda i, j: (i, j, 0)) for w in widths],
        compiler_params=_cparams(("arbitrary", "arbitrary")),
        name="proj",
    )(x, norm_w, w_in_bf, cos_t, sin_t)


def _ret_kernel(q_ref, k_ref, v_ref, g_ref, kpre_ref, vpre_ref, dmat_ref, qwf_ref, qwb_ref,
                kwf_ref, kwb_ref, kwp_ref, gdec_ref, gnw_ref, o_ref, sf_ref, sb_ref, sball_ref):
    ph = pl.program_id(1)
    c = pl.program_id(2)
    nc = pl.num_programs(2)
    contract0 = (((0,), (0,)), ((), ()))
    contract1 = (((1,), (1,)), ((), ()))

    def head(ref, h):
        return ref[0, :, h * RET_D:(h + 1) * RET_D]

    def kv_update(kh, vh, kw):
        kw = (kh.astype(F32) * kw).astype(BF16)
        return lax.dot_general(kw, vh, contract0, preferred_element_type=F32)

    @pl.when(ph == 0)
    def _backward_states():
        @pl.when(c == 0)
        def _():
            sb_ref[...] = jnp.zeros_like(sb_ref)

        j = nc - 1 - c
        for h in range(RET_HEADS):
            sball_ref[j, h] = sb_ref[h].astype(BF16)
            sb_ref[h] = gdec_ref[RET_HEADS + h] * sb_ref[h] + kv_update(head(k_ref, h), head(v_ref, h), kwb_ref[h])

    @pl.when(ph == 1)
    def _outputs():
        @pl.when(c == 0)
        def _():
            for h in range(RET_HEADS):
                kp = kpre_ref[:, h * RET_D:(h + 1) * RET_D]
                vp = vpre_ref[:, h * RET_D:(h + 1) * RET_D]
                sf_ref[h] = kv_update(kp, vp, kwp_ref[h])

        for h in range(RET_HEADS):
            qh, kh, vh = head(q_ref, h), head(k_ref, h), head(v_ref, h)
            s = lax.dot_general(qh, kh, contract1, preferred_element_type=F32) * dmat_ref[h]
            o = jnp.dot(s.astype(BF16), vh, preferred_element_type=F32)
            o = o + jnp.dot(qh, sf_ref[h].astype(BF16), preferred_element_type=F32) * qwf_ref[h]
            o = o + jnp.dot(qh, sball_ref[c, h], preferred_element_type=F32) * qwb_ref[h]
            mu = jnp.mean(o, axis=-1, keepdims=True)
            oc = o - mu
            var = jnp.mean(oc * oc, axis=-1, keepdims=True)
            on = oc * lax.rsqrt(var + GN_EPS) * gnw_ref[:, h * RET_D:(h + 1) * RET_D]
            g = head(g_ref, h)
            o_ref[0, :, h * RET_D:(h + 1) * RET_D] = (g * jax.nn.sigmoid(g) * on).astype(BF16)
            sf_ref[h] = gdec_ref[h] * sf_ref[h] + kv_update(kh, vh, kwf_ref[h])


def _retention(qr, kr, vr, gr, kpre, vpre, dmat, qwf, qwb, kwf, kwb, kwp, gdec, gnw, chunk):
    b, s, w = qr.shape
    nc = s // chunk

    def seq_map(i, ph, c):
        return (i, jnp.where(ph == 0, nc - 1 - c, c), 0)

    def fwd_only(i, ph, c):
        return (i, c * ph, 0)

    const2 = lambda i, ph, c: (0, 0)
    const3 = lambda i, ph, c: (0, 0, 0)
    return pl.pallas_call(
        _ret_kernel,
        out_shape=jax.ShapeDtypeStruct((b, s, w), BF16),
        grid=(b, 2, nc),
        in_specs=[
            pl.BlockSpec((1, chunk, w), fwd_only),
            pl.BlockSpec((1, chunk, w), seq_map),
            pl.BlockSpec((1, chunk, w), seq_map),
            pl.BlockSpec((1, chunk, w), fwd_only),
            pl.BlockSpec(kpre.shape, const2),
            pl.BlockSpec(vpre.shape, const2),
            pl.BlockSpec(dmat.shape, const3),
            pl.BlockSpec(qwf.shape, const3),
            pl.BlockSpec(qwb.shape, const3),
            pl.BlockSpec(kwf.shape, const3),
            pl.BlockSpec(kwb.shape, const3),
            pl.BlockSpec(kwp.shape, const3),
            pl.BlockSpec(memory_space=pltpu.SMEM),
            pl.BlockSpec(gnw.shape, const2),
        ],
        out_specs=pl.BlockSpec((1, chunk, w), fwd_only),
        scratch_shapes=[
            pltpu.VMEM((RET_HEADS, RET_D, RET_D), F32),
            pltpu.VMEM((RET_HEADS, RET_D, RET_D), F32),
            pltpu.VMEM((nc, RET_HEADS, RET_D, RET_D), BF16),
        ],
        compiler_params=_cparams(("arbitrary", "arbitrary", "arbitrary")),
        name="retention",
    )(qr, kr, vr, gr, kpre, vpre, dmat, qwf, qwb, kwf, kwb, kwp, gdec, gnw)


def _attn_kernel(q_ref, kp_ref, kc_ref, kn_ref, vp_ref, vc_ref, vn_ref, km_ref, vm_ref,
                 bias_ref, sink_ref, o_ref):
    j = pl.program_id(1)
    nb = pl.num_programs(1)
    lane = lax.broadcasted_iota(jnp.int32, (1, 4 * ATT_BLOCK), 1)
    first = jnp.logical_and(lane < ATT_BLOCK, j == 0)
    last = jnp.logical_and(jnp.logical_and(lane >= 2 * ATT_BLOCK, lane < 3 * ATT_BLOCK), j == nb - 1)
    pen = jnp.where(jnp.logical_or(first, last), NEG, 0.0).astype(F32)
    contract1 = (((1,), (1,)), ((), ()))
    outs = []
    for g in range(ATT_KV):
        sl = slice(g * ATT_HD, (g + 1) * ATT_HD)
        kall = jnp.concatenate([kp_ref[0][:, sl], kc_ref[0][:, sl], kn_ref[0][:, sl], km_ref[:, sl]], axis=0)
        vall = jnp.concatenate([vp_ref[0][:, sl], vc_ref[0][:, sl], vn_ref[0][:, sl], vm_ref[:, sl]], axis=0)
        for hh in range(ATT_GROUP):
            h = g * ATT_GROUP + hh
            q = q_ref[0, :, h * ATT_HD:(h + 1) * ATT_HD]
            s = lax.dot_general(q, kall, contract1, preferred_element_type=F32)
            s = s + bias_ref[0, h] + pen
            snk = sink_ref[h]
            m = jnp.maximum(jnp.max(s, axis=-1, keepdims=True), snk)
            p = jnp.exp(s - m)
            den = jnp.sum(p, axis=-1, keepdims=True) + jnp.exp(snk - m)
            o = jnp.dot(p.astype(BF16), vall, preferred_element_type=F32)
            outs.append(o / den)
    for h2 in range(ATT_HEADS // 2):
        o_ref[0, :, h2 * LANES:(h2 + 1) * LANES] = jnp.concatenate(
            [outs[2 * h2], outs[2 * h2 + 1]], axis=-1).astype(BF16)


def _attention(qa, ka, va, kmeta, vmeta, bias, sink):
    b, s, w = qa.shape
    nb = s // ATT_BLOCK
    kvw = ka.shape[-1]
    prev = lambda i, j: (i, jnp.maximum(j - 1, 0), 0)
    cur = lambda i, j: (i, j, 0)
    nxt = lambda i, j: (i, jnp.minimum(j + 1, nb - 1), 0)
    const2 = lambda i, j: (0, 0)
    kv = lambda m: pl.BlockSpec((1, ATT_BLOCK, kvw), m)
    return pl.pallas_call(
        _attn_kernel,
        out_shape=jax.ShapeDtypeStruct((b, s, w), BF16),
        grid=(b, nb),
        in_specs=[
            pl.BlockSpec((1, ATT_BLOCK, w), cur),
            kv(prev), kv(cur), kv(nxt), kv(prev), kv(cur), kv(nxt),
            pl.BlockSpec(kmeta.shape, const2),
            pl.BlockSpec(vmeta.shape, const2),
            pl.BlockSpec((1,) + bias.shape[1:], lambda i, j: (jnp.minimum(j, 1), 0, 0, 0)),
            pl.BlockSpec(memory_space=pltpu.SMEM),
        ],
        out_specs=pl.BlockSpec((1, ATT_BLOCK, w), cur),
        compiler_params=_cparams(("arbitrary", "arbitrary")),
        name="attention",
    )(qa, ka, ka, ka, va, va, va, kmeta, vmeta, bias, sink)


def _outproj_kernel(or_ref, oa_ref, x_ref, w_ref, nw_ref, h_ref, xn_ref):
    half = or_ref.shape[-1]
    h = x_ref[0]
    h = h + jnp.dot(or_ref[0], w_ref[:half], preferred_element_type=F32)
    h = h + jnp.dot(oa_ref[0], w_ref[half:], preferred_element_type=F32)
    h_ref[0] = h
    ms = jnp.mean(h * h, axis=-1, keepdims=True)
    xn_ref[0] = h * lax.rsqrt(ms + RMS_EPS) * nw_ref[...]


def _outproj(o_r, o_a, x, w_out_bf, norm_w, tm):
    b, s, d = x.shape
    half = o_r.shape[-1]
    row = lambda i, j: (i, j, 0)
    const2 = lambda i, j: (0, 0)
    return pl.pallas_call(
        _outproj_kernel,
        out_shape=[jax.ShapeDtypeStruct((b, s, d), F32)] * 2,
        grid=(b, s // tm),
        in_specs=[
            pl.BlockSpec((1, tm, half), row),
            pl.BlockSpec((1, tm, half), row),
            pl.BlockSpec((1, tm, d), row),
            pl.BlockSpec(w_out_bf.shape, const2),
            pl.BlockSpec((1, d), const2),
        ],
        out_specs=[pl.BlockSpec((1, tm, d), row)] * 2,
        compiler_params=_cparams(("arbitrary", "arbitrary")),
        name="outproj",
    )(o_r, o_a, x, w_out_bf, norm_w)


def _top16_rows(s, iota):
    nrows = s.shape[0]
    vals, ids = [], []
    for _ in range(PEER_TOPK):
        m = jnp.max(s, axis=0, keepdims=True)
        am = jnp.min(jnp.where(s == m, iota, nrows), axis=0, keepdims=True)
        vals.append(m)
        ids.append(am)
        s = jnp.where(iota == am, -jnp.inf, s)
    return jnp.concatenate(vals, axis=0), jnp.concatenate(ids, axis=0)


_PAIR_GROUPS = ((0, 0, 8), (0, 8, 8), (1, 0, 8), (2, 0, 5), (3, 0, 4), (4, 0, 3), (5, 0, 2), (6, 0, 2), (7, 0, 2))


def _route_kernel(x_ref, wq_ref, pk_ref, idx_ref, gate_ref, sc_ref):
    tm = x_ref.shape[1]
    q = jnp.dot(x_ref[0].astype(BF16), wq_ref[...], preferred_element_type=F32).astype(BF16)
    contract1 = (((1,), (1,)), ((), ()))
    for hp in range(2 * PEER_HEADS):
        sc_ref[hp] = lax.dot_general(pk_ref[hp], q[:, hp * PEER_NKEYS:(hp + 1) * PEER_NKEYS], contract1,
                                     preferred_element_type=F32)
    iota = lax.broadcasted_iota(jnp.int32, (PEER_NKEYS, LANES), 0)
    sub = lax.broadcasted_iota(jnp.int32, (SUBLANES, LANES), 0)
    far = PEER_TOPK * PEER_TOPK

    def head_body(h, carry):
        for lt in range(tm // LANES):
            lanes = slice(lt * LANES, (lt + 1) * LANES)
            s1, i1 = _top16_rows(sc_ref[2 * h, :, lanes], iota)
            s2, i2 = _top16_rows(sc_ref[2 * h + 1, :, lanes], iota)
            i1 = i1 * PEER_NKEYS
            sc, ex, pos = [], [], []
            for i, j0, cnt in _PAIR_GROUPS:
                ok = sub < cnt
                sc.append(jnp.where(ok, s1[i:i + 1] + s2[j0:j0 + SUBLANES], -jnp.inf))
                ex.append(i1[i:i + 1] + i2[j0:j0 + SUBLANES])
                pos.append(jnp.where(ok, i * PEER_TOPK + j0 + sub, far))
            sc.append(s1[SUBLANES:] + s2[0:1])
            ex.append(i1[SUBLANES:] + i2[0:1])
            pos.append((SUBLANES + sub) * PEER_TOPK)
            sc = jnp.concatenate(sc, axis=0)
            ex = jnp.concatenate(ex, axis=0)
            pos = jnp.concatenate(pos, axis=0)
            top, experts = [], []
            for _ in range(PEER_TOPK):
                m = jnp.max(sc, axis=0, keepdims=True)
                pm = jnp.min(jnp.where(sc == m, pos, far), axis=0, keepdims=True)
                hit = pos == pm
                experts.append(jnp.max(jnp.where(hit, ex, -1), axis=0, keepdims=True))
                top.append(m)
                sc = jnp.where(hit, -jnp.inf, sc)
            top = jnp.concatenate(top, axis=0)
            e = jnp.exp(top - top[0:1])
            rows = pl.ds(pl.multiple_of(h * PEER_TOPK, PEER_TOPK), PEER_TOPK)
            gate_ref[0, rows, lanes] = e / jnp.sum(e, axis=0, keepdims=True)
            idx_ref[0, rows, lanes] = jnp.concatenate(experts, axis=0)
        return carry

    lax.fori_loop(0, PEER_HEADS, head_body, 0)


def _route(xn, wq_bf, pk_bf, tm):
    b, s, d = xn.shape
    const2 = lambda i, j: (0, 0)
    slot = lambda i, j: (i, 0, j)
    return pl.pallas_call(
        _route_kernel,
        out_shape=[jax.ShapeDtypeStruct((b, PEER_SLOTS, s), jnp.int32),
                   jax.ShapeDtypeStruct((b, PEER_SLOTS, s), F32)],
        grid=(b, s // tm),
        in_specs=[
            pl.BlockSpec((1, tm, d), lambda i, j: (i, j, 0)),
            pl.BlockSpec(wq_bf.shape, const2),
            pl.BlockSpec(pk_bf.shape, lambda i, j: (0, 0, 0)),
        ],
        out_specs=[pl.BlockSpec((1, PEER_SLOTS, tm), slot)] * 2,
        scratch_shapes=[pltpu.VMEM((2 * PEER_HEADS, PEER_NKEYS, tm), F32)],
        compiler_params=_cparams(("arbitrary", "arbitrary")),
        name="route",
    )(xn, wq_bf, pk_bf)


def _bitrev4(n):
    return ((n & 1) << 3) | ((n & 2) << 1) | ((n & 4) >> 1) | ((n & 8) >> 3)


def _pair_slots():
    out = []
    for n in range(PEER_SLOTS // SUBLANES):
        g = _bitrev4(n)
        for c in range(4):
            top = SUBLANES * g + SLOT_ORDER[c]
            out.append((top, top + 4))
    return out


def _gather_pair(tbl_ref, top_ref, bot_ref, t, upper):
    ra = tbl_ref[pl.ds(pl.multiple_of(top_ref[t], ROWS_PER_EXPERT), SUBLANES), :]
    rb = tbl_ref[pl.ds(pl.multiple_of(bot_ref[t], ROWS_PER_EXPERT), SUBLANES), :]
    row = jnp.where(upper, ra, rb)
    lo = lax.bitcast_convert_type(row << 16, F32)
    hi = lax.bitcast_convert_type(row & jnp.uint32(0xFFFF0000), F32)
    return lo, hi


def _merge_sublanes(a, b, h, sub):
    keep = (sub & h) == 0
    return jnp.where(keep, a + pltpu.roll(a, SUBLANES - h, 0), b + pltpu.roll(b, h, 0))


def _merge_lanes(a, b, h, lane):
    keep = (lane & h) == 0
    return jnp.where(keep, a + pltpu.roll(a, LANES - h, 1), b + pltpu.roll(b, h, 1))


def _slot_specs(tb):
    return [pl.BlockSpec((None, None, tb), lambda i, k=k: (k, 0, i), memory_space=pltpu.SMEM,
                         pipeline_mode=pl.Buffered(1)) for k in range(PEER_SLOTS)]


def _peer_u_kernel(*refs):
    offs = refs[:PEER_SLOTS]
    x_ref, tbl_ref, o_ref = refs[PEER_SLOTS:]
    sub = lax.broadcasted_iota(jnp.int32, (SUBLANES, LANES), 0)
    lane = lax.broadcasted_iota(jnp.int32, (SUBLANES, LANES), 1)
    upper = sub < ROWS_PER_EXPERT
    pairs = _pair_slots()

    def group_body(gi, carry):
        ys = []
        for j in range(SUBLANES):
            t = gi * SUBLANES + j
            x = x_ref[pl.ds(pl.multiple_of(t * SUBLANES, SUBLANES), SUBLANES), :]
            swapped = pltpu.roll(x, ROWS_PER_EXPERT, 0)
            xa = jnp.where(upper, x, swapped)
            xb = jnp.where(upper, swapped, x)
            zs = []
            for n in range(PEER_SLOTS // SUBLANES):
                ps = []
                for c in range(4):
                    top, bot = pairs[4 * n + c]
                    lo, hi = _gather_pair(tbl_ref, offs[top], offs[bot], t, upper)
                    ps.append(lo * xa + hi * xb)
                r0 = _merge_sublanes(ps[0], ps[1], 2, sub)
                r1 = _merge_sublanes(ps[2], ps[3], 2, sub)
                zs.append(_merge_sublanes(r0, r1, 1, sub))
            for h in (64, 32, 16, 8):
                zs = [_merge_lanes(zs[2 * i], zs[2 * i + 1], h, lane) for i in range(len(zs) // 2)]
            ys.append(zs[0])
        w4 = [_merge_lanes(ys[j], ys[j + 4], 4, lane) for j in range(4)]
        w2 = [_merge_lanes(w4[j], w4[j + 2], 2, lane) for j in range(2)]
        o_ref[gi] = _merge_lanes(w2[0], w2[1], 1, lane)
        return carry

    lax.fori_loop(0, o_ref.shape[0], group_body, 0)


def _peer_u(offs_t, x_rows, tbl, tb):
    n = offs_t.shape[-1]
    return pl.pallas_call(
        _peer_u_kernel,
        out_shape=jax.ShapeDtypeStruct((n // SUBLANES, SUBLANES, LANES), F32),
        grid=(n // tb,),
        in_specs=_slot_specs(tb) + [
            pl.BlockSpec((tb * SUBLANES, LANES), lambda i: (i, 0)),
            pl.BlockSpec(tbl.shape, lambda i: (0, 0), pipeline_mode=pl.Buffered(1)),
        ],
        out_specs=pl.BlockSpec((tb // SUBLANES, SUBLANES, LANES), lambda i: (i, 0, 0)),
        compiler_params=_cparams(("arbitrary",)),
        name="peer_u",
    )(*([offs_t] * PEER_SLOTS), x_rows, tbl)


def _peer_w_kernel(gate_ref, act_ref, w_ref):
    a = act_ref[...]
    w_ref[...] = gate_ref[...] * (0.5 * a * (1.0 + lax.erf(a * (2.0 ** -0.5))))


def _peer_w(gate_t, act_t, tm):
    n = gate_t.shape[-1]
    spec = pl.BlockSpec((PEER_SLOTS, tm), lambda i: (0, i))
    return pl.pallas_call(
        _peer_w_kernel,
        out_shape=jax.ShapeDtypeStruct(gate_t.shape, F32),
        grid=(n // tm,),
        in_specs=[spec, spec],
        out_specs=spec,
        compiler_params=_cparams(("arbitrary",)),
        name="peer_w",
    )(gate_t, act_t)


def _peer_v_kernel(*refs):
    offs = refs[:PEER_SLOTS]
    wts = refs[PEER_SLOTS:2 * PEER_SLOTS]
    h_ref, tbl_ref, nw_ref, o_ref = refs[2 * PEER_SLOTS:]
    sub = lax.broadcasted_iota(jnp.int32, (SUBLANES, LANES), 0)
    upper = sub < ROWS_PER_EXPERT
    pairs = _pair_slots()

    def token_body(t, carry):
        acc = [jnp.zeros((SUBLANES, LANES), F32) for _ in range(4)]
        for q, (top, bot) in enumerate(pairs):
            lo, hi = _gather_pair(tbl_ref, offs[top], offs[bot], t, upper)
            w = jnp.where(upper, wts[top][t], wts[bot][t])
            k = 2 * (q % 2)
            acc[k] = acc[k] + lo * w
            acc[k + 1] = acc[k + 1] + hi * w
        lo = acc[0] + acc[2]
        hi = acc[1] + acc[3]
        lo = lo + pltpu.roll(lo, ROWS_PER_EXPERT, 0)
        hi = hi + pltpu.roll(hi, ROWS_PER_EXPERT, 0)
        rows = pl.ds(pl.multiple_of(t * SUBLANES, SUBLANES), SUBLANES)
        h = h_ref[rows, :] + jnp.where(upper, lo, hi)
        ms = jnp.sum(jnp.sum(h * h, axis=1, keepdims=True), axis=0, keepdims=True) * (1.0 / D_MODEL)
        o_ref[rows, :] = h * lax.rsqrt(ms + RMS_EPS) * nw_ref[...]
        return carry

    lax.fori_loop(0, h_ref.shape[0] // SUBLANES, token_body, 0)


def _peer_v(offs_t, w_t, h_rows, tbl, norm_rows, tb):
    n = offs_t.shape[-1]
    rows = pl.BlockSpec((tb * SUBLANES, LANES), lambda i: (i, 0))
    return pl.pallas_call(
        _peer_v_kernel,
        out_shape=jax.ShapeDtypeStruct(h_rows.shape, F32),
        grid=(n // tb,),
        in_specs=_slot_specs(tb) + _slot_specs(tb) + [
            rows,
            pl.BlockSpec(tbl.shape, lambda i: (0, 0), pipeline_mode=pl.Buffered(1)),
            pl.BlockSpec((SUBLANES, LANES), lambda i: (0, 0)),
        ],
        out_specs=rows,
        compiler_params=_cparams(("arbitrary",)),
        name="peer_v",
    )(*([offs_t] * PEER_SLOTS), *([w_t] * PEER_SLOTS), h_rows, tbl, norm_rows)


def _pack_table(t):
    half = D_MODEL // 2
    bits = lax.bitcast_convert_type(t.astype(BF16), jnp.uint16).astype(jnp.uint32)
    words = bits[:, :half] | (bits[:, half:] << 16)
    words = words.reshape(t.shape[0] * ROWS_PER_EXPERT, LANES)
    return jnp.pad(words, ((TABLE_PAD, TABLE_PAD), (0, 0)))


def _t5_bucket(rel):
    half = N_BUCKETS // 2
    exact = half // 2
    n = jnp.abs(rel)
    large = exact + (jnp.log(jnp.maximum(n, 1).astype(F32) / exact)
                     / math.log(MAX_DISTANCE / exact) * (half - exact)).astype(jnp.int32)
    large = jnp.minimum(large, half - 1)
    return jnp.where(rel > 0, half, 0) + jnp.where(n < exact, n, large)


def _attention_bias(rel_bias):
    rb = rel_bias.astype(F32)
    i = jnp.arange(ATT_BLOCK)
    jb = jnp.arange(3 * ATT_BLOCK)
    rel = (jb[None, :] - ATT_BLOCK) - i[:, None]
    band = jnp.where((jnp.abs(rel) <= ATT_BLOCK)[..., None], rb[_t5_bucket(rel)], NEG)
    mpos = PREFIX - N_META + jnp.arange(N_META)
    variants = []
    for blk in (0, 1):
        qpos = PREFIX + blk * ATT_BLOCK + i
        meta = rb[_t5_bucket(mpos[None, :] - qpos[:, None])]
        fill = jnp.full((ATT_BLOCK, ATT_BLOCK - N_META, ATT_HEADS), NEG, F32)
        variants.append(jnp.concatenate([band, meta, fill], axis=1).transpose(2, 0, 1))
    return jnp.stack(variants)


def _rope_tables(pos):
    half = RET_D // 2
    inv = ROPE_BASE ** (-jnp.arange(half, dtype=F32) / half)
    ang = pos.astype(F32)[:, None] * inv[None, :]
    cos, sin = jnp.cos(ang), jnp.sin(ang)
    return jnp.concatenate([cos, cos], axis=1), jnp.concatenate([-sin, sin], axis=1)


def _decay_tables(dec_f, dec_b, chunk):
    lf = jax.nn.log_sigmoid(dec_f.astype(F32))[:, None]
    lb = jax.nn.log_sigmoid(dec_b.astype(F32))[:, None]
    idx = jnp.arange(chunk, dtype=F32)
    diff = idx[:, None] - idx[None, :]
    dmat = (jnp.where(diff >= 0, jnp.exp(jnp.maximum(diff, 0.0)[None] * lf[:, :, None]), 0.0)
            + jnp.where(diff < 0, jnp.exp(jnp.maximum(-diff, 0.0)[None] * lb[:, :, None]), 0.0))
    bc = lambda v: jnp.broadcast_to(v[:, :, None], v.shape + (RET_D,))
    qwf = bc(jnp.exp((idx + 1.0)[None] * lf))
    kwf = bc(jnp.exp((chunk - 1.0 - idx)[None] * lf))
    qwb = bc(jnp.exp((chunk - idx)[None] * lb))
    kwb = bc(jnp.exp(idx[None] * lb))
    pidx = jnp.arange(PREFIX, dtype=F32)
    kwp = bc(jnp.exp((PREFIX - 1.0 - pidx)[None] * lf))
    gdec = jnp.concatenate([jnp.exp(chunk * lf[:, 0]), jnp.exp(chunk * lb[:, 0])])
    return dmat, qwf, qwb, kwf, kwb, kwp, gdec


def _divisor_tile(n, want):
    t = min(n, want)
    while n % t:
        t //= 2
    return t


def _encode(x, shared):
    b, s, d = x.shape
    n = b * s
    tm = _divisor_tile(s, 512)
    chunk = _divisor_tile(s, 256)
    cos_t, sin_t = _rope_tables(jnp.arange(s) + N_META)
    qr, kr, vr, gr, qa, ka, va = _proj(x, shared["norm_mix"], shared["w_in"], cos_t, sin_t, tm)
    o_r = _retention(qr, kr, vr, gr, shared["kr_pre"], shared["vr_pre"], *_decay_tables(
        shared["dec_f"], shared["dec_b"], chunk), shared["gn_w"], chunk)
    o_a = _attention(qa, ka, va, shared["k_meta"], shared["v_meta"], shared["bias"], shared["sink"])
    h1, xn2 = _outproj(o_r, o_a, x, shared["w_out"], shared["norm_ffn"], tm)
    idx_t, gate_t = _route(xn2, shared["wq"], shared["pkeys"], _divisor_tile(s, ROUTE_TOKENS))
    idx_t = idx_t.transpose(1, 0, 2).reshape(PEER_SLOTS, n)
    gate_t = gate_t.transpose(1, 0, 2).reshape(PEER_SLOTS, n)
    top_half = (jnp.arange(PEER_SLOTS) % SUBLANES) < ROWS_PER_EXPERT
    offs_t = (idx_t * ROWS_PER_EXPERT + jnp.where(top_half, TABLE_PAD, 0)[:, None]).reshape(PEER_SLOTS, 1, n)
    tb = _divisor_tile(n, PEER_TOKENS)
    act = _peer_u(offs_t, xn2.reshape(n * SUBLANES, LANES), shared["u_tbl"], tb)
    act = act.reshape(n // SUBLANES, SUBLANES, PEER_SLOTS // SUBLANES, SUBLANES)
    act_t = act.transpose(2, 1, 0, 3).reshape(PEER_SLOTS, n)
    w_t = _peer_w(gate_t, act_t, _divisor_tile(n, 2048)).reshape(PEER_SLOTS, 1, n)
    y = _peer_v(offs_t, w_t, h1.reshape(n * SUBLANES, LANES), shared["v_tbl"], shared["norm_final"], tb)
    return y.reshape(b, s, d)


def kernel(x_prompt, x_sample, meta_tokens, norm_mix_w, w_in, ret_decay_fwd, ret_decay_bwd, ret_gn_w,
           attn_sink, rel_bias, w_out, norm_ffn_w, peer_wq, peer_keys, peer_u, peer_v, norm_final_w):
    layer = 0
    shared = {
        "norm_mix": norm_mix_w[layer][None, :].astype(F32),
        "w_in": w_in[layer].astype(BF16),
        "dec_f": ret_decay_fwd[layer],
        "dec_b": ret_decay_bwd[layer],
        "gn_w": ret_gn_w[layer][None, :].astype(F32),
        "sink": attn_sink[layer].astype(F32),
        "bias": _attention_bias(rel_bias),
        "w_out": w_out[layer].astype(BF16),
        "norm_ffn": norm_ffn_w[layer][None, :].astype(F32),
        "wq": peer_wq[layer].astype(BF16),
        "pkeys": peer_keys[layer].reshape(2 * PEER_HEADS, PEER_NKEYS, PEER_NKEYS).astype(BF16),
        "u_tbl": _pack_table(peer_u[layer]),
        "v_tbl": _pack_table(peer_v[layer]),
        "norm_final": norm_final_w.reshape(SUBLANES, LANES).astype(F32),
    }
    prefix = jnp.concatenate([jnp.zeros((PREFIX - N_META, D_MODEL), x_prompt.dtype),
                              meta_tokens.astype(x_prompt.dtype)], axis=0)[None]
    cos_p, sin_p = _rope_tables(jnp.arange(PREFIX) - (PREFIX - N_META))
    _, kr_p, vr_p, _, _, ka_p, va_p = _proj(prefix, shared["norm_mix"], shared["w_in"], cos_p, sin_p, PREFIX)
    shared["kr_pre"], shared["vr_pre"] = kr_p[0], vr_p[0]
    pad_meta = lambda t: jnp.pad(t[0, PREFIX - N_META:], ((0, ATT_BLOCK - N_META), (0, 0)))
    shared["k_meta"], shared["v_meta"] = pad_meta(ka_p), pad_meta(va_p)
    return (_encode(x_prompt, shared), _encode(x_sample, shared))
```

```python
import functools
import math

import jax
import jax.numpy as jnp
from jax import lax
from jax.experimental import pallas as pl
from jax.experimental.pallas import tpu as pltpu

F32 = jnp.float32
BF16 = jnp.bfloat16

D_MODEL = 1024
N_META = 16
PREFIX = 128
RET_HEADS = 4
RET_D = 128
ATT_HEADS = 8
ATT_KV = 2
ATT_GROUP = ATT_HEADS // ATT_KV
ATT_HD = 64
ATT_BLOCK = 128
N_BUCKETS = 32
MAX_DISTANCE = 128
ROPE_BASE = 10000.0
PEER_HEADS = 8
PEER_NKEYS = 128
PEER_EXPERTS = PEER_NKEYS * PEER_NKEYS
PEER_TOPK = 16
PEER_SLOTS = PEER_HEADS * PEER_TOPK
RMS_EPS = 1e-6
GN_EPS = 1e-5
NEG = -1e30
IN_COLS = (0, 512, 1024, 1536, 2048, 2560, 2688, 2816)

LANES = 128
SUBLANES = 8
VMEM_LIMIT = 56 * 1024 * 1024
ROWS_PER_EXPERT = D_MODEL // 2 // LANES
TABLE_PAD = ROWS_PER_EXPERT
PEER_TOKENS = 256
ROUTE_TOKENS = 256
SLOT_ORDER = (0, 2, 1, 3)
SLOT_GROUP = 8


def _cparams(sem):
    return pltpu.CompilerParams(dimension_semantics=sem, vmem_limit_bytes=VMEM_LIMIT)


def _proj_kernel(x_ref, nw_ref, w_ref, cos_ref, sin_ref,
                 qr_ref, kr_ref, vr_ref, gr_ref, qa_ref, ka_ref, va_ref):
    x = x_ref[0]
    ms = jnp.mean(x * x, axis=-1, keepdims=True)
    xn = (x * lax.rsqrt(ms + RMS_EPS) * nw_ref[...]).astype(BF16)
    cosf = cos_ref[...]
    sinf = sin_ref[...]

    def mm(i):
        return jnp.dot(xn, w_ref[:, IN_COLS[i]:IN_COLS[i + 1]], preferred_element_type=F32)

    def rotary(t, scale):
        for h in range(RET_HEADS):
            th = t[:, h * RET_D:(h + 1) * RET_D]
            yield h, (th * cosf + pltpu.roll(th, RET_D // 2, 1) * sinf) * scale

    for h, r in rotary(mm(0), 1.0):
        qr_ref[0, :, h * RET_D:(h + 1) * RET_D] = r.astype(BF16)
    for h, r in rotary(mm(1), RET_D ** -0.5):
        kr_ref[0, :, h * RET_D:(h + 1) * RET_D] = r.astype(BF16)
    vr_ref[0] = mm(2).astype(BF16)
    gr_ref[0] = mm(3)
    qa_ref[0] = (mm(4) * (ATT_HD ** -0.5)).astype(BF16)
    ka_ref[0] = mm(5).astype(BF16)
    va_ref[0] = mm(6).astype(BF16)


def _proj(x, norm_w, w_in_bf, cos_t, sin_t, tm):
    b, s, d = x.shape
    widths = (512, 512, 512, 512, 512, 128, 128)
    dtypes = (BF16, BF16, BF16, F32, BF16, BF16, BF16)
    return pl.pallas_call(
        _proj_kernel,
        out_shape=[jax.ShapeDtypeStruct((b, s, w), dt) for w, dt in zip(widths, dtypes)],
        grid=(b, s // tm),
        in_specs=[
            pl.BlockSpec((1, tm, d), lambda i, j: (i, j, 0)),
            pl.BlockSpec((1, d), lambda i, j: (0, 0)),
            pl.BlockSpec(w_in_bf.shape, lambda i, j: (0, 0)),
            pl.BlockSpec((tm, LANES), lambda i, j: (j, 0)),
            pl.BlockSpec((tm, LANES), lambda i, j: (j, 0)),
        ],
        out_specs=[pl.BlockSpec((1, tm, w), lambda i, j: (i, j, 0)) for w in widths],
        compiler_params=_cparams(("arbitrary", "arbitrary")),
        name="proj",
    )(x, norm_w, w_in_bf, cos_t, sin_t)


def _ret_kernel(q_ref, k_ref, v_ref, g_ref, kpre_ref, vpre_ref, dmat_ref, qwf_ref, qwb_ref,
                kwf_ref, kwb_ref, kwp_ref, gdec_ref, gnw_ref, o_ref, sf_ref, sb_ref, sball_ref):
    ph = pl.program_id(1)
    c = pl.program_id(2)
    nc = pl.num_programs(2)
    contract0 = (((0,), (0,)), ((), ()))
    contract1 = (((1,), (1,)), ((), ()))

    def head(ref, h):
        return ref[0, :, h * RET_D:(h + 1) * RET_D]

    def kv_update(kh, vh, kw):
        kw = (kh.astype(F32) * kw).astype(BF16)
        return lax.dot_general(kw, vh, contract0, preferred_element_type=F32)

    @pl.when(ph == 0)
    def _backward_states():
        @pl.when(c == 0)
        def _():
            sb_ref[...] = jnp.zeros_like(sb_ref)

        j = nc - 1 - c
        for h in range(RET_HEADS):
            sball_ref[j, h] = sb_ref[h].astype(BF16)
            sb_ref[h] = gdec_ref[RET_HEADS + h] * sb_ref[h] + kv_update(head(k_ref, h), head(v_ref, h), kwb_ref[h])

    @pl.when(ph == 1)
    def _outputs():
        @pl.when(c == 0)
        def _():
            for h in range(RET_HEADS):
                kp = kpre_ref[:, h * RET_D:(h + 1) * RET_D]
                vp = vpre_ref[:, h * RET_D:(h + 1) * RET_D]
                sf_ref[h] = kv_update(kp, vp, kwp_ref[h])

        for h in range(RET_HEADS):
            qh, kh, vh = head(q_ref, h), head(k_ref, h), head(v_ref, h)
            s = lax.dot_general(qh, kh, contract1, preferred_element_type=F32) * dmat_ref[h]
            o = jnp.dot(s.astype(BF16), vh, preferred_element_type=F32)
            o = o + jnp.dot(qh, sf_ref[h].astype(BF16), preferred_element_type=F32) * qwf_ref[h]
            o = o + jnp.dot(qh, sball_ref[c, h], preferred_element_type=F32) * qwb_ref[h]
            mu = jnp.mean(o, axis=-1, keepdims=True)
            oc = o - mu
            var = jnp.mean(oc * oc, axis=-1, keepdims=True)
            on = oc * lax.rsqrt(var + GN_EPS) * gnw_ref[:, h * RET_D:(h + 1) * RET_D]
            g = head(g_ref, h)
            o_ref[0, :, h * RET_D:(h + 1) * RET_D] = (g * jax.nn.sigmoid(g) * on).astype(BF16)
            sf_ref[h] = gdec_ref[h] * sf_ref[h] + kv_update(kh, vh, kwf_ref[h])


def _retention(qr, kr, vr, gr, kpre, vpre, dmat, qwf, qwb, kwf, kwb, kwp, gdec, gnw, chunk):
    b, s, w = qr.shape
    nc = s // chunk

    def seq_map(i, ph, c):
        return (i, jnp.where(ph == 0, nc - 1 - c, c), 0)

    def fwd_only(i, ph, c):
        return (i, c * ph, 0)

    const2 = lambda i, ph, c: (0, 0)
    const3 = lambda i, ph, c: (0, 0, 0)
    return pl.pallas_call(
        _ret_kernel,
        out_shape=jax.ShapeDtypeStruct((b, s, w), BF16),
        grid=(b, 2, nc),
        in_specs=[
            pl.BlockSpec((1, chunk, w), fwd_only),
            pl.BlockSpec((1, chunk, w), seq_map),
            pl.BlockSpec((1, chunk, w), seq_map),
            pl.BlockSpec((1, chunk, w), fwd_only),
            pl.BlockSpec(kpre.shape, const2),
            pl.BlockSpec(vpre.shape, const2),
            pl.BlockSpec(dmat.shape, const3),
            pl.BlockSpec(qwf.shape, const3),
            pl.BlockSpec(qwb.shape, const3),
            pl.BlockSpec(kwf.shape, const3),
            pl.BlockSpec(kwb.shape, const3),
            pl.BlockSpec(kwp.shape, const3),
            pl.BlockSpec(memory_space=pltpu.SMEM),
            pl.BlockSpec(gnw.shape, const2),
        ],
        out_specs=pl.BlockSpec((1, chunk, w), fwd_only),
        scratch_shapes=[
            pltpu.VMEM((RET_HEADS, RET_D, RET_D), F32),
            pltpu.VMEM((RET_HEADS, RET_D, RET_D), F32),
            pltpu.VMEM((nc, RET_HEADS, RET_D, RET_D), BF16),
        ],
        compiler_params=_cparams(("arbitrary", "arbitrary", "arbitrary")),
        name="retention",
    )(qr, kr, vr, gr, kpre, vpre, dmat, qwf, qwb, kwf, kwb, kwp, gdec, gnw)


def _attn_kernel(q_ref, kp_ref, kc_ref, kn_ref, vp_ref, vc_ref, vn_ref, km_ref, vm_ref,
                 bias_ref, sink_ref, o_ref):
    j = pl.program_id(1)
    nb = pl.num_programs(1)
    lane = lax.broadcasted_iota(jnp.int32, (1, 4 * ATT_BLOCK), 1)
    first = jnp.logical_and(lane < ATT_BLOCK, j == 0)
    last = jnp.logical_and(jnp.logical_and(lane >= 2 * ATT_BLOCK, lane < 3 * ATT_BLOCK), j == nb - 1)
    pen = jnp.where(jnp.logical_or(first, last), NEG, 0.0).astype(F32)
    contract1 = (((1,), (1,)), ((), ()))
    outs = []
    for g in range(ATT_KV):
        sl = slice(g * ATT_HD, (g + 1) * ATT_HD)
        kall = jnp.concatenate([kp_ref[0][:, sl], kc_ref[0][:, sl], kn_ref[0][:, sl], km_ref[:, sl]], axis=0)
        vall = jnp.concatenate([vp_ref[0][:, sl], vc_ref[0][:, sl], vn_ref[0][:, sl], vm_ref[:, sl]], axis=0)
        for hh in range(ATT_GROUP):
            h = g * ATT_GROUP + hh
            q = q_ref[0, :, h * ATT_HD:(h + 1) * ATT_HD]
            s = lax.dot_general(q, kall, contract1, preferred_element_type=F32)
            s = s + bias_ref[0, h] + pen
            snk = sink_ref[h]
            m = jnp.maximum(jnp.max(s, axis=-1, keepdims=True), snk)
            p = jnp.exp(s - m)
            den = jnp.sum(p, axis=-1, keepdims=True) + jnp.exp(snk - m)
            o = jnp.dot(p.astype(BF16), vall, preferred_element_type=F32)
            outs.append(o / den)
    for h2 in range(ATT_HEADS // 2):
        o_ref[0, :, h2 * LANES:(h2 + 1) * LANES] = jnp.concatenate(
            [outs[2 * h2], outs[2 * h2 + 1]], axis=-1).astype(BF16)


def _attention(qa, ka, va, kmeta, vmeta, bias, sink):
    b, s, w = qa.shape
    nb = s // ATT_BLOCK
    kvw = ka.shape[-1]
    prev = lambda i, j: (i, jnp.maximum(j - 1, 0), 0)
    cur = lambda i, j: (i, j, 0)
    nxt = lambda i, j: (i, jnp.minimum(j + 1, nb - 1), 0)
    const2 = lambda i, j: (0, 0)
    kv = lambda m: pl.BlockSpec((1, ATT_BLOCK, kvw), m)
    return pl.pallas_call(
        _attn_kernel,
        out_shape=jax.ShapeDtypeStruct((b, s, w), BF16),
        grid=(b, nb),
        in_specs=[
            pl.BlockSpec((1, ATT_BLOCK, w), cur),
            kv(prev), kv(cur), kv(nxt), kv(prev), kv(cur), kv(nxt),
            pl.BlockSpec(kmeta.shape, const2),
            pl.BlockSpec(vmeta.shape, const2),
            pl.BlockSpec((1,) + bias.shape[1:], lambda i, j: (jnp.minimum(j, 1), 0, 0, 0)),
            pl.BlockSpec(memory_space=pltpu.SMEM),
        ],
        out_specs=pl.BlockSpec((1, ATT_BLOCK, w), cur),
        compiler_params=_cparams(("arbitrary", "arbitrary")),
        name="attention",
    )(qa, ka, ka, ka, va, va, va, kmeta, vmeta, bias, sink)


def _outproj_kernel(or_ref, oa_ref, x_ref, w_ref, nw_ref, h_ref, xn_ref):
    half = or_ref.shape[-1]
    h = x_ref[0]
    h = h + jnp.dot(or_ref[0], w_ref[:half], preferred_element_type=F32)
    h = h + jnp.dot(oa_ref[0], w_ref[half:], preferred_element_type=F32)
    h_ref[0] = h
    ms = jnp.mean(h * h, axis=-1, keepdims=True)
    xn_ref[0] = h * lax.rsqrt(ms + RMS_EPS) * nw_ref[...]


def _outproj(o_r, o_a, x, w_out_bf, norm_w, tm):
    b, s, d = x.shape
    half = o_r.shape[-1]
    row = lambda i, j: (i, j, 0)
    const2 = lambda i, j: (0, 0)
    return pl.pallas_call(
        _outproj_kernel,
        out_shape=[jax.ShapeDtypeStruct((b, s, d), F32)] * 2,
        grid=(b, s // tm),
        in_specs=[
            pl.BlockSpec((1, tm, half), row),
            pl.BlockSpec((1, tm, half), row),
            pl.BlockSpec((1, tm, d), row),
            pl.BlockSpec(w_out_bf.shape, const2),
            pl.BlockSpec((1, d), const2),
        ],
        out_specs=[pl.BlockSpec((1, tm, d), row)] * 2,
        compiler_params=_cparams(("arbitrary", "arbitrary")),
        name="outproj",
    )(o_r, o_a, x, w_out_bf, norm_w)


def _top16_rows(s, iota):
    nrows = s.shape[0]
    vals, ids = [], []
    for _ in range(PEER_TOPK):
        m = jnp.max(s, axis=0, keepdims=True)
        am = jnp.min(jnp.where(s == m, iota, nrows), axis=0, keepdims=True)
        vals.append(m)
        ids.append(am)
        s = jnp.where(iota == am, -jnp.inf, s)
    return jnp.concatenate(vals, axis=0), jnp.concatenate(ids, axis=0)


_PAIR_GROUPS = ((0, 0, 8), (0, 8, 8), (1, 0, 8), (2, 0, 5), (3, 0, 4), (4, 0, 3), (5, 0, 2), (6, 0, 2), (7, 0, 2))


def _route_kernel(x_ref, wq_ref, pk_ref, idx_ref, gate_ref, sc_ref):
    tm = x_ref.shape[1]
    q = jnp.dot(x_ref[0].astype(BF16), wq_ref[...], preferred_element_type=F32).astype(BF16)
    contract1 = (((1,), (1,)), ((), ()))
    for hp in range(2 * PEER_HEADS):
        sc_ref[hp] = lax.dot_general(pk_ref[hp], q[:, hp * PEER_NKEYS:(hp + 1) * PEER_NKEYS], contract1,
                                     preferred_element_type=F32)
    iota = lax.broadcasted_iota(jnp.int32, (PEER_NKEYS, LANES), 0)
    sub = lax.broadcasted_iota(jnp.int32, (SUBLANES, LANES), 0)
    far = PEER_TOPK * PEER_TOPK

    def head_body(h, carry):
        for lt in range(tm // LANES):
            lanes = slice(lt * LANES, (lt + 1) * LANES)
            s1, i1 = _top16_rows(sc_ref[2 * h, :, lanes], iota)
            s2, i2 = _top16_rows(sc_ref[2 * h + 1, :, lanes], iota)
            i1 = i1 * PEER_NKEYS
            sc, ex, pos = [], [], []
            for i, j0, cnt in _PAIR_GROUPS:
                ok = sub < cnt
                sc.append(jnp.where(ok, s1[i:i + 1] + s2[j0:j0 + SUBLANES], -jnp.inf))
                ex.append(i1[i:i + 1] + i2[j0:j0 + SUBLANES])
                pos.append(jnp.where(ok, i * PEER_TOPK + j0 + sub, far))
            sc.append(s1[SUBLANES:] + s2[0:1])
            ex.append(i1[SUBLANES:] + i2[0:1])
            pos.append((SUBLANES + sub) * PEER_TOPK)
            sc = jnp.concatenate(sc, axis=0)
            ex = jnp.concatenate(ex, axis=0)
            pos = jnp.concatenate(pos, axis=0)
            top, experts = [], []
            for _ in range(PEER_TOPK):
                m = jnp.max(sc, axis=0, keepdims=True)
                pm = jnp.min(jnp.where(sc == m, pos, far), axis=0, keepdims=True)
                hit = pos == pm
                experts.append(jnp.max(jnp.where(hit, ex, -1), axis=0, keepdims=True))
                top.append(m)
                sc = jnp.where(hit, -jnp.inf, sc)
            top = jnp.concatenate(top, axis=0)
            e = jnp.exp(top - top[0:1])
            rows = pl.ds(pl.multiple_of(h * PEER_TOPK, PEER_TOPK), PEER_TOPK)
            gate_ref[0, rows, lanes] = e / jnp.sum(e, axis=0, keepdims=True)
            idx_ref[0, rows, lanes] = jnp.concatenate(experts, axis=0)
        return carry

    lax.fori_loop(0, PEER_HEADS, head_body, 0)


def _route(xn, wq_bf, pk_bf, tm):
    b, s, d = xn.shape
    const2 = lambda i, j: (0, 0)
    slot = lambda i, j: (i, 0, j)
    return pl.pallas_call(
        _route_kernel,
        out_shape=[jax.ShapeDtypeStruct((b, PEER_SLOTS, s), jnp.int32),
                   jax.ShapeDtypeStruct((b, PEER_SLOTS, s), F32)],
        grid=(b, s // tm),
        in_specs=[
            pl.BlockSpec((1, tm, d), lambda i, j: (i, j, 0)),
            pl.BlockSpec(wq_bf.shape, const2),
            pl.BlockSpec(pk_bf.shape, lambda i, j: (0, 0, 0)),
        ],
        out_specs=[pl.BlockSpec((1, PEER_SLOTS, tm), slot)] * 2,
        scratch_shapes=[pltpu.VMEM((2 * PEER_HEADS, PEER_NKEYS, tm), F32)],
        compiler_params=_cparams(("arbitrary", "arbitrary")),
        name="route",
    )(xn, wq_bf, pk_bf)


def _bitrev4(n):
    return ((n & 1) << 3) | ((n & 2) << 1) | ((n & 4) >> 1) | ((n & 8) >> 3)


def _pair_slots():
    out = []
    for n in range(PEER_SLOTS // SUBLANES):
        g = _bitrev4(n)
        for c in range(4):
            top = SUBLANES * g + SLOT_ORDER[c]
            out.append((top, top + 4))
    return out


def _gather_pair(tbl_ref, top_off, bot_off, upper):
    ra = tbl_ref[pl.ds(pl.multiple_of(top_off, ROWS_PER_EXPERT), SUBLANES), :]
    rb = tbl_ref[pl.ds(pl.multiple_of(bot_off, ROWS_PER_EXPERT), SUBLANES), :]
    row = jnp.where(upper, ra, rb)
    lo = lax.bitcast_convert_type(row << 16, F32)
    hi = lax.bitcast_convert_type(row & jnp.uint32(0xFFFF0000), F32)
    return lo, hi


def _merge_sublanes(a, b, h, sub):
    keep = (sub & h) == 0
    return jnp.where(keep, a + pltpu.roll(a, SUBLANES - h, 0), b + pltpu.roll(b, h, 0))


def _merge_lanes(a, b, h, lane):
    keep = (lane & h) == 0
    return jnp.where(keep, a + pltpu.roll(a, LANES - h, 1), b + pltpu.roll(b, h, 1))


def _slot_specs(tb):
    return [pl.BlockSpec((None, None, SLOT_GROUP * tb), lambda i, j=j: (j, 0, i), memory_space=pltpu.SMEM,
                         pipeline_mode=pl.Buffered(1)) for j in range(PEER_SLOTS // SLOT_GROUP)]


def _group_slots(a, tb):
    n = a.shape[-1]
    a = a.reshape(PEER_SLOTS // SLOT_GROUP, SLOT_GROUP, n // tb, tb).transpose(0, 2, 1, 3)
    return a.reshape(PEER_SLOTS // SLOT_GROUP, 1, n * SLOT_GROUP)


def _slot_reader(refs, t, tb):
    at = [t + g * tb for g in range(SLOT_GROUP)]
    return lambda k: refs[k // SLOT_GROUP][at[k % SLOT_GROUP]]


def _peer_u_kernel(*refs):
    ngroups = PEER_SLOTS // SLOT_GROUP
    offs = refs[:ngroups]
    x_ref, tbl_ref, o_ref = refs[ngroups:]
    tb = x_ref.shape[0] // SUBLANES
    sub = lax.broadcasted_iota(jnp.int32, (SUBLANES, LANES), 0)
    lane = lax.broadcasted_iota(jnp.int32, (SUBLANES, LANES), 1)
    upper = sub < ROWS_PER_EXPERT
    pairs = _pair_slots()

    def group_body(gi, carry):
        ys = []
        for j in range(SUBLANES):
            t = gi * SUBLANES + j
            x = x_ref[pl.ds(pl.multiple_of(t * SUBLANES, SUBLANES), SUBLANES), :]
            swapped = pltpu.roll(x, ROWS_PER_EXPERT, 0)
            xa = jnp.where(upper, x, swapped)
            xb = jnp.where(upper, swapped, x)
            off = _slot_reader(offs, t, tb)
            zs = []
            for n in range(PEER_SLOTS // SUBLANES):
                ps = []
                for c in range(4):
                    top, bot = pairs[4 * n + c]
                    lo, hi = _gather_pair(tbl_ref, off(top), off(bot), upper)
                    ps.append(lo * xa + hi * xb)
                r0 = _merge_sublanes(ps[0], ps[1], 2, sub)
                r1 = _merge_sublanes(ps[2], ps[3], 2, sub)
                zs.append(_merge_sublanes(r0, r1, 1, sub))
            for h in (64, 32, 16, 8):
                zs = [_merge_lanes(zs[2 * i], zs[2 * i + 1], h, lane) for i in range(len(zs) // 2)]
            ys.append(zs[0])
        w4 = [_merge_lanes(ys[j], ys[j + 4], 4, lane) for j in range(4)]
        w2 = [_merge_lanes(w4[j], w4[j + 2], 2, lane) for j in range(2)]
        o_ref[gi] = _merge_lanes(w2[0], w2[1], 1, lane)
        return carry

    lax.fori_loop(0, o_ref.shape[0], group_body, 0)


def _peer_u(offs_g, x_rows, tbl, tb):
    n = x_rows.shape[0] // SUBLANES
    return pl.pallas_call(
        _peer_u_kernel,
        out_shape=jax.ShapeDtypeStruct((n // SUBLANES, SUBLANES, LANES), F32),
        grid=(n // tb,),
        in_specs=_slot_specs(tb) + [
            pl.BlockSpec((tb * SUBLANES, LANES), lambda i: (i, 0)),
            pl.BlockSpec(tbl.shape, lambda i: (0, 0), pipeline_mode=pl.Buffered(1)),
        ],
        out_specs=pl.BlockSpec((tb // SUBLANES, SUBLANES, LANES), lambda i: (i, 0, 0)),
        compiler_params=_cparams(("arbitrary",)),
        name="peer_u",
    )(*([offs_g] * (PEER_SLOTS // SLOT_GROUP)), x_rows, tbl)


def _peer_w_kernel(gate_ref, act_ref, w_ref):
    a = act_ref[...]
    w_ref[...] = gate_ref[...] * (0.5 * a * (1.0 + lax.erf(a * (2.0 ** -0.5))))


def _peer_w(gate_t, act_t, tm):
    n = gate_t.shape[-1]
    spec = pl.BlockSpec((PEER_SLOTS, tm), lambda i: (0, i))
    return pl.pallas_call(
        _peer_w_kernel,
        out_shape=jax.ShapeDtypeStruct(gate_t.shape, F32),
        grid=(n // tm,),
        in_specs=[spec, spec],
        out_specs=spec,
        compiler_params=_cparams(("arbitrary",)),
        name="peer_w",
    )(gate_t, act_t)


def _peer_v_kernel(*refs):
    ngroups = PEER_SLOTS // SLOT_GROUP
    offs = refs[:ngroups]
    wts = refs[ngroups:2 * ngroups]
    h_ref, tbl_ref, nw_ref, o_ref = refs[2 * ngroups:]
    tb = h_ref.shape[0] // SUBLANES
    sub = lax.broadcasted_iota(jnp.int32, (SUBLANES, LANES), 0)
    upper = sub < ROWS_PER_EXPERT
    pairs = _pair_slots()

    def token_body(t, carry):
        off = _slot_reader(offs, t, tb)
        wt = _slot_reader(wts, t, tb)
        acc = [jnp.zeros((SUBLANES, LANES), F32) for _ in range(4)]
        for q, (top, bot) in enumerate(pairs):
            lo, hi = _gather_pair(tbl_ref, off(top), off(bot), upper)
            w = jnp.where(upper, wt(top), wt(bot))
            k = 2 * (q % 2)
            acc[k] = acc[k] + lo * w
            acc[k + 1] = acc[k + 1] + hi * w
        lo = acc[0] + acc[2]
        hi = acc[1] + acc[3]
        lo = lo + pltpu.roll(lo, ROWS_PER_EXPERT, 0)
        hi = hi + pltpu.roll(hi, ROWS_PER_EXPERT, 0)
        rows = pl.ds(pl.multiple_of(t * SUBLANES, SUBLANES), SUBLANES)
        h = h_ref[rows, :] + jnp.where(upper, lo, hi)
        ms = jnp.sum(jnp.sum(h * h, axis=1, keepdims=True), axis=0, keepdims=True) * (1.0 / D_MODEL)
        o_ref[rows, :] = h * lax.rsqrt(ms + RMS_EPS) * nw_ref[...]
        return carry

    lax.fori_loop(0, h_ref.shape[0] // SUBLANES, token_body, 0)


def _peer_v(offs_g, w_g, h_rows, tbl, norm_rows, tb):
    n = h_rows.shape[0] // SUBLANES
    rows = pl.BlockSpec((tb * SUBLANES, LANES), lambda i: (i, 0))
    return pl.pallas_call(
        _peer_v_kernel,
        out_shape=jax.ShapeDtypeStruct(h_rows.shape, F32),
        grid=(n // tb,),
        in_specs=_slot_specs(tb) + _slot_specs(tb) + [
            rows,
            pl.BlockSpec(tbl.shape, lambda i: (0, 0), pipeline_mode=pl.Buffered(1)),
            pl.BlockSpec((SUBLANES, LANES), lambda i: (0, 0)),
        ],
        out_specs=rows,
        compiler_params=_cparams(("arbitrary",)),
        name="peer_v",
    )(*([offs_g] * (PEER_SLOTS // SLOT_GROUP)), *([w_g] * (PEER_SLOTS // SLOT_GROUP)), h_rows, tbl, norm_rows)


def _pack_table(t):
    half = D_MODEL // 2
    bits = lax.bitcast_convert_type(t.astype(BF16), jnp.uint16).astype(jnp.uint32)
    words = bits[:, :half] | (bits[:, half:] << 16)
    words = words.reshape(t.shape[0] * ROWS_PER_EXPERT, LANES)
    return jnp.pad(words, ((TABLE_PAD, TABLE_PAD), (0, 0)))


def _t5_bucket(rel):
    half = N_BUCKETS // 2
    exact = half // 2
    n = jnp.abs(rel)
    large = exact + (jnp.log(jnp.maximum(n, 1).astype(F32) / exact)
                     / math.log(MAX_DISTANCE / exact) * (half - exact)).astype(jnp.int32)
    large = jnp.minimum(large, half - 1)
    return jnp.where(rel > 0, half, 0) + jnp.where(n < exact, n, large)


def _attention_bias(rel_bias):
    rb = rel_bias.astype(F32)
    i = jnp.arange(ATT_BLOCK)
    jb = jnp.arange(3 * ATT_BLOCK)
    rel = (jb[None, :] - ATT_BLOCK) - i[:, None]
    band = jnp.where((jnp.abs(rel) <= ATT_BLOCK)[..., None], rb[_t5_bucket(rel)], NEG)
    mpos = PREFIX - N_META + jnp.arange(N_META)
    variants = []
    for blk in (0, 1):
        qpos = PREFIX + blk * ATT_BLOCK + i
        meta = rb[_t5_bucket(mpos[None, :] - qpos[:, None])]
        fill = jnp.full((ATT_BLOCK, ATT_BLOCK - N_META, ATT_HEADS), NEG, F32)
        variants.append(jnp.concatenate([band, meta, fill], axis=1).transpose(2, 0, 1))
    return jnp.stack(variants)


def _rope_tables(pos):
    half = RET_D // 2
    inv = ROPE_BASE ** (-jnp.arange(half, dtype=F32) / half)
    ang = pos.astype(F32)[:, None] * inv[None, :]
    cos, sin = jnp.cos(ang), jnp.sin(ang)
    return jnp.concatenate([cos, cos], axis=1), jnp.concatenate([-sin, sin], axis=1)


def _decay_tables(dec_f, dec_b, chunk):
    lf = jax.nn.log_sigmoid(dec_f.astype(F32))[:, None]
    lb = jax.nn.log_sigmoid(dec_b.astype(F32))[:, None]
    idx = jnp.arange(chunk, dtype=F32)
    diff = idx[:, None] - idx[None, :]
    dmat = (jnp.where(diff >= 0, jnp.exp(jnp.maximum(diff, 0.0)[None] * lf[:, :, None]), 0.0)
            + jnp.where(diff < 0, jnp.exp(jnp.maximum(-diff, 0.0)[None] * lb[:, :, None]), 0.0))
    bc = lambda v: jnp.broadcast_to(v[:, :, None], v.shape + (RET_D,))
    qwf = bc(jnp.exp((idx + 1.0)[None] * lf))
    kwf = bc(jnp.exp((chunk - 1.0 - idx)[None] * lf))
    qwb = bc(jnp.exp((chunk - idx)[None] * lb))
    kwb = bc(jnp.exp(idx[None] * lb))
    pidx = jnp.arange(PREFIX, dtype=F32)
    kwp = bc(jnp.exp((PREFIX - 1.0 - pidx)[None] * lf))
    gdec = jnp.concatenate([jnp.exp(chunk * lf[:, 0]), jnp.exp(chunk * lb[:, 0])])
    return dmat, qwf, qwb, kwf, kwb, kwp, gdec


def _divisor_tile(n, want):
    t = min(n, want)
    while n % t:
        t //= 2
    return t


def _encode(x, shared):
    b, s, d = x.shape
    n = b * s
    tm = _divisor_tile(s, 512)
    chunk = _divisor_tile(s, 256)
    cos_t, sin_t = _rope_tables(jnp.arange(s) + N_META)
    qr, kr, vr, gr, qa, ka, va = _proj(x, shared["norm_mix"], shared["w_in"], cos_t, sin_t, tm)
    o_r = _retention(qr, kr, vr, gr, shared["kr_pre"], shared["vr_pre"], *_decay_tables(
        shared["dec_f"], shared["dec_b"], chunk), shared["gn_w"], chunk)
    o_a = _attention(qa, ka, va, shared["k_meta"], shared["v_meta"], shared["bias"], shared["sink"])
    h1, xn2 = _outproj(o_r, o_a, x, shared["w_out"], shared["norm_ffn"], tm)
    idx_t, gate_t = _route(xn2, shared["wq"], shared["pkeys"], _divisor_tile(s, ROUTE_TOKENS))
    idx_t = idx_t.transpose(1, 0, 2).reshape(PEER_SLOTS, n)
    gate_t = gate_t.transpose(1, 0, 2).reshape(PEER_SLOTS, n)
    top_half = (jnp.arange(PEER_SLOTS) % SUBLANES) < ROWS_PER_EXPERT
    tb = _divisor_tile(n, PEER_TOKENS)
    offs_g = _group_slots(idx_t * ROWS_PER_EXPERT + jnp.where(top_half, TABLE_PAD, 0)[:, None], tb)
    act = _peer_u(offs_g, xn2.reshape(n * SUBLANES, LANES), shared["u_tbl"], tb)
    act = act.reshape(n // SUBLANES, SUBLANES, PEER_SLOTS // SUBLANES, SUBLANES)
    act_t = act.transpose(2, 1, 0, 3).reshape(PEER_SLOTS, n)
    w_g = _group_slots(_peer_w(gate_t, act_t, _divisor_tile(n, 2048)), tb)
    y = _peer_v(offs_g, w_g, h1.reshape(n * SUBLANES, LANES), shared["v_tbl"], shared["norm_final"], tb)
    return y.reshape(b, s, d)


def kernel(x_prompt, x_sample, meta_tokens, norm_mix_w, w_in, ret_decay_fwd, ret_decay_bwd, ret_gn_w,
           attn_sink, rel_bias, w_out, norm_ffn_w, peer_wq, peer_keys, peer_u, peer_v, norm_final_w):
    layer = 0
    shared = {
        "norm_mix": norm_mix_w[layer][None, :].astype(F32),
        "w_in": w_in[layer].astype(BF16),
        "dec_f": ret_decay_fwd[layer],
        "dec_b": ret_decay_bwd[layer],
        "gn_w": ret_gn_w[layer][None, :].astype(F32),
        "sink": attn_sink[layer].astype(F32),
        "bias": _attention_bias(rel_bias),
        "w_out": w_out[layer].astype(BF16),
        "norm_ffn": norm_ffn_w[layer][None, :].astype(F32),
        "wq": peer_wq[layer].astype(BF16),
        "pkeys": peer_keys[layer].reshape(2 * PEER_HEADS, PEER_NKEYS, PEER_NKEYS).astype(BF16),
        "u_tbl": _pack_table(peer_u[layer]),
        "v_tbl": _pack_table(peer_v[layer]),
        "norm_final": norm_final_w.reshape(SUBLANES, LANES).astype(F32),
    }
    prefix = jnp.concatenate([jnp.zeros((PREFIX - N_META, D_MODEL), x_prompt.dtype),
                              meta_tokens.astype(x_prompt.dtype)], axis=0)[None]
    cos_p, sin_p = _rope_tables(jnp.arange(PREFIX) - (PREFIX - N_META))
    _, kr_p, vr_p, _, _, ka_p, va_p = _proj(prefix, shared["norm_mix"], shared["w_in"], cos_p, sin_p, PREFIX)
    shared["kr_pre"], shared["vr_pre"] = kr_p[0], vr_p[0]
    pad_meta = lambda t: jnp.pad(t[0, PREFIX - N_META:], ((0, ATT_BLOCK - N_META), (0, 0)))
    shared["k_meta"], shared["v_meta"] = pad_meta(ka_p), pad_meta(va_p)
    return (_encode(x_prompt, shared), _encode(x_sample, shared))
```

```python
import functools
import math

import jax
import jax.numpy as jnp
from jax import lax
from jax.experimental import pallas as pl
from jax.experimental.pallas import tpu as pltpu

F32 = jnp.float32
BF16 = jnp.bfloat16

D_MODEL = 1024
N_META = 16
PREFIX = 128
RET_HEADS = 4
RET_D = 128
ATT_HEADS = 8
ATT_KV = 2
ATT_GROUP = ATT_HEADS // ATT_KV
ATT_HD = 64
ATT_BLOCK = 128
N_BUCKETS = 32
MAX_DISTANCE = 128
ROPE_BASE = 10000.0
PEER_HEADS = 8
PEER_NKEYS = 128
PEER_EXPERTS = PEER_NKEYS * PEER_NKEYS
PEER_TOPK = 16
PEER_SLOTS = PEER_HEADS * PEER_TOPK
RMS_EPS = 1e-6
GN_EPS = 1e-5
NEG = -1e30
IN_COLS = (0, 512, 1024, 1536, 2048, 2560, 2688, 2816)

LANES = 128
SUBLANES = 8
VMEM_LIMIT = 56 * 1024 * 1024
ROWS_PER_EXPERT = D_MODEL // 2 // LANES
TABLE_PAD = ROWS_PER_EXPERT
PEER_TOKENS = 256
ROUTE_TOKENS = 256
SLOT_ORDER = (0, 2, 1, 3)
SLOT_GROUP = 8


def _cparams(sem):
    return pltpu.CompilerParams(dimension_semantics=sem, vmem_limit_bytes=VMEM_LIMIT)


def _proj_kernel(x_ref, nw_ref, w_ref, cos_ref, sin_ref,
                 qr_ref, kr_ref, vr_ref, gr_ref, qa_ref, ka_ref, va_ref):
    x = x_ref[0]
    ms = jnp.mean(x * x, axis=-1, keepdims=True)
    xn = (x * lax.rsqrt(ms + RMS_EPS) * nw_ref[...]).astype(BF16)
    cosf = cos_ref[...]
    sinf = sin_ref[...]

    def mm(i):
        return jnp.dot(xn, w_ref[:, IN_COLS[i]:IN_COLS[i + 1]], preferred_element_type=F32)

    def rotary(t, scale):
        for h in range(RET_HEADS):
            th = t[:, h * RET_D:(h + 1) * RET_D]
            yield h, (th * cosf + pltpu.roll(th, RET_D // 2, 1) * sinf) * scale

    for h, r in rotary(mm(0), 1.0):
        qr_ref[0, :, h * RET_D:(h + 1) * RET_D] = r.astype(BF16)
    for h, r in rotary(mm(1), RET_D ** -0.5):
        kr_ref[0, :, h * RET_D:(h + 1) * RET_D] = r.astype(BF16)
    vr_ref[0] = mm(2).astype(BF16)
    gr_ref[0] = mm(3)
    qa_ref[0] = (mm(4) * (ATT_HD ** -0.5)).astype(BF16)
    ka_ref[0] = mm(5).astype(BF16)
    va_ref[0] = mm(6).astype(BF16)


def _proj(x, norm_w, w_in_bf, cos_t, sin_t, tm):
    b, s, d = x.shape
    widths = (512, 512, 512, 512, 512, 128, 128)
    dtypes = (BF16, BF16, BF16, F32, BF16, BF16, BF16)
    return pl.pallas_call(
        _proj_kernel,
        out_shape=[jax.ShapeDtypeStruct((b, s, w), dt) for w, dt in zip(widths, dtypes)],
        grid=(b, s // tm),
        in_specs=[
            pl.BlockSpec((1, tm, d), lambda i, j: (i, j, 0)),
            pl.BlockSpec((1, d), lambda i, j: (0, 0)),
            pl.BlockSpec(w_in_bf.shape, lambda i, j: (0, 0)),
            pl.BlockSpec((tm, LANES), lambda i, j: (j, 0)),
            pl.BlockSpec((tm, LANES), lambda i, j: (j, 0)),
        ],
        out_specs=[pl.BlockSpec((1, tm, w), lambda i, j: (i, j, 0)) for w in widths],
        compiler_params=_cparams(("arbitrary", "arbitrary")),
        name="proj",
    )(x, norm_w, w_in_bf, cos_t, sin_t)


def _ret_kernel(q_ref, k_ref, v_ref, g_ref, kpre_ref, vpre_ref, dmat_ref, qwf_ref, qwb_ref,
                kwf_ref, kwb_ref, kwp_ref, gdec_ref, gnw_ref, o_ref, sf_ref, sb_ref, sball_ref):
    ph = pl.program_id(1)
    c = pl.program_id(2)
    nc = pl.num_programs(2)
    contract0 = (((0,), (0,)), ((), ()))
    contract1 = (((1,), (1,)), ((), ()))

    def head(ref, h):
        return ref[0, :, h * RET_D:(h + 1) * RET_D]

    def kv_update(kh, vh, kw):
        kw = (kh.astype(F32) * kw).astype(BF16)
        return lax.dot_general(kw, vh, contract0, preferred_element_type=F32)

    @pl.when(ph == 0)
    def _backward_states():
        @pl.when(c == 0)
        def _():
            sb_ref[...] = jnp.zeros_like(sb_ref)

        j = nc - 1 - c
        for h in range(RET_HEADS):
            sball_ref[j, h] = sb_ref[h].astype(BF16)
            sb_ref[h] = gdec_ref[RET_HEADS + h] * sb_ref[h] + kv_update(head(k_ref, h), head(v_ref, h), kwb_ref[h])

    @pl.when(ph == 1)
    def _outputs():
        @pl.when(c == 0)
        def _():
            for h in range(RET_HEADS):
                kp = kpre_ref[:, h * RET_D:(h + 1) * RET_D]
                vp = vpre_ref[:, h * RET_D:(h + 1) * RET_D]
                sf_ref[h] = kv_update(kp, vp, kwp_ref[h])

        for h in range(RET_HEADS):
            qh, kh, vh = head(q_ref, h), head(k_ref, h), head(v_ref, h)
            s = lax.dot_general(qh, kh, contract1, preferred_element_type=F32) * dmat_ref[h]
            o = jnp.dot(s.astype(BF16), vh, preferred_element_type=F32)
            o = o + jnp.dot(qh, sf_ref[h].astype(BF16), preferred_element_type=F32) * qwf_ref[h]
            o = o + jnp.dot(qh, sball_ref[c, h], preferred_element_type=F32) * qwb_ref[h]
            mu = jnp.mean(o, axis=-1, keepdims=True)
            oc = o - mu
            var = jnp.mean(oc * oc, axis=-1, keepdims=True)
            on = oc * lax.rsqrt(var + GN_EPS) * gnw_ref[:, h * RET_D:(h + 1) * RET_D]
            g = head(g_ref, h)
            o_ref[0, :, h * RET_D:(h + 1) * RET_D] = (g * jax.nn.sigmoid(g) * on).astype(BF16)
            sf_ref[h] = gdec_ref[h] * sf_ref[h] + kv_update(kh, vh, kwf_ref[h])


def _retention(qr, kr, vr, gr, kpre, vpre, dmat, qwf, qwb, kwf, kwb, kwp, gdec, gnw, chunk):
    b, s, w = qr.shape
    nc = s // chunk

    def seq_map(i, ph, c):
        return (i, jnp.where(ph == 0, nc - 1 - c, c), 0)

    def fwd_only(i, ph, c):
        return (i, c * ph, 0)

    const2 = lambda i, ph, c: (0, 0)
    const3 = lambda i, ph, c: (0, 0, 0)
    return pl.pallas_call(
        _ret_kernel,
        out_shape=jax.ShapeDtypeStruct((b, s, w), BF16),
        grid=(b, 2, nc),
        in_specs=[
            pl.BlockSpec((1, chunk, w), fwd_only),
            pl.BlockSpec((1, chunk, w), seq_map),
            pl.BlockSpec((1, chunk, w), seq_map),
            pl.BlockSpec((1, chunk, w), fwd_only),
            pl.BlockSpec(kpre.shape, const2),
            pl.BlockSpec(vpre.shape, const2),
            pl.BlockSpec(dmat.shape, const3),
            pl.BlockSpec(qwf.shape, const3),
            pl.BlockSpec(qwb.shape, const3),
            pl.BlockSpec(kwf.shape, const3),
            pl.BlockSpec(kwb.shape, const3),
            pl.BlockSpec(kwp.shape, const3),
            pl.BlockSpec(memory_space=pltpu.SMEM),
            pl.BlockSpec(gnw.shape, const2),
        ],
        out_specs=pl.BlockSpec((1, chunk, w), fwd_only),
        scratch_shapes=[
            pltpu.VMEM((RET_HEADS, RET_D, RET_D), F32),
            pltpu.VMEM((RET_HEADS, RET_D, RET_D), F32),
            pltpu.VMEM((nc, RET_HEADS, RET_D, RET_D), BF16),
        ],
        compiler_params=_cparams(("arbitrary", "arbitrary", "arbitrary")),
        name="retention",
    )(qr, kr, vr, gr, kpre, vpre, dmat, qwf, qwb, kwf, kwb, kwp, gdec, gnw)


def _attn_kernel(q_ref, kp_ref, kc_ref, kn_ref, vp_ref, vc_ref, vn_ref, km_ref, vm_ref,
                 bias_ref, sink_ref, o_ref):
    j = pl.program_id(1)
    nb = pl.num_programs(1)
    lane = lax.broadcasted_iota(jnp.int32, (1, 4 * ATT_BLOCK), 1)
    first = jnp.logical_and(lane < ATT_BLOCK, j == 0)
    last = jnp.logical_and(jnp.logical_and(lane >= 2 * ATT_BLOCK, lane < 3 * ATT_BLOCK), j == nb - 1)
    pen = jnp.where(jnp.logical_or(first, last), NEG, 0.0).astype(F32)
    contract1 = (((1,), (1,)), ((), ()))
    outs = []
    for g in range(ATT_KV):
        sl = slice(g * ATT_HD, (g + 1) * ATT_HD)
        kall = jnp.concatenate([kp_ref[0][:, sl], kc_ref[0][:, sl], kn_ref[0][:, sl], km_ref[:, sl]], axis=0)
        vall = jnp.concatenate([vp_ref[0][:, sl], vc_ref[0][:, sl], vn_ref[0][:, sl], vm_ref[:, sl]], axis=0)
        for hh in range(ATT_GROUP):
            h = g * ATT_GROUP + hh
            q = q_ref[0, :, h * ATT_HD:(h + 1) * ATT_HD]
            s = lax.dot_general(q, kall, contract1, preferred_element_type=F32)
            s = s + bias_ref[0, h] + pen
            snk = sink_ref[h]
            m = jnp.maximum(jnp.max(s, axis=-1, keepdims=True), snk)
            p = jnp.exp(s - m)
            den = jnp.sum(p, axis=-1, keepdims=True) + jnp.exp(snk - m)
            o = jnp.dot(p.astype(BF16), vall, preferred_element_type=F32)
            outs.append(o / den)
    for h2 in range(ATT_HEADS // 2):
        o_ref[0, :, h2 * LANES:(h2 + 1) * LANES] = jnp.concatenate(
            [outs[2 * h2], outs[2 * h2 + 1]], axis=-1).astype(BF16)


def _attention(qa, ka, va, kmeta, vmeta, bias, sink):
    b, s, w = qa.shape
    nb = s // ATT_BLOCK
    kvw = ka.shape[-1]
    prev = lambda i, j: (i, jnp.maximum(j - 1, 0), 0)
    cur = lambda i, j: (i, j, 0)
    nxt = lambda i, j: (i, jnp.minimum(j + 1, nb - 1), 0)
    const2 = lambda i, j: (0, 0)
    kv = lambda m: pl.BlockSpec((1, ATT_BLOCK, kvw), m)
    return pl.pallas_call(
        _attn_kernel,
        out_shape=jax.ShapeDtypeStruct((b, s, w), BF16),
        grid=(b, nb),
        in_specs=[
            pl.BlockSpec((1, ATT_BLOCK, w), cur),
            kv(prev), kv(cur), kv(nxt), kv(prev), kv(cur), kv(nxt),
            pl.BlockSpec(kmeta.shape, const2),
            pl.BlockSpec(vmeta.shape, const2),
            pl.BlockSpec((1,) + bias.shape[1:], lambda i, j: (jnp.minimum(j, 1), 0, 0, 0)),
            pl.BlockSpec(memory_space=pltpu.SMEM),
        ],
        out_specs=pl.BlockSpec((1, ATT_BLOCK, w), cur),
        compiler_params=_cparams(("arbitrary", "arbitrary")),
        name="attention",
    )(qa, ka, ka, ka, va, va, va, kmeta, vmeta, bias, sink)


def _outproj_kernel(or_ref, oa_ref, x_ref, w_ref, nw_ref, h_ref, xn_ref):
    half = or_ref.shape[-1]
    h = x_ref[0]
    h = h + jnp.dot(or_ref[0], w_ref[:half], preferred_element_type=F32)
    h = h + jnp.dot(oa_ref[0], w_ref[half:], preferred_element_type=F32)
    h_ref[0] = h
    ms = jnp.mean(h * h, axis=-1, keepdims=True)
    xn_ref[0] = h * lax.rsqrt(ms + RMS_EPS) * nw_ref[...]


def _outproj(o_r, o_a, x, w_out_bf, norm_w, tm):
    b, s, d = x.shape
    half = o_r.shape[-1]
    row = lambda i, j: (i, j, 0)
    const2 = lambda i, j: (0, 0)
    return pl.pallas_call(
        _outproj_kernel,
        out_shape=[jax.ShapeDtypeStruct((b, s, d), F32)] * 2,
        grid=(b, s // tm),
        in_specs=[
            pl.BlockSpec((1, tm, half), row),
            pl.BlockSpec((1, tm, half), row),
            pl.BlockSpec((1, tm, d), row),
            pl.BlockSpec(w_out_bf.shape, const2),
            pl.BlockSpec((1, d), const2),
        ],
        out_specs=[pl.BlockSpec((1, tm, d), row)] * 2,
        compiler_params=_cparams(("arbitrary", "arbitrary")),
        name="outproj",
    )(o_r, o_a, x, w_out_bf, norm_w)


def _top16_rows(s, iota):
    nrows = s.shape[0]
    vals, ids = [], []
    for _ in range(PEER_TOPK):
        m = jnp.max(s, axis=0, keepdims=True)
        am = jnp.min(jnp.where(s == m, iota, nrows), axis=0, keepdims=True)
        vals.append(m)
        ids.append(am)
        s = jnp.where(iota == am, -jnp.inf, s)
    return jnp.concatenate(vals, axis=0), jnp.concatenate(ids, axis=0)


_PAIR_GROUPS = ((0, 0, 8), (0, 8, 8), (1, 0, 8), (2, 0, 5), (3, 0, 4), (4, 0, 3), (5, 0, 2), (6, 0, 2), (7, 0, 2))


def _route_kernel(x_ref, wq_ref, pk_ref, idx_ref, gate_ref, sc_ref):
    tm = x_ref.shape[1]
    q = jnp.dot(x_ref[0].astype(BF16), wq_ref[...], preferred_element_type=F32).astype(BF16)
    contract1 = (((1,), (1,)), ((), ()))
    for hp in range(2 * PEER_HEADS):
        sc_ref[hp] = lax.dot_general(pk_ref[hp], q[:, hp * PEER_NKEYS:(hp + 1) * PEER_NKEYS], contract1,
                                     preferred_element_type=F32)
    iota = lax.broadcasted_iota(jnp.int32, (PEER_NKEYS, LANES), 0)
    sub = lax.broadcasted_iota(jnp.int32, (SUBLANES, LANES), 0)
    far = PEER_TOPK * PEER_TOPK

    def head_body(h, carry):
        for lt in range(tm // LANES):
            lanes = slice(lt * LANES, (lt + 1) * LANES)
            s1, i1 = _top16_rows(sc_ref[2 * h, :, lanes], iota)
            s2, i2 = _top16_rows(sc_ref[2 * h + 1, :, lanes], iota)
            i1 = i1 * PEER_NKEYS
            sc, ex, pos = [], [], []
            for i, j0, cnt in _PAIR_GROUPS:
                ok = sub < cnt
                sc.append(jnp.where(ok, s1[i:i + 1] + s2[j0:j0 + SUBLANES], -jnp.inf))
                ex.append(i1[i:i + 1] + i2[j0:j0 + SUBLANES])
                pos.append(jnp.where(ok, i * PEER_TOPK + j0 + sub, far))
            sc.append(s1[SUBLANES:] + s2[0:1])
            ex.append(i1[SUBLANES:] + i2[0:1])
            pos.append((SUBLANES + sub) * PEER_TOPK)
            sc = jnp.concatenate(sc, axis=0)
            ex = jnp.concatenate(ex, axis=0)
            pos = jnp.concatenate(pos, axis=0)
            top, experts = [], []
            for _ in range(PEER_TOPK):
                m = jnp.max(sc, axis=0, keepdims=True)
                pm = jnp.min(jnp.where(sc == m, pos, far), axis=0, keepdims=True)
                hit = pos == pm
                experts.append(jnp.max(jnp.where(hit, ex, -1), axis=0, keepdims=True))
                top.append(m)
                sc = jnp.where(hit, -jnp.inf, sc)
            top = jnp.concatenate(top, axis=0)
            e = jnp.exp(top - top[0:1])
            rows = pl.ds(pl.multiple_of(h * PEER_TOPK, PEER_TOPK), PEER_TOPK)
            gate_ref[0, rows, lanes] = e / jnp.sum(e, axis=0, keepdims=True)
            idx_ref[0, rows, lanes] = jnp.concatenate(experts, axis=0)
        return carry

    lax.fori_loop(0, PEER_HEADS, head_body, 0)


def _route(xn, wq_bf, pk_bf, tm):
    b, s, d = xn.shape
    const2 = lambda i, j: (0, 0)
    slot = lambda i, j: (i, 0, j)
    return pl.pallas_call(
        _route_kernel,
        out_shape=[jax.ShapeDtypeStruct((b, PEER_SLOTS, s), jnp.int32),
                   jax.ShapeDtypeStruct((b, PEER_SLOTS, s), F32)],
        grid=(b, s // tm),
        in_specs=[
            pl.BlockSpec((1, tm, d), lambda i, j: (i, j, 0)),
            pl.BlockSpec(wq_bf.shape, const2),
            pl.BlockSpec(pk_bf.shape, lambda i, j: (0, 0, 0)),
        ],
        out_specs=[pl.BlockSpec((1, PEER_SLOTS, tm), slot)] * 2,
        scratch_shapes=[pltpu.VMEM((2 * PEER_HEADS, PEER_NKEYS, tm), F32)],
        compiler_params=_cparams(("arbitrary", "arbitrary")),
        name="route",
    )(xn, wq_bf, pk_bf)


def _bitrev4(n):
    return ((n & 1) << 3) | ((n & 2) << 1) | ((n & 4) >> 1) | ((n & 8) >> 3)


def _pair_slots():
    out = []
    for n in range(PEER_SLOTS // SUBLANES):
        g = _bitrev4(n)
        for c in range(4):
            top = SUBLANES * g + SLOT_ORDER[c]
            out.append((top, top + 4))
    return out


def _gather_pair(tbl_ref, top_off, bot_off, upper):
    ra = tbl_ref[pl.ds(pl.multiple_of(top_off, ROWS_PER_EXPERT), SUBLANES), :]
    rb = tbl_ref[pl.ds(pl.multiple_of(bot_off, ROWS_PER_EXPERT), SUBLANES), :]
    row = jnp.where(upper, ra, rb)
    lo = lax.bitcast_convert_type(row << 16, F32)
    hi = lax.bitcast_convert_type(row & jnp.uint32(0xFFFF0000), F32)
    return lo, hi


def _merge_sublanes(a, b, h, sub):
    keep = (sub & h) == 0
    return jnp.where(keep, a + pltpu.roll(a, SUBLANES - h, 0), b + pltpu.roll(b, h, 0))


def _merge_lanes(a, b, h, lane):
    keep = (lane & h) == 0
    return jnp.where(keep, a + pltpu.roll(a, LANES - h, 1), b + pltpu.roll(b, h, 1))


def _slot_specs(tb):
    return [pl.BlockSpec((None, None, SLOT_GROUP * tb), lambda i, j=j: (j, 0, i), memory_space=pltpu.SMEM,
                         pipeline_mode=pl.Buffered(1)) for j in range(PEER_SLOTS // SLOT_GROUP)]


def _group_slots(a, tb, skew):
    n = a.shape[-1]
    assert tb & (tb - 1) == 0
    a = a.reshape(PEER_SLOTS // SLOT_GROUP, SLOT_GROUP, n // tb, tb)
    a = jnp.stack([jnp.roll(a[:, i], i + skew, axis=-1) for i in range(SLOT_GROUP)], axis=2)
    return a.reshape(PEER_SLOTS // SLOT_GROUP, 1, n * SLOT_GROUP)


def _slot_reader(refs, t, tb, skew):
    at = [g * tb + ((t + g + skew) & (tb - 1)) for g in range(SLOT_GROUP)]
    return lambda k: refs[k // SLOT_GROUP][at[k % SLOT_GROUP]]


def _peer_u_kernel(*refs):
    ngroups = PEER_SLOTS // SLOT_GROUP
    offs = refs[:ngroups]
    x_ref, tbl_ref, o_ref = refs[ngroups:]
    tb = x_ref.shape[0] // SUBLANES
    sub = lax.broadcasted_iota(jnp.int32, (SUBLANES, LANES), 0)
    lane = lax.broadcasted_iota(jnp.int32, (SUBLANES, LANES), 1)
    upper = sub < ROWS_PER_EXPERT
    pairs = _pair_slots()

    def group_body(gi, carry):
        ys = []
        for j in range(SUBLANES):
            t = gi * SUBLANES + j
            x = x_ref[pl.ds(pl.multiple_of(t * SUBLANES, SUBLANES), SUBLANES), :]
            swapped = pltpu.roll(x, ROWS_PER_EXPERT, 0)
            xa = jnp.where(upper, x, swapped)
            xb = jnp.where(upper, swapped, x)
            off = _slot_reader(offs, t, tb, 0)
            zs = []
            for n in range(PEER_SLOTS // SUBLANES):
                ps = []
                for c in range(4):
                    top, bot = pairs[4 * n + c]
                    lo, hi = _gather_pair(tbl_ref, off(top), off(bot), upper)
                    ps.append(lo * xa + hi * xb)
                r0 = _merge_sublanes(ps[0], ps[1], 2, sub)
                r1 = _merge_sublanes(ps[2], ps[3], 2, sub)
                zs.append(_merge_sublanes(r0, r1, 1, sub))
            for h in (64, 32, 16, 8):
                zs = [_merge_lanes(zs[2 * i], zs[2 * i + 1], h, lane) for i in range(len(zs) // 2)]
            ys.append(zs[0])
        w4 = [_merge_lanes(ys[j], ys[j + 4], 4, lane) for j in range(4)]
        w2 = [_merge_lanes(w4[j], w4[j + 2], 2, lane) for j in range(2)]
        o_ref[gi] = _merge_lanes(w2[0], w2[1], 1, lane)
        return carry

    lax.fori_loop(0, o_ref.shape[0], group_body, 0)


def _peer_u(offs_g, x_rows, tbl, tb):
    n = x_rows.shape[0] // SUBLANES
    return pl.pallas_call(
        _peer_u_kernel,
        out_shape=jax.ShapeDtypeStruct((n // SUBLANES, SUBLANES, LANES), F32),
        grid=(n // tb,),
        in_specs=_slot_specs(tb) + [
            pl.BlockSpec((tb * SUBLANES, LANES), lambda i: (i, 0)),
            pl.BlockSpec(tbl.shape, lambda i: (0, 0), pipeline_mode=pl.Buffered(1)),
        ],
        out_specs=pl.BlockSpec((tb // SUBLANES, SUBLANES, LANES), lambda i: (i, 0, 0)),
        compiler_params=_cparams(("arbitrary",)),
        name="peer_u",
    )(*([offs_g] * (PEER_SLOTS // SLOT_GROUP)), x_rows, tbl)


def _peer_w_kernel(gate_ref, act_ref, w_ref):
    a = act_ref[...]
    w_ref[...] = gate_ref[...] * (0.5 * a * (1.0 + lax.erf(a * (2.0 ** -0.5))))


def _peer_w(gate_t, act_t, tm):
    n = gate_t.shape[-1]
    spec = pl.BlockSpec((PEER_SLOTS, tm), lambda i: (0, i))
    return pl.pallas_call(
        _peer_w_kernel,
        out_shape=jax.ShapeDtypeStruct(gate_t.shape, F32),
        grid=(n // tm,),
        in_specs=[spec, spec],
        out_specs=spec,
        compiler_params=_cparams(("arbitrary",)),
        name="peer_w",
    )(gate_t, act_t)


def _peer_v_kernel(*refs):
    ngroups = PEER_SLOTS // SLOT_GROUP
    offs = refs[:ngroups]
    wts = refs[ngroups:2 * ngroups]
    h_ref, tbl_ref, nw_ref, o_ref = refs[2 * ngroups:]
    tb = h_ref.shape[0] // SUBLANES
    sub = lax.broadcasted_iota(jnp.int32, (SUBLANES, LANES), 0)
    upper = sub < ROWS_PER_EXPERT
    pairs = _pair_slots()

    def token_body(t, carry):
        off = _slot_reader(offs, t, tb, 0)
        wt = _slot_reader(wts, t, tb, 1)
        acc = [jnp.zeros((SUBLANES, LANES), F32) for _ in range(4)]
        for q, (top, bot) in enumerate(pairs):
            lo, hi = _gather_pair(tbl_ref, off(top), off(bot), upper)
            w = jnp.where(upper, wt(top), wt(bot))
            k = 2 * (q % 2)
            acc[k] = acc[k] + lo * w
            acc[k + 1] = acc[k + 1] + hi * w
        lo = acc[0] + acc[2]
        hi = acc[1] + acc[3]
        lo = lo + pltpu.roll(lo, ROWS_PER_EXPERT, 0)
        hi = hi + pltpu.roll(hi, ROWS_PER_EXPERT, 0)
        rows = pl.ds(pl.multiple_of(t * SUBLANES, SUBLANES), SUBLANES)
        h = h_ref[rows, :] + jnp.where(upper, lo, hi)
        ms = jnp.sum(jnp.sum(h * h, axis=1, keepdims=True), axis=0, keepdims=True) * (1.0 / D_MODEL)
        o_ref[rows, :] = h * lax.rsqrt(ms + RMS_EPS) * nw_ref[...]
        return carry

    lax.fori_loop(0, h_ref.shape[0] // SUBLANES, token_body, 0)


def _peer_v(offs_g, w_g, h_rows, tbl, norm_rows, tb):
    n = h_rows.shape[0] // SUBLANES
    rows = pl.BlockSpec((tb * SUBLANES, LANES), lambda i: (i, 0))
    return pl.pallas_call(
        _peer_v_kernel,
        out_shape=jax.ShapeDtypeStruct(h_rows.shape, F32),
        grid=(n // tb,),
        in_specs=_slot_specs(tb) + _slot_specs(tb) + [
            rows,
            pl.BlockSpec(tbl.shape, lambda i: (0, 0), pipeline_mode=pl.Buffered(1)),
            pl.BlockSpec((SUBLANES, LANES), lambda i: (0, 0)),
        ],
        out_specs=rows,
        compiler_params=_cparams(("arbitrary",)),
        name="peer_v",
    )(*([offs_g] * (PEER_SLOTS // SLOT_GROUP)), *([w_g] * (PEER_SLOTS // SLOT_GROUP)), h_rows, tbl, norm_rows)


def _pack_table(t):
    half = D_MODEL // 2
    bits = lax.bitcast_convert_type(t.astype(BF16), jnp.uint16).astype(jnp.uint32)
    words = bits[:, :half] | (bits[:, half:] << 16)
    words = words.reshape(t.shape[0] * ROWS_PER_EXPERT, LANES)
    return jnp.pad(words, ((TABLE_PAD, TABLE_PAD), (0, 0)))


def _t5_bucket(rel):
    half = N_BUCKETS // 2
    exact = half // 2
    n = jnp.abs(rel)
    large = exact + (jnp.log(jnp.maximum(n, 1).astype(F32) / exact)
                     / math.log(MAX_DISTANCE / exact) * (half - exact)).astype(jnp.int32)
    large = jnp.minimum(large, half - 1)
    return jnp.where(rel > 0, half, 0) + jnp.where(n < exact, n, large)


def _attention_bias(rel_bias):
    rb = rel_bias.astype(F32)
    i = jnp.arange(ATT_BLOCK)
    jb = jnp.arange(3 * ATT_BLOCK)
    rel = (jb[None, :] - ATT_BLOCK) - i[:, None]
    band = jnp.where((jnp.abs(rel) <= ATT_BLOCK)[..., None], rb[_t5_bucket(rel)], NEG)
    mpos = PREFIX - N_META + jnp.arange(N_META)
    variants = []
    for blk in (0, 1):
        qpos = PREFIX + blk * ATT_BLOCK + i
        meta = rb[_t5_bucket(mpos[None, :] - qpos[:, None])]
        fill = jnp.full((ATT_BLOCK, ATT_BLOCK - N_META, ATT_HEADS), NEG, F32)
        variants.append(jnp.concatenate([band, meta, fill], axis=1).transpose(2, 0, 1))
    return jnp.stack(variants)


def _rope_tables(pos):
    half = RET_D // 2
    inv = ROPE_BASE ** (-jnp.arange(half, dtype=F32) / half)
    ang = pos.astype(F32)[:, None] * inv[None, :]
    cos, sin = jnp.cos(ang), jnp.sin(ang)
    return jnp.concatenate([cos, cos], axis=1), jnp.concatenate([-sin, sin], axis=1)


def _decay_tables(dec_f, dec_b, chunk):
    lf = jax.nn.log_sigmoid(dec_f.astype(F32))[:, None]
    lb = jax.nn.log_sigmoid(dec_b.astype(F32))[:, None]
    idx = jnp.arange(chunk, dtype=F32)
    diff = idx[:, None] - idx[None, :]
    dmat = (jnp.where(diff >= 0, jnp.exp(jnp.maximum(diff, 0.0)[None] * lf[:, :, None]), 0.0)
            + jnp.where(diff < 0, jnp.exp(jnp.maximum(-diff, 0.0)[None] * lb[:, :, None]), 0.0))
    bc = lambda v: jnp.broadcast_to(v[:, :, None], v.shape + (RET_D,))
    qwf = bc(jnp.exp((idx + 1.0)[None] * lf))
    kwf = bc(jnp.exp((chunk - 1.0 - idx)[None] * lf))
    qwb = bc(jnp.exp((chunk - idx)[None] * lb))
    kwb = bc(jnp.exp(idx[None] * lb))
    pidx = jnp.arange(PREFIX, dtype=F32)
    kwp = bc(jnp.exp((PREFIX - 1.0 - pidx)[None] * lf))
    gdec = jnp.concatenate([jnp.exp(chunk * lf[:, 0]), jnp.exp(chunk * lb[:, 0])])
    return dmat, qwf, qwb, kwf, kwb, kwp, gdec


def _divisor_tile(n, want):
    t = min(n, want)
    while n % t:
        t //= 2
    return t


def _encode(x, shared):
    b, s, d = x.shape
    n = b * s
    tm = _divisor_tile(s, 512)
    chunk = _divisor_tile(s, 256)
    cos_t, sin_t = _rope_tables(jnp.arange(s) + N_META)
    qr, kr, vr, gr, qa, ka, va = _proj(x, shared["norm_mix"], shared["w_in"], cos_t, sin_t, tm)
    o_r = _retention(qr, kr, vr, gr, shared["kr_pre"], shared["vr_pre"], *_decay_tables(
        shared["dec_f"], shared["dec_b"], chunk), shared["gn_w"], chunk)
    o_a = _attention(qa, ka, va, shared["k_meta"], shared["v_meta"], shared["bias"], shared["sink"])
    h1, xn2 = _outproj(o_r, o_a, x, shared["w_out"], shared["norm_ffn"], tm)
    idx_t, gate_t = _route(xn2, shared["wq"], shared["pkeys"], _divisor_tile(s, ROUTE_TOKENS))
    idx_t = idx_t.transpose(1, 0, 2).reshape(PEER_SLOTS, n)
    gate_t = gate_t.transpose(1, 0, 2).reshape(PEER_SLOTS, n)
    top_half = (jnp.arange(PEER_SLOTS) % SUBLANES) < ROWS_PER_EXPERT
    tb = _divisor_tile(n, PEER_TOKENS)
    offs_g = _group_slots(idx_t * ROWS_PER_EXPERT + jnp.where(top_half, TABLE_PAD, 0)[:, None], tb, 0)
    act = _peer_u(offs_g, xn2.reshape(n * SUBLANES, LANES), shared["u_tbl"], tb)
    act = act.reshape(n // SUBLANES, SUBLANES, PEER_SLOTS // SUBLANES, SUBLANES)
    act_t = act.transpose(2, 1, 0, 3).reshape(PEER_SLOTS, n)
    w_g = _group_slots(_peer_w(gate_t, act_t, _divisor_tile(n, 2048)), tb, 1)
    y = _peer_v(offs_g, w_g, h1.reshape(n * SUBLANES, LANES), shared["v_tbl"], shared["norm_final"], tb)
    return y.reshape(b, s, d)


def kernel(x_prompt, x_sample, meta_tokens, norm_mix_w, w_in, ret_decay_fwd, ret_decay_bwd, ret_gn_w,
           attn_sink, rel_bias, w_out, norm_ffn_w, peer_wq, peer_keys, peer_u, peer_v, norm_final_w):
    layer = 0
    shared = {
        "norm_mix": norm_mix_w[layer][None, :].astype(F32),
        "w_in": w_in[layer].astype(BF16),
        "dec_f": ret_decay_fwd[layer],
        "dec_b": ret_decay_bwd[layer],
        "gn_w": ret_gn_w[layer][None, :].astype(F32),
        "sink": attn_sink[layer].astype(F32),
        "bias": _attention_bias(rel_bias),
        "w_out": w_out[layer].astype(BF16),
        "norm_ffn": norm_ffn_w[layer][None, :].astype(F32),
        "wq": peer_wq[layer].astype(BF16),
        "pkeys": peer_keys[layer].reshape(2 * PEER_HEADS, PEER_NKEYS, PEER_NKEYS).astype(BF16),
        "u_tbl": _pack_table(peer_u[layer]),
        "v_tbl": _pack_table(peer_v[layer]),
        "norm_final": norm_final_w.reshape(SUBLANES, LANES).astype(F32),
    }
    prefix = jnp.concatenate([jnp.zeros((PREFIX - N_META, D_MODEL), x_prompt.dtype),
                              meta_tokens.astype(x_prompt.dtype)], axis=0)[None]
    cos_p, sin_p = _rope_tables(jnp.arange(PREFIX) - (PREFIX - N_META))
    _, kr_p, vr_p, _, _, ka_p, va_p = _proj(prefix, shared["norm_mix"], shared["w_in"], cos_p, sin_p, PREFIX)
    shared["kr_pre"], shared["vr_pre"] = kr_p[0], vr_p[0]
    pad_meta = lambda t: jnp.pad(t[0, PREFIX - N_META:], ((0, ATT_BLOCK - N_META), (0, 0)))
    shared["k_meta"], shared["v_meta"] = pad_meta(ka_p), pad_meta(va_p)
    return (_encode(x_prompt, shared), _encode(x_sample, shared))
```

```python
import functools
import math

import jax
import jax.numpy as jnp
from jax import lax
from jax.experimental import pallas as pl
from jax.experimental.pallas import tpu as pltpu

F32 = jnp.float32
BF16 = jnp.bfloat16

D_MODEL = 1024
N_META = 16
PREFIX = 128
RET_HEADS = 4
RET_D = 128
ATT_HEADS = 8
ATT_KV = 2
ATT_GROUP = ATT_HEADS // ATT_KV
ATT_HD = 64
ATT_BLOCK = 128
N_BUCKETS = 32
MAX_DISTANCE = 128
ROPE_BASE = 10000.0
PEER_HEADS = 8
PEER_NKEYS = 128
PEER_EXPERTS = PEER_NKEYS * PEER_NKEYS
PEER_TOPK = 16
PEER_SLOTS = PEER_HEADS * PEER_TOPK
RMS_EPS = 1e-6
GN_EPS = 1e-5
NEG = -1e30
IN_COLS = (0, 512, 1024, 1536, 2048, 2560, 2688, 2816)

LANES = 128
SUBLANES = 8
VMEM_LIMIT = 56 * 1024 * 1024
ROWS_PER_EXPERT = D_MODEL // 2 // LANES
TABLE_PAD = ROWS_PER_EXPERT
PEER_TOKENS = 256
ROUTE_TOKENS = 256
SLOT_ORDER = (0, 2, 1, 3)
SLOT_GROUP = 8
U_GROUPS_PER_ITER = 2
V_TOKENS_PER_ITER = 4
NORM_TOKENS = 32


def _cparams(sem):
    return pltpu.CompilerParams(dimension_semantics=sem, vmem_limit_bytes=VMEM_LIMIT)


def _proj_kernel(x_ref, nw_ref, w_ref, cos_ref, sin_ref,
                 qr_ref, kr_ref, vr_ref, gr_ref, qa_ref, ka_ref, va_ref):
    x = x_ref[0]
    ms = jnp.mean(x * x, axis=-1, keepdims=True)
    xn = (x * lax.rsqrt(ms + RMS_EPS) * nw_ref[...]).astype(BF16)
    cosf = cos_ref[...]
    sinf = sin_ref[...]

    def mm(i):
        return jnp.dot(xn, w_ref[:, IN_COLS[i]:IN_COLS[i + 1]], preferred_element_type=F32)

    def rotary(t, scale):
        for h in range(RET_HEADS):
            th = t[:, h * RET_D:(h + 1) * RET_D]
            yield h, (th * cosf + pltpu.roll(th, RET_D // 2, 1) * sinf) * scale

    for h, r in rotary(mm(0), 1.0):
        qr_ref[0, :, h * RET_D:(h + 1) * RET_D] = r.astype(BF16)
    for h, r in rotary(mm(1), RET_D ** -0.5):
        kr_ref[0, :, h * RET_D:(h + 1) * RET_D] = r.astype(BF16)
    vr_ref[0] = mm(2).astype(BF16)
    gr_ref[0] = mm(3)
    qa_ref[0] = (mm(4) * (ATT_HD ** -0.5)).astype(BF16)
    ka_ref[0] = mm(5).astype(BF16)
    va_ref[0] = mm(6).astype(BF16)


def _proj(x, norm_w, w_in_bf, cos_t, sin_t, tm):
    b, s, d = x.shape
    widths = (512, 512, 512, 512, 512, 128, 128)
    dtypes = (BF16, BF16, BF16, F32, BF16, BF16, BF16)
    return pl.pallas_call(
        _proj_kernel,
        out_shape=[jax.ShapeDtypeStruct((b, s, w), dt) for w, dt in zip(widths, dtypes)],
        grid=(b, s // tm),
        in_specs=[
            pl.BlockSpec((1, tm, d), lambda i, j: (i, j, 0)),
            pl.BlockSpec((1, d), lambda i, j: (0, 0)),
            pl.BlockSpec(w_in_bf.shape, lambda i, j: (0, 0)),
            pl.BlockSpec((tm, LANES), lambda i, j: (j, 0)),
            pl.BlockSpec((tm, LANES), lambda i, j: (j, 0)),
        ],
        out_specs=[pl.BlockSpec((1, tm, w), lambda i, j: (i, j, 0)) for w in widths],
        compiler_params=_cparams(("arbitrary", "arbitrary")),
        name="proj",
    )(x, norm_w, w_in_bf, cos_t, sin_t)


def _ret_kernel(q_ref, k_ref, v_ref, g_ref, kpre_ref, vpre_ref, dmat_ref, qwf_ref, qwb_ref,
                kwf_ref, kwb_ref, kwp_ref, gdec_ref, gnw_ref, o_ref, sf_ref, sb_ref, sball_ref):
    ph = pl.program_id(1)
    c = pl.program_id(2)
    nc = pl.num_programs(2)
    contract0 = (((0,), (0,)), ((), ()))
    contract1 = (((1,), (1,)), ((), ()))

    def head(ref, h):
        return ref[0, :, h * RET_D:(h + 1) * RET_D]

    def kv_update(kh, vh, kw):
        kw = (kh.astype(F32) * kw).astype(BF16)
        return lax.dot_general(kw, vh, contract0, preferred_element_type=F32)

    @pl.when(ph == 0)
    def _backward_states():
        @pl.when(c == 0)
        def _():
            sb_ref[...] = jnp.zeros_like(sb_ref)

        j = nc - 1 - c
        for h in range(RET_HEADS):
            sball_ref[j, h] = sb_ref[h].astype(BF16)
            sb_ref[h] = gdec_ref[RET_HEADS + h] * sb_ref[h] + kv_update(head(k_ref, h), head(v_ref, h), kwb_ref[h])

    @pl.when(ph == 1)
    def _outputs():
        @pl.when(c == 0)
        def _():
            for h in range(RET_HEADS):
                kp = kpre_ref[:, h * RET_D:(h + 1) * RET_D]
                vp = vpre_ref[:, h * RET_D:(h + 1) * RET_D]
                sf_ref[h] = kv_update(kp, vp, kwp_ref[h])

        for h in range(RET_HEADS):
            qh, kh, vh = head(q_ref, h), head(k_ref, h), head(v_ref, h)
            s = lax.dot_general(qh, kh, contract1, preferred_element_type=F32) * dmat_ref[h]
            o = jnp.dot(s.astype(BF16), vh, preferred_element_type=F32)
            o = o + jnp.dot(qh, sf_ref[h].astype(BF16), preferred_element_type=F32) * qwf_ref[h]
            o = o + jnp.dot(qh, sball_ref[c, h], preferred_element_type=F32) * qwb_ref[h]
            mu = jnp.mean(o, axis=-1, keepdims=True)
            oc = o - mu
            var = jnp.mean(oc * oc, axis=-1, keepdims=True)
            on = oc * lax.rsqrt(var + GN_EPS) * gnw_ref[:, h * RET_D:(h + 1) * RET_D]
            g = head(g_ref, h)
            o_ref[0, :, h * RET_D:(h + 1) * RET_D] = (g * jax.nn.sigmoid(g) * on).astype(BF16)
            sf_ref[h] = gdec_ref[h] * sf_ref[h] + kv_update(kh, vh, kwf_ref[h])


def _retention(qr, kr, vr, gr, kpre, vpre, dmat, qwf, qwb, kwf, kwb, kwp, gdec, gnw, chunk):
    b, s, w = qr.shape
    nc = s // chunk

    def seq_map(i, ph, c):
        return (i, jnp.where(ph == 0, nc - 1 - c, c), 0)

    def fwd_only(i, ph, c):
        return (i, c * ph, 0)

    const2 = lambda i, ph, c: (0, 0)
    const3 = lambda i, ph, c: (0, 0, 0)
    return pl.pallas_call(
        _ret_kernel,
        out_shape=jax.ShapeDtypeStruct((b, s, w), BF16),
        grid=(b, 2, nc),
        in_specs=[
            pl.BlockSpec((1, chunk, w), fwd_only),
            pl.BlockSpec((1, chunk, w), seq_map),
            pl.BlockSpec((1, chunk, w), seq_map),
            pl.BlockSpec((1, chunk, w), fwd_only),
            pl.BlockSpec(kpre.shape, const2),
            pl.BlockSpec(vpre.shape, const2),
            pl.BlockSpec(dmat.shape, const3),
            pl.BlockSpec(qwf.shape, const3),
            pl.BlockSpec(qwb.shape, const3),
            pl.BlockSpec(kwf.shape, const3),
            pl.BlockSpec(kwb.shape, const3),
            pl.BlockSpec(kwp.shape, const3),
            pl.BlockSpec(memory_space=pltpu.SMEM),
            pl.BlockSpec(gnw.shape, const2),
        ],
        out_specs=pl.BlockSpec((1, chunk, w), fwd_only),
        scratch_shapes=[
            pltpu.VMEM((RET_HEADS, RET_D, RET_D), F32),
            pltpu.VMEM((RET_HEADS, RET_D, RET_D), F32),
            pltpu.VMEM((nc, RET_HEADS, RET_D, RET_D), BF16),
        ],
        compiler_params=_cparams(("arbitrary", "arbitrary", "arbitrary")),
        name="retention",
    )(qr, kr, vr, gr, kpre, vpre, dmat, qwf, qwb, kwf, kwb, kwp, gdec, gnw)


def _attn_kernel(q_ref, kp_ref, kc_ref, kn_ref, vp_ref, vc_ref, vn_ref, km_ref, vm_ref,
                 bias_ref, sink_ref, o_ref):
    j = pl.program_id(1)
    nb = pl.num_programs(1)
    lane = lax.broadcasted_iota(jnp.int32, (1, 4 * ATT_BLOCK), 1)
    first = jnp.logical_and(lane < ATT_BLOCK, j == 0)
    last = jnp.logical_and(jnp.logical_and(lane >= 2 * ATT_BLOCK, lane < 3 * ATT_BLOCK), j == nb - 1)
    pen = jnp.where(jnp.logical_or(first, last), NEG, 0.0).astype(F32)
    contract1 = (((1,), (1,)), ((), ()))
    outs = []
    for g in range(ATT_KV):
        sl = slice(g * ATT_HD, (g + 1) * ATT_HD)
        kall = jnp.concatenate([kp_ref[0][:, sl], kc_ref[0][:, sl], kn_ref[0][:, sl], km_ref[:, sl]], axis=0)
        vall = jnp.concatenate([vp_ref[0][:, sl], vc_ref[0][:, sl], vn_ref[0][:, sl], vm_ref[:, sl]], axis=0)
        for hh in range(ATT_GROUP):
            h = g * ATT_GROUP + hh
            q = q_ref[0, :, h * ATT_HD:(h + 1) * ATT_HD]
            s = lax.dot_general(q, kall, contract1, preferred_element_type=F32)
            s = s + bias_ref[0, h] + pen
            snk = sink_ref[h]
            m = jnp.maximum(jnp.max(s, axis=-1, keepdims=True), snk)
            p = jnp.exp(s - m)
            den = jnp.sum(p, axis=-1, keepdims=True) + jnp.exp(snk - m)
            o = jnp.dot(p.astype(BF16), vall, preferred_element_type=F32)
            outs.append(o / den)
    for h2 in range(ATT_HEADS // 2):
        o_ref[0, :, h2 * LANES:(h2 + 1) * LANES] = jnp.concatenate(
            [outs[2 * h2], outs[2 * h2 + 1]], axis=-1).astype(BF16)


def _attention(qa, ka, va, kmeta, vmeta, bias, sink):
    b, s, w = qa.shape
    nb = s // ATT_BLOCK
    kvw = ka.shape[-1]
    prev = lambda i, j: (i, jnp.maximum(j - 1, 0), 0)
    cur = lambda i, j: (i, j, 0)
    nxt = lambda i, j: (i, jnp.minimum(j + 1, nb - 1), 0)
    const2 = lambda i, j: (0, 0)
    kv = lambda m: pl.BlockSpec((1, ATT_BLOCK, kvw), m)
    return pl.pallas_call(
        _attn_kernel,
        out_shape=jax.ShapeDtypeStruct((b, s, w), BF16),
        grid=(b, nb),
        in_specs=[
            pl.BlockSpec((1, ATT_BLOCK, w), cur),
            kv(prev), kv(cur), kv(nxt), kv(prev), kv(cur), kv(nxt),
            pl.BlockSpec(kmeta.shape, const2),
            pl.BlockSpec(vmeta.shape, const2),
            pl.BlockSpec((1,) + bias.shape[1:], lambda i, j: (jnp.minimum(j, 1), 0, 0, 0)),
            pl.BlockSpec(memory_space=pltpu.SMEM),
        ],
        out_specs=pl.BlockSpec((1, ATT_BLOCK, w), cur),
        compiler_params=_cparams(("arbitrary", "arbitrary")),
        name="attention",
    )(qa, ka, ka, ka, va, va, va, kmeta, vmeta, bias, sink)


def _outproj_kernel(or_ref, oa_ref, x_ref, w_ref, nw_ref, h_ref, xn_ref):
    half = or_ref.shape[-1]
    h = x_ref[0]
    h = h + jnp.dot(or_ref[0], w_ref[:half], preferred_element_type=F32)
    h = h + jnp.dot(oa_ref[0], w_ref[half:], preferred_element_type=F32)
    h_ref[0] = h
    ms = jnp.mean(h * h, axis=-1, keepdims=True)
    xn_ref[0] = h * lax.rsqrt(ms + RMS_EPS) * nw_ref[...]


def _outproj(o_r, o_a, x, w_out_bf, norm_w, tm):
    b, s, d = x.shape
    half = o_r.shape[-1]
    row = lambda i, j: (i, j, 0)
    const2 = lambda i, j: (0, 0)
    return pl.pallas_call(
        _outproj_kernel,
        out_shape=[jax.ShapeDtypeStruct((b, s, d), F32)] * 2,
        grid=(b, s // tm),
        in_specs=[
            pl.BlockSpec((1, tm, half), row),
            pl.BlockSpec((1, tm, half), row),
            pl.BlockSpec((1, tm, d), row),
            pl.BlockSpec(w_out_bf.shape, const2),
            pl.BlockSpec((1, d), const2),
        ],
        out_specs=[pl.BlockSpec((1, tm, d), row)] * 2,
        compiler_params=_cparams(("arbitrary", "arbitrary")),
        name="outproj",
    )(o_r, o_a, x, w_out_bf, norm_w)


def _top16_rows(s, iota):
    nrows = s.shape[0]
    vals, ids = [], []
    for _ in range(PEER_TOPK):
        m = jnp.max(s, axis=0, keepdims=True)
        am = jnp.min(jnp.where(s == m, iota, nrows), axis=0, keepdims=True)
        vals.append(m)
        ids.append(am)
        s = jnp.where(iota == am, -jnp.inf, s)
    return jnp.concatenate(vals, axis=0), jnp.concatenate(ids, axis=0)


_PAIR_GROUPS = ((0, 0, 8), (0, 8, 8), (1, 0, 8), (2, 0, 5), (3, 0, 4), (4, 0, 3), (5, 0, 2), (6, 0, 2), (7, 0, 2))


def _route_kernel(x_ref, wq_ref, pk_ref, idx_ref, gate_ref, sc_ref):
    tm = x_ref.shape[1]
    q = jnp.dot(x_ref[0].astype(BF16), wq_ref[...], preferred_element_type=F32).astype(BF16)
    contract1 = (((1,), (1,)), ((), ()))
    for hp in range(2 * PEER_HEADS):
        sc_ref[hp] = lax.dot_general(pk_ref[hp], q[:, hp * PEER_NKEYS:(hp + 1) * PEER_NKEYS], contract1,
                                     preferred_element_type=F32)
    iota = lax.broadcasted_iota(jnp.int32, (PEER_NKEYS, LANES), 0)
    sub = lax.broadcasted_iota(jnp.int32, (SUBLANES, LANES), 0)
    far = PEER_TOPK * PEER_TOPK

    def head_body(h, carry):
        for lt in range(tm // LANES):
            lanes = slice(lt * LANES, (lt + 1) * LANES)
            s1, i1 = _top16_rows(sc_ref[2 * h, :, lanes], iota)
            s2, i2 = _top16_rows(sc_ref[2 * h + 1, :, lanes], iota)
            i1 = i1 * PEER_NKEYS
            sc, ex, pos = [], [], []
            for i, j0, cnt in _PAIR_GROUPS:
                ok = sub < cnt
                sc.append(jnp.where(ok, s1[i:i + 1] + s2[j0:j0 + SUBLANES], -jnp.inf))
                ex.append(i1[i:i + 1] + i2[j0:j0 + SUBLANES])
                pos.append(jnp.where(ok, i * PEER_TOPK + j0 + sub, far))
            sc.append(s1[SUBLANES:] + s2[0:1])
            ex.append(i1[SUBLANES:] + i2[0:1])
            pos.append((SUBLANES + sub) * PEER_TOPK)
            sc = jnp.concatenate(sc, axis=0)
            ex = jnp.concatenate(ex, axis=0)
            pos = jnp.concatenate(pos, axis=0)
            top, experts = [], []
            for _ in range(PEER_TOPK):
                m = jnp.max(sc, axis=0, keepdims=True)
                pm = jnp.min(jnp.where(sc == m, pos, far), axis=0, keepdims=True)
                hit = pos == pm
                experts.append(jnp.max(jnp.where(hit, ex, -1), axis=0, keepdims=True))
                top.append(m)
                sc = jnp.where(hit, -jnp.inf, sc)
            top = jnp.concatenate(top, axis=0)
            e = jnp.exp(top - top[0:1])
            rows = pl.ds(pl.multiple_of(h * PEER_TOPK, PEER_TOPK), PEER_TOPK)
            gate_ref[0, rows, lanes] = e / jnp.sum(e, axis=0, keepdims=True)
            idx_ref[0, rows, lanes] = jnp.concatenate(experts, axis=0)
        return carry

    lax.fori_loop(0, PEER_HEADS, head_body, 0)


def _route(xn, wq_bf, pk_bf, tm):
    b, s, d = xn.shape
    const2 = lambda i, j: (0, 0)
    slot = lambda i, j: (i, 0, j)
    return pl.pallas_call(
        _route_kernel,
        out_shape=[jax.ShapeDtypeStruct((b, PEER_SLOTS, s), jnp.int32),
                   jax.ShapeDtypeStruct((b, PEER_SLOTS, s), F32)],
        grid=(b, s // tm),
        in_specs=[
            pl.BlockSpec((1, tm, d), lambda i, j: (i, j, 0)),
            pl.BlockSpec(wq_bf.shape, const2),
            pl.BlockSpec(pk_bf.shape, lambda i, j: (0, 0, 0)),
        ],
        out_specs=[pl.BlockSpec((1, PEER_SLOTS, tm), slot)] * 2,
        scratch_shapes=[pltpu.VMEM((2 * PEER_HEADS, PEER_NKEYS, tm), F32)],
        compiler_params=_cparams(("arbitrary", "arbitrary")),
        name="route",
    )(xn, wq_bf, pk_bf)


def _bitrev4(n):
    return ((n & 1) << 3) | ((n & 2) << 1) | ((n & 4) >> 1) | ((n & 8) >> 3)


def _pair_slots():
    out = []
    for n in range(PEER_SLOTS // SUBLANES):
        g = _bitrev4(n)
        for c in range(4):
            top = SUBLANES * g + SLOT_ORDER[c]
            out.append((top, top + 4))
    return out


def _gather_pair(tbl_ref, top_off, bot_off, upper):
    ra = tbl_ref[pl.ds(pl.multiple_of(top_off, ROWS_PER_EXPERT), SUBLANES), :]
    rb = tbl_ref[pl.ds(pl.multiple_of(bot_off, ROWS_PER_EXPERT), SUBLANES), :]
    row = jnp.where(upper, ra, rb)
    lo = lax.bitcast_convert_type(row << 16, F32)
    hi = lax.bitcast_convert_type(row & jnp.uint32(0xFFFF0000), F32)
    return lo, hi


def _merge_sublanes(a, b, h, sub):
    keep = (sub & h) == 0
    return jnp.where(keep, a + pltpu.roll(a, SUBLANES - h, 0), b + pltpu.roll(b, h, 0))


def _merge_lanes(a, b, h, lane):
    keep = (lane & h) == 0
    return jnp.where(keep, a + pltpu.roll(a, LANES - h, 1), b + pltpu.roll(b, h, 1))


def _slot_specs(tb):
    return [pl.BlockSpec((None, None, SLOT_GROUP * tb), lambda i, j=j: (j, 0, i), memory_space=pltpu.SMEM,
                         pipeline_mode=pl.Buffered(1)) for j in range(PEER_SLOTS // SLOT_GROUP)]


def _group_slots(a, tb):
    n = a.shape[-1]
    a = a.reshape(PEER_SLOTS // SLOT_GROUP, SLOT_GROUP, n // tb, tb).transpose(0, 2, 1, 3)
    return a.reshape(PEER_SLOTS // SLOT_GROUP, 1, n * SLOT_GROUP)


def _slot_reader(refs, t, tb):
    at = [t + g * tb for g in range(SLOT_GROUP)]
    return lambda k: refs[k // SLOT_GROUP][at[k % SLOT_GROUP]]


def _peer_u_kernel(*refs):
    ngroups = PEER_SLOTS // SLOT_GROUP
    offs = refs[:ngroups]
    x_ref, tbl_ref, o_ref = refs[ngroups:]
    tb = x_ref.shape[0] // SUBLANES
    sub = lax.broadcasted_iota(jnp.int32, (SUBLANES, LANES), 0)
    lane = lax.broadcasted_iota(jnp.int32, (SUBLANES, LANES), 1)
    upper = sub < ROWS_PER_EXPERT
    pairs = _pair_slots()

    def token_group(gi):
        ys = []
        for j in range(SUBLANES):
            t = gi * SUBLANES + j
            x = x_ref[pl.ds(pl.multiple_of(t * SUBLANES, SUBLANES), SUBLANES), :]
            swapped = pltpu.roll(x, ROWS_PER_EXPERT, 0)
            xa = jnp.where(upper, x, swapped)
            xb = jnp.where(upper, swapped, x)
            off = _slot_reader(offs, t, tb)
            zs = []
            for n in range(PEER_SLOTS // SUBLANES):
                ps = []
                for c in range(4):
                    top, bot = pairs[4 * n + c]
                    lo, hi = _gather_pair(tbl_ref, off(top), off(bot), upper)
                    ps.append(lo * xa + hi * xb)
                r0 = _merge_sublanes(ps[0], ps[1], 2, sub)
                r1 = _merge_sublanes(ps[2], ps[3], 2, sub)
                zs.append(_merge_sublanes(r0, r1, 1, sub))
            for h in (64, 32, 16, 8):
                zs = [_merge_lanes(zs[2 * i], zs[2 * i + 1], h, lane) for i in range(len(zs) // 2)]
            ys.append(zs[0])
        w4 = [_merge_lanes(ys[j], ys[j + 4], 4, lane) for j in range(4)]
        w2 = [_merge_lanes(w4[j], w4[j + 2], 2, lane) for j in range(2)]
        return _merge_lanes(w2[0], w2[1], 1, lane)

    def step(i, carry):
        for r in range(U_GROUPS_PER_ITER):
            gi = i * U_GROUPS_PER_ITER + r
            o_ref[gi] = token_group(gi)
        return carry

    lax.fori_loop(0, o_ref.shape[0] // U_GROUPS_PER_ITER, step, 0)


def _peer_u(offs_g, x_rows, tbl, tb):
    n = x_rows.shape[0] // SUBLANES
    assert tb % (SUBLANES * U_GROUPS_PER_ITER) == 0
    return pl.pallas_call(
        _peer_u_kernel,
        out_shape=jax.ShapeDtypeStruct((n // SUBLANES, SUBLANES, LANES), F32),
        grid=(n // tb,),
        in_specs=_slot_specs(tb) + [
            pl.BlockSpec((tb * SUBLANES, LANES), lambda i: (i, 0)),
            pl.BlockSpec(tbl.shape, lambda i: (0, 0), pipeline_mode=pl.Buffered(1)),
        ],
        out_specs=pl.BlockSpec((tb // SUBLANES, SUBLANES, LANES), lambda i: (i, 0, 0)),
        compiler_params=_cparams(("arbitrary",)),
        name="peer_u",
    )(*([offs_g] * (PEER_SLOTS // SLOT_GROUP)), x_rows, tbl)


def _peer_w_kernel(gate_ref, act_ref, w_ref):
    a = act_ref[...]
    w_ref[...] = gate_ref[...] * (0.5 * a * (1.0 + lax.erf(a * (2.0 ** -0.5))))


def _peer_w(gate_t, act_t, tm):
    n = gate_t.shape[-1]
    spec = pl.BlockSpec((PEER_SLOTS, tm), lambda i: (0, i))
    return pl.pallas_call(
        _peer_w_kernel,
        out_shape=jax.ShapeDtypeStruct(gate_t.shape, F32),
        grid=(n // tm,),
        in_specs=[spec, spec],
        out_specs=spec,
        compiler_params=_cparams(("arbitrary",)),
        name="peer_w",
    )(gate_t, act_t)


def _peer_v_kernel(*refs):
    ngroups = PEER_SLOTS // SLOT_GROUP
    offs = refs[:ngroups]
    wts = refs[ngroups:2 * ngroups]
    h_ref, tbl_ref, nw_ref, o_ref = refs[2 * ngroups:]
    tb = h_ref.shape[0] // SUBLANES
    sub = lax.broadcasted_iota(jnp.int32, (SUBLANES, LANES), 0)
    upper = sub < ROWS_PER_EXPERT
    pairs = _pair_slots()

    def token(t):
        off = _slot_reader(offs, t, tb)
        wt = _slot_reader(wts, t, tb)
        acc = [jnp.zeros((SUBLANES, LANES), F32) for _ in range(4)]
        for q, (top, bot) in enumerate(pairs):
            lo, hi = _gather_pair(tbl_ref, off(top), off(bot), upper)
            w = jnp.where(upper, wt(top), wt(bot))
            k = 2 * (q % 2)
            acc[k] = acc[k] + lo * w
            acc[k + 1] = acc[k + 1] + hi * w
        lo = acc[0] + acc[2]
        hi = acc[1] + acc[3]
        lo = lo + pltpu.roll(lo, ROWS_PER_EXPERT, 0)
        hi = hi + pltpu.roll(hi, ROWS_PER_EXPERT, 0)
        rows = pl.ds(pl.multiple_of(t * SUBLANES, SUBLANES), SUBLANES)
        o_ref[rows, :] = h_ref[rows, :] + jnp.where(upper, lo, hi)

    def step(i, carry):
        for r in range(V_TOKENS_PER_ITER):
            token(i * V_TOKENS_PER_ITER + r)
        return carry

    lax.fori_loop(0, tb // V_TOKENS_PER_ITER, step, 0)

    def norm_step(i, carry):
        rows = pl.ds(pl.multiple_of(i * NORM_TOKENS * SUBLANES, NORM_TOKENS * SUBLANES), NORM_TOKENS * SUBLANES)
        h = o_ref[rows, :].reshape(NORM_TOKENS, SUBLANES, LANES)
        ss = jnp.sum(jnp.sum(h * h, axis=2, keepdims=True), axis=1, keepdims=True)
        y = h * lax.rsqrt(ss * (1.0 / D_MODEL) + RMS_EPS) * nw_ref[...][None]
        o_ref[rows, :] = y.reshape(NORM_TOKENS * SUBLANES, LANES)
        return carry

    lax.fori_loop(0, tb // NORM_TOKENS, norm_step, 0)


def _peer_v(offs_g, w_g, h_rows, tbl, norm_rows, tb):
    n = h_rows.shape[0] // SUBLANES
    assert tb % V_TOKENS_PER_ITER == 0 and tb % NORM_TOKENS == 0
    rows = pl.BlockSpec((tb * SUBLANES, LANES), lambda i: (i, 0))
    return pl.pallas_call(
        _peer_v_kernel,
        out_shape=jax.ShapeDtypeStruct(h_rows.shape, F32),
        grid=(n // tb,),
        in_specs=_slot_specs(tb) + _slot_specs(tb) + [
            rows,
            pl.BlockSpec(tbl.shape, lambda i: (0, 0), pipeline_mode=pl.Buffered(1)),
            pl.BlockSpec((SUBLANES, LANES), lambda i: (0, 0)),
        ],
        out_specs=rows,
        compiler_params=_cparams(("arbitrary",)),
        name="peer_v",
    )(*([offs_g] * (PEER_SLOTS // SLOT_GROUP)), *([w_g] * (PEER_SLOTS // SLOT_GROUP)), h_rows, tbl, norm_rows)


def _pack_table(t):
    half = D_MODEL // 2
    bits = lax.bitcast_convert_type(t.astype(BF16), jnp.uint16).astype(jnp.uint32)
    words = bits[:, :half] | (bits[:, half:] << 16)
    words = words.reshape(t.shape[0] * ROWS_PER_EXPERT, LANES)
    return jnp.pad(words, ((TABLE_PAD, TABLE_PAD), (0, 0)))


def _t5_bucket(rel):
    half = N_BUCKETS // 2
    exact = half // 2
    n = jnp.abs(rel)
    large = exact + (jnp.log(jnp.maximum(n, 1).astype(F32) / exact)
                     / math.log(MAX_DISTANCE / exact) * (half - exact)).astype(jnp.int32)
    large = jnp.minimum(large, half - 1)
    return jnp.where(rel > 0, half, 0) + jnp.where(n < exact, n, large)


def _attention_bias(rel_bias):
    rb = rel_bias.astype(F32)
    i = jnp.arange(ATT_BLOCK)
    jb = jnp.arange(3 * ATT_BLOCK)
    rel = (jb[None, :] - ATT_BLOCK) - i[:, None]
    band = jnp.where((jnp.abs(rel) <= ATT_BLOCK)[..., None], rb[_t5_bucket(rel)], NEG)
    mpos = PREFIX - N_META + jnp.arange(N_META)
    variants = []
    for blk in (0, 1):
        qpos = PREFIX + blk * ATT_BLOCK + i
        meta = rb[_t5_bucket(mpos[None, :] - qpos[:, None])]
        fill = jnp.full((ATT_BLOCK, ATT_BLOCK - N_META, ATT_HEADS), NEG, F32)
        variants.append(jnp.concatenate([band, meta, fill], axis=1).transpose(2, 0, 1))
    return jnp.stack(variants)


def _rope_tables(pos):
    half = RET_D // 2
    inv = ROPE_BASE ** (-jnp.arange(half, dtype=F32) / half)
    ang = pos.astype(F32)[:, None] * inv[None, :]
    cos, sin = jnp.cos(ang), jnp.sin(ang)
    return jnp.concatenate([cos, cos], axis=1), jnp.concatenate([-sin, sin], axis=1)


def _decay_tables(dec_f, dec_b, chunk):
    lf = jax.nn.log_sigmoid(dec_f.astype(F32))[:, None]
    lb = jax.nn.log_sigmoid(dec_b.astype(F32))[:, None]
    idx = jnp.arange(chunk, dtype=F32)
    diff = idx[:, None] - idx[None, :]
    dmat = (jnp.where(diff >= 0, jnp.exp(jnp.maximum(diff, 0.0)[None] * lf[:, :, None]), 0.0)
            + jnp.where(diff < 0, jnp.exp(jnp.maximum(-diff, 0.0)[None] * lb[:, :, None]), 0.0))
    bc = lambda v: jnp.broadcast_to(v[:, :, None], v.shape + (RET_D,))
    qwf = bc(jnp.exp((idx + 1.0)[None] * lf))
    kwf = bc(jnp.exp((chunk - 1.0 - idx)[None] * lf))
    qwb = bc(jnp.exp((chunk - idx)[None] * lb))
    kwb = bc(jnp.exp(idx[None] * lb))
    pidx = jnp.arange(PREFIX, dtype=F32)
    kwp = bc(jnp.exp((PREFIX - 1.0 - pidx)[None] * lf))
    gdec = jnp.concatenate([jnp.exp(chunk * lf[:, 0]), jnp.exp(chunk * lb[:, 0])])
    return dmat, qwf, qwb, kwf, kwb, kwp, gdec


def _divisor_tile(n, want):
    t = min(n, want)
    while n % t:
        t //= 2
    return t


def _encode(x, shared):
    b, s, d = x.shape
    n = b * s
    tm = _divisor_tile(s, 512)
    chunk = _divisor_tile(s, 256)
    cos_t, sin_t = _rope_tables(jnp.arange(s) + N_META)
    qr, kr, vr, gr, qa, ka, va = _proj(x, shared["norm_mix"], shared["w_in"], cos_t, sin_t, tm)
    o_r = _retention(qr, kr, vr, gr, shared["kr_pre"], shared["vr_pre"], *_decay_tables(
        shared["dec_f"], shared["dec_b"], chunk), shared["gn_w"], chunk)
    o_a = _attention(qa, ka, va, shared["k_meta"], shared["v_meta"], shared["bias"], shared["sink"])
    h1, xn2 = _outproj(o_r, o_a, x, shared["w_out"], shared["norm_ffn"], tm)
    idx_t, gate_t = _route(xn2, shared["wq"], shared["pkeys"], _divisor_tile(s, ROUTE_TOKENS))
    idx_t = idx_t.transpose(1, 0, 2).reshape(PEER_SLOTS, n)
    gate_t = gate_t.transpose(1, 0, 2).reshape(PEER_SLOTS, n)
    top_half = (jnp.arange(PEER_SLOTS) % SUBLANES) < ROWS_PER_EXPERT
    tb = _divisor_tile(n, PEER_TOKENS)
    offs_g = _group_slots(idx_t * ROWS_PER_EXPERT + jnp.where(top_half, TABLE_PAD, 0)[:, None], tb)
    act = _peer_u(offs_g, xn2.reshape(n * SUBLANES, LANES), shared["u_tbl"], tb)
    act = act.reshape(n // SUBLANES, SUBLANES, PEER_SLOTS // SUBLANES, SUBLANES)
    act_t = act.transpose(2, 1, 0, 3).reshape(PEER_SLOTS, n)
    w_g = _group_slots(_peer_w(gate_t, act_t, _divisor_tile(n, 2048)), tb)
    y = _peer_v(offs_g, w_g, h1.reshape(n * SUBLANES, LANES), shared["v_tbl"], shared["norm_final"], tb)
    return y.reshape(b, s, d)


def kernel(x_prompt, x_sample, meta_tokens, norm_mix_w, w_in, ret_decay_fwd, ret_decay_bwd, ret_gn_w,
           attn_sink, rel_bias, w_out, norm_ffn_w, peer_wq, peer_keys, peer_u, peer_v, norm_final_w):
    layer = 0
    shared = {
        "norm_mix": norm_mix_w[layer][None, :].astype(F32),
        "w_in": w_in[layer].astype(BF16),
        "dec_f": ret_decay_fwd[layer],
        "dec_b": ret_decay_bwd[layer],
        "gn_w": ret_gn_w[layer][None, :].astype(F32),
        "sink": attn_sink[layer].astype(F32),
        "bias": _attention_bias(rel_bias),
        "w_out": w_out[layer].astype(BF16),
        "norm_ffn": norm_ffn_w[layer][None, :].astype(F32),
        "wq": peer_wq[layer].astype(BF16),
        "pkeys": peer_keys[layer].reshape(2 * PEER_HEADS, PEER_NKEYS, PEER_NKEYS).astype(BF16),
        "u_tbl": _pack_table(peer_u[layer]),
        "v_tbl": _pack_table(peer_v[layer]),
        "norm_final": norm_final_w.reshape(SUBLANES, LANES).astype(F32),
    }
    prefix = jnp.concatenate([jnp.zeros((PREFIX - N_META, D_MODEL), x_prompt.dtype),
                              meta_tokens.astype(x_prompt.dtype)], axis=0)[None]
    cos_p, sin_p = _rope_tables(jnp.arange(PREFIX) - (PREFIX - N_META))
    _, kr_p, vr_p, _, _, ka_p, va_p = _proj(prefix, shared["norm_mix"], shared["w_in"], cos_p, sin_p, PREFIX)
    shared["kr_pre"], shared["vr_pre"] = kr_p[0], vr_p[0]
    pad_meta = lambda t: jnp.pad(t[0, PREFIX - N_META:], ((0, ATT_BLOCK - N_META), (0, 0)))
    shared["k_meta"], shared["v_meta"] = pad_meta(ka_p), pad_meta(va_p)
    return (_encode(x_prompt, shared), _encode(x_sample, shared))
```

```python
import functools
import math

import jax
import jax.numpy as jnp
from jax import lax
from jax.experimental import pallas as pl
from jax.experimental.pallas import tpu as pltpu

F32 = jnp.float32
BF16 = jnp.bfloat16

D_MODEL = 1024
N_META = 16
PREFIX = 128
RET_HEADS = 4
RET_D = 128
ATT_HEADS = 8
ATT_KV = 2
ATT_GROUP = ATT_HEADS // ATT_KV
ATT_HD = 64
ATT_BLOCK = 128
N_BUCKETS = 32
MAX_DISTANCE = 128
ROPE_BASE = 10000.0
PEER_HEADS = 8
PEER_NKEYS = 128
PEER_EXPERTS = PEER_NKEYS * PEER_NKEYS
PEER_TOPK = 16
PEER_SLOTS = PEER_HEADS * PEER_TOPK
RMS_EPS = 1e-6
GN_EPS = 1e-5
NEG = -1e30
IN_COLS = (0, 512, 1024, 1536, 2048, 2560, 2688, 2816)

LANES = 128
SUBLANES = 8
VMEM_LIMIT = 56 * 1024 * 1024
ROWS_PER_EXPERT = D_MODEL // 2 // LANES
TABLE_PAD = ROWS_PER_EXPERT
PEER_TOKENS = 256
ROUTE_TOKENS = 256
SLOT_ORDER = (0, 2, 1, 3)
SLOT_GROUP = 8
U_GROUPS_PER_ITER = 2
V_TOKENS_PER_ITER = 4
NORM_TOKENS = 32


def _cparams(sem):
    return pltpu.CompilerParams(dimension_semantics=sem, vmem_limit_bytes=VMEM_LIMIT)


def _proj_kernel(x_ref, nw_ref, w_ref, cos_ref, sin_ref,
                 qr_ref, kr_ref, vr_ref, gr_ref, qa_ref, ka_ref, va_ref):
    x = x_ref[0]
    ms = jnp.mean(x * x, axis=-1, keepdims=True)
    xn = (x * lax.rsqrt(ms + RMS_EPS) * nw_ref[...]).astype(BF16)
    cosf = cos_ref[...]
    sinf = sin_ref[...]

    def mm(i):
        return jnp.dot(xn, w_ref[:, IN_COLS[i]:IN_COLS[i + 1]], preferred_element_type=F32)

    def rotary(t, scale):
        for h in range(RET_HEADS):
            th = t[:, h * RET_D:(h + 1) * RET_D]
            yield h, (th * cosf + pltpu.roll(th, RET_D // 2, 1) * sinf) * scale

    for h, r in rotary(mm(0), 1.0):
        qr_ref[0, :, h * RET_D:(h + 1) * RET_D] = r.astype(BF16)
    for h, r in rotary(mm(1), RET_D ** -0.5):
        kr_ref[0, :, h * RET_D:(h + 1) * RET_D] = r.astype(BF16)
    vr_ref[0] = mm(2).astype(BF16)
    gr_ref[0] = mm(3)
    qa_ref[0] = (mm(4) * (ATT_HD ** -0.5)).astype(BF16)
    ka_ref[0] = mm(5).astype(BF16)
    va_ref[0] = mm(6).astype(BF16)


def _proj(x, norm_w, w_in_bf, cos_t, sin_t, tm):
    b, s, d = x.shape
    widths = (512, 512, 512, 512, 512, 128, 128)
    dtypes = (BF16, BF16, BF16, F32, BF16, BF16, BF16)
    return pl.pallas_call(
        _proj_kernel,
        out_shape=[jax.ShapeDtypeStruct((b, s, w), dt) for w, dt in zip(widths, dtypes)],
        grid=(b, s // tm),
        in_specs=[
            pl.BlockSpec((1, tm, d), lambda i, j: (i, j, 0)),
            pl.BlockSpec((1, d), lambda i, j: (0, 0)),
            pl.BlockSpec(w_in_bf.shape, lambda i, j: (0, 0)),
            pl.BlockSpec((tm, LANES), lambda i, j: (j, 0)),
            pl.BlockSpec((tm, LANES), lambda i, j: (j, 0)),
        ],
        out_specs=[pl.BlockSpec((1, tm, w), lambda i, j: (i, j, 0)) for w in widths],
        compiler_params=_cparams(("arbitrary", "arbitrary")),
        name="proj",
    )(x, norm_w, w_in_bf, cos_t, sin_t)


def _ret_kernel(q_ref, k_ref, v_ref, g_ref, kpre_ref, vpre_ref, dmat_ref, qwf_ref, qwb_ref,
                kwf_ref, kwb_ref, kwp_ref, gdec_ref, gnw_ref, o_ref, sf_ref, sb_ref, sball_ref):
    ph = pl.program_id(1)
    c = pl.program_id(2)
    nc = pl.num_programs(2)
    contract0 = (((0,), (0,)), ((), ()))
    contract1 = (((1,), (1,)), ((), ()))

    def head(ref, h):
        return ref[0, :, h * RET_D:(h + 1) * RET_D]

    def kv_update(kh, vh, kw):
        kw = (kh.astype(F32) * kw).astype(BF16)
        return lax.dot_general(kw, vh, contract0, preferred_element_type=F32)

    @pl.when(ph == 0)
    def _backward_states():
        @pl.when(c == 0)
        def _():
            sb_ref[...] = jnp.zeros_like(sb_ref)

        j = nc - 1 - c
        for h in range(RET_HEADS):
            sball_ref[j, h] = sb_ref[h].astype(BF16)
            sb_ref[h] = gdec_ref[RET_HEADS + h] * sb_ref[h] + kv_update(head(k_ref, h), head(v_ref, h), kwb_ref[h])

    @pl.when(ph == 1)
    def _outputs():
        @pl.when(c == 0)
        def _():
            for h in range(RET_HEADS):
                kp = kpre_ref[:, h * RET_D:(h + 1) * RET_D]
                vp = vpre_ref[:, h * RET_D:(h + 1) * RET_D]
                sf_ref[h] = kv_update(kp, vp, kwp_ref[h])

        for h in range(RET_HEADS):
            qh, kh, vh = head(q_ref, h), head(k_ref, h), head(v_ref, h)
            s = lax.dot_general(qh, kh, contract1, preferred_element_type=F32) * dmat_ref[h]
            o = jnp.dot(s.astype(BF16), vh, preferred_element_type=F32)
            o = o + jnp.dot(qh, sf_ref[h].astype(BF16), preferred_element_type=F32) * qwf_ref[h]
            o = o + jnp.dot(qh, sball_ref[c, h], preferred_element_type=F32) * qwb_ref[h]
            mu = jnp.mean(o, axis=-1, keepdims=True)
            oc = o - mu
            var = jnp.mean(oc * oc, axis=-1, keepdims=True)
            on = oc * lax.rsqrt(var + GN_EPS) * gnw_ref[:, h * RET_D:(h + 1) * RET_D]
            g = head(g_ref, h)
            o_ref[0, :, h * RET_D:(h + 1) * RET_D] = (g * jax.nn.sigmoid(g) * on).astype(BF16)
            sf_ref[h] = gdec_ref[h] * sf_ref[h] + kv_update(kh, vh, kwf_ref[h])


def _retention(qr, kr, vr, gr, kpre, vpre, dmat, qwf, qwb, kwf, kwb, kwp, gdec, gnw, chunk):
    b, s, w = qr.shape
    nc = s // chunk

    def seq_map(i, ph, c):
        return (i, jnp.where(ph == 0, nc - 1 - c, c), 0)

    def fwd_only(i, ph, c):
        return (i, c * ph, 0)

    const2 = lambda i, ph, c: (0, 0)
    const3 = lambda i, ph, c: (0, 0, 0)
    return pl.pallas_call(
        _ret_kernel,
        out_shape=jax.ShapeDtypeStruct((b, s, w), BF16),
        grid=(b, 2, nc),
        in_specs=[
            pl.BlockSpec((1, chunk, w), fwd_only),
            pl.BlockSpec((1, chunk, w), seq_map),
            pl.BlockSpec((1, chunk, w), seq_map),
            pl.BlockSpec((1, chunk, w), fwd_only),
            pl.BlockSpec(kpre.shape, const2),
            pl.BlockSpec(vpre.shape, const2),
            pl.BlockSpec(dmat.shape, const3),
            pl.BlockSpec(qwf.shape, const3),
            pl.BlockSpec(qwb.shape, const3),
            pl.BlockSpec(kwf.shape, const3),
            pl.BlockSpec(kwb.shape, const3),
            pl.BlockSpec(kwp.shape, const3),
            pl.BlockSpec(memory_space=pltpu.SMEM),
            pl.BlockSpec(gnw.shape, const2),
        ],
        out_specs=pl.BlockSpec((1, chunk, w), fwd_only),
        scratch_shapes=[
            pltpu.VMEM((RET_HEADS, RET_D, RET_D), F32),
            pltpu.VMEM((RET_HEADS, RET_D, RET_D), F32),
            pltpu.VMEM((nc, RET_HEADS, RET_D, RET_D), BF16),
        ],
        compiler_params=_cparams(("arbitrary", "arbitrary", "arbitrary")),
        name="retention",
    )(qr, kr, vr, gr, kpre, vpre, dmat, qwf, qwb, kwf, kwb, kwp, gdec, gnw)


def _attn_kernel(q_ref, kp_ref, kc_ref, kn_ref, vp_ref, vc_ref, vn_ref, km_ref, vm_ref,
                 bias_ref, sink_ref, o_ref):
    j = pl.program_id(1)
    nb = pl.num_programs(1)
    lane = lax.broadcasted_iota(jnp.int32, (1, 4 * ATT_BLOCK), 1)
    first = jnp.logical_and(lane < ATT_BLOCK, j == 0)
    last = jnp.logical_and(jnp.logical_and(lane >= 2 * ATT_BLOCK, lane < 3 * ATT_BLOCK), j == nb - 1)
    pen = jnp.where(jnp.logical_or(first, last), NEG, 0.0).astype(F32)
    contract1 = (((1,), (1,)), ((), ()))
    outs = []
    for g in range(ATT_KV):
        sl = slice(g * ATT_HD, (g + 1) * ATT_HD)
        kall = jnp.concatenate([kp_ref[0][:, sl], kc_ref[0][:, sl], kn_ref[0][:, sl], km_ref[:, sl]], axis=0)
        vall = jnp.concatenate([vp_ref[0][:, sl], vc_ref[0][:, sl], vn_ref[0][:, sl], vm_ref[:, sl]], axis=0)
        for hh in range(ATT_GROUP):
            h = g * ATT_GROUP + hh
            q = q_ref[0, :, h * ATT_HD:(h + 1) * ATT_HD]
            s = lax.dot_general(q, kall, contract1, preferred_element_type=F32)
            s = s + bias_ref[0, h] + pen
            snk = sink_ref[h]
            m = jnp.maximum(jnp.max(s, axis=-1, keepdims=True), snk)
            p = jnp.exp(s - m)
            den = jnp.sum(p, axis=-1, keepdims=True) + jnp.exp(snk - m)
            o = jnp.dot(p.astype(BF16), vall, preferred_element_type=F32)
            outs.append(o / den)
    for h2 in range(ATT_HEADS // 2):
        o_ref[0, :, h2 * LANES:(h2 + 1) * LANES] = jnp.concatenate(
            [outs[2 * h2], outs[2 * h2 + 1]], axis=-1).astype(BF16)


def _attention(qa, ka, va, kmeta, vmeta, bias, sink):
    b, s, w = qa.shape
    nb = s // ATT_BLOCK
    kvw = ka.shape[-1]
    prev = lambda i, j: (i, jnp.maximum(j - 1, 0), 0)
    cur = lambda i, j: (i, j, 0)
    nxt = lambda i, j: (i, jnp.minimum(j + 1, nb - 1), 0)
    const2 = lambda i, j: (0, 0)
    kv = lambda m: pl.BlockSpec((1, ATT_BLOCK, kvw), m)
    return pl.pallas_call(
        _attn_kernel,
        out_shape=jax.ShapeDtypeStruct((b, s, w), BF16),
        grid=(b, nb),
        in_specs=[
            pl.BlockSpec((1, ATT_BLOCK, w), cur),
            kv(prev), kv(cur), kv(nxt), kv(prev), kv(cur), kv(nxt),
            pl.BlockSpec(kmeta.shape, const2),
            pl.BlockSpec(vmeta.shape, const2),
            pl.BlockSpec((1,) + bias.shape[1:], lambda i, j: (jnp.minimum(j, 1), 0, 0, 0)),
            pl.BlockSpec(memory_space=pltpu.SMEM),
        ],
        out_specs=pl.BlockSpec((1, ATT_BLOCK, w), cur),
        compiler_params=_cparams(("arbitrary", "arbitrary")),
        name="attention",
    )(qa, ka, ka, ka, va, va, va, kmeta, vmeta, bias, sink)


def _outproj_kernel(or_ref, oa_ref, x_ref, w_ref, nw_ref, h_ref, xn_ref):
    half = or_ref.shape[-1]
    h = x_ref[0]
    h = h + jnp.dot(or_ref[0], w_ref[:half], preferred_element_type=F32)
    h = h + jnp.dot(oa_ref[0], w_ref[half:], preferred_element_type=F32)
    h_ref[0] = h
    ms = jnp.mean(h * h, axis=-1, keepdims=True)
    xn_ref[0] = h * lax.rsqrt(ms + RMS_EPS) * nw_ref[...]


def _outproj(o_r, o_a, x, w_out_bf, norm_w, tm):
    b, s, d = x.shape
    half = o_r.shape[-1]
    row = lambda i, j: (i, j, 0)
    const2 = lambda i, j: (0, 0)
    return pl.pallas_call(
        _outproj_kernel,
        out_shape=[jax.ShapeDtypeStruct((b, s, d), F32)] * 2,
        grid=(b, s // tm),
        in_specs=[
            pl.BlockSpec((1, tm, half), row),
            pl.BlockSpec((1, tm, half), row),
            pl.BlockSpec((1, tm, d), row),
            pl.BlockSpec(w_out_bf.shape, const2),
            pl.BlockSpec((1, d), const2),
        ],
        out_specs=[pl.BlockSpec((1, tm, d), row)] * 2,
        compiler_params=_cparams(("arbitrary", "arbitrary")),
        name="outproj",
    )(o_r, o_a, x, w_out_bf, norm_w)


def _top16_rows(s, iota):
    nrows = s.shape[0]
    vals, ids = [], []
    for _ in range(PEER_TOPK):
        m = jnp.max(s, axis=0, keepdims=True)
        am = jnp.min(jnp.where(s == m, iota, nrows), axis=0, keepdims=True)
        vals.append(m)
        ids.append(am)
        s = jnp.where(iota == am, -jnp.inf, s)
    return jnp.concatenate(vals, axis=0), jnp.concatenate(ids, axis=0)


_PAIR_GROUPS = ((0, 0, 8), (0, 8, 8), (1, 0, 8), (2, 0, 5), (3, 0, 4), (4, 0, 3), (5, 0, 2), (6, 0, 2), (7, 0, 2))


def _route_kernel(x_ref, wq_ref, pk_ref, idx_ref, gate_ref, sc_ref):
    tm = x_ref.shape[1]
    q = jnp.dot(x_ref[0].astype(BF16), wq_ref[...], preferred_element_type=F32).astype(BF16)
    contract1 = (((1,), (1,)), ((), ()))
    for hp in range(2 * PEER_HEADS):
        sc_ref[hp] = lax.dot_general(pk_ref[hp], q[:, hp * PEER_NKEYS:(hp + 1) * PEER_NKEYS], contract1,
                                     preferred_element_type=F32)
    iota = lax.broadcasted_iota(jnp.int32, (PEER_NKEYS, LANES), 0)
    sub = lax.broadcasted_iota(jnp.int32, (SUBLANES, LANES), 0)
    far = PEER_TOPK * PEER_TOPK

    def head_body(h, carry):
        for lt in range(tm // LANES):
            lanes = slice(lt * LANES, (lt + 1) * LANES)
            s1, i1 = _top16_rows(sc_ref[2 * h, :, lanes], iota)
            s2, i2 = _top16_rows(sc_ref[2 * h + 1, :, lanes], iota)
            i1 = i1 * PEER_NKEYS
            sc, ex, pos = [], [], []
            for i, j0, cnt in _PAIR_GROUPS:
                ok = sub < cnt
                sc.append(jnp.where(ok, s1[i:i + 1] + s2[j0:j0 + SUBLANES], -jnp.inf))
                ex.append(i1[i:i + 1] + i2[j0:j0 + SUBLANES])
                pos.append(jnp.where(ok, i * PEER_TOPK + j0 + sub, far))
            sc.append(s1[SUBLANES:] + s2[0:1])
            ex.append(i1[SUBLANES:] + i2[0:1])
            pos.append((SUBLANES + sub) * PEER_TOPK)
            sc = jnp.concatenate(sc, axis=0)
            ex = jnp.concatenate(ex, axis=0)
            pos = jnp.concatenate(pos, axis=0)
            top, experts = [], []
            for _ in range(PEER_TOPK):
                m = jnp.max(sc, axis=0, keepdims=True)
                pm = jnp.min(jnp.where(sc == m, pos, far), axis=0, keepdims=True)
                hit = pos == pm
                experts.append(jnp.max(jnp.where(hit, ex, -1), axis=0, keepdims=True))
                top.append(m)
                sc = jnp.where(hit, -jnp.inf, sc)
            top = jnp.concatenate(top, axis=0)
            e = jnp.exp(top - top[0:1])
            rows = pl.ds(pl.multiple_of(h * PEER_TOPK, PEER_TOPK), PEER_TOPK)
            gate_ref[0, rows, lanes] = e / jnp.sum(e, axis=0, keepdims=True)
            idx_ref[0, rows, lanes] = jnp.concatenate(experts, axis=0)
        return carry

    lax.fori_loop(0, PEER_HEADS, head_body, 0)


def _route(xn, wq_bf, pk_bf, tm):
    b, s, d = xn.shape
    const2 = lambda i, j: (0, 0)
    slot = lambda i, j: (i, 0, j)
    return pl.pallas_call(
        _route_kernel,
        out_shape=[jax.ShapeDtypeStruct((b, PEER_SLOTS, s), jnp.int32),
                   jax.ShapeDtypeStruct((b, PEER_SLOTS, s), F32)],
        grid=(b, s // tm),
        in_specs=[
            pl.BlockSpec((1, tm, d), lambda i, j: (i, j, 0)),
            pl.BlockSpec(wq_bf.shape, const2),
            pl.BlockSpec(pk_bf.shape, lambda i, j: (0, 0, 0)),
        ],
        out_specs=[pl.BlockSpec((1, PEER_SLOTS, tm), slot)] * 2,
        scratch_shapes=[pltpu.VMEM((2 * PEER_HEADS, PEER_NKEYS, tm), F32)],
        compiler_params=_cparams(("arbitrary", "arbitrary")),
        name="route",
    )(xn, wq_bf, pk_bf)


def _pair_slots():
    out = []
    for g in range(PEER_SLOTS // SUBLANES):
        for c in range(4):
            top = SUBLANES * g + SLOT_ORDER[c]
            out.append((top, top + 4))
    return out


def _gather_pair(tbl_ref, top_off, bot_off, upper):
    ra = tbl_ref[pl.ds(pl.multiple_of(top_off, ROWS_PER_EXPERT), SUBLANES), :]
    rb = tbl_ref[pl.ds(pl.multiple_of(bot_off, ROWS_PER_EXPERT), SUBLANES), :]
    row = jnp.where(upper, ra, rb)
    lo = lax.bitcast_convert_type(row << 16, F32)
    hi = lax.bitcast_convert_type(row & jnp.uint32(0xFFFF0000), F32)
    return lo, hi


def _merge_sublanes(a, b, h, sub):
    keep = (sub & h) == 0
    return jnp.where(keep, a + pltpu.roll(a, SUBLANES - h, 0), b + pltpu.roll(b, h, 0))


def _slot_specs(tb):
    return [pl.BlockSpec((None, None, SLOT_GROUP * tb), lambda i, j=j: (j, 0, i), memory_space=pltpu.SMEM,
                         pipeline_mode=pl.Buffered(1)) for j in range(PEER_SLOTS // SLOT_GROUP)]


def _group_slots(a, tb):
    n = a.shape[-1]
    a = a.reshape(PEER_SLOTS // SLOT_GROUP, SLOT_GROUP, n // tb, tb).transpose(0, 2, 1, 3)
    return a.reshape(PEER_SLOTS // SLOT_GROUP, 1, n * SLOT_GROUP)


def _slot_reader(refs, t, tb):
    at = [t + g * tb for g in range(SLOT_GROUP)]
    return lambda k: refs[k // SLOT_GROUP][at[k % SLOT_GROUP]]


def _peer_u_kernel(*refs):
    ngroups = PEER_SLOTS // SLOT_GROUP
    offs = refs[:ngroups]
    x_ref, tbl_ref, o_ref, z_ref, sel_ref = refs[ngroups:]
    tb = x_ref.shape[0] // SUBLANES
    sub = lax.broadcasted_iota(jnp.int32, (SUBLANES, LANES), 0)
    upper = sub < ROWS_PER_EXPERT
    pairs = _pair_slots()
    lane_pair = lax.broadcasted_iota(jnp.int32, (PEER_SLOTS, LANES), 1) >> 1

    row = lax.broadcasted_iota(jnp.int32, (2 * LANES, LANES), 0)
    col = lax.broadcasted_iota(jnp.int32, (2 * LANES, LANES), 1)
    sel_ref[...] = jnp.where((col & 1) == (row >= LANES).astype(jnp.int32), 1.0, 0.0).astype(BF16)
    o_ref[...] = jnp.zeros_like(o_ref)

    def token_group(gi, zbase):
        for j in range(SUBLANES):
            t = gi * SUBLANES + j
            x = x_ref[pl.ds(pl.multiple_of(t * SUBLANES, SUBLANES), SUBLANES), :]
            swapped = pltpu.roll(x, ROWS_PER_EXPERT, 0)
            xa = jnp.where(upper, x, swapped)
            xb = jnp.where(upper, swapped, x)
            off = _slot_reader(offs, t, tb)
            zs = []
            for g in range(PEER_SLOTS // SUBLANES):
                ps = []
                for c in range(4):
                    top, bot = pairs[4 * g + c]
                    lo, hi = _gather_pair(tbl_ref, off(top), off(bot), upper)
                    ps.append(lo * xa + hi * xb)
                r0 = _merge_sublanes(ps[0], ps[1], 2, sub)
                r1 = _merge_sublanes(ps[2], ps[3], 2, sub)
                zs.append(_merge_sublanes(r0, r1, 1, sub))
            rows = zbase + (j // 2) * PEER_SLOTS
            cols = slice((j % 2) * LANES, (j % 2 + 1) * LANES)
            for m in range(len(zs) // 2):
                tile = jnp.concatenate([zs[2 * m], zs[2 * m + 1]], axis=0).astype(BF16)
                z_ref[rows + 2 * SUBLANES * m:rows + 2 * SUBLANES * (m + 1), cols] = tile
        sums = jnp.dot(z_ref[zbase:zbase + 4 * PEER_SLOTS, :], sel_ref[...], preferred_element_type=F32)
        block = pl.ds(pl.multiple_of((gi // (LANES // SUBLANES)) * LANES, LANES), LANES)
        first_pair = (gi % (LANES // SUBLANES)) * 4
        acc = o_ref[:, block]
        for p in range(4):
            acc = jnp.where(lane_pair == first_pair + p, sums[p * PEER_SLOTS:(p + 1) * PEER_SLOTS], acc)
        o_ref[:, block] = acc

    def step(i, carry):
        for r in range(U_GROUPS_PER_ITER):
            token_group(i * U_GROUPS_PER_ITER + r, r * 4 * PEER_SLOTS)
        return carry

    lax.fori_loop(0, tb // (SUBLANES * U_GROUPS_PER_ITER), step, 0)


def _peer_u(offs_g, x_rows, tbl, tb):
    n = x_rows.shape[0] // SUBLANES
    assert tb % LANES == 0 and (tb // SUBLANES) % U_GROUPS_PER_ITER == 0
    return pl.pallas_call(
        _peer_u_kernel,
        out_shape=jax.ShapeDtypeStruct((PEER_SLOTS, n), F32),
        grid=(n // tb,),
        in_specs=_slot_specs(tb) + [
            pl.BlockSpec((tb * SUBLANES, LANES), lambda i: (i, 0)),
            pl.BlockSpec(tbl.shape, lambda i: (0, 0), pipeline_mode=pl.Buffered(1)),
        ],
        out_specs=pl.BlockSpec((PEER_SLOTS, tb), lambda i: (0, i)),
        scratch_shapes=[
            pltpu.VMEM((U_GROUPS_PER_ITER * 4 * PEER_SLOTS, 2 * LANES), BF16),
            pltpu.VMEM((2 * LANES, LANES), BF16),
        ],
        compiler_params=_cparams(("arbitrary",)),
        name="peer_u",
    )(*([offs_g] * (PEER_SLOTS // SLOT_GROUP)), x_rows, tbl)


def _peer_w_kernel(gate_ref, act_ref, w_ref):
    a = act_ref[...]
    w_ref[...] = gate_ref[...] * (0.5 * a * (1.0 + lax.erf(a * (2.0 ** -0.5))))


def _peer_w(gate_t, act_t, tm):
    n = gate_t.shape[-1]
    spec = pl.BlockSpec((PEER_SLOTS, tm), lambda i: (0, i))
    return pl.pallas_call(
        _peer_w_kernel,
        out_shape=jax.ShapeDtypeStruct(gate_t.shape, F32),
        grid=(n // tm,),
        in_specs=[spec, spec],
        out_specs=spec,
        compiler_params=_cparams(("arbitrary",)),
        name="peer_w",
    )(gate_t, act_t)


def _peer_v_kernel(*refs):
    ngroups = PEER_SLOTS // SLOT_GROUP
    offs = refs[:ngroups]
    wts = refs[ngroups:2 * ngroups]
    h_ref, tbl_ref, nw_ref, o_ref = refs[2 * ngroups:]
    tb = h_ref.shape[0] // SUBLANES
    sub = lax.broadcasted_iota(jnp.int32, (SUBLANES, LANES), 0)
    upper = sub < ROWS_PER_EXPERT
    pairs = _pair_slots()

    def token(t):
        off = _slot_reader(offs, t, tb)
        wt = _slot_reader(wts, t, tb)
        acc = [jnp.zeros((SUBLANES, LANES), F32) for _ in range(4)]
        for q, (top, bot) in enumerate(pairs):
            lo, hi = _gather_pair(tbl_ref, off(top), off(bot), upper)
            w = jnp.where(upper, wt(top), wt(bot))
            k = 2 * (q % 2)
            acc[k] = acc[k] + lo * w
            acc[k + 1] = acc[k + 1] + hi * w
        lo = acc[0] + acc[2]
        hi = acc[1] + acc[3]
        lo = lo + pltpu.roll(lo, ROWS_PER_EXPERT, 0)
        hi = hi + pltpu.roll(hi, ROWS_PER_EXPERT, 0)
        rows = pl.ds(pl.multiple_of(t * SUBLANES, SUBLANES), SUBLANES)
        o_ref[rows, :] = h_ref[rows, :] + jnp.where(upper, lo, hi)

    def step(i, carry):
        for r in range(V_TOKENS_PER_ITER):
            token(i * V_TOKENS_PER_ITER + r)
        return carry

    lax.fori_loop(0, tb // V_TOKENS_PER_ITER, step, 0)

    def norm_step(i, carry):
        rows = pl.ds(pl.multiple_of(i * NORM_TOKENS * SUBLANES, NORM_TOKENS * SUBLANES), NORM_TOKENS * SUBLANES)
        h = o_ref[rows, :].reshape(NORM_TOKENS, SUBLANES, LANES)
        ss = jnp.sum(jnp.sum(h * h, axis=2, keepdims=True), axis=1, keepdims=True)
        y = h * lax.rsqrt(ss * (1.0 / D_MODEL) + RMS_EPS) * nw_ref[...][None]
        o_ref[rows, :] = y.reshape(NORM_TOKENS * SUBLANES, LANES)
        return carry

    lax.fori_loop(0, tb // NORM_TOKENS, norm_step, 0)


def _peer_v(offs_g, w_g, h_rows, tbl, norm_rows, tb):
    n = h_rows.shape[0] // SUBLANES
    assert tb % V_TOKENS_PER_ITER == 0 and tb % NORM_TOKENS == 0
    rows = pl.BlockSpec((tb * SUBLANES, LANES), lambda i: (i, 0))
    return pl.pallas_call(
        _peer_v_kernel,
        out_shape=jax.ShapeDtypeStruct(h_rows.shape, F32),
        grid=(n // tb,),
        in_specs=_slot_specs(tb) + _slot_specs(tb) + [
            rows,
            pl.BlockSpec(tbl.shape, lambda i: (0, 0), pipeline_mode=pl.Buffered(1)),
            pl.BlockSpec((SUBLANES, LANES), lambda i: (0, 0)),
        ],
        out_specs=rows,
        compiler_params=_cparams(("arbitrary",)),
        name="peer_v",
    )(*([offs_g] * (PEER_SLOTS // SLOT_GROUP)), *([w_g] * (PEER_SLOTS // SLOT_GROUP)), h_rows, tbl, norm_rows)


def _pack_table(t):
    half = D_MODEL // 2
    bits = lax.bitcast_convert_type(t.astype(BF16), jnp.uint16).astype(jnp.uint32)
    words = bits[:, :half] | (bits[:, half:] << 16)
    words = words.reshape(t.shape[0] * ROWS_PER_EXPERT, LANES)
    return jnp.pad(words, ((TABLE_PAD, TABLE_PAD), (0, 0)))


def _t5_bucket(rel):
    half = N_BUCKETS // 2
    exact = half // 2
    n = jnp.abs(rel)
    large = exact + (jnp.log(jnp.maximum(n, 1).astype(F32) / exact)
                     / math.log(MAX_DISTANCE / exact) * (half - exact)).astype(jnp.int32)
    large = jnp.minimum(large, half - 1)
    return jnp.where(rel > 0, half, 0) + jnp.where(n < exact, n, large)


def _attention_bias(rel_bias):
    rb = rel_bias.astype(F32)
    i = jnp.arange(ATT_BLOCK)
    jb = jnp.arange(3 * ATT_BLOCK)
    rel = (jb[None, :] - ATT_BLOCK) - i[:, None]
    band = jnp.where((jnp.abs(rel) <= ATT_BLOCK)[..., None], rb[_t5_bucket(rel)], NEG)
    mpos = PREFIX - N_META + jnp.arange(N_META)
    variants = []
    for blk in (0, 1):
        qpos = PREFIX + blk * ATT_BLOCK + i
        meta = rb[_t5_bucket(mpos[None, :] - qpos[:, None])]
        fill = jnp.full((ATT_BLOCK, ATT_BLOCK - N_META, ATT_HEADS), NEG, F32)
        variants.append(jnp.concatenate([band, meta, fill], axis=1).transpose(2, 0, 1))
    return jnp.stack(variants)


def _rope_tables(pos):
    half = RET_D // 2
    inv = ROPE_BASE ** (-jnp.arange(half, dtype=F32) / half)
    ang = pos.astype(F32)[:, None] * inv[None, :]
    cos, sin = jnp.cos(ang), jnp.sin(ang)
    return jnp.concatenate([cos, cos], axis=1), jnp.concatenate([-sin, sin], axis=1)


def _decay_tables(dec_f, dec_b, chunk):
    lf = jax.nn.log_sigmoid(dec_f.astype(F32))[:, None]
    lb = jax.nn.log_sigmoid(dec_b.astype(F32))[:, None]
    idx = jnp.arange(chunk, dtype=F32)
    diff = idx[:, None] - idx[None, :]
    dmat = (jnp.where(diff >= 0, jnp.exp(jnp.maximum(diff, 0.0)[None] * lf[:, :, None]), 0.0)
            + jnp.where(diff < 0, jnp.exp(jnp.maximum(-diff, 0.0)[None] * lb[:, :, None]), 0.0))
    bc = lambda v: jnp.broadcast_to(v[:, :, None], v.shape + (RET_D,))
    qwf = bc(jnp.exp((idx + 1.0)[None] * lf))
    kwf = bc(jnp.exp((chunk - 1.0 - idx)[None] * lf))
    qwb = bc(jnp.exp((chunk - idx)[None] * lb))
    kwb = bc(jnp.exp(idx[None] * lb))
    pidx = jnp.arange(PREFIX, dtype=F32)
    kwp = bc(jnp.exp((PREFIX - 1.0 - pidx)[None] * lf))
    gdec = jnp.concatenate([jnp.exp(chunk * lf[:, 0]), jnp.exp(chunk * lb[:, 0])])
    return dmat, qwf, qwb, kwf, kwb, kwp, gdec


def _divisor_tile(n, want):
    t = min(n, want)
    while n % t:
        t //= 2
    return t


def _encode(x, shared):
    b, s, d = x.shape
    n = b * s
    tm = _divisor_tile(s, 512)
    chunk = _divisor_tile(s, 256)
    cos_t, sin_t = _rope_tables(jnp.arange(s) + N_META)
    qr, kr, vr, gr, qa, ka, va = _proj(x, shared["norm_mix"], shared["w_in"], cos_t, sin_t, tm)
    o_r = _retention(qr, kr, vr, gr, shared["kr_pre"], shared["vr_pre"], *_decay_tables(
        shared["dec_f"], shared["dec_b"], chunk), shared["gn_w"], chunk)
    o_a = _attention(qa, ka, va, shared["k_meta"], shared["v_meta"], shared["bias"], shared["sink"])
    h1, xn2 = _outproj(o_r, o_a, x, shared["w_out"], shared["norm_ffn"], tm)
    idx_t, gate_t = _route(xn2, shared["wq"], shared["pkeys"], _divisor_tile(s, ROUTE_TOKENS))
    idx_t = idx_t.transpose(1, 0, 2).reshape(PEER_SLOTS, n)
    gate_t = gate_t.transpose(1, 0, 2).reshape(PEER_SLOTS, n)
    top_half = (jnp.arange(PEER_SLOTS) % SUBLANES) < ROWS_PER_EXPERT
    tb = _divisor_tile(n, PEER_TOKENS)
    offs_g = _group_slots(idx_t * ROWS_PER_EXPERT + jnp.where(top_half, TABLE_PAD, 0)[:, None], tb)
    act_t = _peer_u(offs_g, xn2.reshape(n * SUBLANES, LANES), shared["u_tbl"], tb)
    w_g = _group_slots(_peer_w(gate_t, act_t, _divisor_tile(n, 2048)), tb)
    y = _peer_v(offs_g, w_g, h1.reshape(n * SUBLANES, LANES), shared["v_tbl"], shared["norm_final"], tb)
    return y.reshape(b, s, d)


def kernel(x_prompt, x_sample, meta_tokens, norm_mix_w, w_in, ret_decay_fwd, ret_decay_bwd, ret_gn_w,
           attn_sink, rel_bias, w_out, norm_ffn_w, peer_wq, peer_keys, peer_u, peer_v, norm_final_w):
    layer = 0
    shared = {
        "norm_mix": norm_mix_w[layer][None, :].astype(F32),
        "w_in": w_in[layer].astype(BF16),
        "dec_f": ret_decay_fwd[layer],
        "dec_b": ret_decay_bwd[layer],
        "gn_w": ret_gn_w[layer][None, :].astype(F32),
        "sink": attn_sink[layer].astype(F32),
        "bias": _attention_bias(rel_bias),
        "w_out": w_out[layer].astype(BF16),
        "norm_ffn": norm_ffn_w[layer][None, :].astype(F32),
        "wq": peer_wq[layer].astype(BF16),
        "pkeys": peer_keys[layer].reshape(2 * PEER_HEADS, PEER_NKEYS, PEER_NKEYS).astype(BF16),
        "u_tbl": _pack_table(peer_u[layer]),
        "v_tbl": _pack_table(peer_v[layer]),
        "norm_final": norm_final_w.reshape(SUBLANES, LANES).astype(F32),
    }
    prefix = jnp.concatenate([jnp.zeros((PREFIX - N_META, D_MODEL), x_prompt.dtype),
                              meta_tokens.astype(x_prompt.dtype)], axis=0)[None]
    cos_p, sin_p = _rope_tables(jnp.arange(PREFIX) - (PREFIX - N_META))
    _, kr_p, vr_p, _, _, ka_p, va_p = _proj(prefix, shared["norm_mix"], shared["w_in"], cos_p, sin_p, PREFIX)
    shared["kr_pre"], shared["vr_pre"] = kr_p[0], vr_p[0]
    pad_meta = lambda t: jnp.pad(t[0, PREFIX - N_META:], ((0, ATT_BLOCK - N_META), (0, 0)))
    shared["k_meta"], shared["v_meta"] = pad_meta(ka_p), pad_meta(va_p)
    return (_encode(x_prompt, shared), _encode(x_sample, shared))
```

```python
import functools
import math

import jax
import jax.numpy as jnp
from jax import lax
from jax.experimental import pallas as pl
from jax.experimental.pallas import tpu as pltpu

F32 = jnp.float32
BF16 = jnp.bfloat16

D_MODEL = 1024
N_META = 16
PREFIX = 128
RET_HEADS = 4
RET_D = 128
ATT_HEADS = 8
ATT_KV = 2
ATT_GROUP = ATT_HEADS // ATT_KV
ATT_HD = 64
ATT_BLOCK = 128
N_BUCKETS = 32
MAX_DISTANCE = 128
ROPE_BASE = 10000.0
PEER_HEADS = 8
PEER_NKEYS = 128
PEER_EXPERTS = PEER_NKEYS * PEER_NKEYS
PEER_TOPK = 16
PEER_SLOTS = PEER_HEADS * PEER_TOPK
RMS_EPS = 1e-6
GN_EPS = 1e-5
NEG = -1e30
IN_COLS = (0, 512, 1024, 1536, 2048, 2560, 2688, 2816)

LANES = 128
SUBLANES = 8
VMEM_LIMIT = 56 * 1024 * 1024
ROWS_PER_EXPERT = D_MODEL // 2 // LANES
TABLE_PAD = ROWS_PER_EXPERT
PEER_TOKENS = 256
ROUTE_TOKENS = 256
SLOT_ORDER = (0, 2, 1, 3)
SLOT_GROUP = 8
U_GROUPS_PER_ITER = 2
V_TOKENS_PER_ITER = 4
NORM_TOKENS = 32


def _cparams(sem):
    return pltpu.CompilerParams(dimension_semantics=sem, vmem_limit_bytes=VMEM_LIMIT)


def _proj_kernel(x_ref, nw_ref, w_ref, cos_ref, sin_ref,
                 qr_ref, kr_ref, vr_ref, gr_ref, qa_ref, ka_ref, va_ref):
    x = x_ref[0]
    ms = jnp.mean(x * x, axis=-1, keepdims=True)
    xn = (x * lax.rsqrt(ms + RMS_EPS) * nw_ref[...]).astype(BF16)
    cosf = cos_ref[...]
    sinf = sin_ref[...]

    def mm(i):
        return jnp.dot(xn, w_ref[:, IN_COLS[i]:IN_COLS[i + 1]], preferred_element_type=F32)

    def rotary(t, scale):
        for h in range(RET_HEADS):
            th = t[:, h * RET_D:(h + 1) * RET_D]
            yield h, (th * cosf + pltpu.roll(th, RET_D // 2, 1) * sinf) * scale

    for h, r in rotary(mm(0), 1.0):
        qr_ref[0, :, h * RET_D:(h + 1) * RET_D] = r.astype(BF16)
    for h, r in rotary(mm(1), RET_D ** -0.5):
        kr_ref[0, :, h * RET_D:(h + 1) * RET_D] = r.astype(BF16)
    vr_ref[0] = mm(2).astype(BF16)
    gr_ref[0] = mm(3)
    qa_ref[0] = (mm(4) * (ATT_HD ** -0.5)).astype(BF16)
    ka_ref[0] = mm(5).astype(BF16)
    va_ref[0] = mm(6).astype(BF16)


def _proj(x, norm_w, w_in_bf, cos_t, sin_t, tm):
    b, s, d = x.shape
    widths = (512, 512, 512, 512, 512, 128, 128)
    dtypes = (BF16, BF16, BF16, F32, BF16, BF16, BF16)
    return pl.pallas_call(
        _proj_kernel,
        out_shape=[jax.ShapeDtypeStruct((b, s, w), dt) for w, dt in zip(widths, dtypes)],
        grid=(b, s // tm),
        in_specs=[
            pl.BlockSpec((1, tm, d), lambda i, j: (i, j, 0)),
            pl.BlockSpec((1, d), lambda i, j: (0, 0)),
            pl.BlockSpec(w_in_bf.shape, lambda i, j: (0, 0)),
            pl.BlockSpec((tm, LANES), lambda i, j: (j, 0)),
            pl.BlockSpec((tm, LANES), lambda i, j: (j, 0)),
        ],
        out_specs=[pl.BlockSpec((1, tm, w), lambda i, j: (i, j, 0)) for w in widths],
        compiler_params=_cparams(("arbitrary", "arbitrary")),
        name="proj",
    )(x, norm_w, w_in_bf, cos_t, sin_t)


def _ret_kernel(q_ref, k_ref, v_ref, g_ref, kpre_ref, vpre_ref, dmat_ref, qwf_ref, qwb_ref,
                kwf_ref, kwb_ref, kwp_ref, gdec_ref, gnw_ref, o_ref, sf_ref, sb_ref, sball_ref):
    ph = pl.program_id(1)
    c = pl.program_id(2)
    nc = pl.num_programs(2)
    contract0 = (((0,), (0,)), ((), ()))
    contract1 = (((1,), (1,)), ((), ()))

    def head(ref, h):
        return ref[0, :, h * RET_D:(h + 1) * RET_D]

    def kv_update(kh, vh, kw):
        kw = (kh.astype(F32) * kw).astype(BF16)
        return lax.dot_general(kw, vh, contract0, preferred_element_type=F32)

    @pl.when(ph == 0)
    def _backward_states():
        @pl.when(c == 0)
        def _():
            sb_ref[...] = jnp.zeros_like(sb_ref)

        j = nc - 1 - c
        for h in range(RET_HEADS):
            sball_ref[j, h] = sb_ref[h].astype(BF16)
            sb_ref[h] = gdec_ref[RET_HEADS + h] * sb_ref[h] + kv_update(head(k_ref, h), head(v_ref, h), kwb_ref[h])

    @pl.when(ph == 1)
    def _outputs():
        @pl.when(c == 0)
        def _():
            for h in range(RET_HEADS):
                kp = kpre_ref[:, h * RET_D:(h + 1) * RET_D]
                vp = vpre_ref[:, h * RET_D:(h + 1) * RET_D]
                sf_ref[h] = kv_update(kp, vp, kwp_ref[h])

        for h in range(RET_HEADS):
            qh, kh, vh = head(q_ref, h), head(k_ref, h), head(v_ref, h)
            s = lax.dot_general(qh, kh, contract1, preferred_element_type=F32) * dmat_ref[h]
            o = jnp.dot(s.astype(BF16), vh, preferred_element_type=F32)
            o = o + jnp.dot(qh, sf_ref[h].astype(BF16), preferred_element_type=F32) * qwf_ref[h]
            o = o + jnp.dot(qh, sball_ref[c, h], preferred_element_type=F32) * qwb_ref[h]
            mu = jnp.mean(o, axis=-1, keepdims=True)
            oc = o - mu
            var = jnp.mean(oc * oc, axis=-1, keepdims=True)
            on = oc * lax.rsqrt(var + GN_EPS) * gnw_ref[:, h * RET_D:(h + 1) * RET_D]
            g = head(g_ref, h)
            o_ref[0, :, h * RET_D:(h + 1) * RET_D] = (g * jax.nn.sigmoid(g) * on).astype(BF16)
            sf_ref[h] = gdec_ref[h] * sf_ref[h] + kv_update(kh, vh, kwf_ref[h])


def _retention(qr, kr, vr, gr, kpre, vpre, dmat, qwf, qwb, kwf, kwb, kwp, gdec, gnw, chunk):
    b, s, w = qr.shape
    nc = s // chunk

    def seq_map(i, ph, c):
        return (i, jnp.where(ph == 0, nc - 1 - c, c), 0)

    def fwd_only(i, ph, c):
        return (i, c * ph, 0)

    const2 = lambda i, ph, c: (0, 0)
    const3 = lambda i, ph, c: (0, 0, 0)
    return pl.pallas_call(
        _ret_kernel,
        out_shape=jax.ShapeDtypeStruct((b, s, w), BF16),
        grid=(b, 2, nc),
        in_specs=[
            pl.BlockSpec((1, chunk, w), fwd_only),
            pl.BlockSpec((1, chunk, w), seq_map),
            pl.BlockSpec((1, chunk, w), seq_map),
            pl.BlockSpec((1, chunk, w), fwd_only),
            pl.BlockSpec(kpre.shape, const2),
            pl.BlockSpec(vpre.shape, const2),
            pl.BlockSpec(dmat.shape, const3),
            pl.BlockSpec(qwf.shape, const3),
            pl.BlockSpec(qwb.shape, const3),
            pl.BlockSpec(kwf.shape, const3),
            pl.BlockSpec(kwb.shape, const3),
            pl.BlockSpec(kwp.shape, const3),
            pl.BlockSpec(memory_space=pltpu.SMEM),
            pl.BlockSpec(gnw.shape, const2),
        ],
        out_specs=pl.BlockSpec((1, chunk, w), fwd_only),
        scratch_shapes=[
            pltpu.VMEM((RET_HEADS, RET_D, RET_D), F32),
            pltpu.VMEM((RET_HEADS, RET_D, RET_D), F32),
            pltpu.VMEM((nc, RET_HEADS, RET_D, RET_D), BF16),
        ],
        compiler_params=_cparams(("arbitrary", "arbitrary", "arbitrary")),
        name="retention",
    )(qr, kr, vr, gr, kpre, vpre, dmat, qwf, qwb, kwf, kwb, kwp, gdec, gnw)


def _attn_kernel(q_ref, kp_ref, kc_ref, kn_ref, vp_ref, vc_ref, vn_ref, km_ref, vm_ref,
                 bias_ref, sink_ref, o_ref):
    j = pl.program_id(1)
    nb = pl.num_programs(1)
    lane = lax.broadcasted_iota(jnp.int32, (1, 4 * ATT_BLOCK), 1)
    first = jnp.logical_and(lane < ATT_BLOCK, j == 0)
    last = jnp.logical_and(jnp.logical_and(lane >= 2 * ATT_BLOCK, lane < 3 * ATT_BLOCK), j == nb - 1)
    pen = jnp.where(jnp.logical_or(first, last), NEG, 0.0).astype(F32)
    contract1 = (((1,), (1,)), ((), ()))
    outs = []
    for g in range(ATT_KV):
        sl = slice(g * ATT_HD, (g + 1) * ATT_HD)
        kall = jnp.concatenate([kp_ref[0][:, sl], kc_ref[0][:, sl], kn_ref[0][:, sl], km_ref[:, sl]], axis=0)
        vall = jnp.concatenate([vp_ref[0][:, sl], vc_ref[0][:, sl], vn_ref[0][:, sl], vm_ref[:, sl]], axis=0)
        for hh in range(ATT_GROUP):
            h = g * ATT_GROUP + hh
            q = q_ref[0, :, h * ATT_HD:(h + 1) * ATT_HD]
            s = lax.dot_general(q, kall, contract1, preferred_element_type=F32)
            s = s + bias_ref[0, h] + pen
            snk = sink_ref[h]
            m = jnp.maximum(jnp.max(s, axis=-1, keepdims=True), snk)
            p = jnp.exp(s - m)
            den = jnp.sum(p, axis=-1, keepdims=True) + jnp.exp(snk - m)
            o = jnp.dot(p.astype(BF16), vall, preferred_element_type=F32)
            outs.append(o / den)
    for h2 in range(ATT_HEADS // 2):
        o_ref[0, :, h2 * LANES:(h2 + 1) * LANES] = jnp.concatenate(
            [outs[2 * h2], outs[2 * h2 + 1]], axis=-1).astype(BF16)


def _attention(qa, ka, va, kmeta, vmeta, bias, sink):
    b, s, w = qa.shape
    nb = s // ATT_BLOCK
    kvw = ka.shape[-1]
    prev = lambda i, j: (i, jnp.maximum(j - 1, 0), 0)
    cur = lambda i, j: (i, j, 0)
    nxt = lambda i, j: (i, jnp.minimum(j + 1, nb - 1), 0)
    const2 = lambda i, j: (0, 0)
    kv = lambda m: pl.BlockSpec((1, ATT_BLOCK, kvw), m)
    return pl.pallas_call(
        _attn_kernel,
        out_shape=jax.ShapeDtypeStruct((b, s, w), BF16),
        grid=(b, nb),
        in_specs=[
            pl.BlockSpec((1, ATT_BLOCK, w), cur),
            kv(prev), kv(cur), kv(nxt), kv(prev), kv(cur), kv(nxt),
            pl.BlockSpec(kmeta.shape, const2),
            pl.BlockSpec(vmeta.shape, const2),
            pl.BlockSpec((1,) + bias.shape[1:], lambda i, j: (jnp.minimum(j, 1), 0, 0, 0)),
            pl.BlockSpec(memory_space=pltpu.SMEM),
        ],
        out_specs=pl.BlockSpec((1, ATT_BLOCK, w), cur),
        compiler_params=_cparams(("arbitrary", "arbitrary")),
        name="attention",
    )(qa, ka, ka, ka, va, va, va, kmeta, vmeta, bias, sink)


def _outproj_kernel(or_ref, oa_ref, x_ref, w_ref, nw_ref, h_ref, xn_ref):
    half = or_ref.shape[-1]
    h = x_ref[0]
    h = h + jnp.dot(or_ref[0], w_ref[:half], preferred_element_type=F32)
    h = h + jnp.dot(oa_ref[0], w_ref[half:], preferred_element_type=F32)
    h_ref[0] = h
    ms = jnp.mean(h * h, axis=-1, keepdims=True)
    xn_ref[0] = h * lax.rsqrt(ms + RMS_EPS) * nw_ref[...]


def _outproj(o_r, o_a, x, w_out_bf, norm_w, tm):
    b, s, d = x.shape
    half = o_r.shape[-1]
    row = lambda i, j: (i, j, 0)
    const2 = lambda i, j: (0, 0)
    return pl.pallas_call(
        _outproj_kernel,
        out_shape=[jax.ShapeDtypeStruct((b, s, d), F32)] * 2,
        grid=(b, s // tm),
        in_specs=[
            pl.BlockSpec((1, tm, half), row),
            pl.BlockSpec((1, tm, half), row),
            pl.BlockSpec((1, tm, d), row),
            pl.BlockSpec(w_out_bf.shape, const2),
            pl.BlockSpec((1, d), const2),
        ],
        out_specs=[pl.BlockSpec((1, tm, d), row)] * 2,
        compiler_params=_cparams(("arbitrary", "arbitrary")),
        name="outproj",
    )(o_r, o_a, x, w_out_bf, norm_w)


def _top16_rows(s, iota):
    nrows = float(s.shape[0])
    vals, ids = [], []
    for _ in range(PEER_TOPK):
        m = jnp.max(s, axis=0, keepdims=True)
        am = jnp.min(jnp.where(s == m, iota, nrows), axis=0, keepdims=True)
        vals.append(m)
        ids.append(am)
        s = jnp.where(iota == am, -jnp.inf, s)
    return jnp.concatenate(vals, axis=0), jnp.concatenate(ids, axis=0)


_PAIR_GROUPS = ((0, 0, 8), (0, 8, 8), (1, 0, 8), (2, 0, 5), (3, 0, 4), (4, 0, 3), (5, 0, 2), (6, 0, 2), (7, 0, 2))


def _route_kernel(x_ref, wq_ref, pk_ref, idx_ref, gate_ref, sc_ref):
    tm = x_ref.shape[1]
    q = jnp.dot(x_ref[0].astype(BF16), wq_ref[...], preferred_element_type=F32).astype(BF16)
    contract1 = (((1,), (1,)), ((), ()))
    for hp in range(2 * PEER_HEADS):
        sc_ref[hp] = lax.dot_general(pk_ref[hp], q[:, hp * PEER_NKEYS:(hp + 1) * PEER_NKEYS], contract1,
                                     preferred_element_type=F32)
    iota = lax.broadcasted_iota(jnp.int32, (PEER_NKEYS, LANES), 0).astype(F32)
    sub = lax.broadcasted_iota(jnp.int32, (SUBLANES, LANES), 0)
    subf = sub.astype(F32)
    far = float(PEER_TOPK * PEER_TOPK)

    def head_body(h, carry):
        for lt in range(tm // LANES):
            lanes = slice(lt * LANES, (lt + 1) * LANES)
            s1, i1 = _top16_rows(sc_ref[2 * h, :, lanes], iota)
            s2, i2 = _top16_rows(sc_ref[2 * h + 1, :, lanes], iota)
            i1 = i1 * float(PEER_NKEYS)
            sc, ex, pos = [], [], []
            for i, j0, cnt in _PAIR_GROUPS:
                ok = sub < cnt
                sc.append(jnp.where(ok, s1[i:i + 1] + s2[j0:j0 + SUBLANES], -jnp.inf))
                ex.append(i1[i:i + 1] + i2[j0:j0 + SUBLANES])
                pos.append(jnp.where(ok, float(i * PEER_TOPK + j0) + subf, far))
            sc.append(s1[SUBLANES:] + s2[0:1])
            ex.append(i1[SUBLANES:] + i2[0:1])
            pos.append((float(SUBLANES) + subf) * float(PEER_TOPK))
            sc = jnp.concatenate(sc, axis=0)
            ex = jnp.concatenate(ex, axis=0)
            pos = jnp.concatenate(pos, axis=0)
            top, experts = [], []
            for _ in range(PEER_TOPK):
                m = jnp.max(sc, axis=0, keepdims=True)
                pm = jnp.min(jnp.where(sc == m, pos, far), axis=0, keepdims=True)
                hit = pos == pm
                experts.append(jnp.max(jnp.where(hit, ex, -1.0), axis=0, keepdims=True))
                top.append(m)
                sc = jnp.where(hit, -jnp.inf, sc)
            top = jnp.concatenate(top, axis=0)
            e = jnp.exp(top - top[0:1])
            rows = pl.ds(pl.multiple_of(h * PEER_TOPK, PEER_TOPK), PEER_TOPK)
            gate_ref[0, rows, lanes] = e / jnp.sum(e, axis=0, keepdims=True)
            idx_ref[0, rows, lanes] = jnp.concatenate(experts, axis=0).astype(jnp.int32)
        return carry

    lax.fori_loop(0, PEER_HEADS, head_body, 0)


def _route(xn, wq_bf, pk_bf, tm):
    b, s, d = xn.shape
    const2 = lambda i, j: (0, 0)
    slot = lambda i, j: (i, 0, j)
    return pl.pallas_call(
        _route_kernel,
        out_shape=[jax.ShapeDtypeStruct((b, PEER_SLOTS, s), jnp.int32),
                   jax.ShapeDtypeStruct((b, PEER_SLOTS, s), F32)],
        grid=(b, s // tm),
        in_specs=[
            pl.BlockSpec((1, tm, d), lambda i, j: (i, j, 0)),
            pl.BlockSpec(wq_bf.shape, const2),
            pl.BlockSpec(pk_bf.shape, lambda i, j: (0, 0, 0)),
        ],
        out_specs=[pl.BlockSpec((1, PEER_SLOTS, tm), slot)] * 2,
        scratch_shapes=[pltpu.VMEM((2 * PEER_HEADS, PEER_NKEYS, tm), F32)],
        compiler_params=_cparams(("arbitrary", "arbitrary")),
        name="route",
    )(xn, wq_bf, pk_bf)


def _pair_slots():
    out = []
    for g in range(PEER_SLOTS // SUBLANES):
        for c in range(4):
            top = SUBLANES * g + SLOT_ORDER[c]
            out.append((top, top + 4))
    return out


def _gather_pair(tbl_ref, top_off, bot_off, upper):
    ra = tbl_ref[pl.ds(pl.multiple_of(top_off, ROWS_PER_EXPERT), SUBLANES), :]
    rb = tbl_ref[pl.ds(pl.multiple_of(bot_off, ROWS_PER_EXPERT), SUBLANES), :]
    row = jnp.where(upper, ra, rb)
    lo = lax.bitcast_convert_type(row << 16, F32)
    hi = lax.bitcast_convert_type(row & jnp.uint32(0xFFFF0000), F32)
    return lo, hi


def _merge_sublanes(a, b, h, sub):
    keep = (sub & h) == 0
    return jnp.where(keep, a + pltpu.roll(a, SUBLANES - h, 0), b + pltpu.roll(b, h, 0))


def _slot_specs(tb):
    return [pl.BlockSpec((None, None, SLOT_GROUP * tb), lambda i, j=j: (j, 0, i), memory_space=pltpu.SMEM,
                         pipeline_mode=pl.Buffered(1)) for j in range(PEER_SLOTS // SLOT_GROUP)]


def _group_slots(a, tb):
    n = a.shape[-1]
    a = a.reshape(PEER_SLOTS // SLOT_GROUP, SLOT_GROUP, n // tb, tb).transpose(0, 2, 1, 3)
    return a.reshape(PEER_SLOTS // SLOT_GROUP, 1, n * SLOT_GROUP)


def _slot_reader(refs, t, tb):
    at = [t + g * tb for g in range(SLOT_GROUP)]
    return lambda k: refs[k // SLOT_GROUP][at[k % SLOT_GROUP]]


def _peer_u_kernel(*refs):
    ngroups = PEER_SLOTS // SLOT_GROUP
    offs = refs[:ngroups]
    x_ref, tbl_ref, o_ref, z_ref, sel_ref = refs[ngroups:]
    tb = x_ref.shape[0] // SUBLANES
    sub = lax.broadcasted_iota(jnp.int32, (SUBLANES, LANES), 0)
    upper = sub < ROWS_PER_EXPERT
    pairs = _pair_slots()
    lane_pair = lax.broadcasted_iota(jnp.int32, (PEER_SLOTS, LANES), 1) >> 1

    row = lax.broadcasted_iota(jnp.int32, (2 * LANES, LANES), 0)
    col = lax.broadcasted_iota(jnp.int32, (2 * LANES, LANES), 1)
    sel_ref[...] = jnp.where((col & 1) == (row >= LANES).astype(jnp.int32), 1.0, 0.0).astype(BF16)
    o_ref[...] = jnp.zeros_like(o_ref)

    def token_group(gi, zbase):
        for j in range(SUBLANES):
            t = gi * SUBLANES + j
            x = x_ref[pl.ds(pl.multiple_of(t * SUBLANES, SUBLANES), SUBLANES), :]
            swapped = pltpu.roll(x, ROWS_PER_EXPERT, 0)
            xa = jnp.where(upper, x, swapped)
            xb = jnp.where(upper, swapped, x)
            off = _slot_reader(offs, t, tb)
            zs = []
            for g in range(PEER_SLOTS // SUBLANES):
                ps = []
                for c in range(4):
                    top, bot = pairs[4 * g + c]
                    lo, hi = _gather_pair(tbl_ref, off(top), off(bot), upper)
                    ps.append(lo * xa + hi * xb)
                r0 = _merge_sublanes(ps[0], ps[1], 2, sub)
                r1 = _merge_sublanes(ps[2], ps[3], 2, sub)
                zs.append(_merge_sublanes(r0, r1, 1, sub))
            rows = zbase + (j // 2) * PEER_SLOTS
            cols = slice((j % 2) * LANES, (j % 2 + 1) * LANES)
            for m in range(len(zs) // 2):
                tile = jnp.concatenate([zs[2 * m], zs[2 * m + 1]], axis=0).astype(BF16)
                z_ref[rows + 2 * SUBLANES * m:rows + 2 * SUBLANES * (m + 1), cols] = tile
        sums = jnp.dot(z_ref[zbase:zbase + 4 * PEER_SLOTS, :], sel_ref[...], preferred_element_type=F32)
        block = pl.ds(pl.multiple_of((gi // (LANES // SUBLANES)) * LANES, LANES), LANES)
        first_pair = (gi % (LANES // SUBLANES)) * 4
        acc = o_ref[:, block]
        for p in range(4):
            acc = jnp.where(lane_pair == first_pair + p, sums[p * PEER_SLOTS:(p + 1) * PEER_SLOTS], acc)
        o_ref[:, block] = acc

    def step(i, carry):
        for r in range(U_GROUPS_PER_ITER):
            token_group(i * U_GROUPS_PER_ITER + r, r * 4 * PEER_SLOTS)
        return carry

    lax.fori_loop(0, tb // (SUBLANES * U_GROUPS_PER_ITER), step, 0)


def _peer_u(offs_g, x_rows, tbl, tb):
    n = x_rows.shape[0] // SUBLANES
    assert tb % LANES == 0 and (tb // SUBLANES) % U_GROUPS_PER_ITER == 0
    return pl.pallas_call(
        _peer_u_kernel,
        out_shape=jax.ShapeDtypeStruct((PEER_SLOTS, n), F32),
        grid=(n // tb,),
        in_specs=_slot_specs(tb) + [
            pl.BlockSpec((tb * SUBLANES, LANES), lambda i: (i, 0)),
            pl.BlockSpec(tbl.shape, lambda i: (0, 0), pipeline_mode=pl.Buffered(1)),
        ],
        out_specs=pl.BlockSpec((PEER_SLOTS, tb), lambda i: (0, i)),
        scratch_shapes=[
            pltpu.VMEM((U_GROUPS_PER_ITER * 4 * PEER_SLOTS, 2 * LANES), BF16),
            pltpu.VMEM((2 * LANES, LANES), BF16),
        ],
        compiler_params=_cparams(("arbitrary",)),
        name="peer_u",
    )(*([offs_g] * (PEER_SLOTS // SLOT_GROUP)), x_rows, tbl)


def _peer_w_kernel(gate_ref, act_ref, w_ref):
    a = act_ref[...]
    w = gate_ref[...] * (0.5 * a * (1.0 + lax.erf(a * (2.0 ** -0.5))))
    bits = lax.bitcast_convert_type(w.astype(BF16).astype(F32), jnp.uint32)
    bits = bits.reshape(PEER_SLOTS // SUBLANES, SUBLANES, bits.shape[-1])
    w_ref[...] = (bits | (pltpu.roll(bits, ROWS_PER_EXPERT, 1) >> 16)).reshape(w_ref.shape)


def _peer_w(gate_t, act_t, tm):
    n = gate_t.shape[-1]
    spec = pl.BlockSpec((PEER_SLOTS, tm), lambda i: (0, i))
    return pl.pallas_call(
        _peer_w_kernel,
        out_shape=jax.ShapeDtypeStruct(gate_t.shape, jnp.uint32),
        grid=(n // tm,),
        in_specs=[spec, spec],
        out_specs=spec,
        compiler_params=_cparams(("arbitrary",)),
        name="peer_w",
    )(gate_t, act_t)


def _peer_v_kernel(*refs):
    ngroups = PEER_SLOTS // SLOT_GROUP
    offs = refs[:ngroups]
    wts = refs[ngroups:2 * ngroups]
    h_ref, tbl_ref, nw_ref, o_ref = refs[2 * ngroups:]
    tb = h_ref.shape[0] // SUBLANES
    sub = lax.broadcasted_iota(jnp.int32, (SUBLANES, LANES), 0)
    upper = sub < ROWS_PER_EXPERT
    pairs = _pair_slots()

    def token(t):
        off = _slot_reader(offs, t, tb)
        wt = _slot_reader(wts, t, tb)
        acc = [jnp.zeros((SUBLANES, LANES), F32) for _ in range(4)]
        for q, (top, bot) in enumerate(pairs):
            lo, hi = _gather_pair(tbl_ref, off(top), off(bot), upper)
            word = wt(top)
            w = lax.bitcast_convert_type(jnp.where(upper, word & jnp.uint32(0xFFFF0000), word << 16), F32)
            k = 2 * (q % 2)
            acc[k] = acc[k] + lo * w
            acc[k + 1] = acc[k + 1] + hi * w
        lo = acc[0] + acc[2]
        hi = acc[1] + acc[3]
        lo = lo + pltpu.roll(lo, ROWS_PER_EXPERT, 0)
        hi = hi + pltpu.roll(hi, ROWS_PER_EXPERT, 0)
        rows = pl.ds(pl.multiple_of(t * SUBLANES, SUBLANES), SUBLANES)
        o_ref[rows, :] = h_ref[rows, :] + jnp.where(upper, lo, hi)

    def step(i, carry):
        for r in range(V_TOKENS_PER_ITER):
            token(i * V_TOKENS_PER_ITER + r)
        return carry

    lax.fori_loop(0, tb // V_TOKENS_PER_ITER, step, 0)

    def norm_step(i, carry):
        rows = pl.ds(pl.multiple_of(i * NORM_TOKENS * SUBLANES, NORM_TOKENS * SUBLANES), NORM_TOKENS * SUBLANES)
        h = o_ref[rows, :].reshape(NORM_TOKENS, SUBLANES, LANES)
        ss = jnp.sum(jnp.sum(h * h, axis=2, keepdims=True), axis=1, keepdims=True)
        y = h * lax.rsqrt(ss * (1.0 / D_MODEL) + RMS_EPS) * nw_ref[...][None]
        o_ref[rows, :] = y.reshape(NORM_TOKENS * SUBLANES, LANES)
        return carry

    lax.fori_loop(0, tb // NORM_TOKENS, norm_step, 0)


def _peer_v(offs_g, w_g, h_rows, tbl, norm_rows, tb):
    n = h_rows.shape[0] // SUBLANES
    assert tb % V_TOKENS_PER_ITER == 0 and tb % NORM_TOKENS == 0
    rows = pl.BlockSpec((tb * SUBLANES, LANES), lambda i: (i, 0))
    return pl.pallas_call(
        _peer_v_kernel,
        out_shape=jax.ShapeDtypeStruct(h_rows.shape, F32),
        grid=(n // tb,),
        in_specs=_slot_specs(tb) + _slot_specs(tb) + [
            rows,
            pl.BlockSpec(tbl.shape, lambda i: (0, 0), pipeline_mode=pl.Buffered(1)),
            pl.BlockSpec((SUBLANES, LANES), lambda i: (0, 0)),
        ],
        out_specs=rows,
        compiler_params=_cparams(("arbitrary",)),
        name="peer_v",
    )(*([offs_g] * (PEER_SLOTS // SLOT_GROUP)), *([w_g] * (PEER_SLOTS // SLOT_GROUP)), h_rows, tbl, norm_rows)


def _pack_table(t):
    half = D_MODEL // 2
    bits = lax.bitcast_convert_type(t.astype(BF16), jnp.uint16).astype(jnp.uint32)
    words = bits[:, :half] | (bits[:, half:] << 16)
    words = words.reshape(t.shape[0] * ROWS_PER_EXPERT, LANES)
    return jnp.pad(words, ((TABLE_PAD, TABLE_PAD), (0, 0)))


def _t5_bucket(rel):
    half = N_BUCKETS // 2
    exact = half // 2
    n = jnp.abs(rel)
    large = exact + (jnp.log(jnp.maximum(n, 1).astype(F32) / exact)
                     / math.log(MAX_DISTANCE / exact) * (half - exact)).astype(jnp.int32)
    large = jnp.minimum(large, half - 1)
    return jnp.where(rel > 0, half, 0) + jnp.where(n < exact, n, large)


def _attention_bias(rel_bias):
    rb = rel_bias.astype(F32)
    i = jnp.arange(ATT_BLOCK)
    jb = jnp.arange(3 * ATT_BLOCK)
    rel = (jb[None, :] - ATT_BLOCK) - i[:, None]
    band = jnp.where((jnp.abs(rel) <= ATT_BLOCK)[..., None], rb[_t5_bucket(rel)], NEG)
    mpos = PREFIX - N_META + jnp.arange(N_META)
    variants = []
    for blk in (0, 1):
        qpos = PREFIX + blk * ATT_BLOCK + i
        meta = rb[_t5_bucket(mpos[None, :] - qpos[:, None])]
        fill = jnp.full((ATT_BLOCK, ATT_BLOCK - N_META, ATT_HEADS), NEG, F32)
        variants.append(jnp.concatenate([band, meta, fill], axis=1).transpose(2, 0, 1))
    return jnp.stack(variants)


def _rope_tables(pos):
    half = RET_D // 2
    inv = ROPE_BASE ** (-jnp.arange(half, dtype=F32) / half)
    ang = pos.astype(F32)[:, None] * inv[None, :]
    cos, sin = jnp.cos(ang), jnp.sin(ang)
    return jnp.concatenate([cos, cos], axis=1), jnp.concatenate([-sin, sin], axis=1)


def _decay_tables(dec_f, dec_b, chunk):
    lf = jax.nn.log_sigmoid(dec_f.astype(F32))[:, None]
    lb = jax.nn.log_sigmoid(dec_b.astype(F32))[:, None]
    idx = jnp.arange(chunk, dtype=F32)
    diff = idx[:, None] - idx[None, :]
    dmat = (jnp.where(diff >= 0, jnp.exp(jnp.maximum(diff, 0.0)[None] * lf[:, :, None]), 0.0)
            + jnp.where(diff < 0, jnp.exp(jnp.maximum(-diff, 0.0)[None] * lb[:, :, None]), 0.0))
    bc = lambda v: jnp.broadcast_to(v[:, :, None], v.shape + (RET_D,))
    qwf = bc(jnp.exp((idx + 1.0)[None] * lf))
    kwf = bc(jnp.exp((chunk - 1.0 - idx)[None] * lf))
    qwb = bc(jnp.exp((chunk - idx)[None] * lb))
    kwb = bc(jnp.exp(idx[None] * lb))
    pidx = jnp.arange(PREFIX, dtype=F32)
    kwp = bc(jnp.exp((PREFIX - 1.0 - pidx)[None] * lf))
    gdec = jnp.concatenate([jnp.exp(chunk * lf[:, 0]), jnp.exp(chunk * lb[:, 0])])
    return dmat, qwf, qwb, kwf, kwb, kwp, gdec


def _divisor_tile(n, want):
    t = min(n, want)
    while n % t:
        t //= 2
    return t


def _encode(x, shared):
    b, s, d = x.shape
    n = b * s
    tm = _divisor_tile(s, 512)
    chunk = _divisor_tile(s, 256)
    cos_t, sin_t = _rope_tables(jnp.arange(s) + N_META)
    qr, kr, vr, gr, qa, ka, va = _proj(x, shared["norm_mix"], shared["w_in"], cos_t, sin_t, tm)
    o_r = _retention(qr, kr, vr, gr, shared["kr_pre"], shared["vr_pre"], *_decay_tables(
        shared["dec_f"], shared["dec_b"], chunk), shared["gn_w"], chunk)
    o_a = _attention(qa, ka, va, shared["k_meta"], shared["v_meta"], shared["bias"], shared["sink"])
    h1, xn2 = _outproj(o_r, o_a, x, shared["w_out"], shared["norm_ffn"], tm)
    idx_t, gate_t = _route(xn2, shared["wq"], shared["pkeys"], _divisor_tile(s, ROUTE_TOKENS))
    idx_t = idx_t.transpose(1, 0, 2).reshape(PEER_SLOTS, n)
    gate_t = gate_t.transpose(1, 0, 2).reshape(PEER_SLOTS, n)
    top_half = (jnp.arange(PEER_SLOTS) % SUBLANES) < ROWS_PER_EXPERT
    tb = _divisor_tile(n, PEER_TOKENS)
    offs_g = _group_slots(idx_t * ROWS_PER_EXPERT + jnp.where(top_half, TABLE_PAD, 0)[:, None], tb)
    act_t = _peer_u(offs_g, xn2.reshape(n * SUBLANES, LANES), shared["u_tbl"], tb)
    w_g = _group_slots(_peer_w(gate_t, act_t, _divisor_tile(n, 2048)), tb)
    y = _peer_v(offs_g, w_g, h1.reshape(n * SUBLANES, LANES), shared["v_tbl"], shared["norm_final"], tb)
    return y.reshape(b, s, d)


def kernel(x_prompt, x_sample, meta_tokens, norm_mix_w, w_in, ret_decay_fwd, ret_decay_bwd, ret_gn_w,
           attn_sink, rel_bias, w_out, norm_ffn_w, peer_wq, peer_keys, peer_u, peer_v, norm_final_w):
    layer = 0
    shared = {
        "norm_mix": norm_mix_w[layer][None, :].astype(F32),
        "w_in": w_in[layer].astype(BF16),
        "dec_f": ret_decay_fwd[layer],
        "dec_b": ret_decay_bwd[layer],
        "gn_w": ret_gn_w[layer][None, :].astype(F32),
        "sink": attn_sink[layer].astype(F32),
        "bias": _attention_bias(rel_bias),
        "w_out": w_out[layer].astype(BF16),
        "norm_ffn": norm_ffn_w[layer][None, :].astype(F32),
        "wq": peer_wq[layer].astype(BF16),
        "pkeys": peer_keys[layer].reshape(2 * PEER_HEADS, PEER_NKEYS, PEER_NKEYS).astype(BF16),
        "u_tbl": _pack_table(peer_u[layer]),
        "v_tbl": _pack_table(peer_v[layer]),
        "norm_final": norm_final_w.reshape(SUBLANES, LANES).astype(F32),
    }
    prefix = jnp.concatenate([jnp.zeros((PREFIX - N_META, D_MODEL), x_prompt.dtype),
                              meta_tokens.astype(x_prompt.dtype)], axis=0)[None]
    cos_p, sin_p = _rope_tables(jnp.arange(PREFIX) - (PREFIX - N_META))
    _, kr_p, vr_p, _, _, ka_p, va_p = _proj(prefix, shared["norm_mix"], shared["w_in"], cos_p, sin_p, PREFIX)
    shared["kr_pre"], shared["vr_pre"] = kr_p[0], vr_p[0]
    pad_meta = lambda t: jnp.pad(t[0, PREFIX - N_META:], ((0, ATT_BLOCK - N_META), (0, 0)))
    shared["k_meta"], shared["v_meta"] = pad_meta(ka_p), pad_meta(va_p)
    return (_encode(x_prompt, shared), _encode(x_sample, shared))
```

```python
import functools
import math

import jax
import jax.numpy as jnp
from jax import lax
from jax.experimental import pallas as pl
from jax.experimental.pallas import tpu as pltpu

F32 = jnp.float32
BF16 = jnp.bfloat16

D_MODEL = 1024
N_META = 16
PREFIX = 128
RET_HEADS = 4
RET_D = 128
ATT_HEADS = 8
ATT_KV = 2
ATT_GROUP = ATT_HEADS // ATT_KV
ATT_HD = 64
ATT_BLOCK = 128
N_BUCKETS = 32
MAX_DISTANCE = 128
ROPE_BASE = 10000.0
PEER_HEADS = 8
PEER_NKEYS = 128
PEER_EXPERTS = PEER_NKEYS * PEER_NKEYS
PEER_TOPK = 16
PEER_SLOTS = PEER_HEADS * PEER_TOPK
RMS_EPS = 1e-6
GN_EPS = 1e-5
NEG = -1e30
IN_COLS = (0, 512, 1024, 1536, 2048, 2560, 2688, 2816)

LANES = 128
SUBLANES = 8
VMEM_LIMIT = 56 * 1024 * 1024
ROWS_PER_EXPERT = D_MODEL // 2 // LANES
TABLE_PAD = ROWS_PER_EXPERT
PEER_TOKENS = 256
ROUTE_TOKENS = 256
SLOT_ORDER = (0, 2, 1, 3)
SLOT_GROUP = 8
U_GROUPS_PER_ITER = 2
V_TOKENS_PER_ITER = 4
NORM_TOKENS = 32


def _cparams(sem):
    return pltpu.CompilerParams(dimension_semantics=sem, vmem_limit_bytes=VMEM_LIMIT)


def _proj_kernel(x_ref, nw_ref, w_ref, cos_ref, sin_ref,
                 qr_ref, kr_ref, vr_ref, gr_ref, qa_ref, ka_ref, va_ref):
    x = x_ref[0]
    ms = jnp.mean(x * x, axis=-1, keepdims=True)
    xn = (x * lax.rsqrt(ms + RMS_EPS) * nw_ref[...]).astype(BF16)
    cosf = cos_ref[...]
    sinf = sin_ref[...]

    def mm(i):
        return jnp.dot(xn, w_ref[:, IN_COLS[i]:IN_COLS[i + 1]], preferred_element_type=F32)

    def rotary(t, scale):
        for h in range(RET_HEADS):
            th = t[:, h * RET_D:(h + 1) * RET_D]
            yield h, (th * cosf + pltpu.roll(th, RET_D // 2, 1) * sinf) * scale

    for h, r in rotary(mm(0), 1.0):
        qr_ref[0, :, h * RET_D:(h + 1) * RET_D] = r.astype(BF16)
    for h, r in rotary(mm(1), RET_D ** -0.5):
        kr_ref[0, :, h * RET_D:(h + 1) * RET_D] = r.astype(BF16)
    vr_ref[0] = mm(2).astype(BF16)
    gr_ref[0] = mm(3)
    qa_ref[0] = (mm(4) * (ATT_HD ** -0.5)).astype(BF16)
    ka_ref[0] = mm(5).astype(BF16)
    va_ref[0] = mm(6).astype(BF16)


def _proj(x, norm_w, w_in_bf, cos_t, sin_t, tm):
    b, s, d = x.shape
    widths = (512, 512, 512, 512, 512, 128, 128)
    dtypes = (BF16, BF16, BF16, F32, BF16, BF16, BF16)
    return pl.pallas_call(
        _proj_kernel,
        out_shape=[jax.ShapeDtypeStruct((b, s, w), dt) for w, dt in zip(widths, dtypes)],
        grid=(b, s // tm),
        in_specs=[
            pl.BlockSpec((1, tm, d), lambda i, j: (i, j, 0)),
            pl.BlockSpec((1, d), lambda i, j: (0, 0)),
            pl.BlockSpec(w_in_bf.shape, lambda i, j: (0, 0)),
            pl.BlockSpec((tm, LANES), lambda i, j: (j, 0)),
            pl.BlockSpec((tm, LANES), lambda i, j: (j, 0)),
        ],
        out_specs=[pl.BlockSpec((1, tm, w), lambda i, j: (i, j, 0)) for w in widths],
        compiler_params=_cparams(("arbitrary", "arbitrary")),
        name="proj",
    )(x, norm_w, w_in_bf, cos_t, sin_t)


def _ret_kernel(q_ref, k_ref, v_ref, g_ref, kpre_ref, vpre_ref, dmat_ref, qwf_ref, qwb_ref,
                kwf_ref, kwb_ref, kwp_ref, gdec_ref, gnw_ref, o_ref, sf_ref, sb_ref, sball_ref):
    ph = pl.program_id(1)
    c = pl.program_id(2)
    nc = pl.num_programs(2)
    contract0 = (((0,), (0,)), ((), ()))
    contract1 = (((1,), (1,)), ((), ()))

    def head(ref, h):
        return ref[0, :, h * RET_D:(h + 1) * RET_D]

    def kv_update(kh, vh, kw):
        kw = (kh.astype(F32) * kw).astype(BF16)
        return lax.dot_general(kw, vh, contract0, preferred_element_type=F32)

    @pl.when(ph == 0)
    def _backward_states():
        @pl.when(c == 0)
        def _():
            sb_ref[...] = jnp.zeros_like(sb_ref)

        j = nc - 1 - c
        for h in range(RET_HEADS):
            sball_ref[j, h] = sb_ref[h].astype(BF16)
            sb_ref[h] = gdec_ref[RET_HEADS + h] * sb_ref[h] + kv_update(head(k_ref, h), head(v_ref, h), kwb_ref[h])

    @pl.when(ph == 1)
    def _outputs():
        @pl.when(c == 0)
        def _():
            for h in range(RET_HEADS):
                kp = kpre_ref[:, h * RET_D:(h + 1) * RET_D]
                vp = vpre_ref[:, h * RET_D:(h + 1) * RET_D]
                sf_ref[h] = kv_update(kp, vp, kwp_ref[h])

        for h in range(RET_HEADS):
            qh, kh, vh = head(q_ref, h), head(k_ref, h), head(v_ref, h)
            s = lax.dot_general(qh, kh, contract1, preferred_element_type=F32) * dmat_ref[h]
            o = jnp.dot(s.astype(BF16), vh, preferred_element_type=F32)
            o = o + jnp.dot(qh, sf_ref[h].astype(BF16), preferred_element_type=F32) * qwf_ref[h]
            o = o + jnp.dot(qh, sball_ref[c, h], preferred_element_type=F32) * qwb_ref[h]
            mu = jnp.mean(o, axis=-1, keepdims=True)
            oc = o - mu
            var = jnp.mean(oc * oc, axis=-1, keepdims=True)
            on = oc * lax.rsqrt(var + GN_EPS) * gnw_ref[:, h * RET_D:(h + 1) * RET_D]
            g = head(g_ref, h)
            o_ref[0, :, h * RET_D:(h + 1) * RET_D] = (g * jax.nn.sigmoid(g) * on).astype(BF16)
            sf_ref[h] = gdec_ref[h] * sf_ref[h] + kv_update(kh, vh, kwf_ref[h])


def _retention(qr, kr, vr, gr, kpre, vpre, dmat, qwf, qwb, kwf, kwb, kwp, gdec, gnw, chunk):
    b, s, w = qr.shape
    nc = s // chunk

    def seq_map(i, ph, c):
        return (i, jnp.where(ph == 0, nc - 1 - c, c), 0)

    def fwd_only(i, ph, c):
        return (i, c * ph, 0)

    const2 = lambda i, ph, c: (0, 0)
    const3 = lambda i, ph, c: (0, 0, 0)
    return pl.pallas_call(
        _ret_kernel,
        out_shape=jax.ShapeDtypeStruct((b, s, w), BF16),
        grid=(b, 2, nc),
        in_specs=[
            pl.BlockSpec((1, chunk, w), fwd_only),
            pl.BlockSpec((1, chunk, w), seq_map),
            pl.BlockSpec((1, chunk, w), seq_map),
            pl.BlockSpec((1, chunk, w), fwd_only),
            pl.BlockSpec(kpre.shape, const2),
            pl.BlockSpec(vpre.shape, const2),
            pl.BlockSpec(dmat.shape, const3),
            pl.BlockSpec(qwf.shape, const3),
            pl.BlockSpec(qwb.shape, const3),
            pl.BlockSpec(kwf.shape, const3),
            pl.BlockSpec(kwb.shape, const3),
            pl.BlockSpec(kwp.shape, const3),
            pl.BlockSpec(memory_space=pltpu.SMEM),
            pl.BlockSpec(gnw.shape, const2),
        ],
        out_specs=pl.BlockSpec((1, chunk, w), fwd_only),
        scratch_shapes=[
            pltpu.VMEM((RET_HEADS, RET_D, RET_D), F32),
            pltpu.VMEM((RET_HEADS, RET_D, RET_D), F32),
            pltpu.VMEM((nc, RET_HEADS, RET_D, RET_D), BF16),
        ],
        compiler_params=_cparams(("arbitrary", "arbitrary", "arbitrary")),
        name="retention",
    )(qr, kr, vr, gr, kpre, vpre, dmat, qwf, qwb, kwf, kwb, kwp, gdec, gnw)


def _attn_kernel(q_ref, kp_ref, kc_ref, kn_ref, vp_ref, vc_ref, vn_ref, km_ref, vm_ref,
                 bias_ref, sink_ref, o_ref):
    j = pl.program_id(1)
    nb = pl.num_programs(1)
    lane = lax.broadcasted_iota(jnp.int32, (1, 4 * ATT_BLOCK), 1)
    first = jnp.logical_and(lane < ATT_BLOCK, j == 0)
    last = jnp.logical_and(jnp.logical_and(lane >= 2 * ATT_BLOCK, lane < 3 * ATT_BLOCK), j == nb - 1)
    pen = jnp.where(jnp.logical_or(first, last), NEG, 0.0).astype(F32)
    contract1 = (((1,), (1,)), ((), ()))
    outs = []
    for g in range(ATT_KV):
        sl = slice(g * ATT_HD, (g + 1) * ATT_HD)
        kall = jnp.concatenate([kp_ref[0][:, sl], kc_ref[0][:, sl], kn_ref[0][:, sl], km_ref[:, sl]], axis=0)
        vall = jnp.concatenate([vp_ref[0][:, sl], vc_ref[0][:, sl], vn_ref[0][:, sl], vm_ref[:, sl]], axis=0)
        for hh in range(ATT_GROUP):
            h = g * ATT_GROUP + hh
            q = q_ref[0, :, h * ATT_HD:(h + 1) * ATT_HD]
            s = lax.dot_general(q, kall, contract1, preferred_element_type=F32)
            s = s + bias_ref[0, h] + pen
            snk = sink_ref[h]
            m = jnp.maximum(jnp.max(s, axis=-1, keepdims=True), snk)
            p = jnp.exp(s - m)
            den = jnp.sum(p, axis=-1, keepdims=True) + jnp.exp(snk - m)
            o = jnp.dot(p.astype(BF16), vall, preferred_element_type=F32)
            outs.append(o / den)
    for h2 in range(ATT_HEADS // 2):
        o_ref[0, :, h2 * LANES:(h2 + 1) * LANES] = jnp.concatenate(
            [outs[2 * h2], outs[2 * h2 + 1]], axis=-1).astype(BF16)


def _attention(qa, ka, va, kmeta, vmeta, bias, sink):
    b, s, w = qa.shape
    nb = s // ATT_BLOCK
    kvw = ka.shape[-1]
    prev = lambda i, j: (i, jnp.maximum(j - 1, 0), 0)
    cur = lambda i, j: (i, j, 0)
    nxt = lambda i, j: (i, jnp.minimum(j + 1, nb - 1), 0)
    const2 = lambda i, j: (0, 0)
    kv = lambda m: pl.BlockSpec((1, ATT_BLOCK, kvw), m)
    return pl.pallas_call(
        _attn_kernel,
        out_shape=jax.ShapeDtypeStruct((b, s, w), BF16),
        grid=(b, nb),
        in_specs=[
            pl.BlockSpec((1, ATT_BLOCK, w), cur),
            kv(prev), kv(cur), kv(nxt), kv(prev), kv(cur), kv(nxt),
            pl.BlockSpec(kmeta.shape, const2),
            pl.BlockSpec(vmeta.shape, const2),
            pl.BlockSpec((1,) + bias.shape[1:], lambda i, j: (jnp.minimum(j, 1), 0, 0, 0)),
            pl.BlockSpec(memory_space=pltpu.SMEM),
        ],
        out_specs=pl.BlockSpec((1, ATT_BLOCK, w), cur),
        compiler_params=_cparams(("arbitrary", "arbitrary")),
        name="attention",
    )(qa, ka, ka, ka, va, va, va, kmeta, vmeta, bias, sink)


def _outproj_kernel(or_ref, oa_ref, x_ref, w_ref, nw_ref, h_ref, xn_ref):
    half = or_ref.shape[-1]
    h = x_ref[0]
    h = h + jnp.dot(or_ref[0], w_ref[:half], preferred_element_type=F32)
    h = h + jnp.dot(oa_ref[0], w_ref[half:], preferred_element_type=F32)
    h_ref[0] = h
    ms = jnp.mean(h * h, axis=-1, keepdims=True)
    xn_ref[0] = h * lax.rsqrt(ms + RMS_EPS) * nw_ref[...]


def _outproj(o_r, o_a, x, w_out_bf, norm_w, tm):
    b, s, d = x.shape
    half = o_r.shape[-1]
    row = lambda i, j: (i, j, 0)
    const2 = lambda i, j: (0, 0)
    return pl.pallas_call(
        _outproj_kernel,
        out_shape=[jax.ShapeDtypeStruct((b, s, d), F32)] * 2,
        grid=(b, s // tm),
        in_specs=[
            pl.BlockSpec((1, tm, half), row),
            pl.BlockSpec((1, tm, half), row),
            pl.BlockSpec((1, tm, d), row),
            pl.BlockSpec(w_out_bf.shape, const2),
            pl.BlockSpec((1, d), const2),
        ],
        out_specs=[pl.BlockSpec((1, tm, d), row)] * 2,
        compiler_params=_cparams(("arbitrary", "arbitrary")),
        name="outproj",
    )(o_r, o_a, x, w_out_bf, norm_w)


def _top16_rows(s, iota):
    nrows = float(s.shape[0])
    vals, ids = [], []
    for _ in range(PEER_TOPK):
        m = jnp.max(s, axis=0, keepdims=True)
        am = jnp.min(jnp.where(s == m, iota, nrows), axis=0, keepdims=True)
        vals.append(m)
        ids.append(am)
        s = jnp.where(iota == am, -jnp.inf, s)
    return jnp.concatenate(vals, axis=0), jnp.concatenate(ids, axis=0)


_PAIR_GROUPS = ((0, 0, 8), (0, 8, 8), (1, 0, 8), (2, 0, 5), (3, 0, 4), (4, 0, 3), (5, 0, 2), (6, 0, 2), (7, 0, 2))


def _route_kernel(x_ref, wq_ref, pk_ref, idx_ref, gate_ref, sc_ref):
    tm = x_ref.shape[1]
    q = jnp.dot(x_ref[0].astype(BF16), wq_ref[...], preferred_element_type=F32).astype(BF16)
    contract1 = (((1,), (1,)), ((), ()))
    for hp in range(2 * PEER_HEADS):
        sc_ref[hp] = lax.dot_general(pk_ref[hp], q[:, hp * PEER_NKEYS:(hp + 1) * PEER_NKEYS], contract1,
                                     preferred_element_type=F32)
    iota = lax.broadcasted_iota(jnp.int32, (PEER_NKEYS, LANES), 0).astype(F32)
    sub = lax.broadcasted_iota(jnp.int32, (SUBLANES, LANES), 0)
    subf = sub.astype(F32)
    far = float(PEER_TOPK * PEER_TOPK)

    def head_body(h, carry):
        for lt in range(tm // LANES):
            lanes = slice(lt * LANES, (lt + 1) * LANES)
            s1, i1 = _top16_rows(sc_ref[2 * h, :, lanes], iota)
            s2, i2 = _top16_rows(sc_ref[2 * h + 1, :, lanes], iota)
            i1 = i1 * float(PEER_NKEYS)
            sc, ex, pos = [], [], []
            for i, j0, cnt in _PAIR_GROUPS:
                ok = sub < cnt
                sc.append(jnp.where(ok, s1[i:i + 1] + s2[j0:j0 + SUBLANES], -jnp.inf))
                ex.append(i1[i:i + 1] + i2[j0:j0 + SUBLANES])
                pos.append(jnp.where(ok, float(i * PEER_TOPK + j0) + subf, far))
            sc.append(s1[SUBLANES:] + s2[0:1])
            ex.append(i1[SUBLANES:] + i2[0:1])
            pos.append((float(SUBLANES) + subf) * float(PEER_TOPK))
            sc = jnp.concatenate(sc, axis=0)
            ex = jnp.concatenate(ex, axis=0)
            pos = jnp.concatenate(pos, axis=0)
            top, experts = [], []
            for _ in range(PEER_TOPK):
                m = jnp.max(sc, axis=0, keepdims=True)
                pm = jnp.min(jnp.where(sc == m, pos, far), axis=0, keepdims=True)
                hit = pos == pm
                experts.append(jnp.max(jnp.where(hit, ex, -1.0), axis=0, keepdims=True))
                top.append(m)
                sc = jnp.where(hit, -jnp.inf, sc)
            top = jnp.concatenate(top, axis=0)
            e = jnp.exp(top - top[0:1])
            rows = pl.ds(pl.multiple_of(h * PEER_TOPK, PEER_TOPK), PEER_TOPK)
            gate_ref[0, rows, lanes] = e / jnp.sum(e, axis=0, keepdims=True)
            idx_ref[0, rows, lanes] = jnp.concatenate(experts, axis=0).astype(jnp.int32)
        return carry

    lax.fori_loop(0, PEER_HEADS, head_body, 0)


def _route(xn, wq_bf, pk_bf, tm):
    b, s, d = xn.shape
    const2 = lambda i, j: (0, 0)
    slot = lambda i, j: (i, 0, j)
    return pl.pallas_call(
        _route_kernel,
        out_shape=[jax.ShapeDtypeStruct((b, PEER_SLOTS, s), jnp.int32),
                   jax.ShapeDtypeStruct((b, PEER_SLOTS, s), F32)],
        grid=(b, s // tm),
        in_specs=[
            pl.BlockSpec((1, tm, d), lambda i, j: (i, j, 0)),
            pl.BlockSpec(wq_bf.shape, const2),
            pl.BlockSpec(pk_bf.shape, lambda i, j: (0, 0, 0)),
        ],
        out_specs=[pl.BlockSpec((1, PEER_SLOTS, tm), slot)] * 2,
        scratch_shapes=[pltpu.VMEM((2 * PEER_HEADS, PEER_NKEYS, tm), F32)],
        compiler_params=_cparams(("arbitrary", "arbitrary")),
        name="route",
    )(xn, wq_bf, pk_bf)


def _pair_slots():
    out = []
    for g in range(PEER_SLOTS // SUBLANES):
        for c in range(4):
            top = SUBLANES * g + SLOT_ORDER[c]
            out.append((top, top + 4))
    return out


def _gather_pair(tbl_ref, top_off, bot_off, upper):
    ra = tbl_ref[pl.ds(pl.multiple_of(top_off, ROWS_PER_EXPERT), SUBLANES), :]
    rb = tbl_ref[pl.ds(pl.multiple_of(bot_off, ROWS_PER_EXPERT), SUBLANES), :]
    row = jnp.where(upper, ra, rb)
    lo = lax.bitcast_convert_type(row << 16, F32)
    hi = lax.bitcast_convert_type(row & jnp.uint32(0xFFFF0000), F32)
    return lo, hi


def _merge_sublanes(a, b, h, sub):
    keep = (sub & h) == 0
    return jnp.where(keep, a + pltpu.roll(a, SUBLANES - h, 0), b + pltpu.roll(b, h, 0))


def _slot_specs(tb):
    return [pl.BlockSpec((None, None, SLOT_GROUP * tb), lambda i, j=j: (j, 0, i), memory_space=pltpu.SMEM,
                         pipeline_mode=pl.Buffered(1)) for j in range(PEER_SLOTS // SLOT_GROUP)]


def _group_slots(a, tb):
    n = a.shape[-1]
    a = a.reshape(PEER_SLOTS // SLOT_GROUP, SLOT_GROUP, n // tb, tb).transpose(0, 2, 1, 3)
    return a.reshape(PEER_SLOTS // SLOT_GROUP, 1, n * SLOT_GROUP)


def _slot_reader(refs, t, tb):
    at = [t + g * tb for g in range(SLOT_GROUP)]
    return lambda k: refs[k // SLOT_GROUP][at[k % SLOT_GROUP]]


def _peer_u_kernel(*refs):
    ngroups = PEER_SLOTS // SLOT_GROUP
    offs = refs[:ngroups]
    x_ref, tbl_ref, o_ref, z_ref, sel_ref = refs[ngroups:]
    tb = x_ref.shape[0] // SUBLANES
    sub = lax.broadcasted_iota(jnp.int32, (SUBLANES, LANES), 0)
    upper = sub < ROWS_PER_EXPERT
    pairs = _pair_slots()
    lane_pair = lax.broadcasted_iota(jnp.int32, (PEER_SLOTS, LANES), 1) >> 1

    row = lax.broadcasted_iota(jnp.int32, (2 * LANES, LANES), 0)
    col = lax.broadcasted_iota(jnp.int32, (2 * LANES, LANES), 1)
    sel_ref[...] = jnp.where((col & 1) == (row >= LANES).astype(jnp.int32), 1.0, 0.0).astype(BF16)
    o_ref[...] = jnp.zeros_like(o_ref)

    def token_group(gi, zbase):
        for j in range(SUBLANES):
            t = gi * SUBLANES + j
            x = x_ref[pl.ds(pl.multiple_of(t * SUBLANES, SUBLANES), SUBLANES), :]
            swapped = pltpu.roll(x, ROWS_PER_EXPERT, 0)
            xa = jnp.where(upper, x, swapped)
            xb = jnp.where(upper, swapped, x)
            off = _slot_reader(offs, t, tb)
            zs = []
            for g in range(PEER_SLOTS // SUBLANES):
                ps = []
                for c in range(4):
                    top, bot = pairs[4 * g + c]
                    lo, hi = _gather_pair(tbl_ref, off(top), off(bot), upper)
                    ps.append(lo * xa + hi * xb)
                r0 = _merge_sublanes(ps[0], ps[1], 2, sub)
                r1 = _merge_sublanes(ps[2], ps[3], 2, sub)
                zs.append(_merge_sublanes(r0, r1, 1, sub))
            rows = zbase + (j // 2) * PEER_SLOTS
            cols = slice((j % 2) * LANES, (j % 2 + 1) * LANES)
            for m in range(len(zs) // 2):
                tile = jnp.concatenate([zs[2 * m], zs[2 * m + 1]], axis=0).astype(BF16)
                z_ref[rows + 2 * SUBLANES * m:rows + 2 * SUBLANES * (m + 1), cols] = tile
        sums = jnp.dot(z_ref[zbase:zbase + 4 * PEER_SLOTS, :], sel_ref[...], preferred_element_type=F32)
        block = pl.ds(pl.multiple_of((gi // (LANES // SUBLANES)) * LANES, LANES), LANES)
        first_pair = (gi % (LANES // SUBLANES)) * 4
        acc = o_ref[:, block]
        for p in range(4):
            acc = jnp.where(lane_pair == first_pair + p, sums[p * PEER_SLOTS:(p + 1) * PEER_SLOTS], acc)
        o_ref[:, block] = acc

    def step(i, carry):
        for r in range(U_GROUPS_PER_ITER):
            token_group(i * U_GROUPS_PER_ITER + r, r * 4 * PEER_SLOTS)
        return carry

    lax.fori_loop(0, tb // (SUBLANES * U_GROUPS_PER_ITER), step, 0)


def _peer_u(offs_g, x_rows, tbl, tb):
    n = x_rows.shape[0] // SUBLANES
    assert tb % LANES == 0 and (tb // SUBLANES) % U_GROUPS_PER_ITER == 0
    return pl.pallas_call(
        _peer_u_kernel,
        out_shape=jax.ShapeDtypeStruct((PEER_SLOTS, n), F32),
        grid=(n // tb,),
        in_specs=_slot_specs(tb) + [
            pl.BlockSpec((tb * SUBLANES, LANES), lambda i: (i, 0)),
            pl.BlockSpec(tbl.shape, lambda i: (0, 0), pipeline_mode=pl.Buffered(1)),
        ],
        out_specs=pl.BlockSpec((PEER_SLOTS, tb), lambda i: (0, i)),
        scratch_shapes=[
            pltpu.VMEM((U_GROUPS_PER_ITER * 4 * PEER_SLOTS, 2 * LANES), BF16),
            pltpu.VMEM((2 * LANES, LANES), BF16),
        ],
        compiler_params=_cparams(("arbitrary",)),
        name="peer_u",
    )(*([offs_g] * (PEER_SLOTS // SLOT_GROUP)), x_rows, tbl)


def _peer_w_kernel(gate_ref, act_ref, w_ref):
    a = act_ref[...]
    w_ref[...] = gate_ref[...] * (0.5 * a * (1.0 + lax.erf(a * (2.0 ** -0.5))))


def _peer_w(gate_t, act_t, tm):
    n = gate_t.shape[-1]
    spec = pl.BlockSpec((PEER_SLOTS, tm), lambda i: (0, i))
    return pl.pallas_call(
        _peer_w_kernel,
        out_shape=jax.ShapeDtypeStruct(gate_t.shape, F32),
        grid=(n // tm,),
        in_specs=[spec, spec],
        out_specs=spec,
        compiler_params=_cparams(("arbitrary",)),
        name="peer_w",
    )(gate_t, act_t)


def _peer_v_kernel(*refs):
    ngroups = PEER_SLOTS // SLOT_GROUP
    offs = refs[:ngroups]
    w_ref, h_ref, tbl_ref, nw_ref, o_ref, whi_ref, wlo_ref, rep_ref, ones_ref = refs[ngroups:]
    tb = h_ref.shape[0] // SUBLANES
    sub = lax.broadcasted_iota(jnp.int32, (SUBLANES, LANES), 0)
    upper = sub < ROWS_PER_EXPERT
    pairs = _pair_slots()
    lane = lax.broadcasted_iota(jnp.int32, (PEER_SLOTS, LANES), 1)

    w = w_ref[...]
    whi = w.astype(BF16).astype(F32)
    whi_ref[...] = whi
    wlo_ref[...] = w - whi
    row = lax.broadcasted_iota(jnp.int32, (2 * LANES, 2 * LANES), 0)
    col = lax.broadcasted_iota(jnp.int32, (2 * LANES, 2 * LANES), 1)
    ones_ref[...] = jnp.where((row >= LANES) == (col >= LANES), 1.0, 0.0).astype(BF16)

    def replicate(t, slot):
        block = pl.ds(pl.multiple_of((t // LANES) * LANES, LANES), LANES)
        la = t % LANES
        out = None
        for part_ref in (whi_ref, wlo_ref):
            part = part_ref[:, block]
            lhs = jnp.concatenate([jnp.where(lane == la, part, 0.0), jnp.where(lane == la + 1, part, 0.0)],
                                  axis=1).astype(BF16)
            prod = jnp.dot(lhs, ones_ref[...], preferred_element_type=F32)
            out = prod if out is None else out + prod
        rep_ref[slot] = out

    def token(t, slot, half):
        off = _slot_reader(offs, t, tb)
        lanes = slice(half * LANES, (half + 1) * LANES)
        acc = [jnp.zeros((SUBLANES, LANES), F32) for _ in range(4)]
        for q, (top, bot) in enumerate(pairs):
            lo, hi = _gather_pair(tbl_ref, off(top), off(bot), upper)
            wtop = jnp.broadcast_to(rep_ref[slot, top:top + 1, lanes], (SUBLANES, LANES))
            wbot = jnp.broadcast_to(rep_ref[slot, bot:bot + 1, lanes], (SUBLANES, LANES))
            wv = jnp.where(upper, wtop, wbot)
            k = 2 * (q % 2)
            acc[k] = acc[k] + lo * wv
            acc[k + 1] = acc[k + 1] + hi * wv
        lo = acc[0] + acc[2]
        hi = acc[1] + acc[3]
        lo = lo + pltpu.roll(lo, ROWS_PER_EXPERT, 0)
        hi = hi + pltpu.roll(hi, ROWS_PER_EXPERT, 0)
        rows = pl.ds(pl.multiple_of(t * SUBLANES, SUBLANES), SUBLANES)
        o_ref[rows, :] = h_ref[rows, :] + jnp.where(upper, lo, hi)

    def step(i, carry):
        for r in range(V_TOKENS_PER_ITER // 2):
            replicate(i * V_TOKENS_PER_ITER + 2 * r, r)
        for r in range(V_TOKENS_PER_ITER):
            token(i * V_TOKENS_PER_ITER + r, r // 2, r % 2)
        return carry

    lax.fori_loop(0, tb // V_TOKENS_PER_ITER, step, 0)

    def norm_step(i, carry):
        rows = pl.ds(pl.multiple_of(i * NORM_TOKENS * SUBLANES, NORM_TOKENS * SUBLANES), NORM_TOKENS * SUBLANES)
        h = o_ref[rows, :].reshape(NORM_TOKENS, SUBLANES, LANES)
        ss = jnp.sum(jnp.sum(h * h, axis=2, keepdims=True), axis=1, keepdims=True)
        y = h * lax.rsqrt(ss * (1.0 / D_MODEL) + RMS_EPS) * nw_ref[...][None]
        o_ref[rows, :] = y.reshape(NORM_TOKENS * SUBLANES, LANES)
        return carry

    lax.fori_loop(0, tb // NORM_TOKENS, norm_step, 0)


def _peer_v(offs_g, w_t, h_rows, tbl, norm_rows, tb):
    n = h_rows.shape[0] // SUBLANES
    assert tb % LANES == 0 and V_TOKENS_PER_ITER % 2 == 0 and tb % NORM_TOKENS == 0
    rows = pl.BlockSpec((tb * SUBLANES, LANES), lambda i: (i, 0))
    return pl.pallas_call(
        _peer_v_kernel,
        out_shape=jax.ShapeDtypeStruct(h_rows.shape, F32),
        grid=(n // tb,),
        in_specs=_slot_specs(tb) + [
            pl.BlockSpec((PEER_SLOTS, tb), lambda i: (0, i)),
            rows,
            pl.BlockSpec(tbl.shape, lambda i: (0, 0), pipeline_mode=pl.Buffered(1)),
            pl.BlockSpec((SUBLANES, LANES), lambda i: (0, 0)),
        ],
        out_specs=rows,
        scratch_shapes=[
            pltpu.VMEM((PEER_SLOTS, tb), F32),
            pltpu.VMEM((PEER_SLOTS, tb), F32),
            pltpu.VMEM((V_TOKENS_PER_ITER // 2, PEER_SLOTS, 2 * LANES), F32),
            pltpu.VMEM((2 * LANES, 2 * LANES), BF16),
        ],
        compiler_params=_cparams(("arbitrary",)),
        name="peer_v",
    )(*([offs_g] * (PEER_SLOTS // SLOT_GROUP)), w_t, h_rows, tbl, norm_rows)


def _pack_table(t):
    half = D_MODEL // 2
    bits = lax.bitcast_convert_type(t.astype(BF16), jnp.uint16).astype(jnp.uint32)
    words = bits[:, :half] | (bits[:, half:] << 16)
    words = words.reshape(t.shape[0] * ROWS_PER_EXPERT, LANES)
    return jnp.pad(words, ((TABLE_PAD, TABLE_PAD), (0, 0)))


def _t5_bucket(rel):
    half = N_BUCKETS // 2
    exact = half // 2
    n = jnp.abs(rel)
    large = exact + (jnp.log(jnp.maximum(n, 1).astype(F32) / exact)
                     / math.log(MAX_DISTANCE / exact) * (half - exact)).astype(jnp.int32)
    large = jnp.minimum(large, half - 1)
    return jnp.where(rel > 0, half, 0) + jnp.where(n < exact, n, large)


def _attention_bias(rel_bias):
    rb = rel_bias.astype(F32)
    i = jnp.arange(ATT_BLOCK)
    jb = jnp.arange(3 * ATT_BLOCK)
    rel = (jb[None, :] - ATT_BLOCK) - i[:, None]
    band = jnp.where((jnp.abs(rel) <= ATT_BLOCK)[..., None], rb[_t5_bucket(rel)], NEG)
    mpos = PREFIX - N_META + jnp.arange(N_META)
    variants = []
    for blk in (0, 1):
        qpos = PREFIX + blk * ATT_BLOCK + i
        meta = rb[_t5_bucket(mpos[None, :] - qpos[:, None])]
        fill = jnp.full((ATT_BLOCK, ATT_BLOCK - N_META, ATT_HEADS), NEG, F32)
        variants.append(jnp.concatenate([band, meta, fill], axis=1).transpose(2, 0, 1))
    return jnp.stack(variants)


def _rope_tables(pos):
    half = RET_D // 2
    inv = ROPE_BASE ** (-jnp.arange(half, dtype=F32) / half)
    ang = pos.astype(F32)[:, None] * inv[None, :]
    cos, sin = jnp.cos(ang), jnp.sin(ang)
    return jnp.concatenate([cos, cos], axis=1), jnp.concatenate([-sin, sin], axis=1)


def _decay_tables(dec_f, dec_b, chunk):
    lf = jax.nn.log_sigmoid(dec_f.astype(F32))[:, None]
    lb = jax.nn.log_sigmoid(dec_b.astype(F32))[:, None]
    idx = jnp.arange(chunk, dtype=F32)
    diff = idx[:, None] - idx[None, :]
    dmat = (jnp.where(diff >= 0, jnp.exp(jnp.maximum(diff, 0.0)[None] * lf[:, :, None]), 0.0)
            + jnp.where(diff < 0, jnp.exp(jnp.maximum(-diff, 0.0)[None] * lb[:, :, None]), 0.0))
    bc = lambda v: jnp.broadcast_to(v[:, :, None], v.shape + (RET_D,))
    qwf = bc(jnp.exp((idx + 1.0)[None] * lf))
    kwf = bc(jnp.exp((chunk - 1.0 - idx)[None] * lf))
    qwb = bc(jnp.exp((chunk - idx)[None] * lb))
    kwb = bc(jnp.exp(idx[None] * lb))
    pidx = jnp.arange(PREFIX, dtype=F32)
    kwp = bc(jnp.exp((PREFIX - 1.0 - pidx)[None] * lf))
    gdec = jnp.concatenate([jnp.exp(chunk * lf[:, 0]), jnp.exp(chunk * lb[:, 0])])
    return dmat, qwf, qwb, kwf, kwb, kwp, gdec


def _divisor_tile(n, want):
    t = min(n, want)
    while n % t:
        t //= 2
    return t


def _encode(x, shared):
    b, s, d = x.shape
    n = b * s
    tm = _divisor_tile(s, 512)
    chunk = _divisor_tile(s, 256)
    cos_t, sin_t = _rope_tables(jnp.arange(s) + N_META)
    qr, kr, vr, gr, qa, ka, va = _proj(x, shared["norm_mix"], shared["w_in"], cos_t, sin_t, tm)
    o_r = _retention(qr, kr, vr, gr, shared["kr_pre"], shared["vr_pre"], *_decay_tables(
        shared["dec_f"], shared["dec_b"], chunk), shared["gn_w"], chunk)
    o_a = _attention(qa, ka, va, shared["k_meta"], shared["v_meta"], shared["bias"], shared["sink"])
    h1, xn2 = _outproj(o_r, o_a, x, shared["w_out"], shared["norm_ffn"], tm)
    idx_t, gate_t = _route(xn2, shared["wq"], shared["pkeys"], _divisor_tile(s, ROUTE_TOKENS))
    idx_t = idx_t.transpose(1, 0, 2).reshape(PEER_SLOTS, n)
    gate_t = gate_t.transpose(1, 0, 2).reshape(PEER_SLOTS, n)
    top_half = (jnp.arange(PEER_SLOTS) % SUBLANES) < ROWS_PER_EXPERT
    tb = _divisor_tile(n, PEER_TOKENS)
    offs_g = _group_slots(idx_t * ROWS_PER_EXPERT + jnp.where(top_half, TABLE_PAD, 0)[:, None], tb)
    act_t = _peer_u(offs_g, xn2.reshape(n * SUBLANES, LANES), shared["u_tbl"], tb)
    w_t = _peer_w(gate_t, act_t, _divisor_tile(n, 2048))
    y = _peer_v(offs_g, w_t, h1.reshape(n * SUBLANES, LANES), shared["v_tbl"], shared["norm_final"], tb)
    return y.reshape(b, s, d)


def kernel(x_prompt, x_sample, meta_tokens, norm_mix_w, w_in, ret_decay_fwd, ret_decay_bwd, ret_gn_w,
           attn_sink, rel_bias, w_out, norm_ffn_w, peer_wq, peer_keys, peer_u, peer_v, norm_final_w):
    layer = 0
    shared = {
        "norm_mix": norm_mix_w[layer][None, :].astype(F32),
        "w_in": w_in[layer].astype(BF16),
        "dec_f": ret_decay_fwd[layer],
        "dec_b": ret_decay_bwd[layer],
        "gn_w": ret_gn_w[layer][None, :].astype(F32),
        "sink": attn_sink[layer].astype(F32),
        "bias": _attention_bias(rel_bias),
        "w_out": w_out[layer].astype(BF16),
        "norm_ffn": norm_ffn_w[layer][None, :].astype(F32),
        "wq": peer_wq[layer].astype(BF16),
        "pkeys": peer_keys[layer].reshape(2 * PEER_HEADS, PEER_NKEYS, PEER_NKEYS).astype(BF16),
        "u_tbl": _pack_table(peer_u[layer]),
        "v_tbl": _pack_table(peer_v[layer]),
        "norm_final": norm_final_w.reshape(SUBLANES, LANES).astype(F32),
    }
    prefix = jnp.concatenate([jnp.zeros((PREFIX - N_META, D_MODEL), x_prompt.dtype),
                              meta_tokens.astype(x_prompt.dtype)], axis=0)[None]
    cos_p, sin_p = _rope_tables(jnp.arange(PREFIX) - (PREFIX - N_META))
    _, kr_p, vr_p, _, _, ka_p, va_p = _proj(prefix, shared["norm_mix"], shared["w_in"], cos_p, sin_p, PREFIX)
    shared["kr_pre"], shared["vr_pre"] = kr_p[0], vr_p[0]
    pad_meta = lambda t: jnp.pad(t[0, PREFIX - N_META:], ((0, ATT_BLOCK - N_META), (0, 0)))
    shared["k_meta"], shared["v_meta"] = pad_meta(ka_p), pad_meta(va_p)
    return (_encode(x_prompt, shared), _encode(x_sample, shared))
```

```python
import functools
import math

import jax
import jax.numpy as jnp
from jax import lax
from jax.experimental import pallas as pl
from jax.experimental.pallas import tpu as pltpu

F32 = jnp.float32
BF16 = jnp.bfloat16

D_MODEL = 1024
N_META = 16
PREFIX = 128
RET_HEADS = 4
RET_D = 128
ATT_HEADS = 8
ATT_KV = 2
ATT_GROUP = ATT_HEADS // ATT_KV
ATT_HD = 64
ATT_BLOCK = 128
N_BUCKETS = 32
MAX_DISTANCE = 128
ROPE_BASE = 10000.0
PEER_HEADS = 8
PEER_NKEYS = 128
PEER_EXPERTS = PEER_NKEYS * PEER_NKEYS
PEER_TOPK = 16
PEER_SLOTS = PEER_HEADS * PEER_TOPK
RMS_EPS = 1e-6
GN_EPS = 1e-5
NEG = -1e30
IN_COLS = (0, 512, 1024, 1536, 2048, 2560, 2688, 2816)

LANES = 128
SUBLANES = 8
VMEM_LIMIT = 56 * 1024 * 1024
ROWS_PER_EXPERT = D_MODEL // 2 // LANES
TABLE_PAD = ROWS_PER_EXPERT
PEER_TOKENS = 256
ROUTE_TOKENS = 256
SLOT_ORDER = (0, 2, 1, 3)
SLOT_GROUP = 1
U_GROUPS_PER_ITER = 2
V_TOKENS_PER_ITER = 8
NORM_TOKENS = 32


def _cparams(sem):
    return pltpu.CompilerParams(dimension_semantics=sem, vmem_limit_bytes=VMEM_LIMIT)


def _proj_kernel(x_ref, nw_ref, w_ref, cos_ref, sin_ref,
                 qr_ref, kr_ref, vr_ref, gr_ref, qa_ref, ka_ref, va_ref):
    x = x_ref[0]
    ms = jnp.mean(x * x, axis=-1, keepdims=True)
    xn = (x * lax.rsqrt(ms + RMS_EPS) * nw_ref[...]).astype(BF16)
    cosf = cos_ref[...]
    sinf = sin_ref[...]

    def mm(i):
        return jnp.dot(xn, w_ref[:, IN_COLS[i]:IN_COLS[i + 1]], preferred_element_type=F32)

    def rotary(t, scale):
        for h in range(RET_HEADS):
            th = t[:, h * RET_D:(h + 1) * RET_D]
            yield h, (th * cosf + pltpu.roll(th, RET_D // 2, 1) * sinf) * scale

    for h, r in rotary(mm(0), 1.0):
        qr_ref[0, :, h * RET_D:(h + 1) * RET_D] = r.astype(BF16)
    for h, r in rotary(mm(1), RET_D ** -0.5):
        kr_ref[0, :, h * RET_D:(h + 1) * RET_D] = r.astype(BF16)
    vr_ref[0] = mm(2).astype(BF16)
    gr_ref[0] = mm(3)
    qa_ref[0] = (mm(4) * (ATT_HD ** -0.5)).astype(BF16)
    ka_ref[0] = mm(5).astype(BF16)
    va_ref[0] = mm(6).astype(BF16)


def _proj(x, norm_w, w_in_bf, cos_t, sin_t, tm):
    b, s, d = x.shape
    widths = (512, 512, 512, 512, 512, 128, 128)
    dtypes = (BF16, BF16, BF16, F32, BF16, BF16, BF16)
    return pl.pallas_call(
        _proj_kernel,
        out_shape=[jax.ShapeDtypeStruct((b, s, w), dt) for w, dt in zip(widths, dtypes)],
        grid=(b, s // tm),
        in_specs=[
            pl.BlockSpec((1, tm, d), lambda i, j: (i, j, 0)),
            pl.BlockSpec((1, d), lambda i, j: (0, 0)),
            pl.BlockSpec(w_in_bf.shape, lambda i, j: (0, 0)),
            pl.BlockSpec((tm, LANES), lambda i, j: (j, 0)),
            pl.BlockSpec((tm, LANES), lambda i, j: (j, 0)),
        ],
        out_specs=[pl.BlockSpec((1, tm, w), lambda i, j: (i, j, 0)) for w in widths],
        compiler_params=_cparams(("arbitrary", "arbitrary")),
        name="proj",
    )(x, norm_w, w_in_bf, cos_t, sin_t)


def _ret_kernel(q_ref, k_ref, v_ref, g_ref, kpre_ref, vpre_ref, dmat_ref, qwf_ref, qwb_ref,
                kwf_ref, kwb_ref, kwp_ref, gdec_ref, gnw_ref, o_ref, sf_ref, sb_ref, sball_ref):
    ph = pl.program_id(1)
    c = pl.program_id(2)
    nc = pl.num_programs(2)
    contract0 = (((0,), (0,)), ((), ()))
    contract1 = (((1,), (1,)), ((), ()))

    def head(ref, h):
        return ref[0, :, h * RET_D:(h + 1) * RET_D]

    def kv_update(kh, vh, kw):
        kw = (kh.astype(F32) * kw).astype(BF16)
        return lax.dot_general(kw, vh, contract0, preferred_element_type=F32)

    @pl.when(ph == 0)
    def _backward_states():
        @pl.when(c == 0)
        def _():
            sb_ref[...] = jnp.zeros_like(sb_ref)

        j = nc - 1 - c
        for h in range(RET_HEADS):
            sball_ref[j, h] = sb_ref[h].astype(BF16)
            sb_ref[h] = gdec_ref[RET_HEADS + h] * sb_ref[h] + kv_update(head(k_ref, h), head(v_ref, h), kwb_ref[h])

    @pl.when(ph == 1)
    def _outputs():
        @pl.when(c == 0)
        def _():
            for h in range(RET_HEADS):
                kp = kpre_ref[:, h * RET_D:(h + 1) * RET_D]
                vp = vpre_ref[:, h * RET_D:(h + 1) * RET_D]
                sf_ref[h] = kv_update(kp, vp, kwp_ref[h])

        for h in range(RET_HEADS):
            qh, kh, vh = head(q_ref, h), head(k_ref, h), head(v_ref, h)
            s = lax.dot_general(qh, kh, contract1, preferred_element_type=F32) * dmat_ref[h]
            o = jnp.dot(s.astype(BF16), vh, preferred_element_type=F32)
            o = o + jnp.dot(qh, sf_ref[h].astype(BF16), preferred_element_type=F32) * qwf_ref[h]
            o = o + jnp.dot(qh, sball_ref[c, h], preferred_element_type=F32) * qwb_ref[h]
            mu = jnp.mean(o, axis=-1, keepdims=True)
            oc = o - mu
            var = jnp.mean(oc * oc, axis=-1, keepdims=True)
            on = oc * lax.rsqrt(var + GN_EPS) * gnw_ref[:, h * RET_D:(h + 1) * RET_D]
            g = head(g_ref, h)
            o_ref[0, :, h * RET_D:(h + 1) * RET_D] = (g * jax.nn.sigmoid(g) * on).astype(BF16)
            sf_ref[h] = gdec_ref[h] * sf_ref[h] + kv_update(kh, vh, kwf_ref[h])


def _retention(qr, kr, vr, gr, kpre, vpre, dmat, qwf, qwb, kwf, kwb, kwp, gdec, gnw, chunk):
    b, s, w = qr.shape
    nc = s // chunk

    def seq_map(i, ph, c):
        return (i, jnp.where(ph == 0, nc - 1 - c, c), 0)

    def fwd_only(i, ph, c):
        return (i, c * ph, 0)

    const2 = lambda i, ph, c: (0, 0)
    const3 = lambda i, ph, c: (0, 0, 0)
    return pl.pallas_call(
        _ret_kernel,
        out_shape=jax.ShapeDtypeStruct((b, s, w), BF16),
        grid=(b, 2, nc),
        in_specs=[
            pl.BlockSpec((1, chunk, w), fwd_only),
            pl.BlockSpec((1, chunk, w), seq_map),
            pl.BlockSpec((1, chunk, w), seq_map),
            pl.BlockSpec((1, chunk, w), fwd_only),
            pl.BlockSpec(kpre.shape, const2),
            pl.BlockSpec(vpre.shape, const2),
            pl.BlockSpec(dmat.shape, const3),
            pl.BlockSpec(qwf.shape, const3),
            pl.BlockSpec(qwb.shape, const3),
            pl.BlockSpec(kwf.shape, const3),
            pl.BlockSpec(kwb.shape, const3),
            pl.BlockSpec(kwp.shape, const3),
            pl.BlockSpec(memory_space=pltpu.SMEM),
            pl.BlockSpec(gnw.shape, const2),
        ],
        out_specs=pl.BlockSpec((1, chunk, w), fwd_only),
        scratch_shapes=[
            pltpu.VMEM((RET_HEADS, RET_D, RET_D), F32),
            pltpu.VMEM((RET_HEADS, RET_D, RET_D), F32),
            pltpu.VMEM((nc, RET_HEADS, RET_D, RET_D), BF16),
        ],
        compiler_params=_cparams(("arbitrary", "arbitrary", "arbitrary")),
        name="retention",
    )(qr, kr, vr, gr, kpre, vpre, dmat, qwf, qwb, kwf, kwb, kwp, gdec, gnw)


def _attn_kernel(q_ref, kp_ref, kc_ref, kn_ref, vp_ref, vc_ref, vn_ref, km_ref, vm_ref,
                 bias_ref, sink_ref, o_ref):
    j = pl.program_id(1)
    nb = pl.num_programs(1)
    lane = lax.broadcasted_iota(jnp.int32, (1, 4 * ATT_BLOCK), 1)
    first = jnp.logical_and(lane < ATT_BLOCK, j == 0)
    last = jnp.logical_and(jnp.logical_and(lane >= 2 * ATT_BLOCK, lane < 3 * ATT_BLOCK), j == nb - 1)
    pen = jnp.where(jnp.logical_or(first, last), NEG, 0.0).astype(F32)
    contract1 = (((1,), (1,)), ((), ()))
    outs = []
    for g in range(ATT_KV):
        sl = slice(g * ATT_HD, (g + 1) * ATT_HD)
        kall = jnp.concatenate([kp_ref[0][:, sl], kc_ref[0][:, sl], kn_ref[0][:, sl], km_ref[:, sl]], axis=0)
        vall = jnp.concatenate([vp_ref[0][:, sl], vc_ref[0][:, sl], vn_ref[0][:, sl], vm_ref[:, sl]], axis=0)
        for hh in range(ATT_GROUP):
            h = g * ATT_GROUP + hh
            q = q_ref[0, :, h * ATT_HD:(h + 1) * ATT_HD]
            s = lax.dot_general(q, kall, contract1, preferred_element_type=F32)
            s = s + bias_ref[0, h] + pen
            snk = sink_ref[h]
            m = jnp.maximum(jnp.max(s, axis=-1, keepdims=True), snk)
            p = jnp.exp(s - m)
            den = jnp.sum(p, axis=-1, keepdims=True) + jnp.exp(snk - m)
            o = jnp.dot(p.astype(BF16), vall, preferred_element_type=F32)
            outs.append(o / den)
    for h2 in range(ATT_HEADS // 2):
        o_ref[0, :, h2 * LANES:(h2 + 1) * LANES] = jnp.concatenate(
            [outs[2 * h2], outs[2 * h2 + 1]], axis=-1).astype(BF16)


def _attention(qa, ka, va, kmeta, vmeta, bias, sink):
    b, s, w = qa.shape
    nb = s // ATT_BLOCK
    kvw = ka.shape[-1]
    prev = lambda i, j: (i, jnp.maximum(j - 1, 0), 0)
    cur = lambda i, j: (i, j, 0)
    nxt = lambda i, j: (i, jnp.minimum(j + 1, nb - 1), 0)
    const2 = lambda i, j: (0, 0)
    kv = lambda m: pl.BlockSpec((1, ATT_BLOCK, kvw), m)
    return pl.pallas_call(
        _attn_kernel,
        out_shape=jax.ShapeDtypeStruct((b, s, w), BF16),
        grid=(b, nb),
        in_specs=[
            pl.BlockSpec((1, ATT_BLOCK, w), cur),
            kv(prev), kv(cur), kv(nxt), kv(prev), kv(cur), kv(nxt),
            pl.BlockSpec(kmeta.shape, const2),
            pl.BlockSpec(vmeta.shape, const2),
            pl.BlockSpec((1,) + bias.shape[1:], lambda i, j: (jnp.minimum(j, 1), 0, 0, 0)),
            pl.BlockSpec(memory_space=pltpu.SMEM),
        ],
        out_specs=pl.BlockSpec((1, ATT_BLOCK, w), cur),
        compiler_params=_cparams(("arbitrary", "arbitrary")),
        name="attention",
    )(qa, ka, ka, ka, va, va, va, kmeta, vmeta, bias, sink)


def _outproj_kernel(or_ref, oa_ref, x_ref, w_ref, nw_ref, h_ref, xn_ref):
    half = or_ref.shape[-1]
    h = x_ref[0]
    h = h + jnp.dot(or_ref[0], w_ref[:half], preferred_element_type=F32)
    h = h + jnp.dot(oa_ref[0], w_ref[half:], preferred_element_type=F32)
    h_ref[0] = h
    ms = jnp.mean(h * h, axis=-1, keepdims=True)
    xn_ref[0] = h * lax.rsqrt(ms + RMS_EPS) * nw_ref[...]


def _outproj(o_r, o_a, x, w_out_bf, norm_w, tm):
    b, s, d = x.shape
    half = o_r.shape[-1]
    row = lambda i, j: (i, j, 0)
    const2 = lambda i, j: (0, 0)
    return pl.pallas_call(
        _outproj_kernel,
        out_shape=[jax.ShapeDtypeStruct((b, s, d), F32)] * 2,
        grid=(b, s // tm),
        in_specs=[
            pl.BlockSpec((1, tm, half), row),
            pl.BlockSpec((1, tm, half), row),
            pl.BlockSpec((1, tm, d), row),
            pl.BlockSpec(w_out_bf.shape, const2),
            pl.BlockSpec((1, d), const2),
        ],
        out_specs=[pl.BlockSpec((1, tm, d), row)] * 2,
        compiler_params=_cparams(("arbitrary", "arbitrary")),
        name="outproj",
    )(o_r, o_a, x, w_out_bf, norm_w)


def _top16_rows(s, iota):
    nrows = float(s.shape[0])
    vals, ids = [], []
    for _ in range(PEER_TOPK):
        m = jnp.max(s, axis=0, keepdims=True)
        am = jnp.min(jnp.where(s == m, iota, nrows), axis=0, keepdims=True)
        vals.append(m)
        ids.append(am)
        s = jnp.where(iota == am, -jnp.inf, s)
    return jnp.concatenate(vals, axis=0), jnp.concatenate(ids, axis=0)


_PAIR_GROUPS = ((0, 0, 8), (0, 8, 8), (1, 0, 8), (2, 0, 5), (3, 0, 4), (4, 0, 3), (5, 0, 2), (6, 0, 2), (7, 0, 2))


def _route_kernel(x_ref, wq_ref, pk_ref, idx_ref, gate_ref, sc_ref):
    tm = x_ref.shape[1]
    q = jnp.dot(x_ref[0].astype(BF16), wq_ref[...], preferred_element_type=F32).astype(BF16)
    contract1 = (((1,), (1,)), ((), ()))
    for hp in range(2 * PEER_HEADS):
        sc_ref[hp] = lax.dot_general(pk_ref[hp], q[:, hp * PEER_NKEYS:(hp + 1) * PEER_NKEYS], contract1,
                                     preferred_element_type=F32)
    iota = lax.broadcasted_iota(jnp.int32, (PEER_NKEYS, LANES), 0).astype(F32)
    sub = lax.broadcasted_iota(jnp.int32, (SUBLANES, LANES), 0)
    subf = sub.astype(F32)
    far = float(PEER_TOPK * PEER_TOPK)

    def head_body(h, carry):
        for lt in range(tm // LANES):
            lanes = slice(lt * LANES, (lt + 1) * LANES)
            s1, i1 = _top16_rows(sc_ref[2 * h, :, lanes], iota)
            s2, i2 = _top16_rows(sc_ref[2 * h + 1, :, lanes], iota)
            i1 = i1 * float(PEER_NKEYS)
            sc, ex, pos = [], [], []
            for i, j0, cnt in _PAIR_GROUPS:
                ok = sub < cnt
                sc.append(jnp.where(ok, s1[i:i + 1] + s2[j0:j0 + SUBLANES], -jnp.inf))
                ex.append(i1[i:i + 1] + i2[j0:j0 + SUBLANES])
                pos.append(jnp.where(ok, float(i * PEER_TOPK + j0) + subf, far))
            sc.append(s1[SUBLANES:] + s2[0:1])
            ex.append(i1[SUBLANES:] + i2[0:1])
            pos.append((float(SUBLANES) + subf) * float(PEER_TOPK))
            sc = jnp.concatenate(sc, axis=0)
            ex = jnp.concatenate(ex, axis=0)
            pos = jnp.concatenate(pos, axis=0)
            top, experts = [], []
            for _ in range(PEER_TOPK):
                m = jnp.max(sc, axis=0, keepdims=True)
                pm = jnp.min(jnp.where(sc == m, pos, far), axis=0, keepdims=True)
                hit = pos == pm
                experts.append(jnp.max(jnp.where(hit, ex, -1.0), axis=0, keepdims=True))
                top.append(m)
                sc = jnp.where(hit, -jnp.inf, sc)
            top = jnp.concatenate(top, axis=0)
            e = jnp.exp(top - top[0:1])
            rows = pl.ds(pl.multiple_of(h * PEER_TOPK, PEER_TOPK), PEER_TOPK)
            gate_ref[0, rows, lanes] = e / jnp.sum(e, axis=0, keepdims=True)
            idx_ref[0, rows, lanes] = jnp.concatenate(experts, axis=0).astype(jnp.int32)
        return carry

    lax.fori_loop(0, PEER_HEADS, head_body, 0)


def _route(xn, wq_bf, pk_bf, tm):
    b, s, d = xn.shape
    const2 = lambda i, j: (0, 0)
    slot = lambda i, j: (i, 0, j)
    return pl.pallas_call(
        _route_kernel,
        out_shape=[jax.ShapeDtypeStruct((b, PEER_SLOTS, s), jnp.int32),
                   jax.ShapeDtypeStruct((b, PEER_SLOTS, s), F32)],
        grid=(b, s // tm),
        in_specs=[
            pl.BlockSpec((1, tm, d), lambda i, j: (i, j, 0)),
            pl.BlockSpec(wq_bf.shape, const2),
            pl.BlockSpec(pk_bf.shape, lambda i, j: (0, 0, 0)),
        ],
        out_specs=[pl.BlockSpec((1, PEER_SLOTS, tm), slot)] * 2,
        scratch_shapes=[pltpu.VMEM((2 * PEER_HEADS, PEER_NKEYS, tm), F32)],
        compiler_params=_cparams(("arbitrary", "arbitrary")),
        name="route",
    )(xn, wq_bf, pk_bf)


def _pair_slots():
    out = []
    for g in range(PEER_SLOTS // SUBLANES):
        for c in range(4):
            top = SUBLANES * g + SLOT_ORDER[c]
            out.append((top, top + 4))
    return out


def _gather_pair(tbl_ref, top_off, bot_off, upper):
    ra = tbl_ref[pl.ds(pl.multiple_of(top_off, ROWS_PER_EXPERT), SUBLANES), :]
    rb = tbl_ref[pl.ds(pl.multiple_of(bot_off, ROWS_PER_EXPERT), SUBLANES), :]
    row = jnp.where(upper, ra, rb)
    lo = lax.bitcast_convert_type(row << 16, F32)
    hi = lax.bitcast_convert_type(row & jnp.uint32(0xFFFF0000), F32)
    return lo, hi


def _merge_sublanes(a, b, h, sub):
    keep = (sub & h) == 0
    return jnp.where(keep, a + pltpu.roll(a, SUBLANES - h, 0), b + pltpu.roll(b, h, 0))


def _slot_specs(tb):
    return [pl.BlockSpec((None, None, SLOT_GROUP * tb), lambda i, j=j: (j, 0, i), memory_space=pltpu.SMEM,
                         pipeline_mode=pl.Buffered(1)) for j in range(PEER_SLOTS // SLOT_GROUP)]


def _group_slots(a, tb):
    n = a.shape[-1]
    a = a.reshape(PEER_SLOTS // SLOT_GROUP, SLOT_GROUP, n // tb, tb).transpose(0, 2, 1, 3)
    return a.reshape(PEER_SLOTS // SLOT_GROUP, 1, n * SLOT_GROUP)


def _slot_reader(refs, t, tb):
    at = [t + g * tb for g in range(SLOT_GROUP)]
    return lambda k: refs[k // SLOT_GROUP][at[k % SLOT_GROUP]]


def _peer_u_kernel(*refs):
    ngroups = PEER_SLOTS // SLOT_GROUP
    offs = refs[:ngroups]
    x_ref, tbl_ref, o_ref, z_ref, sel_ref = refs[ngroups:]
    tb = x_ref.shape[0] // SUBLANES
    sub = lax.broadcasted_iota(jnp.int32, (SUBLANES, LANES), 0)
    upper = sub < ROWS_PER_EXPERT
    pairs = _pair_slots()
    lane_pair = lax.broadcasted_iota(jnp.int32, (PEER_SLOTS, LANES), 1) >> 1

    row = lax.broadcasted_iota(jnp.int32, (2 * LANES, LANES), 0)
    col = lax.broadcasted_iota(jnp.int32, (2 * LANES, LANES), 1)
    sel_ref[...] = jnp.where((col & 1) == (row >= LANES).astype(jnp.int32), 1.0, 0.0).astype(BF16)
    o_ref[...] = jnp.zeros_like(o_ref)

    def token_group(gi, zbase):
        for j in range(SUBLANES):
            t = gi * SUBLANES + j
            x = x_ref[pl.ds(pl.multiple_of(t * SUBLANES, SUBLANES), SUBLANES), :]
            swapped = pltpu.roll(x, ROWS_PER_EXPERT, 0)
            xa = jnp.where(upper, x, swapped)
            xb = jnp.where(upper, swapped, x)
            off = _slot_reader(offs, t, tb)
            zs = []
            for g in range(PEER_SLOTS // SUBLANES):
                ps = []
                for c in range(4):
                    top, bot = pairs[4 * g + c]
                    lo, hi = _gather_pair(tbl_ref, off(top), off(bot), upper)
                    ps.append(lo * xa + hi * xb)
                r0 = _merge_sublanes(ps[0], ps[1], 2, sub)
                r1 = _merge_sublanes(ps[2], ps[3], 2, sub)
                zs.append(_merge_sublanes(r0, r1, 1, sub))
            rows = zbase + (j // 2) * PEER_SLOTS
            cols = slice((j % 2) * LANES, (j % 2 + 1) * LANES)
            for m in range(len(zs) // 2):
                tile = jnp.concatenate([zs[2 * m], zs[2 * m + 1]], axis=0).astype(BF16)
                z_ref[rows + 2 * SUBLANES * m:rows + 2 * SUBLANES * (m + 1), cols] = tile
        sums = jnp.dot(z_ref[zbase:zbase + 4 * PEER_SLOTS, :], sel_ref[...], preferred_element_type=F32)
        block = pl.ds(pl.multiple_of((gi // (LANES // SUBLANES)) * LANES, LANES), LANES)
        first_pair = (gi % (LANES // SUBLANES)) * 4
        acc = o_ref[:, block]
        for p in range(4):
            acc = jnp.where(lane_pair == first_pair + p, sums[p * PEER_SLOTS:(p + 1) * PEER_SLOTS], acc)
        o_ref[:, block] = acc

    def step(i, carry):
        for r in range(U_GROUPS_PER_ITER):
            token_group(i * U_GROUPS_PER_ITER + r, r * 4 * PEER_SLOTS)
        return carry

    lax.fori_loop(0, tb // (SUBLANES * U_GROUPS_PER_ITER), step, 0)


def _peer_u(offs_g, x_rows, tbl, tb):
    n = x_rows.shape[0] // SUBLANES
    assert tb % LANES == 0 and (tb // SUBLANES) % U_GROUPS_PER_ITER == 0
    return pl.pallas_call(
        _peer_u_kernel,
        out_shape=jax.ShapeDtypeStruct((PEER_SLOTS, n), F32),
        grid=(n // tb,),
        in_specs=_slot_specs(tb) + [
            pl.BlockSpec((tb * SUBLANES, LANES), lambda i: (i, 0)),
            pl.BlockSpec(tbl.shape, lambda i: (0, 0), pipeline_mode=pl.Buffered(1)),
        ],
        out_specs=pl.BlockSpec((PEER_SLOTS, tb), lambda i: (0, i)),
        scratch_shapes=[
            pltpu.VMEM((U_GROUPS_PER_ITER * 4 * PEER_SLOTS, 2 * LANES), BF16),
            pltpu.VMEM((2 * LANES, LANES), BF16),
        ],
        compiler_params=_cparams(("arbitrary",)),
        name="peer_u",
    )(*([offs_g] * (PEER_SLOTS // SLOT_GROUP)), x_rows, tbl)


def _peer_w_kernel(gate_ref, act_ref, w_ref):
    a = act_ref[...]
    w_ref[...] = gate_ref[...] * (0.5 * a * (1.0 + lax.erf(a * (2.0 ** -0.5))))


def _peer_w(gate_t, act_t, tm):
    n = gate_t.shape[-1]
    spec = pl.BlockSpec((PEER_SLOTS, tm), lambda i: (0, i))
    return pl.pallas_call(
        _peer_w_kernel,
        out_shape=jax.ShapeDtypeStruct(gate_t.shape, F32),
        grid=(n // tm,),
        in_specs=[spec, spec],
        out_specs=spec,
        compiler_params=_cparams(("arbitrary",)),
        name="peer_w",
    )(gate_t, act_t)


def _peer_v_kernel(*refs):
    ngroups = PEER_SLOTS // SLOT_GROUP
    offs = refs[:ngroups]
    w_ref, h_ref, tbl_ref, nw_ref, o_ref, whi_ref, wlo_ref, rep_ref, ones_ref = refs[ngroups:]
    tb = h_ref.shape[0] // SUBLANES
    sub = lax.broadcasted_iota(jnp.int32, (SUBLANES, LANES), 0)
    upper = sub < ROWS_PER_EXPERT
    pairs = _pair_slots()
    lane = lax.broadcasted_iota(jnp.int32, (PEER_SLOTS, LANES), 1)

    w = w_ref[...]
    whi = w.astype(BF16).astype(F32)
    whi_ref[...] = whi
    wlo_ref[...] = w - whi
    row = lax.broadcasted_iota(jnp.int32, (2 * LANES, 2 * LANES), 0)
    col = lax.broadcasted_iota(jnp.int32, (2 * LANES, 2 * LANES), 1)
    ones_ref[...] = jnp.where((row >= LANES) == (col >= LANES), 1.0, 0.0).astype(BF16)

    def replicate(t, slot):
        block = pl.ds(pl.multiple_of((t // LANES) * LANES, LANES), LANES)
        la = t % LANES
        out = None
        for part_ref in (whi_ref, wlo_ref):
            part = part_ref[:, block]
            lhs = jnp.concatenate([jnp.where(lane == la, part, 0.0), jnp.where(lane == la + 1, part, 0.0)],
                                  axis=1).astype(BF16)
            prod = jnp.dot(lhs, ones_ref[...], preferred_element_type=F32)
            out = prod if out is None else out + prod
        rep_ref[slot] = out

    def token(t, slot, half):
        off = _slot_reader(offs, t, tb)
        lanes = slice(half * LANES, (half + 1) * LANES)
        acc = [jnp.zeros((SUBLANES, LANES), F32) for _ in range(4)]
        for q, (top, bot) in enumerate(pairs):
            lo, hi = _gather_pair(tbl_ref, off(top), off(bot), upper)
            wtop = jnp.broadcast_to(rep_ref[slot, top:top + 1, lanes], (SUBLANES, LANES))
            wbot = jnp.broadcast_to(rep_ref[slot, bot:bot + 1, lanes], (SUBLANES, LANES))
            wv = jnp.where(upper, wtop, wbot)
            k = 2 * (q % 2)
            acc[k] = acc[k] + lo * wv
            acc[k + 1] = acc[k + 1] + hi * wv
        lo = acc[0] + acc[2]
        hi = acc[1] + acc[3]
        lo = lo + pltpu.roll(lo, ROWS_PER_EXPERT, 0)
        hi = hi + pltpu.roll(hi, ROWS_PER_EXPERT, 0)
        rows = pl.ds(pl.multiple_of(t * SUBLANES, SUBLANES), SUBLANES)
        o_ref[rows, :] = h_ref[rows, :] + jnp.where(upper, lo, hi)

    def step(i, carry):
        for r in range(V_TOKENS_PER_ITER // 2):
            replicate(i * V_TOKENS_PER_ITER + 2 * r, r)
        for r in range(V_TOKENS_PER_ITER):
            token(i * V_TOKENS_PER_ITER + r, r // 2, r % 2)
        return carry

    lax.fori_loop(0, tb // V_TOKENS_PER_ITER, step, 0)

    def norm_step(i, carry):
        rows = pl.ds(pl.multiple_of(i * NORM_TOKENS * SUBLANES, NORM_TOKENS * SUBLANES), NORM_TOKENS * SUBLANES)
        h = o_ref[rows, :].reshape(NORM_TOKENS, SUBLANES, LANES)
        ss = jnp.sum(jnp.sum(h * h, axis=2, keepdims=True), axis=1, keepdims=True)
        y = h * lax.rsqrt(ss * (1.0 / D_MODEL) + RMS_EPS) * nw_ref[...][None]
        o_ref[rows, :] = y.reshape(NORM_TOKENS * SUBLANES, LANES)
        return carry

    lax.fori_loop(0, tb // NORM_TOKENS, norm_step, 0)


def _peer_v(offs_g, w_t, h_rows, tbl, norm_rows, tb):
    n = h_rows.shape[0] // SUBLANES
    assert tb % LANES == 0 and V_TOKENS_PER_ITER % 2 == 0 and tb % NORM_TOKENS == 0
    rows = pl.BlockSpec((tb * SUBLANES, LANES), lambda i: (i, 0))
    return pl.pallas_call(
        _peer_v_kernel,
        out_shape=jax.ShapeDtypeStruct(h_rows.shape, F32),
        grid=(n // tb,),
        in_specs=_slot_specs(tb) + [
            pl.BlockSpec((PEER_SLOTS, tb), lambda i: (0, i)),
            rows,
            pl.BlockSpec(tbl.shape, lambda i: (0, 0), pipeline_mode=pl.Buffered(1)),
            pl.BlockSpec((SUBLANES, LANES), lambda i: (0, 0)),
        ],
        out_specs=rows,
        scratch_shapes=[
            pltpu.VMEM((PEER_SLOTS, tb), F32),
            pltpu.VMEM((PEER_SLOTS, tb), F32),
            pltpu.VMEM((V_TOKENS_PER_ITER // 2, PEER_SLOTS, 2 * LANES), F32),
            pltpu.VMEM((2 * LANES, 2 * LANES), BF16),
        ],
        compiler_params=_cparams(("arbitrary",)),
        name="peer_v",
    )(*([offs_g] * (PEER_SLOTS // SLOT_GROUP)), w_t, h_rows, tbl, norm_rows)


def _pack_table(t):
    half = D_MODEL // 2
    bits = lax.bitcast_convert_type(t.astype(BF16), jnp.uint16).astype(jnp.uint32)
    words = bits[:, :half] | (bits[:, half:] << 16)
    words = words.reshape(t.shape[0] * ROWS_PER_EXPERT, LANES)
    return jnp.pad(words, ((TABLE_PAD, TABLE_PAD), (0, 0)))


def _t5_bucket(rel):
    half = N_BUCKETS // 2
    exact = half // 2
    n = jnp.abs(rel)
    large = exact + (jnp.log(jnp.maximum(n, 1).astype(F32) / exact)
                     / math.log(MAX_DISTANCE / exact) * (half - exact)).astype(jnp.int32)
    large = jnp.minimum(large, half - 1)
    return jnp.where(rel > 0, half, 0) + jnp.where(n < exact, n, large)


def _attention_bias(rel_bias):
    rb = rel_bias.astype(F32)
    i = jnp.arange(ATT_BLOCK)
    jb = jnp.arange(3 * ATT_BLOCK)
    rel = (jb[None, :] - ATT_BLOCK) - i[:, None]
    band = jnp.where((jnp.abs(rel) <= ATT_BLOCK)[..., None], rb[_t5_bucket(rel)], NEG)
    mpos = PREFIX - N_META + jnp.arange(N_META)
    variants = []
    for blk in (0, 1):
        qpos = PREFIX + blk * ATT_BLOCK + i
        meta = rb[_t5_bucket(mpos[None, :] - qpos[:, None])]
        fill = jnp.full((ATT_BLOCK, ATT_BLOCK - N_META, ATT_HEADS), NEG, F32)
        variants.append(jnp.concatenate([band, meta, fill], axis=1).transpose(2, 0, 1))
    return jnp.stack(variants)


def _rope_tables(pos):
    half = RET_D // 2
    inv = ROPE_BASE ** (-jnp.arange(half, dtype=F32) / half)
    ang = pos.astype(F32)[:, None] * inv[None, :]
    cos, sin = jnp.cos(ang), jnp.sin(ang)
    return jnp.concatenate([cos, cos], axis=1), jnp.concatenate([-sin, sin], axis=1)


def _decay_tables(dec_f, dec_b, chunk):
    lf = jax.nn.log_sigmoid(dec_f.astype(F32))[:, None]
    lb = jax.nn.log_sigmoid(dec_b.astype(F32))[:, None]
    idx = jnp.arange(chunk, dtype=F32)
    diff = idx[:, None] - idx[None, :]
    dmat = (jnp.where(diff >= 0, jnp.exp(jnp.maximum(diff, 0.0)[None] * lf[:, :, None]), 0.0)
            + jnp.where(diff < 0, jnp.exp(jnp.maximum(-diff, 0.0)[None] * lb[:, :, None]), 0.0))
    bc = lambda v: jnp.broadcast_to(v[:, :, None], v.shape + (RET_D,))
    qwf = bc(jnp.exp((idx + 1.0)[None] * lf))
    kwf = bc(jnp.exp((chunk - 1.0 - idx)[None] * lf))
    qwb = bc(jnp.exp((chunk - idx)[None] * lb))
    kwb = bc(jnp.exp(idx[None] * lb))
    pidx = jnp.arange(PREFIX, dtype=F32)
    kwp = bc(jnp.exp((PREFIX - 1.0 - pidx)[None] * lf))
    gdec = jnp.concatenate([jnp.exp(chunk * lf[:, 0]), jnp.exp(chunk * lb[:, 0])])
    return dmat, qwf, qwb, kwf, kwb, kwp, gdec


def _divisor_tile(n, want):
    t = min(n, want)
    while n % t:
        t //= 2
    return t


def _encode(x, shared):
    b, s, d = x.shape
    n = b * s
    tm = _divisor_tile(s, 512)
    chunk = _divisor_tile(s, 256)
    cos_t, sin_t = _rope_tables(jnp.arange(s) + N_META)
    qr, kr, vr, gr, qa, ka, va = _proj(x, shared["norm_mix"], shared["w_in"], cos_t, sin_t, tm)
    o_r = _retention(qr, kr, vr, gr, shared["kr_pre"], shared["vr_pre"], *_decay_tables(
        shared["dec_f"], shared["dec_b"], chunk), shared["gn_w"], chunk)
    o_a = _attention(qa, ka, va, shared["k_meta"], shared["v_meta"], shared["bias"], shared["sink"])
    h1, xn2 = _outproj(o_r, o_a, x, shared["w_out"], shared["norm_ffn"], tm)
    idx_t, gate_t = _route(xn2, shared["wq"], shared["pkeys"], _divisor_tile(s, ROUTE_TOKENS))
    idx_t = idx_t.transpose(1, 0, 2).reshape(PEER_SLOTS, n)
    gate_t = gate_t.transpose(1, 0, 2).reshape(PEER_SLOTS, n)
    top_half = (jnp.arange(PEER_SLOTS) % SUBLANES) < ROWS_PER_EXPERT
    tb = _divisor_tile(n, PEER_TOKENS)
    offs_g = _group_slots(idx_t * ROWS_PER_EXPERT + jnp.where(top_half, TABLE_PAD, 0)[:, None], tb)
    act_t = _peer_u(offs_g, xn2.reshape(n * SUBLANES, LANES), shared["u_tbl"], tb)
    w_t = _peer_w(gate_t, act_t, _divisor_tile(n, 2048))
    y = _peer_v(offs_g, w_t, h1.reshape(n * SUBLANES, LANES), shared["v_tbl"], shared["norm_final"], tb)
    return y.reshape(b, s, d)


def kernel(x_prompt, x_sample, meta_tokens, norm_mix_w, w_in, ret_decay_fwd, ret_decay_bwd, ret_gn_w,
           attn_sink, rel_bias, w_out, norm_ffn_w, peer_wq, peer_keys, peer_u, peer_v, norm_final_w):
    layer = 0
    shared = {
        "norm_mix": norm_mix_w[layer][None, :].astype(F32),
        "w_in": w_in[layer].astype(BF16),
        "dec_f": ret_decay_fwd[layer],
        "dec_b": ret_decay_bwd[layer],
        "gn_w": ret_gn_w[layer][None, :].astype(F32),
        "sink": attn_sink[layer].astype(F32),
        "bias": _attention_bias(rel_bias),
        "w_out": w_out[layer].astype(BF16),
        "norm_ffn": norm_ffn_w[layer][None, :].astype(F32),
        "wq": peer_wq[layer].astype(BF16),
        "pkeys": peer_keys[layer].reshape(2 * PEER_HEADS, PEER_NKEYS, PEER_NKEYS).astype(BF16),
        "u_tbl": _pack_table(peer_u[layer]),
        "v_tbl": _pack_table(peer_v[layer]),
        "norm_final": norm_final_w.reshape(SUBLANES, LANES).astype(F32),
    }
    prefix = jnp.concatenate([jnp.zeros((PREFIX - N_META, D_MODEL), x_prompt.dtype),
                              meta_tokens.astype(x_prompt.dtype)], axis=0)[None]
    cos_p, sin_p = _rope_tables(jnp.arange(PREFIX) - (PREFIX - N_META))
    _, kr_p, vr_p, _, _, ka_p, va_p = _proj(prefix, shared["norm_mix"], shared["w_in"], cos_p, sin_p, PREFIX)
    shared["kr_pre"], shared["vr_pre"] = kr_p[0], vr_p[0]
    pad_meta = lambda t: jnp.pad(t[0, PREFIX - N_META:], ((0, ATT_BLOCK - N_META), (0, 0)))
    shared["k_meta"], shared["v_meta"] = pad_meta(ka_p), pad_meta(va_p)
    return (_encode(x_prompt, shared), _encode(x_sample, shared))
```

```python
import functools
import math

import jax
import jax.numpy as jnp
from jax import lax
from jax.experimental import pallas as pl
from jax.experimental.pallas import tpu as pltpu

F32 = jnp.float32
BF16 = jnp.bfloat16

D_MODEL = 1024
N_META = 16
PREFIX = 128
RET_HEADS = 4
RET_D = 128
ATT_HEADS = 8
ATT_KV = 2
ATT_GROUP = ATT_HEADS // ATT_KV
ATT_HD = 64
ATT_BLOCK = 128
N_BUCKETS = 32
MAX_DISTANCE = 128
ROPE_BASE = 10000.0
PEER_HEADS = 8
PEER_NKEYS = 128
PEER_EXPERTS = PEER_NKEYS * PEER_NKEYS
PEER_TOPK = 16
PEER_SLOTS = PEER_HEADS * PEER_TOPK
RMS_EPS = 1e-6
GN_EPS = 1e-5
NEG = -1e30
IN_COLS = (0, 512, 1024, 1536, 2048, 2560, 2688, 2816)

LANES = 128
SUBLANES = 8
VMEM_LIMIT = 56 * 1024 * 1024
ROWS_PER_EXPERT = D_MODEL // 2 // LANES
TABLE_PAD = ROWS_PER_EXPERT
PEER_TOKENS = 256
ROUTE_TOKENS = 256
SLOT_ORDER = (0, 2, 1, 3)
SLOT_GROUP = 1
U_GROUPS_PER_ITER = 2
V_TOKENS_PER_ITER = 8
NORM_TOKENS = 32


def _cparams(sem):
    return pltpu.CompilerParams(dimension_semantics=sem, vmem_limit_bytes=VMEM_LIMIT)


def _proj_kernel(x_ref, nw_ref, w_ref, cos_ref, sin_ref,
                 qr_ref, kr_ref, vr_ref, gr_ref, qa_ref, ka_ref, va_ref):
    x = x_ref[0]
    ms = jnp.mean(x * x, axis=-1, keepdims=True)
    xn = (x * lax.rsqrt(ms + RMS_EPS) * nw_ref[...]).astype(BF16)
    cosf = cos_ref[...]
    sinf = sin_ref[...]

    def mm(i):
        return jnp.dot(xn, w_ref[:, IN_COLS[i]:IN_COLS[i + 1]], preferred_element_type=F32)

    def rotary(t, scale):
        for h in range(RET_HEADS):
            th = t[:, h * RET_D:(h + 1) * RET_D]
            yield h, (th * cosf + pltpu.roll(th, RET_D // 2, 1) * sinf) * scale

    for h, r in rotary(mm(0), 1.0):
        qr_ref[0, :, h * RET_D:(h + 1) * RET_D] = r.astype(BF16)
    for h, r in rotary(mm(1), RET_D ** -0.5):
        kr_ref[0, :, h * RET_D:(h + 1) * RET_D] = r.astype(BF16)
    vr_ref[0] = mm(2).astype(BF16)
    gr_ref[0] = mm(3)
    qa_ref[0] = (mm(4) * (ATT_HD ** -0.5)).astype(BF16)
    ka_ref[0] = mm(5).astype(BF16)
    va_ref[0] = mm(6).astype(BF16)


def _proj(x, norm_w, w_in_bf, cos_t, sin_t, tm):
    b, s, d = x.shape
    widths = (512, 512, 512, 512, 512, 128, 128)
    dtypes = (BF16, BF16, BF16, F32, BF16, BF16, BF16)
    return pl.pallas_call(
        _proj_kernel,
        out_shape=[jax.ShapeDtypeStruct((b, s, w), dt) for w, dt in zip(widths, dtypes)],
        grid=(b, s // tm),
        in_specs=[
            pl.BlockSpec((1, tm, d), lambda i, j: (i, j, 0)),
            pl.BlockSpec((1, d), lambda i, j: (0, 0)),
            pl.BlockSpec(w_in_bf.shape, lambda i, j: (0, 0)),
            pl.BlockSpec((tm, LANES), lambda i, j: (j, 0)),
            pl.BlockSpec((tm, LANES), lambda i, j: (j, 0)),
        ],
        out_specs=[pl.BlockSpec((1, tm, w), lambda i, j: (i, j, 0)) for w in widths],
        compiler_params=_cparams(("arbitrary", "arbitrary")),
        name="proj",
    )(x, norm_w, w_in_bf, cos_t, sin_t)


def _ret_kernel(q_ref, k_ref, v_ref, g_ref, kpre_ref, vpre_ref, dmat_ref, qwf_ref, qwb_ref,
                kwf_ref, kwb_ref, kwp_ref, gdec_ref, gnw_ref, o_ref, sf_ref, sb_ref, sball_ref):
    ph = pl.program_id(1)
    c = pl.program_id(2)
    nc = pl.num_programs(2)
    contract0 = (((0,), (0,)), ((), ()))
    contract1 = (((1,), (1,)), ((), ()))

    def head(ref, h):
        return ref[0, :, h * RET_D:(h + 1) * RET_D]

    def kv_update(kh, vh, kw):
        kw = (kh.astype(F32) * kw).astype(BF16)
        return lax.dot_general(kw, vh, contract0, preferred_element_type=F32)

    @pl.when(ph == 0)
    def _backward_states():
        @pl.when(c == 0)
        def _():
            sb_ref[...] = jnp.zeros_like(sb_ref)

        j = nc - 1 - c
        for h in range(RET_HEADS):
            sball_ref[j, h] = sb_ref[h].astype(BF16)
            sb_ref[h] = gdec_ref[RET_HEADS + h] * sb_ref[h] + kv_update(head(k_ref, h), head(v_ref, h), kwb_ref[h])

    @pl.when(ph == 1)
    def _outputs():
        @pl.when(c == 0)
        def _():
            for h in range(RET_HEADS):
                kp = kpre_ref[:, h * RET_D:(h + 1) * RET_D]
                vp = vpre_ref[:, h * RET_D:(h + 1) * RET_D]
                sf_ref[h] = kv_update(kp, vp, kwp_ref[h])

        for h in range(RET_HEADS):
            qh, kh, vh = head(q_ref, h), head(k_ref, h), head(v_ref, h)
            s = lax.dot_general(qh, kh, contract1, preferred_element_type=F32) * dmat_ref[h]
            o = jnp.dot(s.astype(BF16), vh, preferred_element_type=F32)
            o = o + jnp.dot(qh, sf_ref[h].astype(BF16), preferred_element_type=F32) * qwf_ref[h]
            o = o + jnp.dot(qh, sball_ref[c, h], preferred_element_type=F32) * qwb_ref[h]
            mu = jnp.mean(o, axis=-1, keepdims=True)
            oc = o - mu
            var = jnp.mean(oc * oc, axis=-1, keepdims=True)
            on = oc * lax.rsqrt(var + GN_EPS) * gnw_ref[:, h * RET_D:(h + 1) * RET_D]
            g = head(g_ref, h)
            o_ref[0, :, h * RET_D:(h + 1) * RET_D] = (g * jax.nn.sigmoid(g) * on).astype(BF16)
            sf_ref[h] = gdec_ref[h] * sf_ref[h] + kv_update(kh, vh, kwf_ref[h])


def _retention(qr, kr, vr, gr, kpre, vpre, dmat, qwf, qwb, kwf, kwb, kwp, gdec, gnw, chunk):
    b, s, w = qr.shape
    nc = s // chunk

    def seq_map(i, ph, c):
        return (i, jnp.where(ph == 0, nc - 1 - c, c), 0)

    def fwd_only(i, ph, c):
        return (i, c * ph, 0)

    const2 = lambda i, ph, c: (0, 0)
    const3 = lambda i, ph, c: (0, 0, 0)
    return pl.pallas_call(
        _ret_kernel,
        out_shape=jax.ShapeDtypeStruct((b, s, w), BF16),
        grid=(b, 2, nc),
        in_specs=[
            pl.BlockSpec((1, chunk, w), fwd_only),
            pl.BlockSpec((1, chunk, w), seq_map),
            pl.BlockSpec((1, chunk, w), seq_map),
            pl.BlockSpec((1, chunk, w), fwd_only),
            pl.BlockSpec(kpre.shape, const2),
            pl.BlockSpec(vpre.shape, const2),
            pl.BlockSpec(dmat.shape, const3),
            pl.BlockSpec(qwf.shape, const3),
            pl.BlockSpec(qwb.shape, const3),
            pl.BlockSpec(kwf.shape, const3),
            pl.BlockSpec(kwb.shape, const3),
            pl.BlockSpec(kwp.shape, const3),
            pl.BlockSpec(memory_space=pltpu.SMEM),
            pl.BlockSpec(gnw.shape, const2),
        ],
        out_specs=pl.BlockSpec((1, chunk, w), fwd_only),
        scratch_shapes=[
            pltpu.VMEM((RET_HEADS, RET_D, RET_D), F32),
            pltpu.VMEM((RET_HEADS, RET_D, RET_D), F32),
            pltpu.VMEM((nc, RET_HEADS, RET_D, RET_D), BF16),
        ],
        compiler_params=_cparams(("arbitrary", "arbitrary", "arbitrary")),
        name="retention",
    )(qr, kr, vr, gr, kpre, vpre, dmat, qwf, qwb, kwf, kwb, kwp, gdec, gnw)


def _attn_kernel(q_ref, kp_ref, kc_ref, kn_ref, vp_ref, vc_ref, vn_ref, km_ref, vm_ref,
                 bias_ref, sink_ref, o_ref):
    j = pl.program_id(1)
    nb = pl.num_programs(1)
    lane = lax.broadcasted_iota(jnp.int32, (1, 4 * ATT_BLOCK), 1)
    first = jnp.logical_and(lane < ATT_BLOCK, j == 0)
    last = jnp.logical_and(jnp.logical_and(lane >= 2 * ATT_BLOCK, lane < 3 * ATT_BLOCK), j == nb - 1)
    pen = jnp.where(jnp.logical_or(first, last), NEG, 0.0).astype(F32)
    contract1 = (((1,), (1,)), ((), ()))
    outs = []
    for g in range(ATT_KV):
        sl = slice(g * ATT_HD, (g + 1) * ATT_HD)
        kall = jnp.concatenate([kp_ref[0][:, sl], kc_ref[0][:, sl], kn_ref[0][:, sl], km_ref[:, sl]], axis=0)
        vall = jnp.concatenate([vp_ref[0][:, sl], vc_ref[0][:, sl], vn_ref[0][:, sl], vm_ref[:, sl]], axis=0)
        for hh in range(ATT_GROUP):
            h = g * ATT_GROUP + hh
            q = q_ref[0, :, h * ATT_HD:(h + 1) * ATT_HD]
            s = lax.dot_general(q, kall, contract1, preferred_element_type=F32)
            s = s + bias_ref[0, h] + pen
            snk = sink_ref[h]
            m = jnp.maximum(jnp.max(s, axis=-1, keepdims=True), snk)
            p = jnp.exp(s - m)
            den = jnp.sum(p, axis=-1, keepdims=True) + jnp.exp(snk - m)
            o = jnp.dot(p.astype(BF16), vall, preferred_element_type=F32)
            outs.append(o / den)
    for h2 in range(ATT_HEADS // 2):
        o_ref[0, :, h2 * LANES:(h2 + 1) * LANES] = jnp.concatenate(
            [outs[2 * h2], outs[2 * h2 + 1]], axis=-1).astype(BF16)


def _attention(qa, ka, va, kmeta, vmeta, bias, sink):
    b, s, w = qa.shape
    nb = s // ATT_BLOCK
    kvw = ka.shape[-1]
    prev = lambda i, j: (i, jnp.maximum(j - 1, 0), 0)
    cur = lambda i, j: (i, j, 0)
    nxt = lambda i, j: (i, jnp.minimum(j + 1, nb - 1), 0)
    const2 = lambda i, j: (0, 0)
    kv = lambda m: pl.BlockSpec((1, ATT_BLOCK, kvw), m)
    return pl.pallas_call(
        _attn_kernel,
        out_shape=jax.ShapeDtypeStruct((b, s, w), BF16),
        grid=(b, nb),
        in_specs=[
            pl.BlockSpec((1, ATT_BLOCK, w), cur),
            kv(prev), kv(cur), kv(nxt), kv(prev), kv(cur), kv(nxt),
            pl.BlockSpec(kmeta.shape, const2),
            pl.BlockSpec(vmeta.shape, const2),
            pl.BlockSpec((1,) + bias.shape[1:], lambda i, j: (jnp.minimum(j, 1), 0, 0, 0)),
            pl.BlockSpec(memory_space=pltpu.SMEM),
        ],
        out_specs=pl.BlockSpec((1, ATT_BLOCK, w), cur),
        compiler_params=_cparams(("arbitrary", "arbitrary")),
        name="attention",
    )(qa, ka, ka, ka, va, va, va, kmeta, vmeta, bias, sink)


def _outproj_kernel(or_ref, oa_ref, x_ref, w_ref, nw_ref, h_ref, xn_ref):
    half = or_ref.shape[-1]
    h = x_ref[0]
    h = h + jnp.dot(or_ref[0], w_ref[:half], preferred_element_type=F32)
    h = h + jnp.dot(oa_ref[0], w_ref[half:], preferred_element_type=F32)
    h_ref[0] = h
    ms = jnp.mean(h * h, axis=-1, keepdims=True)
    xn_ref[0] = h * lax.rsqrt(ms + RMS_EPS) * nw_ref[...]


def _outproj(o_r, o_a, x, w_out_bf, norm_w, tm):
    b, s, d = x.shape
    half = o_r.shape[-1]
    row = lambda i, j: (i, j, 0)
    const2 = lambda i, j: (0, 0)
    return pl.pallas_call(
        _outproj_kernel,
        out_shape=[jax.ShapeDtypeStruct((b, s, d), F32)] * 2,
        grid=(b, s // tm),
        in_specs=[
            pl.BlockSpec((1, tm, half), row),
            pl.BlockSpec((1, tm, half), row),
            pl.BlockSpec((1, tm, d), row),
            pl.BlockSpec(w_out_bf.shape, const2),
            pl.BlockSpec((1, d), const2),
        ],
        out_specs=[pl.BlockSpec((1, tm, d), row)] * 2,
        compiler_params=_cparams(("arbitrary", "arbitrary")),
        name="outproj",
    )(o_r, o_a, x, w_out_bf, norm_w)


def _top16_rows(s, iota):
    nrows = float(s.shape[0])
    vals, ids = [], []
    for _ in range(PEER_TOPK):
        m = jnp.max(s, axis=0, keepdims=True)
        am = jnp.min(jnp.where(s == m, iota, nrows), axis=0, keepdims=True)
        vals.append(m)
        ids.append(am)
        s = jnp.where(iota == am, -jnp.inf, s)
    return jnp.concatenate(vals, axis=0), jnp.concatenate(ids, axis=0)


_PAIR_GROUPS = ((0, 0, 8), (0, 8, 8), (1, 0, 8), (2, 0, 5), (3, 0, 4), (4, 0, 3), (5, 0, 2), (6, 0, 2), (7, 0, 2))


def _route_kernel(x_ref, wq_ref, pk_ref, idx_ref, gate_ref, sc_ref):
    tm = x_ref.shape[1]
    q = jnp.dot(x_ref[0].astype(BF16), wq_ref[...], preferred_element_type=F32).astype(BF16)
    contract1 = (((1,), (1,)), ((), ()))
    for hp in range(2 * PEER_HEADS):
        sc_ref[hp] = lax.dot_general(pk_ref[hp], q[:, hp * PEER_NKEYS:(hp + 1) * PEER_NKEYS], contract1,
                                     preferred_element_type=F32)
    iota = lax.broadcasted_iota(jnp.int32, (PEER_NKEYS, LANES), 0).astype(F32)
    sub = lax.broadcasted_iota(jnp.int32, (SUBLANES, LANES), 0)
    subf = sub.astype(F32)
    far = float(PEER_TOPK * PEER_TOPK)

    def head_body(h, carry):
        for lt in range(tm // LANES):
            lanes = slice(lt * LANES, (lt + 1) * LANES)
            s1, i1 = _top16_rows(sc_ref[2 * h, :, lanes], iota)
            s2, i2 = _top16_rows(sc_ref[2 * h + 1, :, lanes], iota)
            i1 = i1 * float(PEER_NKEYS)
            sc, ex, pos = [], [], []
            for i, j0, cnt in _PAIR_GROUPS:
                ok = sub < cnt
                sc.append(jnp.where(ok, s1[i:i + 1] + s2[j0:j0 + SUBLANES], -jnp.inf))
                ex.append(i1[i:i + 1] + i2[j0:j0 + SUBLANES])
                pos.append(jnp.where(ok, float(i * PEER_TOPK + j0) + subf, far))
            sc.append(s1[SUBLANES:] + s2[0:1])
            ex.append(i1[SUBLANES:] + i2[0:1])
            pos.append((float(SUBLANES) + subf) * float(PEER_TOPK))
            sc = jnp.concatenate(sc, axis=0)
            ex = jnp.concatenate(ex, axis=0)
            pos = jnp.concatenate(pos, axis=0)
            top, experts = [], []
            for _ in range(PEER_TOPK):
                m = jnp.max(sc, axis=0, keepdims=True)
                pm = jnp.min(jnp.where(sc == m, pos, far), axis=0, keepdims=True)
                hit = pos == pm
                experts.append(jnp.max(jnp.where(hit, ex, -1.0), axis=0, keepdims=True))
                top.append(m)
                sc = jnp.where(hit, -jnp.inf, sc)
            top = jnp.concatenate(top, axis=0)
            e = jnp.exp(top - top[0:1])
            rows = pl.ds(pl.multiple_of(h * PEER_TOPK, PEER_TOPK), PEER_TOPK)
            gate_ref[0, rows, lanes] = e / jnp.sum(e, axis=0, keepdims=True)
            idx_ref[0, rows, lanes] = jnp.concatenate(experts, axis=0).astype(jnp.int32)
        return carry

    lax.fori_loop(0, PEER_HEADS, head_body, 0)


def _route(xn, wq_bf, pk_bf, tm):
    b, s, d = xn.shape
    const2 = lambda i, j: (0, 0)
    slot = lambda i, j: (i, 0, j)
    return pl.pallas_call(
        _route_kernel,
        out_shape=[jax.ShapeDtypeStruct((b, PEER_SLOTS, s), jnp.int32),
                   jax.ShapeDtypeStruct((b, PEER_SLOTS, s), F32)],
        grid=(b, s // tm),
        in_specs=[
            pl.BlockSpec((1, tm, d), lambda i, j: (i, j, 0)),
            pl.BlockSpec(wq_bf.shape, const2),
            pl.BlockSpec(pk_bf.shape, lambda i, j: (0, 0, 0)),
        ],
        out_specs=[pl.BlockSpec((1, PEER_SLOTS, tm), slot)] * 2,
        scratch_shapes=[pltpu.VMEM((2 * PEER_HEADS, PEER_NKEYS, tm), F32)],
        compiler_params=_cparams(("arbitrary", "arbitrary")),
        name="route",
    )(xn, wq_bf, pk_bf)


def _pair_slots():
    out = []
    for g in range(PEER_SLOTS // SUBLANES):
        for c in range(4):
            top = SUBLANES * g + SLOT_ORDER[c]
            out.append((top, top + 4))
    return out


def _gather_pair(tbl_ref, top_off, bot_off, upper):
    ra = tbl_ref[pl.ds(pl.multiple_of(top_off, ROWS_PER_EXPERT), SUBLANES), :]
    rb = tbl_ref[pl.ds(pl.multiple_of(bot_off, ROWS_PER_EXPERT), SUBLANES), :]
    row = jnp.where(upper, ra, rb)
    lo = lax.bitcast_convert_type(row << 16, F32)
    hi = lax.bitcast_convert_type(row & jnp.uint32(0xFFFF0000), F32)
    return lo, hi


def _merge_sublanes(a, b, h, sub):
    keep = (sub & h) == 0
    return jnp.where(keep, a + pltpu.roll(a, SUBLANES - h, 0), b + pltpu.roll(b, h, 0))


def _slot_specs(tb):
    return [pl.BlockSpec((None, None, SLOT_GROUP * tb), lambda i, j=j: (j, 0, i), memory_space=pltpu.SMEM,
                         pipeline_mode=pl.Buffered(1)) for j in range(PEER_SLOTS // SLOT_GROUP)]


def _group_slots(a, tb):
    n = a.shape[-1]
    a = a.reshape(PEER_SLOTS // SLOT_GROUP, SLOT_GROUP, n // tb, tb).transpose(0, 2, 1, 3)
    return a.reshape(PEER_SLOTS // SLOT_GROUP, 1, n * SLOT_GROUP)


def _slot_reader(refs, t, tb):
    at = [t + g * tb for g in range(SLOT_GROUP)]
    return lambda k: refs[k // SLOT_GROUP][at[k % SLOT_GROUP]]


def _peer_u_kernel(*refs):
    ngroups = PEER_SLOTS // SLOT_GROUP
    offs = refs[:ngroups]
    x_ref, tbl_ref, o_ref, z_ref, sel_ref = refs[ngroups:]
    tb = x_ref.shape[0] // SUBLANES
    sub = lax.broadcasted_iota(jnp.int32, (SUBLANES, LANES), 0)
    upper = sub < ROWS_PER_EXPERT
    pairs = _pair_slots()
    lane_pair = lax.broadcasted_iota(jnp.int32, (PEER_SLOTS, LANES), 1) >> 1

    row = lax.broadcasted_iota(jnp.int32, (2 * LANES, LANES), 0)
    col = lax.broadcasted_iota(jnp.int32, (2 * LANES, LANES), 1)
    sel_ref[...] = jnp.where((col & 1) == (row >= LANES).astype(jnp.int32), 1.0, 0.0).astype(BF16)
    o_ref[...] = jnp.zeros_like(o_ref)

    def token_group(gi, zbase):
        for j in range(SUBLANES):
            t = gi * SUBLANES + j
            x = x_ref[pl.ds(pl.multiple_of(t * SUBLANES, SUBLANES), SUBLANES), :]
            swapped = pltpu.roll(x, ROWS_PER_EXPERT, 0)
            xa = jnp.where(upper, x, swapped)
            xb = jnp.where(upper, swapped, x)
            off = _slot_reader(offs, t, tb)
            zs = []
            for g in range(PEER_SLOTS // SUBLANES):
                ps = []
                for c in range(4):
                    top, bot = pairs[4 * g + c]
                    lo, hi = _gather_pair(tbl_ref, off(top), off(bot), upper)
                    ps.append(lo * xa + hi * xb)
                r0 = _merge_sublanes(ps[0], ps[1], 2, sub)
                r1 = _merge_sublanes(ps[2], ps[3], 2, sub)
                zs.append(_merge_sublanes(r0, r1, 1, sub))
            rows = zbase + (j // 2) * PEER_SLOTS
            cols = slice((j % 2) * LANES, (j % 2 + 1) * LANES)
            for m in range(len(zs) // 2):
                tile = jnp.concatenate([zs[2 * m], zs[2 * m + 1]], axis=0).astype(BF16)
                z_ref[rows + 2 * SUBLANES * m:rows + 2 * SUBLANES * (m + 1), cols] = tile
        sums = jnp.dot(z_ref[zbase:zbase + 4 * PEER_SLOTS, :], sel_ref[...], preferred_element_type=F32)
        block = pl.ds(pl.multiple_of((gi // (LANES // SUBLANES)) * LANES, LANES), LANES)
        first_pair = (gi % (LANES // SUBLANES)) * 4
        acc = o_ref[:, block]
        for p in range(4):
            acc = jnp.where(lane_pair == first_pair + p, sums[p * PEER_SLOTS:(p + 1) * PEER_SLOTS], acc)
        o_ref[:, block] = acc

    def step(i, carry):
        for r in range(U_GROUPS_PER_ITER):
            token_group(i * U_GROUPS_PER_ITER + r, r * 4 * PEER_SLOTS)
        return carry

    lax.fori_loop(0, tb // (SUBLANES * U_GROUPS_PER_ITER), step, 0)


def _peer_u(offs_g, x_rows, tbl, tb):
    n = x_rows.shape[0] // SUBLANES
    assert tb % LANES == 0 and (tb // SUBLANES) % U_GROUPS_PER_ITER == 0
    return pl.pallas_call(
        _peer_u_kernel,
        out_shape=jax.ShapeDtypeStruct((PEER_SLOTS, n), F32),
        grid=(n // tb,),
        in_specs=_slot_specs(tb) + [
            pl.BlockSpec((tb * SUBLANES, LANES), lambda i: (i, 0)),
            pl.BlockSpec(tbl.shape, lambda i: (0, 0), pipeline_mode=pl.Buffered(1)),
        ],
        out_specs=pl.BlockSpec((PEER_SLOTS, tb), lambda i: (0, i)),
        scratch_shapes=[
            pltpu.VMEM((U_GROUPS_PER_ITER * 4 * PEER_SLOTS, 2 * LANES), BF16),
            pltpu.VMEM((2 * LANES, LANES), BF16),
        ],
        compiler_params=_cparams(("arbitrary",)),
        name="peer_u",
    )(*([offs_g] * (PEER_SLOTS // SLOT_GROUP)), x_rows, tbl)


def _peer_w_kernel(gate_ref, act_ref, w_ref):
    a = act_ref[...]
    w_ref[...] = gate_ref[...] * (0.5 * a * (1.0 + lax.erf(a * (2.0 ** -0.5))))


def _peer_w(gate_t, act_t, tm):
    n = gate_t.shape[-1]
    spec = pl.BlockSpec((PEER_SLOTS, tm), lambda i: (0, i))
    return pl.pallas_call(
        _peer_w_kernel,
        out_shape=jax.ShapeDtypeStruct(gate_t.shape, F32),
        grid=(n // tm,),
        in_specs=[spec, spec],
        out_specs=spec,
        compiler_params=_cparams(("arbitrary",)),
        name="peer_w",
    )(gate_t, act_t)


def _peer_v_kernel(*refs):
    ngroups = PEER_SLOTS // SLOT_GROUP
    offs = refs[:ngroups]
    w_ref, h_ref, tbl_ref, nw_ref, o_ref, whi_ref, wlo_ref, repa_ref, repb_ref, ones_ref = refs[ngroups:]
    tb = h_ref.shape[0] // SUBLANES
    sub = lax.broadcasted_iota(jnp.int32, (SUBLANES, LANES), 0)
    upper = sub < ROWS_PER_EXPERT
    pairs = _pair_slots()
    lane = lax.broadcasted_iota(jnp.int32, (PEER_SLOTS, LANES), 1)

    w = w_ref[...]
    whi = w.astype(BF16).astype(F32)
    whi_ref[...] = whi
    wlo_ref[...] = w - whi
    row = lax.broadcasted_iota(jnp.int32, (2 * LANES, 2 * LANES), 0)
    col = lax.broadcasted_iota(jnp.int32, (2 * LANES, 2 * LANES), 1)
    ones_ref[...] = jnp.where((row >= LANES) == (col >= LANES), 1.0, 0.0).astype(BF16)

    def replicate(t0, rep_ref):
        for r in range(V_TOKENS_PER_ITER // 2):
            t = t0 + 2 * r
            block = pl.ds(pl.multiple_of((t // LANES) * LANES, LANES), LANES)
            la = t % LANES
            out = None
            for part_ref in (whi_ref, wlo_ref):
                part = part_ref[:, block]
                lhs = jnp.concatenate([jnp.where(lane == la, part, 0.0), jnp.where(lane == la + 1, part, 0.0)],
                                      axis=1).astype(BF16)
                prod = jnp.dot(lhs, ones_ref[...], preferred_element_type=F32)
                out = prod if out is None else out + prod
            rep_ref[r] = out

    def token(t, rep_ref, slot, half):
        off = _slot_reader(offs, t, tb)
        lanes = slice(half * LANES, (half + 1) * LANES)
        acc = [jnp.zeros((SUBLANES, LANES), F32) for _ in range(4)]
        for q, (top, bot) in enumerate(pairs):
            lo, hi = _gather_pair(tbl_ref, off(top), off(bot), upper)
            wtop = jnp.broadcast_to(rep_ref[slot, top:top + 1, lanes], (SUBLANES, LANES))
            wbot = jnp.broadcast_to(rep_ref[slot, bot:bot + 1, lanes], (SUBLANES, LANES))
            wv = jnp.where(upper, wtop, wbot)
            k = 2 * (q % 2)
            acc[k] = acc[k] + lo * wv
            acc[k + 1] = acc[k + 1] + hi * wv
        lo = acc[0] + acc[2]
        hi = acc[1] + acc[3]
        lo = lo + pltpu.roll(lo, ROWS_PER_EXPERT, 0)
        hi = hi + pltpu.roll(hi, ROWS_PER_EXPERT, 0)
        rows = pl.ds(pl.multiple_of(t * SUBLANES, SUBLANES), SUBLANES)
        o_ref[rows, :] = h_ref[rows, :] + jnp.where(upper, lo, hi)

    def tokens(t0, rep_ref):
        for r in range(V_TOKENS_PER_ITER):
            token(t0 + r, rep_ref, r // 2, r % 2)

    replicate(0, repa_ref)

    def step(i, carry):
        t0 = i * 2 * V_TOKENS_PER_ITER
        replicate(t0 + V_TOKENS_PER_ITER, repb_ref)
        tokens(t0, repa_ref)
        replicate(jnp.minimum(t0 + 2 * V_TOKENS_PER_ITER, tb - V_TOKENS_PER_ITER), repa_ref)
        tokens(t0 + V_TOKENS_PER_ITER, repb_ref)
        return carry

    lax.fori_loop(0, tb // (2 * V_TOKENS_PER_ITER), step, 0)

    def norm_step(i, carry):
        rows = pl.ds(pl.multiple_of(i * NORM_TOKENS * SUBLANES, NORM_TOKENS * SUBLANES), NORM_TOKENS * SUBLANES)
        h = o_ref[rows, :].reshape(NORM_TOKENS, SUBLANES, LANES)
        ss = jnp.sum(jnp.sum(h * h, axis=2, keepdims=True), axis=1, keepdims=True)
        y = h * lax.rsqrt(ss * (1.0 / D_MODEL) + RMS_EPS) * nw_ref[...][None]
        o_ref[rows, :] = y.reshape(NORM_TOKENS * SUBLANES, LANES)
        return carry

    lax.fori_loop(0, tb // NORM_TOKENS, norm_step, 0)


def _peer_v(offs_g, w_t, h_rows, tbl, norm_rows, tb):
    n = h_rows.shape[0] // SUBLANES
    assert tb % LANES == 0 and V_TOKENS_PER_ITER % 2 == 0 and tb % (2 * V_TOKENS_PER_ITER) == 0 and tb % NORM_TOKENS == 0
    rows = pl.BlockSpec((tb * SUBLANES, LANES), lambda i: (i, 0))
    return pl.pallas_call(
        _peer_v_kernel,
        out_shape=jax.ShapeDtypeStruct(h_rows.shape, F32),
        grid=(n // tb,),
        in_specs=_slot_specs(tb) + [
            pl.BlockSpec((PEER_SLOTS, tb), lambda i: (0, i)),
            rows,
            pl.BlockSpec(tbl.shape, lambda i: (0, 0), pipeline_mode=pl.Buffered(1)),
            pl.BlockSpec((SUBLANES, LANES), lambda i: (0, 0)),
        ],
        out_specs=rows,
        scratch_shapes=[
            pltpu.VMEM((PEER_SLOTS, tb), F32),
            pltpu.VMEM((PEER_SLOTS, tb), F32),
            pltpu.VMEM((V_TOKENS_PER_ITER // 2, PEER_SLOTS, 2 * LANES), F32),
            pltpu.VMEM((V_TOKENS_PER_ITER // 2, PEER_SLOTS, 2 * LANES), F32),
            pltpu.VMEM((2 * LANES, 2 * LANES), BF16),
        ],
        compiler_params=_cparams(("arbitrary",)),
        name="peer_v",
    )(*([offs_g] * (PEER_SLOTS // SLOT_GROUP)), w_t, h_rows, tbl, norm_rows)


def _pack_table(t):
    half = D_MODEL // 2
    bits = lax.bitcast_convert_type(t.astype(BF16), jnp.uint16).astype(jnp.uint32)
    words = bits[:, :half] | (bits[:, half:] << 16)
    words = words.reshape(t.shape[0] * ROWS_PER_EXPERT, LANES)
    return jnp.pad(words, ((TABLE_PAD, TABLE_PAD), (0, 0)))


def _t5_bucket(rel):
    half = N_BUCKETS // 2
    exact = half // 2
    n = jnp.abs(rel)
    large = exact + (jnp.log(jnp.maximum(n, 1).astype(F32) / exact)
                     / math.log(MAX_DISTANCE / exact) * (half - exact)).astype(jnp.int32)
    large = jnp.minimum(large, half - 1)
    return jnp.where(rel > 0, half, 0) + jnp.where(n < exact, n, large)


def _attention_bias(rel_bias):
    rb = rel_bias.astype(F32)
    i = jnp.arange(ATT_BLOCK)
    jb = jnp.arange(3 * ATT_BLOCK)
    rel = (jb[None, :] - ATT_BLOCK) - i[:, None]
    band = jnp.where((jnp.abs(rel) <= ATT_BLOCK)[..., None], rb[_t5_bucket(rel)], NEG)
    mpos = PREFIX - N_META + jnp.arange(N_META)
    variants = []
    for blk in (0, 1):
        qpos = PREFIX + blk * ATT_BLOCK + i
        meta = rb[_t5_bucket(mpos[None, :] - qpos[:, None])]
        fill = jnp.full((ATT_BLOCK, ATT_BLOCK - N_META, ATT_HEADS), NEG, F32)
        variants.append(jnp.concatenate([band, meta, fill], axis=1).transpose(2, 0, 1))
    return jnp.stack(variants)


def _rope_tables(pos):
    half = RET_D // 2
    inv = ROPE_BASE ** (-jnp.arange(half, dtype=F32) / half)
    ang = pos.astype(F32)[:, None] * inv[None, :]
    cos, sin = jnp.cos(ang), jnp.sin(ang)
    return jnp.concatenate([cos, cos], axis=1), jnp.concatenate([-sin, sin], axis=1)


def _decay_tables(dec_f, dec_b, chunk):
    lf = jax.nn.log_sigmoid(dec_f.astype(F32))[:, None]
    lb = jax.nn.log_sigmoid(dec_b.astype(F32))[:, None]
    idx = jnp.arange(chunk, dtype=F32)
    diff = idx[:, None] - idx[None, :]
    dmat = (jnp.where(diff >= 0, jnp.exp(jnp.maximum(diff, 0.0)[None] * lf[:, :, None]), 0.0)
            + jnp.where(diff < 0, jnp.exp(jnp.maximum(-diff, 0.0)[None] * lb[:, :, None]), 0.0))
    bc = lambda v: jnp.broadcast_to(v[:, :, None], v.shape + (RET_D,))
    qwf = bc(jnp.exp((idx + 1.0)[None] * lf))
    kwf = bc(jnp.exp((chunk - 1.0 - idx)[None] * lf))
    qwb = bc(jnp.exp((chunk - idx)[None] * lb))
    kwb = bc(jnp.exp(idx[None] * lb))
    pidx = jnp.arange(PREFIX, dtype=F32)
    kwp = bc(jnp.exp((PREFIX - 1.0 - pidx)[None] * lf))
    gdec = jnp.concatenate([jnp.exp(chunk * lf[:, 0]), jnp.exp(chunk * lb[:, 0])])
    return dmat, qwf, qwb, kwf, kwb, kwp, gdec


def _divisor_tile(n, want):
    t = min(n, want)
    while n % t:
        t //= 2
    return t


def _encode(x, shared):
    b, s, d = x.shape
    n = b * s
    tm = _divisor_tile(s, 512)
    chunk = _divisor_tile(s, 256)
    cos_t, sin_t = _rope_tables(jnp.arange(s) + N_META)
    qr, kr, vr, gr, qa, ka, va = _proj(x, shared["norm_mix"], shared["w_in"], cos_t, sin_t, tm)
    o_r = _retention(qr, kr, vr, gr, shared["kr_pre"], shared["vr_pre"], *_decay_tables(
        shared["dec_f"], shared["dec_b"], chunk), shared["gn_w"], chunk)
    o_a = _attention(qa, ka, va, shared["k_meta"], shared["v_meta"], shared["bias"], shared["sink"])
    h1, xn2 = _outproj(o_r, o_a, x, shared["w_out"], shared["norm_ffn"], tm)
    idx_t, gate_t = _route(xn2, shared["wq"], shared["pkeys"], _divisor_tile(s, ROUTE_TOKENS))
    idx_t = idx_t.transpose(1, 0, 2).reshape(PEER_SLOTS, n)
    gate_t = gate_t.transpose(1, 0, 2).reshape(PEER_SLOTS, n)
    top_half = (jnp.arange(PEER_SLOTS) % SUBLANES) < ROWS_PER_EXPERT
    tb = _divisor_tile(n, PEER_TOKENS)
    offs_g = _group_slots(idx_t * ROWS_PER_EXPERT + jnp.where(top_half, TABLE_PAD, 0)[:, None], tb)
    act_t = _peer_u(offs_g, xn2.reshape(n * SUBLANES, LANES), shared["u_tbl"], tb)
    w_t = _peer_w(gate_t, act_t, _divisor_tile(n, 2048))
    y = _peer_v(offs_g, w_t, h1.reshape(n * SUBLANES, LANES), shared["v_tbl"], shared["norm_final"], tb)
    return y.reshape(b, s, d)


def kernel(x_prompt, x_sample, meta_tokens, norm_mix_w, w_in, ret_decay_fwd, ret_decay_bwd, ret_gn_w,
           attn_sink, rel_bias, w_out, norm_ffn_w, peer_wq, peer_keys, peer_u, peer_v, norm_final_w):
    layer = 0
    shared = {
        "norm_mix": norm_mix_w[layer][None, :].astype(F32),
        "w_in": w_in[layer].astype(BF16),
        "dec_f": ret_decay_fwd[layer],
        "dec_b": ret_decay_bwd[layer],
        "gn_w": ret_gn_w[layer][None, :].astype(F32),
        "sink": attn_sink[layer].astype(F32),
        "bias": _attention_bias(rel_bias),
        "w_out": w_out[layer].astype(BF16),
        "norm_ffn": norm_ffn_w[layer][None, :].astype(F32),
        "wq": peer_wq[layer].astype(BF16),
        "pkeys": peer_keys[layer].reshape(2 * PEER_HEADS, PEER_NKEYS, PEER_NKEYS).astype(BF16),
        "u_tbl": _pack_table(peer_u[layer]),
        "v_tbl": _pack_table(peer_v[layer]),
        "norm_final": norm_final_w.reshape(SUBLANES, LANES).astype(F32),
    }
    prefix = jnp.concatenate([jnp.zeros((PREFIX - N_META, D_MODEL), x_prompt.dtype),
                              meta_tokens.astype(x_prompt.dtype)], axis=0)[None]
    cos_p, sin_p = _rope_tables(jnp.arange(PREFIX) - (PREFIX - N_META))
    _, kr_p, vr_p, _, _, ka_p, va_p = _proj(prefix, shared["norm_mix"], shared["w_in"], cos_p, sin_p, PREFIX)
    shared["kr_pre"], shared["vr_pre"] = kr_p[0], vr_p[0]
    pad_meta = lambda t: jnp.pad(t[0, PREFIX - N_META:], ((0, ATT_BLOCK - N_META), (0, 0)))
    shared["k_meta"], shared["v_meta"] = pad_meta(ka_p), pad_meta(va_p)
    return (_encode(x_prompt, shared), _encode(x_sample, shared))
```

```python
import functools
import math

import jax
import jax.numpy as jnp
from jax import lax
from jax.experimental import pallas as pl
from jax.experimental.pallas import tpu as pltpu

F32 = jnp.float32
BF16 = jnp.bfloat16

D_MODEL = 1024
N_META = 16
PREFIX = 128
RET_HEADS = 4
RET_D = 128
ATT_HEADS = 8
ATT_KV = 2
ATT_GROUP = ATT_HEADS // ATT_KV
ATT_HD = 64
ATT_BLOCK = 128
N_BUCKETS = 32
MAX_DISTANCE = 128
ROPE_BASE = 10000.0
PEER_HEADS = 8
PEER_NKEYS = 128
PEER_EXPERTS = PEER_NKEYS * PEER_NKEYS
PEER_TOPK = 16
PEER_SLOTS = PEER_HEADS * PEER_TOPK
RMS_EPS = 1e-6
GN_EPS = 1e-5
NEG = -1e30
IN_COLS = (0, 512, 1024, 1536, 2048, 2560, 2688, 2816)

LANES = 128
SUBLANES = 8
VMEM_LIMIT = 56 * 1024 * 1024
ROWS_PER_EXPERT = D_MODEL // 2 // LANES
TABLE_PAD = ROWS_PER_EXPERT
PEER_TOKENS = 256
ROUTE_TOKENS = 256
SLOT_ORDER = (0, 2, 1, 3)
SLOT_GROUP = 1
U_GROUPS_PER_ITER = 1
V_TOKENS_PER_ITER = 8
NORM_TOKENS = 32


def _cparams(sem):
    return pltpu.CompilerParams(dimension_semantics=sem, vmem_limit_bytes=VMEM_LIMIT)


def _proj_kernel(x_ref, nw_ref, w_ref, cos_ref, sin_ref,
                 qr_ref, kr_ref, vr_ref, gr_ref, qa_ref, ka_ref, va_ref):
    x = x_ref[0]
    ms = jnp.mean(x * x, axis=-1, keepdims=True)
    xn = (x * lax.rsqrt(ms + RMS_EPS) * nw_ref[...]).astype(BF16)
    cosf = cos_ref[...]
    sinf = sin_ref[...]

    def mm(i):
        return jnp.dot(xn, w_ref[:, IN_COLS[i]:IN_COLS[i + 1]], preferred_element_type=F32)

    def rotary(t, scale):
        for h in range(RET_HEADS):
            th = t[:, h * RET_D:(h + 1) * RET_D]
            yield h, (th * cosf + pltpu.roll(th, RET_D // 2, 1) * sinf) * scale

    for h, r in rotary(mm(0), 1.0):
        qr_ref[0, :, h * RET_D:(h + 1) * RET_D] = r.astype(BF16)
    for h, r in rotary(mm(1), RET_D ** -0.5):
        kr_ref[0, :, h * RET_D:(h + 1) * RET_D] = r.astype(BF16)
    vr_ref[0] = mm(2).astype(BF16)
    gr_ref[0] = mm(3)
    qa_ref[0] = (mm(4) * (ATT_HD ** -0.5)).astype(BF16)
    ka_ref[0] = mm(5).astype(BF16)
    va_ref[0] = mm(6).astype(BF16)


def _proj(x, norm_w, w_in_bf, cos_t, sin_t, tm):
    b, s, d = x.shape
    widths = (512, 512, 512, 512, 512, 128, 128)
    dtypes = (BF16, BF16, BF16, F32, BF16, BF16, BF16)
    return pl.pallas_call(
        _proj_kernel,
        out_shape=[jax.ShapeDtypeStruct((b, s, w), dt) for w, dt in zip(widths, dtypes)],
        grid=(b, s // tm),
        in_specs=[
            pl.BlockSpec((1, tm, d), lambda i, j: (i, j, 0)),
            pl.BlockSpec((1, d), lambda i, j: (0, 0)),
            pl.BlockSpec(w_in_bf.shape, lambda i, j: (0, 0)),
            pl.BlockSpec((tm, LANES), lambda i, j: (j, 0)),
            pl.BlockSpec((tm, LANES), lambda i, j: (j, 0)),
        ],
        out_specs=[pl.BlockSpec((1, tm, w), lambda i, j: (i, j, 0)) for w in widths],
        compiler_params=_cparams(("arbitrary", "arbitrary")),
        name="proj",
    )(x, norm_w, w_in_bf, cos_t, sin_t)


def _ret_kernel(q_ref, k_ref, v_ref, g_ref, kpre_ref, vpre_ref, dmat_ref, qwf_ref, qwb_ref,
                kwf_ref, kwb_ref, kwp_ref, gdec_ref, gnw_ref, o_ref, sf_ref, sb_ref, sball_ref):
    ph = pl.program_id(1)
    c = pl.program_id(2)
    nc = pl.num_programs(2)
    contract0 = (((0,), (0,)), ((), ()))
    contract1 = (((1,), (1,)), ((), ()))

    def head(ref, h):
        return ref[0, :, h * RET_D:(h + 1) * RET_D]

    def kv_update(kh, vh, kw):
        kw = (kh.astype(F32) * kw).astype(BF16)
        return lax.dot_general(kw, vh, contract0, preferred_element_type=F32)

    @pl.when(ph == 0)
    def _backward_states():
        @pl.when(c == 0)
        def _():
            sb_ref[...] = jnp.zeros_like(sb_ref)

        j = nc - 1 - c
        for h in range(RET_HEADS):
            sball_ref[j, h] = sb_ref[h].astype(BF16)
            sb_ref[h] = gdec_ref[RET_HEADS + h] * sb_ref[h] + kv_update(head(k_ref, h), head(v_ref, h), kwb_ref[h])

    @pl.when(ph == 1)
    def _outputs():
        @pl.when(c == 0)
        def _():
            for h in range(RET_HEADS):
                kp = kpre_ref[:, h * RET_D:(h + 1) * RET_D]
                vp = vpre_ref[:, h * RET_D:(h + 1) * RET_D]
                sf_ref[h] = kv_update(kp, vp, kwp_ref[h])

        for h in range(RET_HEADS):
            qh, kh, vh = head(q_ref, h), head(k_ref, h), head(v_ref, h)
            s = lax.dot_general(qh, kh, contract1, preferred_element_type=F32) * dmat_ref[h]
            o = jnp.dot(s.astype(BF16), vh, preferred_element_type=F32)
            o = o + jnp.dot(qh, sf_ref[h].astype(BF16), preferred_element_type=F32) * qwf_ref[h]
            o = o + jnp.dot(qh, sball_ref[c, h], preferred_element_type=F32) * qwb_ref[h]
            mu = jnp.mean(o, axis=-1, keepdims=True)
            oc = o - mu
            var = jnp.mean(oc * oc, axis=-1, keepdims=True)
            on = oc * lax.rsqrt(var + GN_EPS) * gnw_ref[:, h * RET_D:(h + 1) * RET_D]
            g = head(g_ref, h)
            o_ref[0, :, h * RET_D:(h + 1) * RET_D] = (g * jax.nn.sigmoid(g) * on).astype(BF16)
            sf_ref[h] = gdec_ref[h] * sf_ref[h] + kv_update(kh, vh, kwf_ref[h])


def _retention(qr, kr, vr, gr, kpre, vpre, dmat, qwf, qwb, kwf, kwb, kwp, gdec, gnw, chunk):
    b, s, w = qr.shape
    nc = s // chunk

    def seq_map(i, ph, c):
        return (i, jnp.where(ph == 0, nc - 1 - c, c), 0)

    def fwd_only(i, ph, c):
        return (i, c * ph, 0)

    const2 = lambda i, ph, c: (0, 0)
    const3 = lambda i, ph, c: (0, 0, 0)
    return pl.pallas_call(
        _ret_kernel,
        out_shape=jax.ShapeDtypeStruct((b, s, w), BF16),
        grid=(b, 2, nc),
        in_specs=[
            pl.BlockSpec((1, chunk, w), fwd_only),
            pl.BlockSpec((1, chunk, w), seq_map),
            pl.BlockSpec((1, chunk, w), seq_map),
            pl.BlockSpec((1, chunk, w), fwd_only),
            pl.BlockSpec(kpre.shape, const2),
            pl.BlockSpec(vpre.shape, const2),
            pl.BlockSpec(dmat.shape, const3),
            pl.BlockSpec(qwf.shape, const3),
            pl.BlockSpec(qwb.shape, const3),
            pl.BlockSpec(kwf.shape, const3),
            pl.BlockSpec(kwb.shape, const3),
            pl.BlockSpec(kwp.shape, const3),
            pl.BlockSpec(memory_space=pltpu.SMEM),
            pl.BlockSpec(gnw.shape, const2),
        ],
        out_specs=pl.BlockSpec((1, chunk, w), fwd_only),
        scratch_shapes=[
            pltpu.VMEM((RET_HEADS, RET_D, RET_D), F32),
            pltpu.VMEM((RET_HEADS, RET_D, RET_D), F32),
            pltpu.VMEM((nc, RET_HEADS, RET_D, RET_D), BF16),
        ],
        compiler_params=_cparams(("arbitrary", "arbitrary", "arbitrary")),
        name="retention",
    )(qr, kr, vr, gr, kpre, vpre, dmat, qwf, qwb, kwf, kwb, kwp, gdec, gnw)


def _attn_kernel(q_ref, kp_ref, kc_ref, kn_ref, vp_ref, vc_ref, vn_ref, km_ref, vm_ref,
                 bias_ref, sink_ref, o_ref):
    j = pl.program_id(1)
    nb = pl.num_programs(1)
    lane = lax.broadcasted_iota(jnp.int32, (1, 4 * ATT_BLOCK), 1)
    first = jnp.logical_and(lane < ATT_BLOCK, j == 0)
    last = jnp.logical_and(jnp.logical_and(lane >= 2 * ATT_BLOCK, lane < 3 * ATT_BLOCK), j == nb - 1)
    pen = jnp.where(jnp.logical_or(first, last), NEG, 0.0).astype(F32)
    contract1 = (((1,), (1,)), ((), ()))
    outs = []
    for g in range(ATT_KV):
        sl = slice(g * ATT_HD, (g + 1) * ATT_HD)
        kall = jnp.concatenate([kp_ref[0][:, sl], kc_ref[0][:, sl], kn_ref[0][:, sl], km_ref[:, sl]], axis=0)
        vall = jnp.concatenate([vp_ref[0][:, sl], vc_ref[0][:, sl], vn_ref[0][:, sl], vm_ref[:, sl]], axis=0)
        for hh in range(ATT_GROUP):
            h = g * ATT_GROUP + hh
            q = q_ref[0, :, h * ATT_HD:(h + 1) * ATT_HD]
            s = lax.dot_general(q, kall, contract1, preferred_element_type=F32)
            s = s + bias_ref[0, h] + pen
            snk = sink_ref[h]
            m = jnp.maximum(jnp.max(s, axis=-1, keepdims=True), snk)
            p = jnp.exp(s - m)
            den = jnp.sum(p, axis=-1, keepdims=True) + jnp.exp(snk - m)
            o = jnp.dot(p.astype(BF16), vall, preferred_element_type=F32)
            outs.append(o / den)
    for h2 in range(ATT_HEADS // 2):
        o_ref[0, :, h2 * LANES:(h2 + 1) * LANES] = jnp.concatenate(
            [outs[2 * h2], outs[2 * h2 + 1]], axis=-1).astype(BF16)


def _attention(qa, ka, va, kmeta, vmeta, bias, sink):
    b, s, w = qa.shape
    nb = s // ATT_BLOCK
    kvw = ka.shape[-1]
    prev = lambda i, j: (i, jnp.maximum(j - 1, 0), 0)
    cur = lambda i, j: (i, j, 0)
    nxt = lambda i, j: (i, jnp.minimum(j + 1, nb - 1), 0)
    const2 = lambda i, j: (0, 0)
    kv = lambda m: pl.BlockSpec((1, ATT_BLOCK, kvw), m)
    return pl.pallas_call(
        _attn_kernel,
        out_shape=jax.ShapeDtypeStruct((b, s, w), BF16),
        grid=(b, nb),
        in_specs=[
            pl.BlockSpec((1, ATT_BLOCK, w), cur),
            kv(prev), kv(cur), kv(nxt), kv(prev), kv(cur), kv(nxt),
            pl.BlockSpec(kmeta.shape, const2),
            pl.BlockSpec(vmeta.shape, const2),
            pl.BlockSpec((1,) + bias.shape[1:], lambda i, j: (jnp.minimum(j, 1), 0, 0, 0)),
            pl.BlockSpec(memory_space=pltpu.SMEM),
        ],
        out_specs=pl.BlockSpec((1, ATT_BLOCK, w), cur),
        compiler_params=_cparams(("arbitrary", "arbitrary")),
        name="attention",
    )(qa, ka, ka, ka, va, va, va, kmeta, vmeta, bias, sink)


def _store_token_rows(ref, v):
    for c in range(D_MODEL // LANES):
        ref[pl.ds(c, v.shape[0], stride=SUBLANES), :] = v[:, c * LANES:(c + 1) * LANES]


def _outproj_kernel(or_ref, oa_ref, x_ref, w_ref, nw_ref, h_ref, xn_ref, xnb_ref):
    half = or_ref.shape[-1]
    h = x_ref[0]
    h = h + jnp.dot(or_ref[0], w_ref[:half], preferred_element_type=F32)
    h = h + jnp.dot(oa_ref[0], w_ref[half:], preferred_element_type=F32)
    _store_token_rows(h_ref, h)
    ms = jnp.mean(h * h, axis=-1, keepdims=True)
    xn = h * lax.rsqrt(ms + RMS_EPS) * nw_ref[...]
    _store_token_rows(xn_ref, xn)
    xnb_ref[0] = xn.astype(BF16)


def _outproj(o_r, o_a, x, w_out_bf, norm_w, tm):
    b, s, d = x.shape
    half = o_r.shape[-1]
    row = lambda i, j: (i, j, 0)
    const2 = lambda i, j: (0, 0)
    token_rows = pl.BlockSpec((tm * SUBLANES, LANES), lambda i, j: (i * (s // tm) + j, 0))
    return pl.pallas_call(
        _outproj_kernel,
        out_shape=[jax.ShapeDtypeStruct((b * s * SUBLANES, LANES), F32)] * 2 + [jax.ShapeDtypeStruct((b, s, d), BF16)],
        grid=(b, s // tm),
        in_specs=[
            pl.BlockSpec((1, tm, half), row),
            pl.BlockSpec((1, tm, half), row),
            pl.BlockSpec((1, tm, d), row),
            pl.BlockSpec(w_out_bf.shape, const2),
            pl.BlockSpec((1, d), const2),
        ],
        out_specs=[token_rows, token_rows, pl.BlockSpec((1, tm, d), row)],
        compiler_params=_cparams(("arbitrary", "arbitrary")),
        name="outproj",
    )(o_r, o_a, x, w_out_bf, norm_w)


def _top16_rows(s, iota):
    nrows = float(s.shape[0])
    vals, ids = [], []
    for _ in range(PEER_TOPK):
        m = jnp.max(s, axis=0, keepdims=True)
        am = jnp.min(jnp.where(s == m, iota, nrows), axis=0, keepdims=True)
        vals.append(m)
        ids.append(am)
        s = jnp.where(iota == am, -jnp.inf, s)
    return jnp.concatenate(vals, axis=0), jnp.concatenate(ids, axis=0)


_PAIR_GROUPS = ((0, 0, 8), (0, 8, 8), (1, 0, 8), (2, 0, 5), (3, 0, 4), (4, 0, 3), (5, 0, 2), (6, 0, 2), (7, 0, 2))


def _route_kernel(x_ref, wq_ref, pk_ref, idx_ref, gate_ref, sc_ref):
    tm = x_ref.shape[1]
    q = jnp.dot(x_ref[0], wq_ref[...], preferred_element_type=F32).astype(BF16)
    contract1 = (((1,), (1,)), ((), ()))
    for hp in range(2 * PEER_HEADS):
        sc_ref[hp] = lax.dot_general(pk_ref[hp], q[:, hp * PEER_NKEYS:(hp + 1) * PEER_NKEYS], contract1,
                                     preferred_element_type=F32)
    iota = lax.broadcasted_iota(jnp.int32, (PEER_NKEYS, LANES), 0).astype(F32)
    sub = lax.broadcasted_iota(jnp.int32, (SUBLANES, LANES), 0)
    subf = sub.astype(F32)
    far = float(PEER_TOPK * PEER_TOPK)

    def head_body(h, carry):
        for lt in range(tm // LANES):
            lanes = slice(lt * LANES, (lt + 1) * LANES)
            s1, i1 = _top16_rows(sc_ref[2 * h, :, lanes], iota)
            s2, i2 = _top16_rows(sc_ref[2 * h + 1, :, lanes], iota)
            i1 = i1 * float(PEER_NKEYS)
            sc, ex, pos = [], [], []
            for i, j0, cnt in _PAIR_GROUPS:
                ok = sub < cnt
                sc.append(jnp.where(ok, s1[i:i + 1] + s2[j0:j0 + SUBLANES], -jnp.inf))
                ex.append(i1[i:i + 1] + i2[j0:j0 + SUBLANES])
                pos.append(jnp.where(ok, float(i * PEER_TOPK + j0) + subf, far))
            sc.append(s1[SUBLANES:] + s2[0:1])
            ex.append(i1[SUBLANES:] + i2[0:1])
            pos.append((float(SUBLANES) + subf) * float(PEER_TOPK))
            sc = jnp.concatenate(sc, axis=0)
            ex = jnp.concatenate(ex, axis=0)
            pos = jnp.concatenate(pos, axis=0)
            top, experts = [], []
            for _ in range(PEER_TOPK):
                m = jnp.max(sc, axis=0, keepdims=True)
                pm = jnp.min(jnp.where(sc == m, pos, far), axis=0, keepdims=True)
                hit = pos == pm
                experts.append(jnp.max(jnp.where(hit, ex, -1.0), axis=0, keepdims=True))
                top.append(m)
                sc = jnp.where(hit, -jnp.inf, sc)
            top = jnp.concatenate(top, axis=0)
            e = jnp.exp(top - top[0:1])
            rows = pl.ds(pl.multiple_of(h * PEER_TOPK, PEER_TOPK), PEER_TOPK)
            gate_ref[0, rows, lanes] = e / jnp.sum(e, axis=0, keepdims=True)
            idx_ref[0, rows, lanes] = jnp.concatenate(experts, axis=0).astype(jnp.int32)
        return carry

    lax.fori_loop(0, PEER_HEADS, head_body, 0)


def _route(xn, wq_bf, pk_bf, tm):
    b, s, d = xn.shape
    const2 = lambda i, j: (0, 0)
    slot = lambda i, j: (i, 0, j)
    return pl.pallas_call(
        _route_kernel,
        out_shape=[jax.ShapeDtypeStruct((b, PEER_SLOTS, s), jnp.int32),
                   jax.ShapeDtypeStruct((b, PEER_SLOTS, s), F32)],
        grid=(b, s // tm),
        in_specs=[
            pl.BlockSpec((1, tm, d), lambda i, j: (i, j, 0)),
            pl.BlockSpec(wq_bf.shape, const2),
            pl.BlockSpec(pk_bf.shape, lambda i, j: (0, 0, 0)),
        ],
        out_specs=[pl.BlockSpec((1, PEER_SLOTS, tm), slot)] * 2,
        scratch_shapes=[pltpu.VMEM((2 * PEER_HEADS, PEER_NKEYS, tm), F32)],
        compiler_params=_cparams(("arbitrary", "arbitrary")),
        name="route",
    )(xn, wq_bf, pk_bf)


def _pair_slots():
    out = []
    for g in range(PEER_SLOTS // SUBLANES):
        for c in range(4):
            top = SUBLANES * g + SLOT_ORDER[c]
            out.append((top, top + 4))
    return out


def _gather_pair(tbl_ref, top_off, bot_off, upper):
    ra = tbl_ref[pl.ds(pl.multiple_of(top_off, ROWS_PER_EXPERT), SUBLANES), :]
    rb = tbl_ref[pl.ds(pl.multiple_of(bot_off, ROWS_PER_EXPERT), SUBLANES), :]
    row = jnp.where(upper, ra, rb)
    lo = lax.bitcast_convert_type(row << 16, F32)
    hi = lax.bitcast_convert_type(row & jnp.uint32(0xFFFF0000), F32)
    return lo, hi


def _merge_sublanes(a, b, h, sub):
    keep = (sub & h) == 0
    return jnp.where(keep, a + pltpu.roll(a, SUBLANES - h, 0), b + pltpu.roll(b, h, 0))


def _slot_specs(tb):
    return [pl.BlockSpec((None, None, SLOT_GROUP * tb), lambda i, j=j: (j, 0, i), memory_space=pltpu.SMEM,
                         pipeline_mode=pl.Buffered(1)) for j in range(PEER_SLOTS // SLOT_GROUP)]


def _group_slots(a, tb):
    n = a.shape[-1]
    a = a.reshape(PEER_SLOTS // SLOT_GROUP, SLOT_GROUP, n // tb, tb).transpose(0, 2, 1, 3)
    return a.reshape(PEER_SLOTS // SLOT_GROUP, 1, n * SLOT_GROUP)


def _slot_reader(refs, t, tb):
    at = [t + g * tb for g in range(SLOT_GROUP)]
    return lambda k: refs[k // SLOT_GROUP][at[k % SLOT_GROUP]]


def _peer_u_kernel(*refs):
    ngroups = PEER_SLOTS // SLOT_GROUP
    offs = refs[:ngroups]
    x_ref, tbl_ref, o_ref, z_ref, sel_ref = refs[ngroups:]
    tb = x_ref.shape[0] // SUBLANES
    sub = lax.broadcasted_iota(jnp.int32, (SUBLANES, LANES), 0)
    upper = sub < ROWS_PER_EXPERT
    pairs = _pair_slots()
    lane_pair = lax.broadcasted_iota(jnp.int32, (PEER_SLOTS, LANES), 1) >> 1

    row = lax.broadcasted_iota(jnp.int32, (2 * LANES, LANES), 0)
    col = lax.broadcasted_iota(jnp.int32, (2 * LANES, LANES), 1)
    sel_ref[...] = jnp.where((col & 1) == (row >= LANES).astype(jnp.int32), 1.0, 0.0).astype(BF16)
    o_ref[...] = jnp.zeros_like(o_ref)

    def token_group(gi, zbase):
        for j in range(SUBLANES):
            t = gi * SUBLANES + j
            x = x_ref[pl.ds(pl.multiple_of(t * SUBLANES, SUBLANES), SUBLANES), :]
            swapped = pltpu.roll(x, ROWS_PER_EXPERT, 0)
            xa = jnp.where(upper, x, swapped)
            xb = jnp.where(upper, swapped, x)
            off = _slot_reader(offs, t, tb)
            zs = []
            for g in range(PEER_SLOTS // SUBLANES):
                ps = []
                for c in range(4):
                    top, bot = pairs[4 * g + c]
                    lo, hi = _gather_pair(tbl_ref, off(top), off(bot), upper)
                    ps.append(lo * xa + hi * xb)
                r0 = _merge_sublanes(ps[0], ps[1], 2, sub)
                r1 = _merge_sublanes(ps[2], ps[3], 2, sub)
                zs.append(_merge_sublanes(r0, r1, 1, sub))
            rows = zbase + (j // 2) * PEER_SLOTS
            cols = slice((j % 2) * LANES, (j % 2 + 1) * LANES)
            for m in range(len(zs) // 2):
                tile = jnp.concatenate([zs[2 * m], zs[2 * m + 1]], axis=0).astype(BF16)
                z_ref[rows + 2 * SUBLANES * m:rows + 2 * SUBLANES * (m + 1), cols] = tile
        sums = jnp.dot(z_ref[zbase:zbase + 4 * PEER_SLOTS, :], sel_ref[...], preferred_element_type=F32)
        block = pl.ds(pl.multiple_of((gi // (LANES // SUBLANES)) * LANES, LANES), LANES)
        first_pair = (gi % (LANES // SUBLANES)) * 4
        acc = o_ref[:, block]
        for p in range(4):
            acc = jnp.where(lane_pair == first_pair + p, sums[p * PEER_SLOTS:(p + 1) * PEER_SLOTS], acc)
        o_ref[:, block] = acc

    def step(i, carry):
        for r in range(U_GROUPS_PER_ITER):
            token_group(i * U_GROUPS_PER_ITER + r, r * 4 * PEER_SLOTS)
        return carry

    lax.fori_loop(0, tb // (SUBLANES * U_GROUPS_PER_ITER), step, 0)


def _peer_u(offs_g, x_rows, tbl, tb):
    n = x_rows.shape[0] // SUBLANES
    assert tb % LANES == 0 and (tb // SUBLANES) % U_GROUPS_PER_ITER == 0
    return pl.pallas_call(
        _peer_u_kernel,
        out_shape=jax.ShapeDtypeStruct((PEER_SLOTS, n), F32),
        grid=(n // tb,),
        in_specs=_slot_specs(tb) + [
            pl.BlockSpec((tb * SUBLANES, LANES), lambda i: (i, 0)),
            pl.BlockSpec(tbl.shape, lambda i: (0, 0), pipeline_mode=pl.Buffered(1)),
        ],
        out_specs=pl.BlockSpec((PEER_SLOTS, tb), lambda i: (0, i)),
        scratch_shapes=[
            pltpu.VMEM((U_GROUPS_PER_ITER * 4 * PEER_SLOTS, 2 * LANES), BF16),
            pltpu.VMEM((2 * LANES, LANES), BF16),
        ],
        compiler_params=_cparams(("arbitrary",)),
        name="peer_u",
    )(*([offs_g] * (PEER_SLOTS // SLOT_GROUP)), x_rows, tbl)


def _peer_w_kernel(gate_ref, act_ref, w_ref):
    a = act_ref[...]
    w_ref[...] = gate_ref[...] * (0.5 * a * (1.0 + lax.erf(a * (2.0 ** -0.5))))


def _peer_w(gate_t, act_t, tm):
    n = gate_t.shape[-1]
    spec = pl.BlockSpec((PEER_SLOTS, tm), lambda i: (0, i))
    return pl.pallas_call(
        _peer_w_kernel,
        out_shape=jax.ShapeDtypeStruct(gate_t.shape, F32),
        grid=(n // tm,),
        in_specs=[spec, spec],
        out_specs=spec,
        compiler_params=_cparams(("arbitrary",)),
        name="peer_w",
    )(gate_t, act_t)


def _peer_v_kernel(*refs):
    ngroups = PEER_SLOTS // SLOT_GROUP
    offs = refs[:ngroups]
    w_ref, h_ref, tbl_ref, nw_ref, o_ref, acc_ref, whi_ref, wlo_ref, repa_ref, repb_ref, ones_ref = refs[ngroups:]
    tb = h_ref.shape[0] // SUBLANES
    sub = lax.broadcasted_iota(jnp.int32, (SUBLANES, LANES), 0)
    upper = sub < ROWS_PER_EXPERT
    pairs = _pair_slots()
    lane = lax.broadcasted_iota(jnp.int32, (PEER_SLOTS, LANES), 1)

    w = w_ref[...]
    whi = w.astype(BF16).astype(F32)
    whi_ref[...] = whi
    wlo_ref[...] = w - whi
    row = lax.broadcasted_iota(jnp.int32, (2 * LANES, 2 * LANES), 0)
    col = lax.broadcasted_iota(jnp.int32, (2 * LANES, 2 * LANES), 1)
    ones_ref[...] = jnp.where((row >= LANES) == (col >= LANES), 1.0, 0.0).astype(BF16)

    def replicate(t0, rep_ref):
        for r in range(V_TOKENS_PER_ITER // 2):
            t = t0 + 2 * r
            block = pl.ds(pl.multiple_of((t // LANES) * LANES, LANES), LANES)
            la = t % LANES
            out = None
            for part_ref in (whi_ref, wlo_ref):
                part = part_ref[:, block]
                lhs = jnp.concatenate([jnp.where(lane == la, part, 0.0), jnp.where(lane == la + 1, part, 0.0)],
                                      axis=1).astype(BF16)
                prod = jnp.dot(lhs, ones_ref[...], preferred_element_type=F32)
                out = prod if out is None else out + prod
            rep_ref[r] = out

    def token(t, rep_ref, slot, half):
        off = _slot_reader(offs, t, tb)
        lanes = slice(half * LANES, (half + 1) * LANES)
        acc = [jnp.zeros((SUBLANES, LANES), F32) for _ in range(4)]
        for q, (top, bot) in enumerate(pairs):
            lo, hi = _gather_pair(tbl_ref, off(top), off(bot), upper)
            wtop = jnp.broadcast_to(rep_ref[slot, top:top + 1, lanes], (SUBLANES, LANES))
            wbot = jnp.broadcast_to(rep_ref[slot, bot:bot + 1, lanes], (SUBLANES, LANES))
            wv = jnp.where(upper, wtop, wbot)
            k = 2 * (q % 2)
            acc[k] = acc[k] + lo * wv
            acc[k + 1] = acc[k + 1] + hi * wv
        lo = acc[0] + acc[2]
        hi = acc[1] + acc[3]
        lo = lo + pltpu.roll(lo, ROWS_PER_EXPERT, 0)
        hi = hi + pltpu.roll(hi, ROWS_PER_EXPERT, 0)
        rows = pl.ds(pl.multiple_of(t * SUBLANES, SUBLANES), SUBLANES)
        acc_ref[rows, :] = h_ref[rows, :] + jnp.where(upper, lo, hi)

    def tokens(t0, rep_ref):
        for r in range(V_TOKENS_PER_ITER):
            token(t0 + r, rep_ref, r // 2, r % 2)

    replicate(0, repa_ref)

    def step(i, carry):
        t0 = i * 2 * V_TOKENS_PER_ITER
        replicate(t0 + V_TOKENS_PER_ITER, repb_ref)
        tokens(t0, repa_ref)
        replicate(jnp.minimum(t0 + 2 * V_TOKENS_PER_ITER, tb - V_TOKENS_PER_ITER), repa_ref)
        tokens(t0 + V_TOKENS_PER_ITER, repb_ref)
        return carry

    lax.fori_loop(0, tb // (2 * V_TOKENS_PER_ITER), step, 0)

    def norm_step(i, carry):
        rows = pl.ds(pl.multiple_of(i * NORM_TOKENS * SUBLANES, NORM_TOKENS * SUBLANES), NORM_TOKENS * SUBLANES)
        h = acc_ref[rows, :].reshape(NORM_TOKENS, SUBLANES, LANES)
        ss = jnp.sum(jnp.sum(h * h, axis=2, keepdims=True), axis=1, keepdims=True)
        y = h * lax.rsqrt(ss * (1.0 / D_MODEL) + RMS_EPS) * nw_ref[...][None]
        acc_ref[rows, :] = y.reshape(NORM_TOKENS * SUBLANES, LANES)
        return carry

    lax.fori_loop(0, tb // NORM_TOKENS, norm_step, 0)
    for c in range(D_MODEL // LANES):
        o_ref[:, c * LANES:(c + 1) * LANES] = acc_ref[pl.ds(c, tb, stride=SUBLANES), :]


def _peer_v(offs_g, w_t, h_rows, tbl, norm_rows, tb):
    n = h_rows.shape[0] // SUBLANES
    assert tb % LANES == 0 and V_TOKENS_PER_ITER % 2 == 0 and tb % (2 * V_TOKENS_PER_ITER) == 0 and tb % NORM_TOKENS == 0
    rows = pl.BlockSpec((tb * SUBLANES, LANES), lambda i: (i, 0))
    return pl.pallas_call(
        _peer_v_kernel,
        out_shape=jax.ShapeDtypeStruct((n, D_MODEL), F32),
        grid=(n // tb,),
        in_specs=_slot_specs(tb) + [
            pl.BlockSpec((PEER_SLOTS, tb), lambda i: (0, i)),
            rows,
            pl.BlockSpec(tbl.shape, lambda i: (0, 0), pipeline_mode=pl.Buffered(1)),
            pl.BlockSpec((SUBLANES, LANES), lambda i: (0, 0)),
        ],
        out_specs=pl.BlockSpec((tb, D_MODEL), lambda i: (i, 0)),
        scratch_shapes=[
            pltpu.VMEM((tb * SUBLANES, LANES), F32),
            pltpu.VMEM((PEER_SLOTS, tb), F32),
            pltpu.VMEM((PEER_SLOTS, tb), F32),
            pltpu.VMEM((V_TOKENS_PER_ITER // 2, PEER_SLOTS, 2 * LANES), F32),
            pltpu.VMEM((V_TOKENS_PER_ITER // 2, PEER_SLOTS, 2 * LANES), F32),
            pltpu.VMEM((2 * LANES, 2 * LANES), BF16),
        ],
        compiler_params=_cparams(("arbitrary",)),
        name="peer_v",
    )(*([offs_g] * (PEER_SLOTS // SLOT_GROUP)), w_t, h_rows, tbl, norm_rows)


def _pack_table(t):
    half = D_MODEL // 2
    bits = lax.bitcast_convert_type(t.astype(BF16), jnp.uint16).astype(jnp.uint32)
    words = bits[:, :half] | (bits[:, half:] << 16)
    words = words.reshape(t.shape[0] * ROWS_PER_EXPERT, LANES)
    return jnp.pad(words, ((TABLE_PAD, TABLE_PAD), (0, 0)))


def _t5_bucket(rel):
    half = N_BUCKETS // 2
    exact = half // 2
    n = jnp.abs(rel)
    large = exact + (jnp.log(jnp.maximum(n, 1).astype(F32) / exact)
                     / math.log(MAX_DISTANCE / exact) * (half - exact)).astype(jnp.int32)
    large = jnp.minimum(large, half - 1)
    return jnp.where(rel > 0, half, 0) + jnp.where(n < exact, n, large)


def _attention_bias(rel_bias):
    rb = rel_bias.astype(F32)
    i = jnp.arange(ATT_BLOCK)
    jb = jnp.arange(3 * ATT_BLOCK)
    rel = (jb[None, :] - ATT_BLOCK) - i[:, None]
    band = jnp.where((jnp.abs(rel) <= ATT_BLOCK)[..., None], rb[_t5_bucket(rel)], NEG)
    mpos = PREFIX - N_META + jnp.arange(N_META)
    variants = []
    for blk in (0, 1):
        qpos = PREFIX + blk * ATT_BLOCK + i
        meta = rb[_t5_bucket(mpos[None, :] - qpos[:, None])]
        fill = jnp.full((ATT_BLOCK, ATT_BLOCK - N_META, ATT_HEADS), NEG, F32)
        variants.append(jnp.concatenate([band, meta, fill], axis=1).transpose(2, 0, 1))
    return jnp.stack(variants)


def _rope_tables(pos):
    half = RET_D // 2
    inv = ROPE_BASE ** (-jnp.arange(half, dtype=F32) / half)
    ang = pos.astype(F32)[:, None] * inv[None, :]
    cos, sin = jnp.cos(ang), jnp.sin(ang)
    return jnp.concatenate([cos, cos], axis=1), jnp.concatenate([-sin, sin], axis=1)


def _decay_tables(dec_f, dec_b, chunk):
    lf = jax.nn.log_sigmoid(dec_f.astype(F32))[:, None]
    lb = jax.nn.log_sigmoid(dec_b.astype(F32))[:, None]
    idx = jnp.arange(chunk, dtype=F32)
    diff = idx[:, None] - idx[None, :]
    dmat = (jnp.where(diff >= 0, jnp.exp(jnp.maximum(diff, 0.0)[None] * lf[:, :, None]), 0.0)
            + jnp.where(diff < 0, jnp.exp(jnp.maximum(-diff, 0.0)[None] * lb[:, :, None]), 0.0))
    bc = lambda v: jnp.broadcast_to(v[:, :, None], v.shape + (RET_D,))
    qwf = bc(jnp.exp((idx + 1.0)[None] * lf))
    kwf = bc(jnp.exp((chunk - 1.0 - idx)[None] * lf))
    qwb = bc(jnp.exp((chunk - idx)[None] * lb))
    kwb = bc(jnp.exp(idx[None] * lb))
    pidx = jnp.arange(PREFIX, dtype=F32)
    kwp = bc(jnp.exp((PREFIX - 1.0 - pidx)[None] * lf))
    gdec = jnp.concatenate([jnp.exp(chunk * lf[:, 0]), jnp.exp(chunk * lb[:, 0])])
    return dmat, qwf, qwb, kwf, kwb, kwp, gdec


def _divisor_tile(n, want):
    t = min(n, want)
    while n % t:
        t //= 2
    return t


def _encode(x, shared):
    b, s, d = x.shape
    n = b * s
    tm = _divisor_tile(s, 512)
    chunk = _divisor_tile(s, 256)
    cos_t, sin_t = _rope_tables(jnp.arange(s) + N_META)
    qr, kr, vr, gr, qa, ka, va = _proj(x, shared["norm_mix"], shared["w_in"], cos_t, sin_t, tm)
    o_r = _retention(qr, kr, vr, gr, shared["kr_pre"], shared["vr_pre"], *_decay_tables(
        shared["dec_f"], shared["dec_b"], chunk), shared["gn_w"], chunk)
    o_a = _attention(qa, ka, va, shared["k_meta"], shared["v_meta"], shared["bias"], shared["sink"])
    h_rows, xn_rows, xn_bf = _outproj(o_r, o_a, x, shared["w_out"], shared["norm_ffn"], tm)
    idx_t, gate_t = _route(xn_bf, shared["wq"], shared["pkeys"], _divisor_tile(s, ROUTE_TOKENS))
    idx_t = idx_t.transpose(1, 0, 2).reshape(PEER_SLOTS, n)
    gate_t = gate_t.transpose(1, 0, 2).reshape(PEER_SLOTS, n)
    top_half = (jnp.arange(PEER_SLOTS) % SUBLANES) < ROWS_PER_EXPERT
    tb = _divisor_tile(n, PEER_TOKENS)
    offs_g = _group_slots(idx_t * ROWS_PER_EXPERT + jnp.where(top_half, TABLE_PAD, 0)[:, None], tb)
    act_t = _peer_u(offs_g, xn_rows, shared["u_tbl"], tb)
    w_t = _peer_w(gate_t, act_t, _divisor_tile(n, 2048))
    y = _peer_v(offs_g, w_t, h_rows, shared["v_tbl"], shared["norm_final"], tb)
    return y.reshape(b, s, d)


def kernel(x_prompt, x_sample, meta_tokens, norm_mix_w, w_in, ret_decay_fwd, ret_decay_bwd, ret_gn_w,
           attn_sink, rel_bias, w_out, norm_ffn_w, peer_wq, peer_keys, peer_u, peer_v, norm_final_w):
    layer = 0
    shared = {
        "norm_mix": norm_mix_w[layer][None, :].astype(F32),
        "w_in": w_in[layer].astype(BF16),
        "dec_f": ret_decay_fwd[layer],
        "dec_b": ret_decay_bwd[layer],
        "gn_w": ret_gn_w[layer][None, :].astype(F32),
        "sink": attn_sink[layer].astype(F32),
        "bias": _attention_bias(rel_bias),
        "w_out": w_out[layer].astype(BF16),
        "norm_ffn": norm_ffn_w[layer][None, :].astype(F32),
        "wq": peer_wq[layer].astype(BF16),
        "pkeys": peer_keys[layer].reshape(2 * PEER_HEADS, PEER_NKEYS, PEER_NKEYS).astype(BF16),
        "u_tbl": _pack_table(peer_u[layer]),
        "v_tbl": _pack_table(peer_v[layer]),
        "norm_final": norm_final_w.reshape(SUBLANES, LANES).astype(F32),
    }
    prefix = jnp.concatenate([jnp.zeros((PREFIX - N_META, D_MODEL), x_prompt.dtype),
                              meta_tokens.astype(x_prompt.dtype)], axis=0)[None]
    cos_p, sin_p = _rope_tables(jnp.arange(PREFIX) - (PREFIX - N_META))
    _, kr_p, vr_p, _, _, ka_p, va_p = _proj(prefix, shared["norm_mix"], shared["w_in"], cos_p, sin_p, PREFIX)
    shared["kr_pre"], shared["vr_pre"] = kr_p[0], vr_p[0]
    pad_meta = lambda t: jnp.pad(t[0, PREFIX - N_META:], ((0, ATT_BLOCK - N_META), (0, 0)))
    shared["k_meta"], shared["v_meta"] = pad_meta(ka_p), pad_meta(va_p)
    return (_encode(x_prompt, shared), _encode(x_sample, shared))
```

```python
import functools
import math

import jax
import jax.numpy as jnp
from jax import lax
from jax.experimental import pallas as pl
from jax.experimental.pallas import tpu as pltpu

F32 = jnp.float32
BF16 = jnp.bfloat16

D_MODEL = 1024
N_META = 16
PREFIX = 128
RET_HEADS = 4
RET_D = 128
ATT_HEADS = 8
ATT_KV = 2
ATT_GROUP = ATT_HEADS // ATT_KV
ATT_HD = 64
ATT_BLOCK = 128
N_BUCKETS = 32
MAX_DISTANCE = 128
ROPE_BASE = 10000.0
PEER_HEADS = 8
PEER_NKEYS = 128
PEER_EXPERTS = PEER_NKEYS * PEER_NKEYS
PEER_TOPK = 16
PEER_SLOTS = PEER_HEADS * PEER_TOPK
RMS_EPS = 1e-6
GN_EPS = 1e-5
NEG = -1e30
IN_COLS = (0, 512, 1024, 1536, 2048, 2560, 2688, 2816)

LANES = 128
SUBLANES = 8
VMEM_LIMIT = 56 * 1024 * 1024
ROWS_PER_EXPERT = D_MODEL // 2 // LANES
TABLE_PAD = ROWS_PER_EXPERT
PEER_TOKENS = 256
ROUTE_TOKENS = 256
SLOT_ORDER = (0, 2, 1, 3)
SLOT_GROUP = 1
V_TOKENS_PER_ITER = 8
NORM_TOKENS = 32


def _cparams(sem):
    return pltpu.CompilerParams(dimension_semantics=sem, vmem_limit_bytes=VMEM_LIMIT)


def _proj_kernel(x_ref, nw_ref, w_ref, cos_ref, sin_ref,
                 qr_ref, kr_ref, vr_ref, gr_ref, qa_ref, ka_ref, va_ref):
    x = x_ref[0]
    ms = jnp.mean(x * x, axis=-1, keepdims=True)
    xn = (x * lax.rsqrt(ms + RMS_EPS) * nw_ref[...]).astype(BF16)
    cosf = cos_ref[...]
    sinf = sin_ref[...]

    def mm(i):
        return jnp.dot(xn, w_ref[:, IN_COLS[i]:IN_COLS[i + 1]], preferred_element_type=F32)

    def rotary(t, scale):
        for h in range(RET_HEADS):
            th = t[:, h * RET_D:(h + 1) * RET_D]
            yield h, (th * cosf + pltpu.roll(th, RET_D // 2, 1) * sinf) * scale

    for h, r in rotary(mm(0), 1.0):
        qr_ref[0, :, h * RET_D:(h + 1) * RET_D] = r.astype(BF16)
    for h, r in rotary(mm(1), RET_D ** -0.5):
        kr_ref[0, :, h * RET_D:(h + 1) * RET_D] = r.astype(BF16)
    vr_ref[0] = mm(2).astype(BF16)
    gr_ref[0] = mm(3)
    qa_ref[0] = (mm(4) * (ATT_HD ** -0.5)).astype(BF16)
    ka_ref[0] = mm(5).astype(BF16)
    va_ref[0] = mm(6).astype(BF16)


def _proj(x, norm_w, w_in_bf, cos_t, sin_t, tm):
    b, s, d = x.shape
    widths = (512, 512, 512, 512, 512, 128, 128)
    dtypes = (BF16, BF16, BF16, F32, BF16, BF16, BF16)
    return pl.pallas_call(
        _proj_kernel,
        out_shape=[jax.ShapeDtypeStruct((b, s, w), dt) for w, dt in zip(widths, dtypes)],
        grid=(b, s // tm),
        in_specs=[
            pl.BlockSpec((1, tm, d), lambda i, j: (i, j, 0)),
            pl.BlockSpec((1, d), lambda i, j: (0, 0)),
            pl.BlockSpec(w_in_bf.shape, lambda i, j: (0, 0)),
            pl.BlockSpec((tm, LANES), lambda i, j: (j, 0)),
            pl.BlockSpec((tm, LANES), lambda i, j: (j, 0)),
        ],
        out_specs=[pl.BlockSpec((1, tm, w), lambda i, j: (i, j, 0)) for w in widths],
        compiler_params=_cparams(("arbitrary", "arbitrary")),
        name="proj",
    )(x, norm_w, w_in_bf, cos_t, sin_t)


def _ret_kernel(q_ref, k_ref, v_ref, g_ref, kpre_ref, vpre_ref, dmat_ref, qwf_ref, qwb_ref,
                kwf_ref, kwb_ref, kwp_ref, gdec_ref, gnw_ref, o_ref, sf_ref, sb_ref, sball_ref):
    ph = pl.program_id(1)
    c = pl.program_id(2)
    nc = pl.num_programs(2)
    contract0 = (((0,), (0,)), ((), ()))
    contract1 = (((1,), (1,)), ((), ()))

    def head(ref, h):
        return ref[0, :, h * RET_D:(h + 1) * RET_D]

    def kv_update(kh, vh, kw):
        kw = (kh.astype(F32) * kw).astype(BF16)
        return lax.dot_general(kw, vh, contract0, preferred_element_type=F32)

    @pl.when(ph == 0)
    def _backward_states():
        @pl.when(c == 0)
        def _():
            sb_ref[...] = jnp.zeros_like(sb_ref)

        j = nc - 1 - c
        for h in range(RET_HEADS):
            sball_ref[j, h] = sb_ref[h].astype(BF16)
            sb_ref[h] = gdec_ref[RET_HEADS + h] * sb_ref[h] + kv_update(head(k_ref, h), head(v_ref, h), kwb_ref[h])

    @pl.when(ph == 1)
    def _outputs():
        @pl.when(c == 0)
        def _():
            for h in range(RET_HEADS):
                kp = kpre_ref[:, h * RET_D:(h + 1) * RET_D]
                vp = vpre_ref[:, h * RET_D:(h + 1) * RET_D]
                sf_ref[h] = kv_update(kp, vp, kwp_ref[h])

        for h in range(RET_HEADS):
            qh, kh, vh = head(q_ref, h), head(k_ref, h), head(v_ref, h)
            s = lax.dot_general(qh, kh, contract1, preferred_element_type=F32) * dmat_ref[h]
            o = jnp.dot(s.astype(BF16), vh, preferred_element_type=F32)
            o = o + jnp.dot(qh, sf_ref[h].astype(BF16), preferred_element_type=F32) * qwf_ref[h]
            o = o + jnp.dot(qh, sball_ref[c, h], preferred_element_type=F32) * qwb_ref[h]
            mu = jnp.mean(o, axis=-1, keepdims=True)
            oc = o - mu
            var = jnp.mean(oc * oc, axis=-1, keepdims=True)
            on = oc * lax.rsqrt(var + GN_EPS) * gnw_ref[:, h * RET_D:(h + 1) * RET_D]
            g = head(g_ref, h)
            o_ref[0, :, h * RET_D:(h + 1) * RET_D] = (g * jax.nn.sigmoid(g) * on).astype(BF16)
            sf_ref[h] = gdec_ref[h] * sf_ref[h] + kv_update(kh, vh, kwf_ref[h])


def _retention(qr, kr, vr, gr, kpre, vpre, dmat, qwf, qwb, kwf, kwb, kwp, gdec, gnw, chunk):
    b, s, w = qr.shape
    nc = s // chunk

    def seq_map(i, ph, c):
        return (i, jnp.where(ph == 0, nc - 1 - c, c), 0)

    def fwd_only(i, ph, c):
        return (i, c * ph, 0)

    const2 = lambda i, ph, c: (0, 0)
    const3 = lambda i, ph, c: (0, 0, 0)
    return pl.pallas_call(
        _ret_kernel,
        out_shape=jax.ShapeDtypeStruct((b, s, w), BF16),
        grid=(b, 2, nc),
        in_specs=[
            pl.BlockSpec((1, chunk, w), fwd_only),
            pl.BlockSpec((1, chunk, w), seq_map),
            pl.BlockSpec((1, chunk, w), seq_map),
            pl.BlockSpec((1, chunk, w), fwd_only),
            pl.BlockSpec(kpre.shape, const2),
            pl.BlockSpec(vpre.shape, const2),
            pl.BlockSpec(dmat.shape, const3),
            pl.BlockSpec(qwf.shape, const3),
            pl.BlockSpec(qwb.shape, const3),
            pl.BlockSpec(kwf.shape, const3),
            pl.BlockSpec(kwb.shape, const3),
            pl.BlockSpec(kwp.shape, const3),
            pl.BlockSpec(memory_space=pltpu.SMEM),
            pl.BlockSpec(gnw.shape, const2),
        ],
        out_specs=pl.BlockSpec((1, chunk, w), fwd_only),
        scratch_shapes=[
            pltpu.VMEM((RET_HEADS, RET_D, RET_D), F32),
            pltpu.VMEM((RET_HEADS, RET_D, RET_D), F32),
            pltpu.VMEM((nc, RET_HEADS, RET_D, RET_D), BF16),
        ],
        compiler_params=_cparams(("arbitrary", "arbitrary", "arbitrary")),
        name="retention",
    )(qr, kr, vr, gr, kpre, vpre, dmat, qwf, qwb, kwf, kwb, kwp, gdec, gnw)


def _attn_kernel(q_ref, kp_ref, kc_ref, kn_ref, vp_ref, vc_ref, vn_ref, km_ref, vm_ref,
                 bias_ref, sink_ref, o_ref):
    j = pl.program_id(1)
    nb = pl.num_programs(1)
    lane = lax.broadcasted_iota(jnp.int32, (1, 4 * ATT_BLOCK), 1)
    first = jnp.logical_and(lane < ATT_BLOCK, j == 0)
    last = jnp.logical_and(jnp.logical_and(lane >= 2 * ATT_BLOCK, lane < 3 * ATT_BLOCK), j == nb - 1)
    pen = jnp.where(jnp.logical_or(first, last), NEG, 0.0).astype(F32)
    contract1 = (((1,), (1,)), ((), ()))
    outs = []
    for g in range(ATT_KV):
        sl = slice(g * ATT_HD, (g + 1) * ATT_HD)
        kall = jnp.concatenate([kp_ref[0][:, sl], kc_ref[0][:, sl], kn_ref[0][:, sl], km_ref[:, sl]], axis=0)
        vall = jnp.concatenate([vp_ref[0][:, sl], vc_ref[0][:, sl], vn_ref[0][:, sl], vm_ref[:, sl]], axis=0)
        for hh in range(ATT_GROUP):
            h = g * ATT_GROUP + hh
            q = q_ref[0, :, h * ATT_HD:(h + 1) * ATT_HD]
            s = lax.dot_general(q, kall, contract1, preferred_element_type=F32)
            s = s + bias_ref[0, h] + pen
            snk = sink_ref[h]
            m = jnp.maximum(jnp.max(s, axis=-1, keepdims=True), snk)
            p = jnp.exp(s - m)
            den = jnp.sum(p, axis=-1, keepdims=True) + jnp.exp(snk - m)
            o = jnp.dot(p.astype(BF16), vall, preferred_element_type=F32)
            outs.append(o / den)
    for h2 in range(ATT_HEADS // 2):
        o_ref[0, :, h2 * LANES:(h2 + 1) * LANES] = jnp.concatenate(
            [outs[2 * h2], outs[2 * h2 + 1]], axis=-1).astype(BF16)


def _attention(qa, ka, va, kmeta, vmeta, bias, sink):
    b, s, w = qa.shape
    nb = s // ATT_BLOCK
    kvw = ka.shape[-1]
    prev = lambda i, j: (i, jnp.maximum(j - 1, 0), 0)
    cur = lambda i, j: (i, j, 0)
    nxt = lambda i, j: (i, jnp.minimum(j + 1, nb - 1), 0)
    const2 = lambda i, j: (0, 0)
    kv = lambda m: pl.BlockSpec((1, ATT_BLOCK, kvw), m)
    return pl.pallas_call(
        _attn_kernel,
        out_shape=jax.ShapeDtypeStruct((b, s, w), BF16),
        grid=(b, nb),
        in_specs=[
            pl.BlockSpec((1, ATT_BLOCK, w), cur),
            kv(prev), kv(cur), kv(nxt), kv(prev), kv(cur), kv(nxt),
            pl.BlockSpec(kmeta.shape, const2),
            pl.BlockSpec(vmeta.shape, const2),
            pl.BlockSpec((1,) + bias.shape[1:], lambda i, j: (jnp.minimum(j, 1), 0, 0, 0)),
            pl.BlockSpec(memory_space=pltpu.SMEM),
        ],
        out_specs=pl.BlockSpec((1, ATT_BLOCK, w), cur),
        compiler_params=_cparams(("arbitrary", "arbitrary")),
        name="attention",
    )(qa, ka, ka, ka, va, va, va, kmeta, vmeta, bias, sink)


def _store_token_rows(ref, v):
    for c in range(D_MODEL // LANES):
        ref[pl.ds(c, v.shape[0], stride=SUBLANES), :] = v[:, c * LANES:(c + 1) * LANES]


def _outproj_kernel(or_ref, oa_ref, x_ref, w_ref, nw_ref, h_ref, xn_ref, xnb_ref):
    half = or_ref.shape[-1]
    h = x_ref[0]
    h = h + jnp.dot(or_ref[0], w_ref[:half], preferred_element_type=F32)
    h = h + jnp.dot(oa_ref[0], w_ref[half:], preferred_element_type=F32)
    _store_token_rows(h_ref, h)
    ms = jnp.mean(h * h, axis=-1, keepdims=True)
    xn = h * lax.rsqrt(ms + RMS_EPS) * nw_ref[...]
    _store_token_rows(xn_ref, xn)
    xnb_ref[0] = xn.astype(BF16)


def _outproj(o_r, o_a, x, w_out_bf, norm_w, tm):
    b, s, d = x.shape
    half = o_r.shape[-1]
    row = lambda i, j: (i, j, 0)
    const2 = lambda i, j: (0, 0)
    token_rows = pl.BlockSpec((tm * SUBLANES, LANES), lambda i, j: (i * (s // tm) + j, 0))
    return pl.pallas_call(
        _outproj_kernel,
        out_shape=[jax.ShapeDtypeStruct((b * s * SUBLANES, LANES), F32)] * 2 + [jax.ShapeDtypeStruct((b, s, d), BF16)],
        grid=(b, s // tm),
        in_specs=[
            pl.BlockSpec((1, tm, half), row),
            pl.BlockSpec((1, tm, half), row),
            pl.BlockSpec((1, tm, d), row),
            pl.BlockSpec(w_out_bf.shape, const2),
            pl.BlockSpec((1, d), const2),
        ],
        out_specs=[token_rows, token_rows, pl.BlockSpec((1, tm, d), row)],
        compiler_params=_cparams(("arbitrary", "arbitrary")),
        name="outproj",
    )(o_r, o_a, x, w_out_bf, norm_w)


def _top16_rows(s, iota):
    nrows = float(s.shape[0])
    vals, ids = [], []
    for _ in range(PEER_TOPK):
        m = jnp.max(s, axis=0, keepdims=True)
        am = jnp.min(jnp.where(s == m, iota, nrows), axis=0, keepdims=True)
        vals.append(m)
        ids.append(am)
        s = jnp.where(iota == am, -jnp.inf, s)
    return jnp.concatenate(vals, axis=0), jnp.concatenate(ids, axis=0)


_PAIR_GROUPS = ((0, 0, 8), (0, 8, 8), (1, 0, 8), (2, 0, 5), (3, 0, 4), (4, 0, 3), (5, 0, 2), (6, 0, 2), (7, 0, 2))


def _route_kernel(x_ref, wq_ref, pk_ref, idx_ref, gate_ref, sc_ref):
    tm = x_ref.shape[1]
    q = jnp.dot(x_ref[0], wq_ref[...], preferred_element_type=F32).astype(BF16)
    contract1 = (((1,), (1,)), ((), ()))
    for hp in range(2 * PEER_HEADS):
        sc_ref[hp] = lax.dot_general(pk_ref[hp], q[:, hp * PEER_NKEYS:(hp + 1) * PEER_NKEYS], contract1,
                                     preferred_element_type=F32)
    iota = lax.broadcasted_iota(jnp.int32, (PEER_NKEYS, LANES), 0).astype(F32)
    sub = lax.broadcasted_iota(jnp.int32, (SUBLANES, LANES), 0)
    subf = sub.astype(F32)
    far = float(PEER_TOPK * PEER_TOPK)

    def head_body(h, carry):
        for lt in range(tm // LANES):
            lanes = slice(lt * LANES, (lt + 1) * LANES)
            s1, i1 = _top16_rows(sc_ref[2 * h, :, lanes], iota)
            s2, i2 = _top16_rows(sc_ref[2 * h + 1, :, lanes], iota)
            i1 = i1 * float(PEER_NKEYS)
            sc, ex, pos = [], [], []
            for i, j0, cnt in _PAIR_GROUPS:
                ok = sub < cnt
                sc.append(jnp.where(ok, s1[i:i + 1] + s2[j0:j0 + SUBLANES], -jnp.inf))
                ex.append(i1[i:i + 1] + i2[j0:j0 + SUBLANES])
                pos.append(jnp.where(ok, float(i * PEER_TOPK + j0) + subf, far))
            sc.append(s1[SUBLANES:] + s2[0:1])
            ex.append(i1[SUBLANES:] + i2[0:1])
            pos.append((float(SUBLANES) + subf) * float(PEER_TOPK))
            sc = jnp.concatenate(sc, axis=0)
            ex = jnp.concatenate(ex, axis=0)
            pos = jnp.concatenate(pos, axis=0)
            top, experts = [], []
            for _ in range(PEER_TOPK):
                m = jnp.max(sc, axis=0, keepdims=True)
                pm = jnp.min(jnp.where(sc == m, pos, far), axis=0, keepdims=True)
                hit = pos == pm
                experts.append(jnp.max(jnp.where(hit, ex, -1.0), axis=0, keepdims=True))
                top.append(m)
                sc = jnp.where(hit, -jnp.inf, sc)
            top = jnp.concatenate(top, axis=0)
            e = jnp.exp(top - top[0:1])
            rows = pl.ds(pl.multiple_of(h * PEER_TOPK, PEER_TOPK), PEER_TOPK)
            gate_ref[0, rows, lanes] = e / jnp.sum(e, axis=0, keepdims=True)
            idx_ref[0, rows, lanes] = jnp.concatenate(experts, axis=0).astype(jnp.int32)
        return carry

    lax.fori_loop(0, PEER_HEADS, head_body, 0)


def _route(xn, wq_bf, pk_bf, tm):
    b, s, d = xn.shape
    const2 = lambda i, j: (0, 0)
    slot = lambda i, j: (i, 0, j)
    return pl.pallas_call(
        _route_kernel,
        out_shape=[jax.ShapeDtypeStruct((b, PEER_SLOTS, s), jnp.int32),
                   jax.ShapeDtypeStruct((b, PEER_SLOTS, s), F32)],
        grid=(b, s // tm),
        in_specs=[
            pl.BlockSpec((1, tm, d), lambda i, j: (i, j, 0)),
            pl.BlockSpec(wq_bf.shape, const2),
            pl.BlockSpec(pk_bf.shape, lambda i, j: (0, 0, 0)),
        ],
        out_specs=[pl.BlockSpec((1, PEER_SLOTS, tm), slot)] * 2,
        scratch_shapes=[pltpu.VMEM((2 * PEER_HEADS, PEER_NKEYS, tm), F32)],
        compiler_params=_cparams(("arbitrary", "arbitrary")),
        name="route",
    )(xn, wq_bf, pk_bf)


def _pair_slots():
    out = []
    for g in range(PEER_SLOTS // SUBLANES):
        for c in range(4):
            top = SUBLANES * g + SLOT_ORDER[c]
            out.append((top, top + 4))
    return out


def _gather_pair(tbl_ref, top_off, bot_off, upper):
    ra = tbl_ref[pl.ds(pl.multiple_of(top_off, ROWS_PER_EXPERT), SUBLANES), :]
    rb = tbl_ref[pl.ds(pl.multiple_of(bot_off, ROWS_PER_EXPERT), SUBLANES), :]
    row = jnp.where(upper, ra, rb)
    lo = lax.bitcast_convert_type(row << 16, F32)
    hi = lax.bitcast_convert_type(row & jnp.uint32(0xFFFF0000), F32)
    return lo, hi


def _merge_sublanes(a, b, h, sub):
    keep = (sub & h) == 0
    return jnp.where(keep, a + pltpu.roll(a, SUBLANES - h, 0), b + pltpu.roll(b, h, 0))


def _slot_specs(tb):
    return [pl.BlockSpec((None, None, SLOT_GROUP * tb), lambda i, j=j: (j, 0, i), memory_space=pltpu.SMEM,
                         pipeline_mode=pl.Buffered(1)) for j in range(PEER_SLOTS // SLOT_GROUP)]


def _group_slots(a, tb):
    n = a.shape[-1]
    a = a.reshape(PEER_SLOTS // SLOT_GROUP, SLOT_GROUP, n // tb, tb).transpose(0, 2, 1, 3)
    return a.reshape(PEER_SLOTS // SLOT_GROUP, 1, n * SLOT_GROUP)


def _slot_reader(refs, t, tb):
    at = [t + g * tb for g in range(SLOT_GROUP)]
    return lambda k: refs[k // SLOT_GROUP][at[k % SLOT_GROUP]]


def _peer_u_kernel(*refs):
    ngroups = PEER_SLOTS // SLOT_GROUP
    offs = refs[:ngroups]
    x_ref, tbl_ref, o_ref, z_ref, sel_ref = refs[ngroups:]
    tb = x_ref.shape[0] // SUBLANES
    sub = lax.broadcasted_iota(jnp.int32, (SUBLANES, LANES), 0)
    upper = sub < ROWS_PER_EXPERT
    pairs = _pair_slots()
    lane_pair = lax.broadcasted_iota(jnp.int32, (PEER_SLOTS, LANES), 1) >> 1

    row = lax.broadcasted_iota(jnp.int32, (2 * LANES, LANES), 0)
    col = lax.broadcasted_iota(jnp.int32, (2 * LANES, LANES), 1)
    sel_ref[...] = jnp.where((col & 1) == (row >= LANES).astype(jnp.int32), 1.0, 0.0).astype(BF16)
    o_ref[...] = jnp.zeros_like(o_ref)

    group_rows = 4 * PEER_SLOTS
    z_ref[...] = jnp.zeros_like(z_ref)

    def lane_partials(gi, zbase):
        for j in range(SUBLANES):
            t = gi * SUBLANES + j
            x = x_ref[pl.ds(pl.multiple_of(t * SUBLANES, SUBLANES), SUBLANES), :]
            swapped = pltpu.roll(x, ROWS_PER_EXPERT, 0)
            xa = jnp.where(upper, x, swapped)
            xb = jnp.where(upper, swapped, x)
            off = _slot_reader(offs, t, tb)
            zs = []
            for g in range(PEER_SLOTS // SUBLANES):
                ps = []
                for c in range(4):
                    top, bot = pairs[4 * g + c]
                    lo, hi = _gather_pair(tbl_ref, off(top), off(bot), upper)
                    ps.append(lo * xa + hi * xb)
                r0 = _merge_sublanes(ps[0], ps[1], 2, sub)
                r1 = _merge_sublanes(ps[2], ps[3], 2, sub)
                zs.append(_merge_sublanes(r0, r1, 1, sub))
            rows = zbase + (j // 2) * PEER_SLOTS
            cols = slice((j % 2) * LANES, (j % 2 + 1) * LANES)
            for m in range(len(zs) // 2):
                tile = jnp.concatenate([zs[2 * m], zs[2 * m + 1]], axis=0).astype(BF16)
                z_ref[pl.ds(pl.multiple_of(rows + 2 * SUBLANES * m, 2 * SUBLANES), 2 * SUBLANES), cols] = tile

    def lane_sums(gi, zbase):
        sums = jnp.dot(z_ref[pl.ds(zbase, group_rows), :], sel_ref[...], preferred_element_type=F32)
        block = pl.ds(pl.multiple_of((gi // (LANES // SUBLANES)) * LANES, LANES), LANES)
        first_pair = (gi % (LANES // SUBLANES)) * 4
        acc = o_ref[:, block]
        for p in range(4):
            acc = jnp.where(lane_pair == first_pair + p, sums[p * PEER_SLOTS:(p + 1) * PEER_SLOTS], acc)
        o_ref[:, block] = acc

    def zbase_of(gi):
        return pl.multiple_of((gi % 2) * group_rows, group_rows)

    def step(gi, carry):
        prev = jnp.maximum(gi - 1, 0)
        lane_sums(prev, zbase_of(gi + 1))
        lane_partials(gi, zbase_of(gi))
        return carry

    ngroups_tokens = tb // SUBLANES
    lax.fori_loop(0, ngroups_tokens, step, 0)
    lane_sums(ngroups_tokens - 1, (ngroups_tokens - 1) % 2 * group_rows)


def _peer_u(offs_g, x_rows, tbl, tb):
    n = x_rows.shape[0] // SUBLANES
    assert tb % LANES == 0
    return pl.pallas_call(
        _peer_u_kernel,
        out_shape=jax.ShapeDtypeStruct((PEER_SLOTS, n), F32),
        grid=(n // tb,),
        in_specs=_slot_specs(tb) + [
            pl.BlockSpec((tb * SUBLANES, LANES), lambda i: (i, 0)),
            pl.BlockSpec(tbl.shape, lambda i: (0, 0), pipeline_mode=pl.Buffered(1)),
        ],
        out_specs=pl.BlockSpec((PEER_SLOTS, tb), lambda i: (0, i)),
        scratch_shapes=[
            pltpu.VMEM((2 * 4 * PEER_SLOTS, 2 * LANES), BF16),
            pltpu.VMEM((2 * LANES, LANES), BF16),
        ],
        compiler_params=_cparams(("arbitrary",)),
        name="peer_u",
    )(*([offs_g] * (PEER_SLOTS // SLOT_GROUP)), x_rows, tbl)


def _peer_w_kernel(gate_ref, act_ref, w_ref):
    a = act_ref[...]
    w_ref[...] = gate_ref[...] * (0.5 * a * (1.0 + lax.erf(a * (2.0 ** -0.5))))


def _peer_w(gate_t, act_t, tm):
    n = gate_t.shape[-1]
    spec = pl.BlockSpec((PEER_SLOTS, tm), lambda i: (0, i))
    return pl.pallas_call(
        _peer_w_kernel,
        out_shape=jax.ShapeDtypeStruct(gate_t.shape, F32),
        grid=(n // tm,),
        in_specs=[spec, spec],
        out_specs=spec,
        compiler_params=_cparams(("arbitrary",)),
        name="peer_w",
    )(gate_t, act_t)


def _peer_v_kernel(*refs):
    ngroups = PEER_SLOTS // SLOT_GROUP
    offs = refs[:ngroups]
    w_ref, h_ref, tbl_ref, nw_ref, o_ref, acc_ref, whi_ref, wlo_ref, repa_ref, repb_ref, ones_ref = refs[ngroups:]
    tb = h_ref.shape[0] // SUBLANES
    sub = lax.broadcasted_iota(jnp.int32, (SUBLANES, LANES), 0)
    upper = sub < ROWS_PER_EXPERT
    pairs = _pair_slots()
    lane = lax.broadcasted_iota(jnp.int32, (PEER_SLOTS, LANES), 1)

    w = w_ref[...]
    whi = w.astype(BF16).astype(F32)
    whi_ref[...] = whi
    wlo_ref[...] = w - whi
    row = lax.broadcasted_iota(jnp.int32, (2 * LANES, 2 * LANES), 0)
    col = lax.broadcasted_iota(jnp.int32, (2 * LANES, 2 * LANES), 1)
    ones_ref[...] = jnp.where((row >= LANES) == (col >= LANES), 1.0, 0.0).astype(BF16)

    def replicate(t0, rep_ref):
        for r in range(V_TOKENS_PER_ITER // 2):
            t = t0 + 2 * r
            block = pl.ds(pl.multiple_of((t // LANES) * LANES, LANES), LANES)
            la = t % LANES
            out = None
            for part_ref in (whi_ref, wlo_ref):
                part = part_ref[:, block]
                lhs = jnp.concatenate([jnp.where(lane == la, part, 0.0), jnp.where(lane == la + 1, part, 0.0)],
                                      axis=1).astype(BF16)
                prod = jnp.dot(lhs, ones_ref[...], preferred_element_type=F32)
                out = prod if out is None else out + prod
            rep_ref[r] = out

    def token(t, rep_ref, slot, half):
        off = _slot_reader(offs, t, tb)
        lanes = slice(half * LANES, (half + 1) * LANES)
        acc = [jnp.zeros((SUBLANES, LANES), F32) for _ in range(4)]
        for q, (top, bot) in enumerate(pairs):
            lo, hi = _gather_pair(tbl_ref, off(top), off(bot), upper)
            wtop = jnp.broadcast_to(rep_ref[slot, top:top + 1, lanes], (SUBLANES, LANES))
            wbot = jnp.broadcast_to(rep_ref[slot, bot:bot + 1, lanes], (SUBLANES, LANES))
            wv = jnp.where(upper, wtop, wbot)
            k = 2 * (q % 2)
            acc[k] = acc[k] + lo * wv
            acc[k + 1] = acc[k + 1] + hi * wv
        lo = acc[0] + acc[2]
        hi = acc[1] + acc[3]
        lo = lo + pltpu.roll(lo, ROWS_PER_EXPERT, 0)
        hi = hi + pltpu.roll(hi, ROWS_PER_EXPERT, 0)
        rows = pl.ds(pl.multiple_of(t * SUBLANES, SUBLANES), SUBLANES)
        acc_ref[rows, :] = h_ref[rows, :] + jnp.where(upper, lo, hi)

    def tokens(t0, rep_ref):
        for r in range(V_TOKENS_PER_ITER):
            token(t0 + r, rep_ref, r // 2, r % 2)

    replicate(0, repa_ref)

    def step(i, carry):
        t0 = i * 2 * V_TOKENS_PER_ITER
        replicate(t0 + V_TOKENS_PER_ITER, repb_ref)
        tokens(t0, repa_ref)
        replicate(jnp.minimum(t0 + 2 * V_TOKENS_PER_ITER, tb - V_TOKENS_PER_ITER), repa_ref)
        tokens(t0 + V_TOKENS_PER_ITER, repb_ref)
        return carry

    lax.fori_loop(0, tb // (2 * V_TOKENS_PER_ITER), step, 0)

    def norm_step(i, carry):
        rows = pl.ds(pl.multiple_of(i * NORM_TOKENS * SUBLANES, NORM_TOKENS * SUBLANES), NORM_TOKENS * SUBLANES)
        h = acc_ref[rows, :].reshape(NORM_TOKENS, SUBLANES, LANES)
        ss = jnp.sum(jnp.sum(h * h, axis=2, keepdims=True), axis=1, keepdims=True)
        y = h * lax.rsqrt(ss * (1.0 / D_MODEL) + RMS_EPS) * nw_ref[...][None]
        acc_ref[rows, :] = y.reshape(NORM_TOKENS * SUBLANES, LANES)
        return carry

    lax.fori_loop(0, tb // NORM_TOKENS, norm_step, 0)
    for c in range(D_MODEL // LANES):
        o_ref[:, c * LANES:(c + 1) * LANES] = acc_ref[pl.ds(c, tb, stride=SUBLANES), :]


def _peer_v(offs_g, w_t, h_rows, tbl, norm_rows, tb):
    n = h_rows.shape[0] // SUBLANES
    assert tb % LANES == 0 and V_TOKENS_PER_ITER % 2 == 0 and tb % (2 * V_TOKENS_PER_ITER) == 0 and tb % NORM_TOKENS == 0
    rows = pl.BlockSpec((tb * SUBLANES, LANES), lambda i: (i, 0))
    return pl.pallas_call(
        _peer_v_kernel,
        out_shape=jax.ShapeDtypeStruct((n, D_MODEL), F32),
        grid=(n // tb,),
        in_specs=_slot_specs(tb) + [
            pl.BlockSpec((PEER_SLOTS, tb), lambda i: (0, i)),
            rows,
            pl.BlockSpec(tbl.shape, lambda i: (0, 0), pipeline_mode=pl.Buffered(1)),
            pl.BlockSpec((SUBLANES, LANES), lambda i: (0, 0)),
        ],
        out_specs=pl.BlockSpec((tb, D_MODEL), lambda i: (i, 0)),
        scratch_shapes=[
            pltpu.VMEM((tb * SUBLANES, LANES), F32),
            pltpu.VMEM((PEER_SLOTS, tb), F32),
            pltpu.VMEM((PEER_SLOTS, tb), F32),
            pltpu.VMEM((V_TOKENS_PER_ITER // 2, PEER_SLOTS, 2 * LANES), F32),
            pltpu.VMEM((V_TOKENS_PER_ITER // 2, PEER_SLOTS, 2 * LANES), F32),
            pltpu.VMEM((2 * LANES, 2 * LANES), BF16),
        ],
        compiler_params=_cparams(("arbitrary",)),
        name="peer_v",
    )(*([offs_g] * (PEER_SLOTS // SLOT_GROUP)), w_t, h_rows, tbl, norm_rows)


def _pack_table(t):
    half = D_MODEL // 2
    bits = lax.bitcast_convert_type(t.astype(BF16), jnp.uint16).astype(jnp.uint32)
    words = bits[:, :half] | (bits[:, half:] << 16)
    words = words.reshape(t.shape[0] * ROWS_PER_EXPERT, LANES)
    return jnp.pad(words, ((TABLE_PAD, TABLE_PAD), (0, 0)))


def _t5_bucket(rel):
    half = N_BUCKETS // 2
    exact = half // 2
    n = jnp.abs(rel)
    large = exact + (jnp.log(jnp.maximum(n, 1).astype(F32) / exact)
                     / math.log(MAX_DISTANCE / exact) * (half - exact)).astype(jnp.int32)
    large = jnp.minimum(large, half - 1)
    return jnp.where(rel > 0, half, 0) + jnp.where(n < exact, n, large)


def _attention_bias(rel_bias):
    rb = rel_bias.astype(F32)
    i = jnp.arange(ATT_BLOCK)
    jb = jnp.arange(3 * ATT_BLOCK)
    rel = (jb[None, :] - ATT_BLOCK) - i[:, None]
    band = jnp.where((jnp.abs(rel) <= ATT_BLOCK)[..., None], rb[_t5_bucket(rel)], NEG)
    mpos = PREFIX - N_META + jnp.arange(N_META)
    variants = []
    for blk in (0, 1):
        qpos = PREFIX + blk * ATT_BLOCK + i
        meta = rb[_t5_bucket(mpos[None, :] - qpos[:, None])]
        fill = jnp.full((ATT_BLOCK, ATT_BLOCK - N_META, ATT_HEADS), NEG, F32)
        variants.append(jnp.concatenate([band, meta, fill], axis=1).transpose(2, 0, 1))
    return jnp.stack(variants)


def _rope_tables(pos):
    half = RET_D // 2
    inv = ROPE_BASE ** (-jnp.arange(half, dtype=F32) / half)
    ang = pos.astype(F32)[:, None] * inv[None, :]
    cos, sin = jnp.cos(ang), jnp.sin(ang)
    return jnp.concatenate([cos, cos], axis=1), jnp.concatenate([-sin, sin], axis=1)


def _decay_tables(dec_f, dec_b, chunk):
    lf = jax.nn.log_sigmoid(dec_f.astype(F32))[:, None]
    lb = jax.nn.log_sigmoid(dec_b.astype(F32))[:, None]
    idx = jnp.arange(chunk, dtype=F32)
    diff = idx[:, None] - idx[None, :]
    dmat = (jnp.where(diff >= 0, jnp.exp(jnp.maximum(diff, 0.0)[None] * lf[:, :, None]), 0.0)
            + jnp.where(diff < 0, jnp.exp(jnp.maximum(-diff, 0.0)[None] * lb[:, :, None]), 0.0))
    bc = lambda v: jnp.broadcast_to(v[:, :, None], v.shape + (RET_D,))
    qwf = bc(jnp.exp((idx + 1.0)[None] * lf))
    kwf = bc(jnp.exp((chunk - 1.0 - idx)[None] * lf))
    qwb = bc(jnp.exp((chunk - idx)[None] * lb))
    kwb = bc(jnp.exp(idx[None] * lb))
    pidx = jnp.arange(PREFIX, dtype=F32)
    kwp = bc(jnp.exp((PREFIX - 1.0 - pidx)[None] * lf))
    gdec = jnp.concatenate([jnp.exp(chunk * lf[:, 0]), jnp.exp(chunk * lb[:, 0])])
    return dmat, qwf, qwb, kwf, kwb, kwp, gdec


def _divisor_tile(n, want):
    t = min(n, want)
    while n % t:
        t //= 2
    return t


def _encode(x, shared):
    b, s, d = x.shape
    n = b * s
    tm = _divisor_tile(s, 512)
    chunk = _divisor_tile(s, 256)
    cos_t, sin_t = _rope_tables(jnp.arange(s) + N_META)
    qr, kr, vr, gr, qa, ka, va = _proj(x, shared["norm_mix"], shared["w_in"], cos_t, sin_t, tm)
    o_r = _retention(qr, kr, vr, gr, shared["kr_pre"], shared["vr_pre"], *_decay_tables(
        shared["dec_f"], shared["dec_b"], chunk), shared["gn_w"], chunk)
    o_a = _attention(qa, ka, va, shared["k_meta"], shared["v_meta"], shared["bias"], shared["sink"])
    h_rows, xn_rows, xn_bf = _outproj(o_r, o_a, x, shared["w_out"], shared["norm_ffn"], tm)
    idx_t, gate_t = _route(xn_bf, shared["wq"], shared["pkeys"], _divisor_tile(s, ROUTE_TOKENS))
    idx_t = idx_t.transpose(1, 0, 2).reshape(PEER_SLOTS, n)
    gate_t = gate_t.transpose(1, 0, 2).reshape(PEER_SLOTS, n)
    top_half = (jnp.arange(PEER_SLOTS) % SUBLANES) < ROWS_PER_EXPERT
    tb = _divisor_tile(n, PEER_TOKENS)
    offs_g = _group_slots(idx_t * ROWS_PER_EXPERT + jnp.where(top_half, TABLE_PAD, 0)[:, None], tb)
    act_t = _peer_u(offs_g, xn_rows, shared["u_tbl"], tb)
    w_t = _peer_w(gate_t, act_t, _divisor_tile(n, 2048))
    y = _peer_v(offs_g, w_t, h_rows, shared["v_tbl"], shared["norm_final"], tb)
    return y.reshape(b, s, d)


def kernel(x_prompt, x_sample, meta_tokens, norm_mix_w, w_in, ret_decay_fwd, ret_decay_bwd, ret_gn_w,
           attn_sink, rel_bias, w_out, norm_ffn_w, peer_wq, peer_keys, peer_u, peer_v, norm_final_w):
    layer = 0
    shared = {
        "norm_mix": norm_mix_w[layer][None, :].astype(F32),
        "w_in": w_in[layer].astype(BF16),
        "dec_f": ret_decay_fwd[layer],
        "dec_b": ret_decay_bwd[layer],
        "gn_w": ret_gn_w[layer][None, :].astype(F32),
        "sink": attn_sink[layer].astype(F32),
        "bias": _attention_bias(rel_bias),
        "w_out": w_out[layer].astype(BF16),
        "norm_ffn": norm_ffn_w[layer][None, :].astype(F32),
        "wq": peer_wq[layer].astype(BF16),
        "pkeys": peer_keys[layer].reshape(2 * PEER_HEADS, PEER_NKEYS, PEER_NKEYS).astype(BF16),
        "u_tbl": _pack_table(peer_u[layer]),
        "v_tbl": _pack_table(peer_v[layer]),
        "norm_final": norm_final_w.reshape(SUBLANES, LANES).astype(F32),
    }
    prefix = jnp.concatenate([jnp.zeros((PREFIX - N_META, D_MODEL), x_prompt.dtype),
                              meta_tokens.astype(x_prompt.dtype)], axis=0)[None]
    cos_p, sin_p = _rope_tables(jnp.arange(PREFIX) - (PREFIX - N_META))
    _, kr_p, vr_p, _, _, ka_p, va_p = _proj(prefix, shared["norm_mix"], shared["w_in"], cos_p, sin_p, PREFIX)
    shared["kr_pre"], shared["vr_pre"] = kr_p[0], vr_p[0]
    pad_meta = lambda t: jnp.pad(t[0, PREFIX - N_META:], ((0, ATT_BLOCK - N_META), (0, 0)))
    shared["k_meta"], shared["v_meta"] = pad_meta(ka_p), pad_meta(va_p)
    return (_encode(x_prompt, shared), _encode(x_sample, shared))
```

```python
import functools
import math

import jax
import jax.numpy as jnp
from jax import lax
from jax.experimental import pallas as pl
from jax.experimental.pallas import tpu as pltpu

F32 = jnp.float32
BF16 = jnp.bfloat16

D_MODEL = 1024
N_META = 16
PREFIX = 128
RET_HEADS = 4
RET_D = 128
ATT_HEADS = 8
ATT_KV = 2
ATT_GROUP = ATT_HEADS // ATT_KV
ATT_HD = 64
ATT_BLOCK = 128
N_BUCKETS = 32
MAX_DISTANCE = 128
ROPE_BASE = 10000.0
PEER_HEADS = 8
PEER_NKEYS = 128
PEER_EXPERTS = PEER_NKEYS * PEER_NKEYS
PEER_TOPK = 16
PEER_SLOTS = PEER_HEADS * PEER_TOPK
RMS_EPS = 1e-6
GN_EPS = 1e-5
NEG = -1e30
IN_COLS = (0, 512, 1024, 1536, 2048, 2560, 2688, 2816)

LANES = 128
SUBLANES = 8
VMEM_LIMIT = 56 * 1024 * 1024
ROWS_PER_EXPERT = D_MODEL // 2 // LANES
TABLE_PAD = ROWS_PER_EXPERT
PEER_TOKENS = 256
ROUTE_TOKENS = 256
SLOT_ORDER = (0, 2, 1, 3)
SLOT_GROUP = 1
V_TOKENS_PER_ITER = 8
NORM_TOKENS = 32


def _cparams(sem):
    return pltpu.CompilerParams(dimension_semantics=sem, vmem_limit_bytes=VMEM_LIMIT)


def _proj_kernel(x_ref, nw_ref, w_ref, cos_ref, sin_ref,
                 qr_ref, kr_ref, vr_ref, gr_ref, qa_ref, ka_ref, va_ref):
    x = x_ref[0]
    ms = jnp.mean(x * x, axis=-1, keepdims=True)
    xn = (x * lax.rsqrt(ms + RMS_EPS) * nw_ref[...]).astype(BF16)
    cosf = cos_ref[...]
    sinf = sin_ref[...]

    def mm(i):
        return jnp.dot(xn, w_ref[:, IN_COLS[i]:IN_COLS[i + 1]], preferred_element_type=F32)

    def rotary(t, scale):
        for h in range(RET_HEADS):
            th = t[:, h * RET_D:(h + 1) * RET_D]
            yield h, (th * cosf + pltpu.roll(th, RET_D // 2, 1) * sinf) * scale

    for h, r in rotary(mm(0), 1.0):
        qr_ref[0, :, h * RET_D:(h + 1) * RET_D] = r.astype(BF16)
    for h, r in rotary(mm(1), RET_D ** -0.5):
        kr_ref[0, :, h * RET_D:(h + 1) * RET_D] = r.astype(BF16)
    vr_ref[0] = mm(2).astype(BF16)
    gr_ref[0] = mm(3)
    qa_ref[0] = (mm(4) * (ATT_HD ** -0.5)).astype(BF16)
    ka_ref[0] = mm(5).astype(BF16)
    va_ref[0] = mm(6).astype(BF16)


def _proj(x, norm_w, w_in_bf, cos_t, sin_t, tm):
    b, s, d = x.shape
    widths = (512, 512, 512, 512, 512, 128, 128)
    dtypes = (BF16, BF16, BF16, F32, BF16, BF16, BF16)
    return pl.pallas_call(
        _proj_kernel,
        out_shape=[jax.ShapeDtypeStruct((b, s, w), dt) for w, dt in zip(widths, dtypes)],
        grid=(b, s // tm),
        in_specs=[
            pl.BlockSpec((1, tm, d), lambda i, j: (i, j, 0)),
            pl.BlockSpec((1, d), lambda i, j: (0, 0)),
            pl.BlockSpec(w_in_bf.shape, lambda i, j: (0, 0)),
            pl.BlockSpec((tm, LANES), lambda i, j: (j, 0)),
            pl.BlockSpec((tm, LANES), lambda i, j: (j, 0)),
        ],
        out_specs=[pl.BlockSpec((1, tm, w), lambda i, j: (i, j, 0)) for w in widths],
        compiler_params=_cparams(("arbitrary", "arbitrary")),
        name="proj",
    )(x, norm_w, w_in_bf, cos_t, sin_t)


def _ret_kernel(q_ref, k_ref, v_ref, g_ref, kpre_ref, vpre_ref, dmat_ref, qwf_ref, qwb_ref,
                kwf_ref, kwb_ref, kwp_ref, gdec_ref, gnw_ref, o_ref, sf_ref, sb_ref, sball_ref):
    ph = pl.program_id(1)
    c = pl.program_id(2)
    nc = pl.num_programs(2)
    contract0 = (((0,), (0,)), ((), ()))
    contract1 = (((1,), (1,)), ((), ()))

    def head(ref, h):
        return ref[0, :, h * RET_D:(h + 1) * RET_D]

    def kv_update(kh, vh, kw):
        kw = (kh.astype(F32) * kw).astype(BF16)
        return lax.dot_general(kw, vh, contract0, preferred_element_type=F32)

    @pl.when(ph == 0)
    def _backward_states():
        @pl.when(c == 0)
        def _():
            sb_ref[...] = jnp.zeros_like(sb_ref)

        j = nc - 1 - c
        for h in range(RET_HEADS):
            sball_ref[j, h] = sb_ref[h].astype(BF16)
            sb_ref[h] = gdec_ref[RET_HEADS + h] * sb_ref[h] + kv_update(head(k_ref, h), head(v_ref, h), kwb_ref[h])

    @pl.when(ph == 1)
    def _outputs():
        @pl.when(c == 0)
        def _():
            for h in range(RET_HEADS):
                kp = kpre_ref[:, h * RET_D:(h + 1) * RET_D]
                vp = vpre_ref[:, h * RET_D:(h + 1) * RET_D]
                sf_ref[h] = kv_update(kp, vp, kwp_ref[h])

        for h in range(RET_HEADS):
            qh, kh, vh = head(q_ref, h), head(k_ref, h), head(v_ref, h)
            s = lax.dot_general(qh, kh, contract1, preferred_element_type=F32) * dmat_ref[h]
            o = jnp.dot(s.astype(BF16), vh, preferred_element_type=F32)
            o = o + jnp.dot(qh, sf_ref[h].astype(BF16), preferred_element_type=F32) * qwf_ref[h]
            o = o + jnp.dot(qh, sball_ref[c, h], preferred_element_type=F32) * qwb_ref[h]
            mu = jnp.mean(o, axis=-1, keepdims=True)
            oc = o - mu
            var = jnp.mean(oc * oc, axis=-1, keepdims=True)
            on = oc * lax.rsqrt(var + GN_EPS) * gnw_ref[:, h * RET_D:(h + 1) * RET_D]
            g = head(g_ref, h)
            o_ref[0, :, h * RET_D:(h + 1) * RET_D] = (g * jax.nn.sigmoid(g) * on).astype(BF16)
            sf_ref[h] = gdec_ref[h] * sf_ref[h] + kv_update(kh, vh, kwf_ref[h])


def _retention(qr, kr, vr, gr, kpre, vpre, dmat, qwf, qwb, kwf, kwb, kwp, gdec, gnw, chunk):
    b, s, w = qr.shape
    nc = s // chunk

    def seq_map(i, ph, c):
        return (i, jnp.where(ph == 0, nc - 1 - c, c), 0)

    def fwd_only(i, ph, c):
        return (i, c * ph, 0)

    const2 = lambda i, ph, c: (0, 0)
    const3 = lambda i, ph, c: (0, 0, 0)
    return pl.pallas_call(
        _ret_kernel,
        out_shape=jax.ShapeDtypeStruct((b, s, w), BF16),
        grid=(b, 2, nc),
        in_specs=[
            pl.BlockSpec((1, chunk, w), fwd_only),
            pl.BlockSpec((1, chunk, w), seq_map),
            pl.BlockSpec((1, chunk, w), seq_map),
            pl.BlockSpec((1, chunk, w), fwd_only),
            pl.BlockSpec(kpre.shape, const2),
            pl.BlockSpec(vpre.shape, const2),
            pl.BlockSpec(dmat.shape, const3),
            pl.BlockSpec(qwf.shape, const3),
            pl.BlockSpec(qwb.shape, const3),
            pl.BlockSpec(kwf.shape, const3),
            pl.BlockSpec(kwb.shape, const3),
            pl.BlockSpec(kwp.shape, const3),
            pl.BlockSpec(memory_space=pltpu.SMEM),
            pl.BlockSpec(gnw.shape, const2),
        ],
        out_specs=pl.BlockSpec((1, chunk, w), fwd_only),
        scratch_shapes=[
            pltpu.VMEM((RET_HEADS, RET_D, RET_D), F32),
            pltpu.VMEM((RET_HEADS, RET_D, RET_D), F32),
            pltpu.VMEM((nc, RET_HEADS, RET_D, RET_D), BF16),
        ],
        compiler_params=_cparams(("arbitrary", "arbitrary", "arbitrary")),
        name="retention",
    )(qr, kr, vr, gr, kpre, vpre, dmat, qwf, qwb, kwf, kwb, kwp, gdec, gnw)


def _attn_kernel(q_ref, kp_ref, kc_ref, kn_ref, vp_ref, vc_ref, vn_ref, km_ref, vm_ref,
                 bias_ref, sink_ref, o_ref):
    j = pl.program_id(1)
    nb = pl.num_programs(1)
    lane = lax.broadcasted_iota(jnp.int32, (1, 4 * ATT_BLOCK), 1)
    first = jnp.logical_and(lane < ATT_BLOCK, j == 0)
    last = jnp.logical_and(jnp.logical_and(lane >= 2 * ATT_BLOCK, lane < 3 * ATT_BLOCK), j == nb - 1)
    pen = jnp.where(jnp.logical_or(first, last), NEG, 0.0).astype(F32)
    contract1 = (((1,), (1,)), ((), ()))
    outs = []
    for g in range(ATT_KV):
        sl = slice(g * ATT_HD, (g + 1) * ATT_HD)
        kall = jnp.concatenate([kp_ref[0][:, sl], kc_ref[0][:, sl], kn_ref[0][:, sl], km_ref[:, sl]], axis=0)
        vall = jnp.concatenate([vp_ref[0][:, sl], vc_ref[0][:, sl], vn_ref[0][:, sl], vm_ref[:, sl]], axis=0)
        for hh in range(ATT_GROUP):
            h = g * ATT_GROUP + hh
            q = q_ref[0, :, h * ATT_HD:(h + 1) * ATT_HD]
            s = lax.dot_general(q, kall, contract1, preferred_element_type=F32)
            s = s + bias_ref[0, h] + pen
            snk = sink_ref[h]
            m = jnp.maximum(jnp.max(s, axis=-1, keepdims=True), snk)
            p = jnp.exp(s - m)
            den = jnp.sum(p, axis=-1, keepdims=True) + jnp.exp(snk - m)
            o = jnp.dot(p.astype(BF16), vall, preferred_element_type=F32)
            outs.append(o / den)
    for h2 in range(ATT_HEADS // 2):
        o_ref[0, :, h2 * LANES:(h2 + 1) * LANES] = jnp.concatenate(
            [outs[2 * h2], outs[2 * h2 + 1]], axis=-1).astype(BF16)


def _attention(qa, ka, va, kmeta, vmeta, bias, sink):
    b, s, w = qa.shape
    nb = s // ATT_BLOCK
    kvw = ka.shape[-1]
    prev = lambda i, j: (i, jnp.maximum(j - 1, 0), 0)
    cur = lambda i, j: (i, j, 0)
    nxt = lambda i, j: (i, jnp.minimum(j + 1, nb - 1), 0)
    const2 = lambda i, j: (0, 0)
    kv = lambda m: pl.BlockSpec((1, ATT_BLOCK, kvw), m)
    return pl.pallas_call(
        _attn_kernel,
        out_shape=jax.ShapeDtypeStruct((b, s, w), BF16),
        grid=(b, nb),
        in_specs=[
            pl.BlockSpec((1, ATT_BLOCK, w), cur),
            kv(prev), kv(cur), kv(nxt), kv(prev), kv(cur), kv(nxt),
            pl.BlockSpec(kmeta.shape, const2),
            pl.BlockSpec(vmeta.shape, const2),
            pl.BlockSpec((1,) + bias.shape[1:], lambda i, j: (jnp.minimum(j, 1), 0, 0, 0)),
            pl.BlockSpec(memory_space=pltpu.SMEM),
        ],
        out_specs=pl.BlockSpec((1, ATT_BLOCK, w), cur),
        compiler_params=_cparams(("arbitrary", "arbitrary")),
        name="attention",
    )(qa, ka, ka, ka, va, va, va, kmeta, vmeta, bias, sink)


def _store_token_rows(ref, v):
    for c in range(D_MODEL // LANES):
        ref[pl.ds(c, v.shape[0], stride=SUBLANES), :] = v[:, c * LANES:(c + 1) * LANES]


def _outproj_kernel(or_ref, oa_ref, x_ref, w_ref, nw_ref, h_ref, xn_ref, xnb_ref):
    half = or_ref.shape[-1]
    h = x_ref[0]
    h = h + jnp.dot(or_ref[0], w_ref[:half], preferred_element_type=F32)
    h = h + jnp.dot(oa_ref[0], w_ref[half:], preferred_element_type=F32)
    _store_token_rows(h_ref, h)
    ms = jnp.mean(h * h, axis=-1, keepdims=True)
    xn = h * lax.rsqrt(ms + RMS_EPS) * nw_ref[...]
    _store_token_rows(xn_ref, xn)
    xnb_ref[0] = xn.astype(BF16)


def _outproj(o_r, o_a, x, w_out_bf, norm_w, tm):
    b, s, d = x.shape
    half = o_r.shape[-1]
    row = lambda i, j: (i, j, 0)
    const2 = lambda i, j: (0, 0)
    token_rows = pl.BlockSpec((tm * SUBLANES, LANES), lambda i, j: (i * (s // tm) + j, 0))
    return pl.pallas_call(
        _outproj_kernel,
        out_shape=[jax.ShapeDtypeStruct((b * s * SUBLANES, LANES), F32)] * 2 + [jax.ShapeDtypeStruct((b, s, d), BF16)],
        grid=(b, s // tm),
        in_specs=[
            pl.BlockSpec((1, tm, half), row),
            pl.BlockSpec((1, tm, half), row),
            pl.BlockSpec((1, tm, d), row),
            pl.BlockSpec(w_out_bf.shape, const2),
            pl.BlockSpec((1, d), const2),
        ],
        out_specs=[token_rows, token_rows, pl.BlockSpec((1, tm, d), row)],
        compiler_params=_cparams(("arbitrary", "arbitrary")),
        name="outproj",
    )(o_r, o_a, x, w_out_bf, norm_w)


def _top16_rows(s, iota):
    nrows = float(s.shape[0])
    vals, ids = [], []
    for _ in range(PEER_TOPK):
        m = jnp.max(s, axis=0, keepdims=True)
        am = jnp.min(jnp.where(s == m, iota, nrows), axis=0, keepdims=True)
        vals.append(m)
        ids.append(am)
        s = jnp.where(iota == am, -jnp.inf, s)
    return jnp.concatenate(vals, axis=0), jnp.concatenate(ids, axis=0)


_PAIR_GROUPS = ((0, 0, 8), (0, 8, 8), (1, 0, 8), (2, 0, 5), (3, 0, 4), (4, 0, 3), (5, 0, 2), (6, 0, 2), (7, 0, 2))


def _route_kernel(x_ref, wq_ref, pk_ref, idx_ref, gate_ref, sc_ref):
    tm = x_ref.shape[1]
    q = jnp.dot(x_ref[0], wq_ref[...], preferred_element_type=F32).astype(BF16)
    contract1 = (((1,), (1,)), ((), ()))
    for hp in range(2 * PEER_HEADS):
        sc_ref[hp] = lax.dot_general(pk_ref[hp], q[:, hp * PEER_NKEYS:(hp + 1) * PEER_NKEYS], contract1,
                                     preferred_element_type=F32)
    iota = lax.broadcasted_iota(jnp.int32, (PEER_NKEYS, LANES), 0).astype(F32)
    sub = lax.broadcasted_iota(jnp.int32, (SUBLANES, LANES), 0)
    subf = sub.astype(F32)
    far = float(PEER_TOPK * PEER_TOPK)

    def head_body(h, carry):
        for lt in range(tm // LANES):
            lanes = slice(lt * LANES, (lt + 1) * LANES)
            s1, i1 = _top16_rows(sc_ref[2 * h, :, lanes], iota)
            s2, i2 = _top16_rows(sc_ref[2 * h + 1, :, lanes], iota)
            i1 = i1 * float(PEER_NKEYS)
            sc, ex, pos = [], [], []
            for i, j0, cnt in _PAIR_GROUPS:
                ok = sub < cnt
                sc.append(jnp.where(ok, s1[i:i + 1] + s2[j0:j0 + SUBLANES], -jnp.inf))
                ex.append(i1[i:i + 1] + i2[j0:j0 + SUBLANES])
                pos.append(jnp.where(ok, float(i * PEER_TOPK + j0) + subf, far))
            sc.append(s1[SUBLANES:] + s2[0:1])
            ex.append(i1[SUBLANES:] + i2[0:1])
            pos.append((float(SUBLANES) + subf) * float(PEER_TOPK))
            sc = jnp.concatenate(sc, axis=0)
            ex = jnp.concatenate(ex, axis=0)
            pos = jnp.concatenate(pos, axis=0)
            top, experts = [], []
            for _ in range(PEER_TOPK):
                m = jnp.max(sc, axis=0, keepdims=True)
                pm = jnp.min(jnp.where(sc == m, pos, far), axis=0, keepdims=True)
                hit = pos == pm
                experts.append(jnp.max(jnp.where(hit, ex, -1.0), axis=0, keepdims=True))
                top.append(m)
                sc = jnp.where(hit, -jnp.inf, sc)
            top = jnp.concatenate(top, axis=0)
            e = jnp.exp(top - top[0:1])
            rows = pl.ds(pl.multiple_of(h * PEER_TOPK, PEER_TOPK), PEER_TOPK)
            gate_ref[0, rows, lanes] = e / jnp.sum(e, axis=0, keepdims=True)
            idx_ref[0, rows, lanes] = jnp.concatenate(experts, axis=0).astype(jnp.int32)
        return carry

    lax.fori_loop(0, PEER_HEADS, head_body, 0)


def _route(xn, wq_bf, pk_bf, tm):
    b, s, d = xn.shape
    const2 = lambda i, j: (0, 0)
    slot = lambda i, j: (i, 0, j)
    return pl.pallas_call(
        _route_kernel,
        out_shape=[jax.ShapeDtypeStruct((b, PEER_SLOTS, s), jnp.int32),
                   jax.ShapeDtypeStruct((b, PEER_SLOTS, s), F32)],
        grid=(b, s // tm),
        in_specs=[
            pl.BlockSpec((1, tm, d), lambda i, j: (i, j, 0)),
            pl.BlockSpec(wq_bf.shape, const2),
            pl.BlockSpec(pk_bf.shape, lambda i, j: (0, 0, 0)),
        ],
        out_specs=[pl.BlockSpec((1, PEER_SLOTS, tm), slot)] * 2,
        scratch_shapes=[pltpu.VMEM((2 * PEER_HEADS, PEER_NKEYS, tm), F32)],
        compiler_params=_cparams(("arbitrary", "arbitrary")),
        name="route",
    )(xn, wq_bf, pk_bf)


def _pair_slots():
    out = []
    for g in range(PEER_SLOTS // SUBLANES):
        for c in range(4):
            top = SUBLANES * g + SLOT_ORDER[c]
            out.append((top, top + 4))
    return out


def _gather_words(tbl_ref, top_off, bot_off, upper):
    ra = tbl_ref[pl.ds(pl.multiple_of(top_off, ROWS_PER_EXPERT), SUBLANES), :]
    rb = tbl_ref[pl.ds(pl.multiple_of(bot_off, ROWS_PER_EXPERT), SUBLANES), :]
    return jnp.where(upper, ra, rb)


def _unpack_words(words):
    lo = lax.bitcast_convert_type(words << 16, F32)
    hi = lax.bitcast_convert_type(words & jnp.uint32(0xFFFF0000), F32)
    return lo, hi


def _gather_pair(tbl_ref, top_off, bot_off, upper):
    return _unpack_words(_gather_words(tbl_ref, top_off, bot_off, upper))


def _merge_sublanes(a, b, h, sub):
    keep = (sub & h) == 0
    return jnp.where(keep, a + pltpu.roll(a, SUBLANES - h, 0), b + pltpu.roll(b, h, 0))


def _slot_specs(tb):
    return [pl.BlockSpec((None, None, SLOT_GROUP * tb), lambda i, j=j: (j, 0, i), memory_space=pltpu.SMEM,
                         pipeline_mode=pl.Buffered(1)) for j in range(PEER_SLOTS // SLOT_GROUP)]


def _group_slots(a, tb):
    n = a.shape[-1]
    a = a.reshape(PEER_SLOTS // SLOT_GROUP, SLOT_GROUP, n // tb, tb).transpose(0, 2, 1, 3)
    return a.reshape(PEER_SLOTS // SLOT_GROUP, 1, n * SLOT_GROUP)


def _slot_reader(refs, t, tb):
    at = [t + g * tb for g in range(SLOT_GROUP)]
    return lambda k: refs[k // SLOT_GROUP][at[k % SLOT_GROUP]]


def _peer_u_kernel(*refs):
    ngroups = PEER_SLOTS // SLOT_GROUP
    offs = refs[:ngroups]
    x_ref, tbl_ref, o_ref, z_ref, sel_ref = refs[ngroups:]
    tb = x_ref.shape[0] // SUBLANES
    sub = lax.broadcasted_iota(jnp.int32, (SUBLANES, LANES), 0)
    upper = sub < ROWS_PER_EXPERT
    pairs = _pair_slots()
    lane_pair = lax.broadcasted_iota(jnp.int32, (PEER_SLOTS, LANES), 1) >> 1

    row = lax.broadcasted_iota(jnp.int32, (2 * LANES, LANES), 0)
    col = lax.broadcasted_iota(jnp.int32, (2 * LANES, LANES), 1)
    sel_ref[...] = jnp.where((col & 1) == (row >= LANES).astype(jnp.int32), 1.0, 0.0).astype(BF16)
    o_ref[...] = jnp.zeros_like(o_ref)

    group_rows = 4 * PEER_SLOTS
    z_ref[...] = jnp.zeros_like(z_ref)

    def lane_partials(gi, zbase):
        for j in range(SUBLANES):
            t = gi * SUBLANES + j
            x = x_ref[pl.ds(pl.multiple_of(t * SUBLANES, SUBLANES), SUBLANES), :]
            swapped = pltpu.roll(x, ROWS_PER_EXPERT, 0)
            xa = jnp.where(upper, x, swapped)
            xb = jnp.where(upper, swapped, x)
            xa_bits = lax.bitcast_convert_type(xa.astype(BF16).astype(F32), jnp.uint32)
            xb_bits = lax.bitcast_convert_type(xb.astype(BF16).astype(F32), jnp.uint32)
            xw = pltpu.bitcast((xb_bits & jnp.uint32(0xFFFF0000)) | (xa_bits >> 16), BF16)
            off = _slot_reader(offs, t, tb)
            zs = []
            for g in range(PEER_SLOTS // SUBLANES):
                ps = []
                for c in range(4):
                    top, bot = pairs[4 * g + c]
                    words = _gather_words(tbl_ref, off(top), off(bot), upper)
                    lo, hi = _unpack_words(pltpu.bitcast(pltpu.bitcast(words, BF16) * xw, jnp.uint32))
                    ps.append(lo + hi)
                r0 = _merge_sublanes(ps[0], ps[1], 2, sub)
                r1 = _merge_sublanes(ps[2], ps[3], 2, sub)
                zs.append(_merge_sublanes(r0, r1, 1, sub))
            rows = zbase + (j // 2) * PEER_SLOTS
            cols = slice((j % 2) * LANES, (j % 2 + 1) * LANES)
            for m in range(len(zs) // 2):
                tile = jnp.concatenate([zs[2 * m], zs[2 * m + 1]], axis=0).astype(BF16)
                z_ref[pl.ds(pl.multiple_of(rows + 2 * SUBLANES * m, 2 * SUBLANES), 2 * SUBLANES), cols] = tile

    def lane_sums(gi, zbase):
        sums = jnp.dot(z_ref[pl.ds(zbase, group_rows), :], sel_ref[...], preferred_element_type=F32)
        block = pl.ds(pl.multiple_of((gi // (LANES // SUBLANES)) * LANES, LANES), LANES)
        first_pair = (gi % (LANES // SUBLANES)) * 4
        acc = o_ref[:, block]
        for p in range(4):
            acc = jnp.where(lane_pair == first_pair + p, sums[p * PEER_SLOTS:(p + 1) * PEER_SLOTS], acc)
        o_ref[:, block] = acc

    def zbase_of(gi):
        return pl.multiple_of((gi % 2) * group_rows, group_rows)

    def step(gi, carry):
        prev = jnp.maximum(gi - 1, 0)
        lane_sums(prev, zbase_of(gi + 1))
        lane_partials(gi, zbase_of(gi))
        return carry

    ngroups_tokens = tb // SUBLANES
    lax.fori_loop(0, ngroups_tokens, step, 0)
    lane_sums(ngroups_tokens - 1, (ngroups_tokens - 1) % 2 * group_rows)


def _peer_u(offs_g, x_rows, tbl, tb):
    n = x_rows.shape[0] // SUBLANES
    assert tb % LANES == 0
    return pl.pallas_call(
        _peer_u_kernel,
        out_shape=jax.ShapeDtypeStruct((PEER_SLOTS, n), F32),
        grid=(n // tb,),
        in_specs=_slot_specs(tb) + [
            pl.BlockSpec((tb * SUBLANES, LANES), lambda i: (i, 0)),
            pl.BlockSpec(tbl.shape, lambda i: (0, 0), pipeline_mode=pl.Buffered(1)),
        ],
        out_specs=pl.BlockSpec((PEER_SLOTS, tb), lambda i: (0, i)),
        scratch_shapes=[
            pltpu.VMEM((2 * 4 * PEER_SLOTS, 2 * LANES), BF16),
            pltpu.VMEM((2 * LANES, LANES), BF16),
        ],
        compiler_params=_cparams(("arbitrary",)),
        name="peer_u",
    )(*([offs_g] * (PEER_SLOTS // SLOT_GROUP)), x_rows, tbl)


def _peer_w_kernel(gate_ref, act_ref, w_ref):
    a = act_ref[...]
    w_ref[...] = gate_ref[...] * (0.5 * a * (1.0 + lax.erf(a * (2.0 ** -0.5))))


def _peer_w(gate_t, act_t, tm):
    n = gate_t.shape[-1]
    spec = pl.BlockSpec((PEER_SLOTS, tm), lambda i: (0, i))
    return pl.pallas_call(
        _peer_w_kernel,
        out_shape=jax.ShapeDtypeStruct(gate_t.shape, F32),
        grid=(n // tm,),
        in_specs=[spec, spec],
        out_specs=spec,
        compiler_params=_cparams(("arbitrary",)),
        name="peer_w",
    )(gate_t, act_t)


def _peer_v_kernel(*refs):
    ngroups = PEER_SLOTS // SLOT_GROUP
    offs = refs[:ngroups]
    w_ref, h_ref, tbl_ref, nw_ref, o_ref, acc_ref, whi_ref, wlo_ref, repa_ref, repb_ref, ones_ref = refs[ngroups:]
    tb = h_ref.shape[0] // SUBLANES
    sub = lax.broadcasted_iota(jnp.int32, (SUBLANES, LANES), 0)
    upper = sub < ROWS_PER_EXPERT
    pairs = _pair_slots()
    lane = lax.broadcasted_iota(jnp.int32, (PEER_SLOTS, LANES), 1)

    w = w_ref[...]
    whi = w.astype(BF16).astype(F32)
    whi_ref[...] = whi
    wlo_ref[...] = w - whi
    row = lax.broadcasted_iota(jnp.int32, (2 * LANES, 2 * LANES), 0)
    col = lax.broadcasted_iota(jnp.int32, (2 * LANES, 2 * LANES), 1)
    ones_ref[...] = jnp.where((row >= LANES) == (col >= LANES), 1.0, 0.0).astype(BF16)

    def replicate(t0, rep_ref):
        for r in range(V_TOKENS_PER_ITER // 2):
            t = t0 + 2 * r
            block = pl.ds(pl.multiple_of((t // LANES) * LANES, LANES), LANES)
            la = t % LANES
            out = None
            for part_ref in (whi_ref, wlo_ref):
                part = part_ref[:, block]
                lhs = jnp.concatenate([jnp.where(lane == la, part, 0.0), jnp.where(lane == la + 1, part, 0.0)],
                                      axis=1).astype(BF16)
                prod = jnp.dot(lhs, ones_ref[...], preferred_element_type=F32)
                out = prod if out is None else out + prod
            rep_ref[r] = out

    def token(t, rep_ref, slot, half):
        off = _slot_reader(offs, t, tb)
        lanes = slice(half * LANES, (half + 1) * LANES)
        acc = [jnp.zeros((SUBLANES, LANES), F32) for _ in range(4)]
        for q, (top, bot) in enumerate(pairs):
            lo, hi = _gather_pair(tbl_ref, off(top), off(bot), upper)
            wtop = jnp.broadcast_to(rep_ref[slot, top:top + 1, lanes], (SUBLANES, LANES))
            wbot = jnp.broadcast_to(rep_ref[slot, bot:bot + 1, lanes], (SUBLANES, LANES))
            wv = jnp.where(upper, wtop, wbot)
            k = 2 * (q % 2)
            acc[k] = acc[k] + lo * wv
            acc[k + 1] = acc[k + 1] + hi * wv
        lo = acc[0] + acc[2]
        hi = acc[1] + acc[3]
        lo = lo + pltpu.roll(lo, ROWS_PER_EXPERT, 0)
        hi = hi + pltpu.roll(hi, ROWS_PER_EXPERT, 0)
        rows = pl.ds(pl.multiple_of(t * SUBLANES, SUBLANES), SUBLANES)
        acc_ref[rows, :] = h_ref[rows, :] + jnp.where(upper, lo, hi)

    def tokens(t0, rep_ref):
        for r in range(V_TOKENS_PER_ITER):
            token(t0 + r, rep_ref, r // 2, r % 2)

    replicate(0, repa_ref)

    def step(i, carry):
        t0 = i * 2 * V_TOKENS_PER_ITER
        replicate(t0 + V_TOKENS_PER_ITER, repb_ref)
        tokens(t0, repa_ref)
        replicate(jnp.minimum(t0 + 2 * V_TOKENS_PER_ITER, tb - V_TOKENS_PER_ITER), repa_ref)
        tokens(t0 + V_TOKENS_PER_ITER, repb_ref)
        return carry

    lax.fori_loop(0, tb // (2 * V_TOKENS_PER_ITER), step, 0)

    def norm_step(i, carry):
        rows = pl.ds(pl.multiple_of(i * NORM_TOKENS * SUBLANES, NORM_TOKENS * SUBLANES), NORM_TOKENS * SUBLANES)
        h = acc_ref[rows, :].reshape(NORM_TOKENS, SUBLANES, LANES)
        ss = jnp.sum(jnp.sum(h * h, axis=2, keepdims=True), axis=1, keepdims=True)
        y = h * lax.rsqrt(ss * (1.0 / D_MODEL) + RMS_EPS) * nw_ref[...][None]
        acc_ref[rows, :] = y.reshape(NORM_TOKENS * SUBLANES, LANES)
        return carry

    lax.fori_loop(0, tb // NORM_TOKENS, norm_step, 0)
    for c in range(D_MODEL // LANES):
        o_ref[:, c * LANES:(c + 1) * LANES] = acc_ref[pl.ds(c, tb, stride=SUBLANES), :]


def _peer_v(offs_g, w_t, h_rows, tbl, norm_rows, tb):
    n = h_rows.shape[0] // SUBLANES
    assert tb % LANES == 0 and V_TOKENS_PER_ITER % 2 == 0 and tb % (2 * V_TOKENS_PER_ITER) == 0 and tb % NORM_TOKENS == 0
    rows = pl.BlockSpec((tb * SUBLANES, LANES), lambda i: (i, 0))
    return pl.pallas_call(
        _peer_v_kernel,
        out_shape=jax.ShapeDtypeStruct((n, D_MODEL), F32),
        grid=(n // tb,),
        in_specs=_slot_specs(tb) + [
            pl.BlockSpec((PEER_SLOTS, tb), lambda i: (0, i)),
            rows,
            pl.BlockSpec(tbl.shape, lambda i: (0, 0), pipeline_mode=pl.Buffered(1)),
            pl.BlockSpec((SUBLANES, LANES), lambda i: (0, 0)),
        ],
        out_specs=pl.BlockSpec((tb, D_MODEL), lambda i: (i, 0)),
        scratch_shapes=[
            pltpu.VMEM((tb * SUBLANES, LANES), F32),
            pltpu.VMEM((PEER_SLOTS, tb), F32),
            pltpu.VMEM((PEER_SLOTS, tb), F32),
            pltpu.VMEM((V_TOKENS_PER_ITER // 2, PEER_SLOTS, 2 * LANES), F32),
            pltpu.VMEM((V_TOKENS_PER_ITER // 2, PEER_SLOTS, 2 * LANES), F32),
            pltpu.VMEM((2 * LANES, 2 * LANES), BF16),
        ],
        compiler_params=_cparams(("arbitrary",)),
        name="peer_v",
    )(*([offs_g] * (PEER_SLOTS // SLOT_GROUP)), w_t, h_rows, tbl, norm_rows)


def _pack_table(t):
    half = D_MODEL // 2
    bits = lax.bitcast_convert_type(t.astype(BF16), jnp.uint16).astype(jnp.uint32)
    words = bits[:, :half] | (bits[:, half:] << 16)
    words = words.reshape(t.shape[0] * ROWS_PER_EXPERT, LANES)
    return jnp.pad(words, ((TABLE_PAD, TABLE_PAD), (0, 0)))


def _t5_bucket(rel):
    half = N_BUCKETS // 2
    exact = half // 2
    n = jnp.abs(rel)
    large = exact + (jnp.log(jnp.maximum(n, 1).astype(F32) / exact)
                     / math.log(MAX_DISTANCE / exact) * (half - exact)).astype(jnp.int32)
    large = jnp.minimum(large, half - 1)
    return jnp.where(rel > 0, half, 0) + jnp.where(n < exact, n, large)


def _attention_bias(rel_bias):
    rb = rel_bias.astype(F32)
    i = jnp.arange(ATT_BLOCK)
    jb = jnp.arange(3 * ATT_BLOCK)
    rel = (jb[None, :] - ATT_BLOCK) - i[:, None]
    band = jnp.where((jnp.abs(rel) <= ATT_BLOCK)[..., None], rb[_t5_bucket(rel)], NEG)
    mpos = PREFIX - N_META + jnp.arange(N_META)
    variants = []
    for blk in (0, 1):
        qpos = PREFIX + blk * ATT_BLOCK + i
        meta = rb[_t5_bucket(mpos[None, :] - qpos[:, None])]
        fill = jnp.full((ATT_BLOCK, ATT_BLOCK - N_META, ATT_HEADS), NEG, F32)
        variants.append(jnp.concatenate([band, meta, fill], axis=1).transpose(2, 0, 1))
    return jnp.stack(variants)


def _rope_tables(pos):
    half = RET_D // 2
    inv = ROPE_BASE ** (-jnp.arange(half, dtype=F32) / half)
    ang = pos.astype(F32)[:, None] * inv[None, :]
    cos, sin = jnp.cos(ang), jnp.sin(ang)
    return jnp.concatenate([cos, cos], axis=1), jnp.concatenate([-sin, sin], axis=1)


def _decay_tables(dec_f, dec_b, chunk):
    lf = jax.nn.log_sigmoid(dec_f.astype(F32))[:, None]
    lb = jax.nn.log_sigmoid(dec_b.astype(F32))[:, None]
    idx = jnp.arange(chunk, dtype=F32)
    diff = idx[:, None] - idx[None, :]
    dmat = (jnp.where(diff >= 0, jnp.exp(jnp.maximum(diff, 0.0)[None] * lf[:, :, None]), 0.0)
            + jnp.where(diff < 0, jnp.exp(jnp.maximum(-diff, 0.0)[None] * lb[:, :, None]), 0.0))
    bc = lambda v: jnp.broadcast_to(v[:, :, None], v.shape + (RET_D,))
    qwf = bc(jnp.exp((idx + 1.0)[None] * lf))
    kwf = bc(jnp.exp((chunk - 1.0 - idx)[None] * lf))
    qwb = bc(jnp.exp((chunk - idx)[None] * lb))
    kwb = bc(jnp.exp(idx[None] * lb))
    pidx = jnp.arange(PREFIX, dtype=F32)
    kwp = bc(jnp.exp((PREFIX - 1.0 - pidx)[None] * lf))
    gdec = jnp.concatenate([jnp.exp(chunk * lf[:, 0]), jnp.exp(chunk * lb[:, 0])])
    return dmat, qwf, qwb, kwf, kwb, kwp, gdec


def _divisor_tile(n, want):
    t = min(n, want)
    while n % t:
        t //= 2
    return t


def _encode(x, shared):
    b, s, d = x.shape
    n = b * s
    tm = _divisor_tile(s, 512)
    chunk = _divisor_tile(s, 256)
    cos_t, sin_t = _rope_tables(jnp.arange(s) + N_META)
    qr, kr, vr, gr, qa, ka, va = _proj(x, shared["norm_mix"], shared["w_in"], cos_t, sin_t, tm)
    o_r = _retention(qr, kr, vr, gr, shared["kr_pre"], shared["vr_pre"], *_decay_tables(
        shared["dec_f"], shared["dec_b"], chunk), shared["gn_w"], chunk)
    o_a = _attention(qa, ka, va, shared["k_meta"], shared["v_meta"], shared["bias"], shared["sink"])
    h_rows, xn_rows, xn_bf = _outproj(o_r, o_a, x, shared["w_out"], shared["norm_ffn"], tm)
    idx_t, gate_t = _route(xn_bf, shared["wq"], shared["pkeys"], _divisor_tile(s, ROUTE_TOKENS))
    idx_t = idx_t.transpose(1, 0, 2).reshape(PEER_SLOTS, n)
    gate_t = gate_t.transpose(1, 0, 2).reshape(PEER_SLOTS, n)
    top_half = (jnp.arange(PEER_SLOTS) % SUBLANES) < ROWS_PER_EXPERT
    tb = _divisor_tile(n, PEER_TOKENS)
    offs_g = _group_slots(idx_t * ROWS_PER_EXPERT + jnp.where(top_half, TABLE_PAD, 0)[:, None], tb)
    act_t = _peer_u(offs_g, xn_rows, shared["u_tbl"], tb)
    w_t = _peer_w(gate_t, act_t, _divisor_tile(n, 2048))
    y = _peer_v(offs_g, w_t, h_rows, shared["v_tbl"], shared["norm_final"], tb)
    return y.reshape(b, s, d)


def kernel(x_prompt, x_sample, meta_tokens, norm_mix_w, w_in, ret_decay_fwd, ret_decay_bwd, ret_gn_w,
           attn_sink, rel_bias, w_out, norm_ffn_w, peer_wq, peer_keys, peer_u, peer_v, norm_final_w):
    layer = 0
    shared = {
        "norm_mix": norm_mix_w[layer][None, :].astype(F32),
        "w_in": w_in[layer].astype(BF16),
        "dec_f": ret_decay_fwd[layer],
        "dec_b": ret_decay_bwd[layer],
        "gn_w": ret_gn_w[layer][None, :].astype(F32),
        "sink": attn_sink[layer].astype(F32),
        "bias": _attention_bias(rel_bias),
        "w_out": w_out[layer].astype(BF16),
        "norm_ffn": norm_ffn_w[layer][None, :].astype(F32),
        "wq": peer_wq[layer].astype(BF16),
        "pkeys": peer_keys[layer].reshape(2 * PEER_HEADS, PEER_NKEYS, PEER_NKEYS).astype(BF16),
        "u_tbl": _pack_table(peer_u[layer]),
        "v_tbl": _pack_table(peer_v[layer]),
        "norm_final": norm_final_w.reshape(SUBLANES, LANES).astype(F32),
    }
    prefix = jnp.concatenate([jnp.zeros((PREFIX - N_META, D_MODEL), x_prompt.dtype),
                              meta_tokens.astype(x_prompt.dtype)], axis=0)[None]
    cos_p, sin_p = _rope_tables(jnp.arange(PREFIX) - (PREFIX - N_META))
    _, kr_p, vr_p, _, _, ka_p, va_p = _proj(prefix, shared["norm_mix"], shared["w_in"], cos_p, sin_p, PREFIX)
    shared["kr_pre"], shared["vr_pre"] = kr_p[0], vr_p[0]
    pad_meta = lambda t: jnp.pad(t[0, PREFIX - N_META:], ((0, ATT_BLOCK - N_META), (0, 0)))
    shared["k_meta"], shared["v_meta"] = pad_meta(ka_p), pad_meta(va_p)
    return (_encode(x_prompt, shared), _encode(x_sample, shared))
```

```python
import functools
import math

import jax
import jax.numpy as jnp
from jax import lax
from jax.experimental import pallas as pl
from jax.experimental.pallas import tpu as pltpu

F32 = jnp.float32
BF16 = jnp.bfloat16

D_MODEL = 1024
N_META = 16
PREFIX = 128
RET_HEADS = 4
RET_D = 128
ATT_HEADS = 8
ATT_KV = 2
ATT_GROUP = ATT_HEADS // ATT_KV
ATT_HD = 64
ATT_BLOCK = 128
N_BUCKETS = 32
MAX_DISTANCE = 128
ROPE_BASE = 10000.0
PEER_HEADS = 8
PEER_NKEYS = 128
PEER_EXPERTS = PEER_NKEYS * PEER_NKEYS
PEER_TOPK = 16
PEER_SLOTS = PEER_HEADS * PEER_TOPK
RMS_EPS = 1e-6
GN_EPS = 1e-5
NEG = -1e30
IN_COLS = (0, 512, 1024, 1536, 2048, 2560, 2688, 2816)

LANES = 128
SUBLANES = 8
VMEM_LIMIT = 56 * 1024 * 1024
ROWS_PER_EXPERT = D_MODEL // 2 // LANES
TABLE_PAD = ROWS_PER_EXPERT
PEER_TOKENS = 256
ROUTE_TOKENS = 256
SLOT_ORDER = (0, 2, 1, 3)
SLOT_GROUP = 1
V_TOKENS_PER_ITER = 8
NORM_TOKENS = 32


def _cparams(sem):
    return pltpu.CompilerParams(dimension_semantics=sem, vmem_limit_bytes=VMEM_LIMIT)


def _proj_kernel(x_ref, nw_ref, w_ref, cos_ref, sin_ref,
                 qr_ref, kr_ref, vr_ref, gr_ref, qa_ref, ka_ref, va_ref):
    x = x_ref[0]
    ms = jnp.mean(x * x, axis=-1, keepdims=True)
    xn = (x * lax.rsqrt(ms + RMS_EPS) * nw_ref[...]).astype(BF16)
    cosf = cos_ref[...]
    sinf = sin_ref[...]

    def mm(i):
        return jnp.dot(xn, w_ref[:, IN_COLS[i]:IN_COLS[i + 1]], preferred_element_type=F32)

    def rotary(t, scale):
        for h in range(RET_HEADS):
            th = t[:, h * RET_D:(h + 1) * RET_D]
            yield h, (th * cosf + pltpu.roll(th, RET_D // 2, 1) * sinf) * scale

    for h, r in rotary(mm(0), 1.0):
        qr_ref[0, :, h * RET_D:(h + 1) * RET_D] = r.astype(BF16)
    for h, r in rotary(mm(1), RET_D ** -0.5):
        kr_ref[0, :, h * RET_D:(h + 1) * RET_D] = r.astype(BF16)
    vr_ref[0] = mm(2).astype(BF16)
    gr_ref[0] = mm(3)
    qa_ref[0] = (mm(4) * (ATT_HD ** -0.5)).astype(BF16)
    ka_ref[0] = mm(5).astype(BF16)
    va_ref[0] = mm(6).astype(BF16)


def _proj(x, norm_w, w_in_bf, cos_t, sin_t, tm):
    b, s, d = x.shape
    widths = (512, 512, 512, 512, 512, 128, 128)
    dtypes = (BF16, BF16, BF16, F32, BF16, BF16, BF16)
    return pl.pallas_call(
        _proj_kernel,
        out_shape=[jax.ShapeDtypeStruct((b, s, w), dt) for w, dt in zip(widths, dtypes)],
        grid=(b, s // tm),
        in_specs=[
            pl.BlockSpec((1, tm, d), lambda i, j: (i, j, 0)),
            pl.BlockSpec((1, d), lambda i, j: (0, 0)),
            pl.BlockSpec(w_in_bf.shape, lambda i, j: (0, 0)),
            pl.BlockSpec((tm, LANES), lambda i, j: (j, 0)),
            pl.BlockSpec((tm, LANES), lambda i, j: (j, 0)),
        ],
        out_specs=[pl.BlockSpec((1, tm, w), lambda i, j: (i, j, 0)) for w in widths],
        compiler_params=_cparams(("arbitrary", "arbitrary")),
        name="proj",
    )(x, norm_w, w_in_bf, cos_t, sin_t)


def _ret_kernel(q_ref, k_ref, v_ref, g_ref, kpre_ref, vpre_ref, dmat_ref, qwf_ref, qwb_ref,
                kwf_ref, kwb_ref, kwp_ref, gdec_ref, gnw_ref, o_ref, sf_ref, sb_ref, sball_ref):
    ph = pl.program_id(1)
    c = pl.program_id(2)
    nc = pl.num_programs(2)
    contract0 = (((0,), (0,)), ((), ()))
    contract1 = (((1,), (1,)), ((), ()))

    def head(ref, h):
        return ref[0, :, h * RET_D:(h + 1) * RET_D]

    def kv_update(kh, vh, kw):
        kw = (kh.astype(F32) * kw).astype(BF16)
        return lax.dot_general(kw, vh, contract0, preferred_element_type=F32)

    @pl.when(ph == 0)
    def _backward_states():
        @pl.when(c == 0)
        def _():
            sb_ref[...] = jnp.zeros_like(sb_ref)

        j = nc - 1 - c
        for h in range(RET_HEADS):
            sball_ref[j, h] = sb_ref[h].astype(BF16)
            sb_ref[h] = gdec_ref[RET_HEADS + h] * sb_ref[h] + kv_update(head(k_ref, h), head(v_ref, h), kwb_ref[h])

    @pl.when(ph == 1)
    def _outputs():
        @pl.when(c == 0)
        def _():
            for h in range(RET_HEADS):
                kp = kpre_ref[:, h * RET_D:(h + 1) * RET_D]
                vp = vpre_ref[:, h * RET_D:(h + 1) * RET_D]
                sf_ref[h] = kv_update(kp, vp, kwp_ref[h])

        for h in range(RET_HEADS):
            qh, kh, vh = head(q_ref, h), head(k_ref, h), head(v_ref, h)
            s = lax.dot_general(qh, kh, contract1, preferred_element_type=F32) * dmat_ref[h]
            o = jnp.dot(s.astype(BF16), vh, preferred_element_type=F32)
            o = o + jnp.dot(qh, sf_ref[h].astype(BF16), preferred_element_type=F32) * qwf_ref[h]
            o = o + jnp.dot(qh, sball_ref[c, h], preferred_element_type=F32) * qwb_ref[h]
            mu = jnp.mean(o, axis=-1, keepdims=True)
            oc = o - mu
            var = jnp.mean(oc * oc, axis=-1, keepdims=True)
            on = oc * lax.rsqrt(var + GN_EPS) * gnw_ref[:, h * RET_D:(h + 1) * RET_D]
            g = head(g_ref, h)
            o_ref[0, :, h * RET_D:(h + 1) * RET_D] = (g * jax.nn.sigmoid(g) * on).astype(BF16)
            sf_ref[h] = gdec_ref[h] * sf_ref[h] + kv_update(kh, vh, kwf_ref[h])


def _retention(qr, kr, vr, gr, kpre, vpre, dmat, qwf, qwb, kwf, kwb, kwp, gdec, gnw, chunk):
    b, s, w = qr.shape
    nc = s // chunk

    def seq_map(i, ph, c):
        return (i, jnp.where(ph == 0, nc - 1 - c, c), 0)

    def fwd_only(i, ph, c):
        return (i, c * ph, 0)

    const2 = lambda i, ph, c: (0, 0)
    const3 = lambda i, ph, c: (0, 0, 0)
    return pl.pallas_call(
        _ret_kernel,
        out_shape=jax.ShapeDtypeStruct((b, s, w), BF16),
        grid=(b, 2, nc),
        in_specs=[
            pl.BlockSpec((1, chunk, w), fwd_only),
            pl.BlockSpec((1, chunk, w), seq_map),
            pl.BlockSpec((1, chunk, w), seq_map),
            pl.BlockSpec((1, chunk, w), fwd_only),
            pl.BlockSpec(kpre.shape, const2),
            pl.BlockSpec(vpre.shape, const2),
            pl.BlockSpec(dmat.shape, const3),
            pl.BlockSpec(qwf.shape, const3),
            pl.BlockSpec(qwb.shape, const3),
            pl.BlockSpec(kwf.shape, const3),
            pl.BlockSpec(kwb.shape, const3),
            pl.BlockSpec(kwp.shape, const3),
            pl.BlockSpec(memory_space=pltpu.SMEM),
            pl.BlockSpec(gnw.shape, const2),
        ],
        out_specs=pl.BlockSpec((1, chunk, w), fwd_only),
        scratch_shapes=[
            pltpu.VMEM((RET_HEADS, RET_D, RET_D), F32),
            pltpu.VMEM((RET_HEADS, RET_D, RET_D), F32),
            pltpu.VMEM((nc, RET_HEADS, RET_D, RET_D), BF16),
        ],
        compiler_params=_cparams(("arbitrary", "arbitrary", "arbitrary")),
        name="retention",
    )(qr, kr, vr, gr, kpre, vpre, dmat, qwf, qwb, kwf, kwb, kwp, gdec, gnw)


def _attn_kernel(q_ref, kp_ref, kc_ref, kn_ref, vp_ref, vc_ref, vn_ref, km_ref, vm_ref,
                 bias_ref, sink_ref, o_ref):
    j = pl.program_id(1)
    nb = pl.num_programs(1)
    lane = lax.broadcasted_iota(jnp.int32, (1, 4 * ATT_BLOCK), 1)
    first = jnp.logical_and(lane < ATT_BLOCK, j == 0)
    last = jnp.logical_and(jnp.logical_and(lane >= 2 * ATT_BLOCK, lane < 3 * ATT_BLOCK), j == nb - 1)
    pen = jnp.where(jnp.logical_or(first, last), NEG, 0.0).astype(F32)
    contract1 = (((1,), (1,)), ((), ()))
    rows = lax.broadcasted_iota(jnp.int32, (ATT_GROUP * ATT_BLOCK, 1), 0)
    for g in range(ATT_KV):
        sl = slice(g * ATT_HD, (g + 1) * ATT_HD)
        kall = jnp.concatenate([kp_ref[0][:, sl], kc_ref[0][:, sl], kn_ref[0][:, sl], km_ref[:, sl]], axis=0)
        vall = jnp.concatenate([vp_ref[0][:, sl], vc_ref[0][:, sl], vn_ref[0][:, sl], vm_ref[:, sl]], axis=0)
        heads = range(g * ATT_GROUP, (g + 1) * ATT_GROUP)
        q = jnp.concatenate([q_ref[0, :, h * ATT_HD:(h + 1) * ATT_HD] for h in heads], axis=0)
        bias = bias_ref[0, g * ATT_GROUP:(g + 1) * ATT_GROUP].reshape(ATT_GROUP * ATT_BLOCK, 4 * ATT_BLOCK)
        snk = jnp.zeros((ATT_GROUP * ATT_BLOCK, 1), F32)
        for hh, h in enumerate(heads):
            snk = jnp.where(rows // ATT_BLOCK == hh, sink_ref[h], snk)
        s = lax.dot_general(q, kall, contract1, preferred_element_type=F32) + bias + pen
        m = jnp.maximum(jnp.max(s, axis=-1, keepdims=True), snk)
        p = jnp.exp(s - m)
        den = jnp.sum(p, axis=-1, keepdims=True) + jnp.exp(snk - m)
        o = jnp.dot(p.astype(BF16), vall, preferred_element_type=F32) / den
        for hh in range(0, ATT_GROUP, 2):
            lanes = slice((g * ATT_GROUP + hh) * ATT_HD, (g * ATT_GROUP + hh + 2) * ATT_HD)
            o_ref[0, :, lanes] = jnp.concatenate(
                [o[hh * ATT_BLOCK:(hh + 1) * ATT_BLOCK], o[(hh + 1) * ATT_BLOCK:(hh + 2) * ATT_BLOCK]],
                axis=-1).astype(BF16)


def _attention(qa, ka, va, kmeta, vmeta, bias, sink):
    b, s, w = qa.shape
    nb = s // ATT_BLOCK
    kvw = ka.shape[-1]
    prev = lambda i, j: (i, jnp.maximum(j - 1, 0), 0)
    cur = lambda i, j: (i, j, 0)
    nxt = lambda i, j: (i, jnp.minimum(j + 1, nb - 1), 0)
    const2 = lambda i, j: (0, 0)
    kv = lambda m: pl.BlockSpec((1, ATT_BLOCK, kvw), m)
    return pl.pallas_call(
        _attn_kernel,
        out_shape=jax.ShapeDtypeStruct((b, s, w), BF16),
        grid=(b, nb),
        in_specs=[
            pl.BlockSpec((1, ATT_BLOCK, w), cur),
            kv(prev), kv(cur), kv(nxt), kv(prev), kv(cur), kv(nxt),
            pl.BlockSpec(kmeta.shape, const2),
            pl.BlockSpec(vmeta.shape, const2),
            pl.BlockSpec((1,) + bias.shape[1:], lambda i, j: (jnp.minimum(j, 1), 0, 0, 0)),
            pl.BlockSpec(memory_space=pltpu.SMEM),
        ],
        out_specs=pl.BlockSpec((1, ATT_BLOCK, w), cur),
        compiler_params=_cparams(("arbitrary", "arbitrary")),
        name="attention",
    )(qa, ka, ka, ka, va, va, va, kmeta, vmeta, bias, sink)


def _store_token_rows(ref, v):
    for c in range(D_MODEL // LANES):
        ref[pl.ds(c, v.shape[0], stride=SUBLANES), :] = v[:, c * LANES:(c + 1) * LANES]


def _outproj_kernel(or_ref, oa_ref, x_ref, w_ref, nw_ref, h_ref, xn_ref, xnb_ref):
    half = or_ref.shape[-1]
    h = x_ref[0]
    h = h + jnp.dot(or_ref[0], w_ref[:half], preferred_element_type=F32)
    h = h + jnp.dot(oa_ref[0], w_ref[half:], preferred_element_type=F32)
    _store_token_rows(h_ref, h)
    ms = jnp.mean(h * h, axis=-1, keepdims=True)
    xn = h * lax.rsqrt(ms + RMS_EPS) * nw_ref[...]
    _store_token_rows(xn_ref, xn)
    xnb_ref[0] = xn.astype(BF16)


def _outproj(o_r, o_a, x, w_out_bf, norm_w, tm):
    b, s, d = x.shape
    half = o_r.shape[-1]
    row = lambda i, j: (i, j, 0)
    const2 = lambda i, j: (0, 0)
    token_rows = pl.BlockSpec((tm * SUBLANES, LANES), lambda i, j: (i * (s // tm) + j, 0))
    return pl.pallas_call(
        _outproj_kernel,
        out_shape=[jax.ShapeDtypeStruct((b * s * SUBLANES, LANES), F32)] * 2 + [jax.ShapeDtypeStruct((b, s, d), BF16)],
        grid=(b, s // tm),
        in_specs=[
            pl.BlockSpec((1, tm, half), row),
            pl.BlockSpec((1, tm, half), row),
            pl.BlockSpec((1, tm, d), row),
            pl.BlockSpec(w_out_bf.shape, const2),
            pl.BlockSpec((1, d), const2),
        ],
        out_specs=[token_rows, token_rows, pl.BlockSpec((1, tm, d), row)],
        compiler_params=_cparams(("arbitrary", "arbitrary")),
        name="outproj",
    )(o_r, o_a, x, w_out_bf, norm_w)


def _top16_rows(s, iota):
    nrows = float(s.shape[0])
    vals, ids = [], []
    for _ in range(PEER_TOPK):
        m = jnp.max(s, axis=0, keepdims=True)
        am = jnp.min(jnp.where(s == m, iota, nrows), axis=0, keepdims=True)
        vals.append(m)
        ids.append(am)
        s = jnp.where(iota == am, -jnp.inf, s)
    return jnp.concatenate(vals, axis=0), jnp.concatenate(ids, axis=0)


_PAIR_GROUPS = ((0, 0, 8), (0, 8, 8), (1, 0, 8), (2, 0, 5), (3, 0, 4), (4, 0, 3), (5, 0, 2), (6, 0, 2), (7, 0, 2))


def _route_kernel(x_ref, wq_ref, pk_ref, idx_ref, gate_ref, sc_ref):
    tm = x_ref.shape[1]
    q = jnp.dot(x_ref[0], wq_ref[...], preferred_element_type=F32).astype(BF16)
    contract1 = (((1,), (1,)), ((), ()))
    for hp in range(2 * PEER_HEADS):
        sc_ref[hp] = lax.dot_general(pk_ref[hp], q[:, hp * PEER_NKEYS:(hp + 1) * PEER_NKEYS], contract1,
                                     preferred_element_type=F32)
    iota = lax.broadcasted_iota(jnp.int32, (PEER_NKEYS, LANES), 0).astype(F32)
    sub = lax.broadcasted_iota(jnp.int32, (SUBLANES, LANES), 0)
    subf = sub.astype(F32)
    far = float(PEER_TOPK * PEER_TOPK)

    def head_body(h, carry):
        for lt in range(tm // LANES):
            lanes = slice(lt * LANES, (lt + 1) * LANES)
            s1, i1 = _top16_rows(sc_ref[2 * h, :, lanes], iota)
            s2, i2 = _top16_rows(sc_ref[2 * h + 1, :, lanes], iota)
            i1 = i1 * float(PEER_NKEYS)
            sc, ex, pos = [], [], []
            for i, j0, cnt in _PAIR_GROUPS:
                ok = sub < cnt
                sc.append(jnp.where(ok, s1[i:i + 1] + s2[j0:j0 + SUBLANES], -jnp.inf))
                ex.append(i1[i:i + 1] + i2[j0:j0 + SUBLANES])
                pos.append(jnp.where(ok, float(i * PEER_TOPK + j0) + subf, far))
            sc.append(s1[SUBLANES:] + s2[0:1])
            ex.append(i1[SUBLANES:] + i2[0:1])
            pos.append((float(SUBLANES) + subf) * float(PEER_TOPK))
            sc = jnp.concatenate(sc, axis=0)
            ex = jnp.concatenate(ex, axis=0)
            pos = jnp.concatenate(pos, axis=0)
            top, experts = [], []
            for _ in range(PEER_TOPK):
                m = jnp.max(sc, axis=0, keepdims=True)
                pm = jnp.min(jnp.where(sc == m, pos, far), axis=0, keepdims=True)
                hit = pos == pm
                experts.append(jnp.max(jnp.where(hit, ex, -1.0), axis=0, keepdims=True))
                top.append(m)
                sc = jnp.where(hit, -jnp.inf, sc)
            top = jnp.concatenate(top, axis=0)
            e = jnp.exp(top - top[0:1])
            rows = pl.ds(pl.multiple_of(h * PEER_TOPK, PEER_TOPK), PEER_TOPK)
            gate_ref[0, rows, lanes] = e / jnp.sum(e, axis=0, keepdims=True)
            idx_ref[0, rows, lanes] = jnp.concatenate(experts, axis=0).astype(jnp.int32)
        return carry

    lax.fori_loop(0, PEER_HEADS, head_body, 0)


def _route(xn, wq_bf, pk_bf, tm):
    b, s, d = xn.shape
    const2 = lambda i, j: (0, 0)
    slot = lambda i, j: (i, 0, j)
    return pl.pallas_call(
        _route_kernel,
        out_shape=[jax.ShapeDtypeStruct((b, PEER_SLOTS, s), jnp.int32),
                   jax.ShapeDtypeStruct((b, PEER_SLOTS, s), F32)],
        grid=(b, s // tm),
        in_specs=[
            pl.BlockSpec((1, tm, d), lambda i, j: (i, j, 0)),
            pl.BlockSpec(wq_bf.shape, const2),
            pl.BlockSpec(pk_bf.shape, lambda i, j: (0, 0, 0)),
        ],
        out_specs=[pl.BlockSpec((1, PEER_SLOTS, tm), slot)] * 2,
        scratch_shapes=[pltpu.VMEM((2 * PEER_HEADS, PEER_NKEYS, tm), F32)],
        compiler_params=_cparams(("arbitrary", "arbitrary")),
        name="route",
    )(xn, wq_bf, pk_bf)


def _pair_slots():
    out = []
    for g in range(PEER_SLOTS // SUBLANES):
        for c in range(4):
            top = SUBLANES * g + SLOT_ORDER[c]
            out.append((top, top + 4))
    return out


def _gather_words(tbl_ref, top_off, bot_off, upper):
    ra = tbl_ref[pl.ds(pl.multiple_of(top_off, ROWS_PER_EXPERT), SUBLANES), :]
    rb = tbl_ref[pl.ds(pl.multiple_of(bot_off, ROWS_PER_EXPERT), SUBLANES), :]
    return jnp.where(upper, ra, rb)


def _unpack_words(words):
    lo = lax.bitcast_convert_type(words << 16, F32)
    hi = lax.bitcast_convert_type(words & jnp.uint32(0xFFFF0000), F32)
    return lo, hi


def _gather_pair(tbl_ref, top_off, bot_off, upper):
    return _unpack_words(_gather_words(tbl_ref, top_off, bot_off, upper))


def _merge_sublanes(a, b, h, sub):
    keep = (sub & h) == 0
    return jnp.where(keep, a + pltpu.roll(a, SUBLANES - h, 0), b + pltpu.roll(b, h, 0))


def _slot_specs(tb):
    return [pl.BlockSpec((None, None, SLOT_GROUP * tb), lambda i, j=j: (j, 0, i), memory_space=pltpu.SMEM,
                         pipeline_mode=pl.Buffered(1)) for j in range(PEER_SLOTS // SLOT_GROUP)]


def _group_slots(a, tb):
    n = a.shape[-1]
    a = a.reshape(PEER_SLOTS // SLOT_GROUP, SLOT_GROUP, n // tb, tb).transpose(0, 2, 1, 3)
    return a.reshape(PEER_SLOTS // SLOT_GROUP, 1, n * SLOT_GROUP)


def _slot_reader(refs, t, tb):
    at = [t + g * tb for g in range(SLOT_GROUP)]
    return lambda k: refs[k // SLOT_GROUP][at[k % SLOT_GROUP]]


def _peer_u_kernel(*refs):
    ngroups = PEER_SLOTS // SLOT_GROUP
    offs = refs[:ngroups]
    x_ref, tbl_ref, o_ref, z_ref, sel_ref = refs[ngroups:]
    tb = x_ref.shape[0] // SUBLANES
    sub = lax.broadcasted_iota(jnp.int32, (SUBLANES, LANES), 0)
    upper = sub < ROWS_PER_EXPERT
    pairs = _pair_slots()
    lane_pair = lax.broadcasted_iota(jnp.int32, (PEER_SLOTS, LANES), 1) >> 1

    row = lax.broadcasted_iota(jnp.int32, (2 * LANES, LANES), 0)
    col = lax.broadcasted_iota(jnp.int32, (2 * LANES, LANES), 1)
    sel_ref[...] = jnp.where((col & 1) == (row >= LANES).astype(jnp.int32), 1.0, 0.0).astype(BF16)
    o_ref[...] = jnp.zeros_like(o_ref)

    group_rows = 4 * PEER_SLOTS
    z_ref[...] = jnp.zeros_like(z_ref)

    def lane_partials(gi, zbase):
        for j in range(SUBLANES):
            t = gi * SUBLANES + j
            x = x_ref[pl.ds(pl.multiple_of(t * SUBLANES, SUBLANES), SUBLANES), :]
            swapped = pltpu.roll(x, ROWS_PER_EXPERT, 0)
            xa = jnp.where(upper, x, swapped)
            xb = jnp.where(upper, swapped, x)
            xa_bits = lax.bitcast_convert_type(xa.astype(BF16).astype(F32), jnp.uint32)
            xb_bits = lax.bitcast_convert_type(xb.astype(BF16).astype(F32), jnp.uint32)
            xw = pltpu.bitcast((xb_bits & jnp.uint32(0xFFFF0000)) | (xa_bits >> 16), BF16)
            off = _slot_reader(offs, t, tb)
            zs = []
            for g in range(PEER_SLOTS // SUBLANES):
                ps = []
                for c in range(4):
                    top, bot = pairs[4 * g + c]
                    words = _gather_words(tbl_ref, off(top), off(bot), upper)
                    lo, hi = _unpack_words(pltpu.bitcast(pltpu.bitcast(words, BF16) * xw, jnp.uint32))
                    ps.append(lo + hi)
                r0 = _merge_sublanes(ps[0], ps[1], 2, sub)
                r1 = _merge_sublanes(ps[2], ps[3], 2, sub)
                zs.append(_merge_sublanes(r0, r1, 1, sub))
            rows = zbase + (j // 2) * PEER_SLOTS
            cols = slice((j % 2) * LANES, (j % 2 + 1) * LANES)
            for m in range(len(zs) // 2):
                tile = jnp.concatenate([zs[2 * m], zs[2 * m + 1]], axis=0).astype(BF16)
                z_ref[pl.ds(pl.multiple_of(rows + 2 * SUBLANES * m, 2 * SUBLANES), 2 * SUBLANES), cols] = tile

    def lane_sums(gi, zbase):
        sums = jnp.dot(z_ref[pl.ds(zbase, group_rows), :], sel_ref[...], preferred_element_type=F32)
        block = pl.ds(pl.multiple_of((gi // (LANES // SUBLANES)) * LANES, LANES), LANES)
        first_pair = (gi % (LANES // SUBLANES)) * 4
        acc = o_ref[:, block]
        for p in range(4):
            acc = jnp.where(lane_pair == first_pair + p, sums[p * PEER_SLOTS:(p + 1) * PEER_SLOTS], acc)
        o_ref[:, block] = acc

    def zbase_of(gi):
        return pl.multiple_of((gi % 2) * group_rows, group_rows)

    def step(gi, carry):
        prev = jnp.maximum(gi - 1, 0)
        lane_sums(prev, zbase_of(gi + 1))
        lane_partials(gi, zbase_of(gi))
        return carry

    ngroups_tokens = tb // SUBLANES
    lax.fori_loop(0, ngroups_tokens, step, 0)
    lane_sums(ngroups_tokens - 1, (ngroups_tokens - 1) % 2 * group_rows)


def _peer_u(offs_g, x_rows, tbl, tb):
    n = x_rows.shape[0] // SUBLANES
    assert tb % LANES == 0
    return pl.pallas_call(
        _peer_u_kernel,
        out_shape=jax.ShapeDtypeStruct((PEER_SLOTS, n), F32),
        grid=(n // tb,),
        in_specs=_slot_specs(tb) + [
            pl.BlockSpec((tb * SUBLANES, LANES), lambda i: (i, 0)),
            pl.BlockSpec(tbl.shape, lambda i: (0, 0), pipeline_mode=pl.Buffered(1)),
        ],
        out_specs=pl.BlockSpec((PEER_SLOTS, tb), lambda i: (0, i)),
        scratch_shapes=[
            pltpu.VMEM((2 * 4 * PEER_SLOTS, 2 * LANES), BF16),
            pltpu.VMEM((2 * LANES, LANES), BF16),
        ],
        compiler_params=_cparams(("arbitrary",)),
        name="peer_u",
    )(*([offs_g] * (PEER_SLOTS // SLOT_GROUP)), x_rows, tbl)


def _peer_w_kernel(gate_ref, act_ref, w_ref):
    a = act_ref[...]
    w_ref[...] = gate_ref[...] * (0.5 * a * (1.0 + lax.erf(a * (2.0 ** -0.5))))


def _peer_w(gate_t, act_t, tm):
    n = gate_t.shape[-1]
    spec = pl.BlockSpec((PEER_SLOTS, tm), lambda i: (0, i))
    return pl.pallas_call(
        _peer_w_kernel,
        out_shape=jax.ShapeDtypeStruct(gate_t.shape, F32),
        grid=(n // tm,),
        in_specs=[spec, spec],
        out_specs=spec,
        compiler_params=_cparams(("arbitrary",)),
        name="peer_w",
    )(gate_t, act_t)


def _peer_v_kernel(*refs):
    ngroups = PEER_SLOTS // SLOT_GROUP
    offs = refs[:ngroups]
    w_ref, h_ref, tbl_ref, nw_ref, o_ref, acc_ref, whi_ref, wlo_ref, repa_ref, repb_ref, ones_ref = refs[ngroups:]
    tb = h_ref.shape[0] // SUBLANES
    sub = lax.broadcasted_iota(jnp.int32, (SUBLANES, LANES), 0)
    upper = sub < ROWS_PER_EXPERT
    pairs = _pair_slots()
    lane = lax.broadcasted_iota(jnp.int32, (PEER_SLOTS, LANES), 1)

    w = w_ref[...]
    whi = w.astype(BF16).astype(F32)
    whi_ref[...] = whi
    wlo_ref[...] = w - whi
    row = lax.broadcasted_iota(jnp.int32, (2 * LANES, 2 * LANES), 0)
    col = lax.broadcasted_iota(jnp.int32, (2 * LANES, 2 * LANES), 1)
    ones_ref[...] = jnp.where((row >= LANES) == (col >= LANES), 1.0, 0.0).astype(BF16)

    def replicate(t0, rep_ref):
        for r in range(V_TOKENS_PER_ITER // 2):
            t = t0 + 2 * r
            block = pl.ds(pl.multiple_of((t // LANES) * LANES, LANES), LANES)
            la = t % LANES
            out = None
            for part_ref in (whi_ref, wlo_ref):
                part = part_ref[:, block]
                lhs = jnp.concatenate([jnp.where(lane == la, part, 0.0), jnp.where(lane == la + 1, part, 0.0)],
                                      axis=1).astype(BF16)
                prod = jnp.dot(lhs, ones_ref[...], preferred_element_type=F32)
                out = prod if out is None else out + prod
            rep_ref[r] = out

    def token(t, rep_ref, slot, half):
        off = _slot_reader(offs, t, tb)
        lanes = slice(half * LANES, (half + 1) * LANES)
        acc = [jnp.zeros((SUBLANES, LANES), F32) for _ in range(4)]
        for q, (top, bot) in enumerate(pairs):
            lo, hi = _gather_pair(tbl_ref, off(top), off(bot), upper)
            wtop = jnp.broadcast_to(rep_ref[slot, top:top + 1, lanes], (SUBLANES, LANES))
            wbot = jnp.broadcast_to(rep_ref[slot, bot:bot + 1, lanes], (SUBLANES, LANES))
            wv = jnp.where(upper, wtop, wbot)
            k = 2 * (q % 2)
            acc[k] = acc[k] + lo * wv
            acc[k + 1] = acc[k + 1] + hi * wv
        lo = acc[0] + acc[2]
        hi = acc[1] + acc[3]
        lo = lo + pltpu.roll(lo, ROWS_PER_EXPERT, 0)
        hi = hi + pltpu.roll(hi, ROWS_PER_EXPERT, 0)
        rows = pl.ds(pl.multiple_of(t * SUBLANES, SUBLANES), SUBLANES)
        acc_ref[rows, :] = h_ref[rows, :] + jnp.where(upper, lo, hi)

    def tokens(t0, rep_ref):
        for r in range(V_TOKENS_PER_ITER):
            token(t0 + r, rep_ref, r // 2, r % 2)

    replicate(0, repa_ref)

    def step(i, carry):
        t0 = i * 2 * V_TOKENS_PER_ITER
        replicate(t0 + V_TOKENS_PER_ITER, repb_ref)
        tokens(t0, repa_ref)
        replicate(jnp.minimum(t0 + 2 * V_TOKENS_PER_ITER, tb - V_TOKENS_PER_ITER), repa_ref)
        tokens(t0 + V_TOKENS_PER_ITER, repb_ref)
        return carry

    lax.fori_loop(0, tb // (2 * V_TOKENS_PER_ITER), step, 0)

    def norm_step(i, carry):
        rows = pl.ds(pl.multiple_of(i * NORM_TOKENS * SUBLANES, NORM_TOKENS * SUBLANES), NORM_TOKENS * SUBLANES)
        h = acc_ref[rows, :].reshape(NORM_TOKENS, SUBLANES, LANES)
        ss = jnp.sum(jnp.sum(h * h, axis=2, keepdims=True), axis=1, keepdims=True)
        y = h * lax.rsqrt(ss * (1.0 / D_MODEL) + RMS_EPS) * nw_ref[...][None]
        acc_ref[rows, :] = y.reshape(NORM_TOKENS * SUBLANES, LANES)
        return carry

    lax.fori_loop(0, tb // NORM_TOKENS, norm_step, 0)
    for c in range(D_MODEL // LANES):
        o_ref[:, c * LANES:(c + 1) * LANES] = acc_ref[pl.ds(c, tb, stride=SUBLANES), :]


def _peer_v(offs_g, w_t, h_rows, tbl, norm_rows, tb):
    n = h_rows.shape[0] // SUBLANES
    assert tb % LANES == 0 and V_TOKENS_PER_ITER % 2 == 0 and tb % (2 * V_TOKENS_PER_ITER) == 0 and tb % NORM_TOKENS == 0
    rows = pl.BlockSpec((tb * SUBLANES, LANES), lambda i: (i, 0))
    return pl.pallas_call(
        _peer_v_kernel,
        out_shape=jax.ShapeDtypeStruct((n, D_MODEL), F32),
        grid=(n // tb,),
        in_specs=_slot_specs(tb) + [
            pl.BlockSpec((PEER_SLOTS, tb), lambda i: (0, i)),
            rows,
            pl.BlockSpec(tbl.shape, lambda i: (0, 0), pipeline_mode=pl.Buffered(1)),
            pl.BlockSpec((SUBLANES, LANES), lambda i: (0, 0)),
        ],
        out_specs=pl.BlockSpec((tb, D_MODEL), lambda i: (i, 0)),
        scratch_shapes=[
            pltpu.VMEM((tb * SUBLANES, LANES), F32),
            pltpu.VMEM((PEER_SLOTS, tb), F32),
            pltpu.VMEM((PEER_SLOTS, tb), F32),
            pltpu.VMEM((V_TOKENS_PER_ITER // 2, PEER_SLOTS, 2 * LANES), F32),
            pltpu.VMEM((V_TOKENS_PER_ITER // 2, PEER_SLOTS, 2 * LANES), F32),
            pltpu.VMEM((2 * LANES, 2 * LANES), BF16),
        ],
        compiler_params=_cparams(("arbitrary",)),
        name="peer_v",
    )(*([offs_g] * (PEER_SLOTS // SLOT_GROUP)), w_t, h_rows, tbl, norm_rows)


def _pack_table(t):
    half = D_MODEL // 2
    bits = lax.bitcast_convert_type(t.astype(BF16), jnp.uint16).astype(jnp.uint32)
    words = bits[:, :half] | (bits[:, half:] << 16)
    words = words.reshape(t.shape[0] * ROWS_PER_EXPERT, LANES)
    return jnp.pad(words, ((TABLE_PAD, TABLE_PAD), (0, 0)))


def _t5_bucket(rel):
    half = N_BUCKETS // 2
    exact = half // 2
    n = jnp.abs(rel)
    large = exact + (jnp.log(jnp.maximum(n, 1).astype(F32) / exact)
                     / math.log(MAX_DISTANCE / exact) * (half - exact)).astype(jnp.int32)
    large = jnp.minimum(large, half - 1)
    return jnp.where(rel > 0, half, 0) + jnp.where(n < exact, n, large)


def _attention_bias(rel_bias):
    rb = rel_bias.astype(F32)
    i = jnp.arange(ATT_BLOCK)
    jb = jnp.arange(3 * ATT_BLOCK)
    rel = (jb[None, :] - ATT_BLOCK) - i[:, None]
    band = jnp.where((jnp.abs(rel) <= ATT_BLOCK)[..., None], rb[_t5_bucket(rel)], NEG)
    mpos = PREFIX - N_META + jnp.arange(N_META)
    variants = []
    for blk in (0, 1):
        qpos = PREFIX + blk * ATT_BLOCK + i
        meta = rb[_t5_bucket(mpos[None, :] - qpos[:, None])]
        fill = jnp.full((ATT_BLOCK, ATT_BLOCK - N_META, ATT_HEADS), NEG, F32)
        variants.append(jnp.concatenate([band, meta, fill], axis=1).transpose(2, 0, 1))
    return jnp.stack(variants)


def _rope_tables(pos):
    half = RET_D // 2
    inv = ROPE_BASE ** (-jnp.arange(half, dtype=F32) / half)
    ang = pos.astype(F32)[:, None] * inv[None, :]
    cos, sin = jnp.cos(ang), jnp.sin(ang)
    return jnp.concatenate([cos, cos], axis=1), jnp.concatenate([-sin, sin], axis=1)


def _decay_tables(dec_f, dec_b, chunk):
    lf = jax.nn.log_sigmoid(dec_f.astype(F32))[:, None]
    lb = jax.nn.log_sigmoid(dec_b.astype(F32))[:, None]
    idx = jnp.arange(chunk, dtype=F32)
    diff = idx[:, None] - idx[None, :]
    dmat = (jnp.where(diff >= 0, jnp.exp(jnp.maximum(diff, 0.0)[None] * lf[:, :, None]), 0.0)
            + jnp.where(diff < 0, jnp.exp(jnp.maximum(-diff, 0.0)[None] * lb[:, :, None]), 0.0))
    bc = lambda v: jnp.broadcast_to(v[:, :, None], v.shape + (RET_D,))
    qwf = bc(jnp.exp((idx + 1.0)[None] * lf))
    kwf = bc(jnp.exp((chunk - 1.0 - idx)[None] * lf))
    qwb = bc(jnp.exp((chunk - idx)[None] * lb))
    kwb = bc(jnp.exp(idx[None] * lb))
    pidx = jnp.arange(PREFIX, dtype=F32)
    kwp = bc(jnp.exp((PREFIX - 1.0 - pidx)[None] * lf))
    gdec = jnp.concatenate([jnp.exp(chunk * lf[:, 0]), jnp.exp(chunk * lb[:, 0])])
    return dmat, qwf, qwb, kwf, kwb, kwp, gdec


def _divisor_tile(n, want):
    t = min(n, want)
    while n % t:
        t //= 2
    return t


def _encode(x, shared):
    b, s, d = x.shape
    n = b * s
    tm = _divisor_tile(s, 512)
    chunk = _divisor_tile(s, 256)
    cos_t, sin_t = _rope_tables(jnp.arange(s) + N_META)
    qr, kr, vr, gr, qa, ka, va = _proj(x, shared["norm_mix"], shared["w_in"], cos_t, sin_t, tm)
    o_r = _retention(qr, kr, vr, gr, shared["kr_pre"], shared["vr_pre"], *_decay_tables(
        shared["dec_f"], shared["dec_b"], chunk), shared["gn_w"], chunk)
    o_a = _attention(qa, ka, va, shared["k_meta"], shared["v_meta"], shared["bias"], shared["sink"])
    h_rows, xn_rows, xn_bf = _outproj(o_r, o_a, x, shared["w_out"], shared["norm_ffn"], tm)
    idx_t, gate_t = _route(xn_bf, shared["wq"], shared["pkeys"], _divisor_tile(s, ROUTE_TOKENS))
    idx_t = idx_t.transpose(1, 0, 2).reshape(PEER_SLOTS, n)
    gate_t = gate_t.transpose(1, 0, 2).reshape(PEER_SLOTS, n)
    top_half = (jnp.arange(PEER_SLOTS) % SUBLANES) < ROWS_PER_EXPERT
    tb = _divisor_tile(n, PEER_TOKENS)
    offs_g = _group_slots(idx_t * ROWS_PER_EXPERT + jnp.where(top_half, TABLE_PAD, 0)[:, None], tb)
    act_t = _peer_u(offs_g, xn_rows, shared["u_tbl"], tb)
    w_t = _peer_w(gate_t, act_t, _divisor_tile(n, 2048))
    y = _peer_v(offs_g, w_t, h_rows, shared["v_tbl"], shared["norm_final"], tb)
    return y.reshape(b, s, d)


def kernel(x_prompt, x_sample, meta_tokens, norm_mix_w, w_in, ret_decay_fwd, ret_decay_bwd, ret_gn_w,
           attn_sink, rel_bias, w_out, norm_ffn_w, peer_wq, peer_keys, peer_u, peer_v, norm_final_w):
    layer = 0
    shared = {
        "norm_mix": norm_mix_w[layer][None, :].astype(F32),
        "w_in": w_in[layer].astype(BF16),
        "dec_f": ret_decay_fwd[layer],
        "dec_b": ret_decay_bwd[layer],
        "gn_w": ret_gn_w[layer][None, :].astype(F32),
        "sink": attn_sink[layer].astype(F32),
        "bias": _attention_bias(rel_bias),
        "w_out": w_out[layer].astype(BF16),
        "norm_ffn": norm_ffn_w[layer][None, :].astype(F32),
        "wq": peer_wq[layer].astype(BF16),
        "pkeys": peer_keys[layer].reshape(2 * PEER_HEADS, PEER_NKEYS, PEER_NKEYS).astype(BF16),
        "u_tbl": _pack_table(peer_u[layer]),
        "v_tbl": _pack_table(peer_v[layer]),
        "norm_final": norm_final_w.reshape(SUBLANES, LANES).astype(F32),
    }
    prefix = jnp.concatenate([jnp.zeros((PREFIX - N_META, D_MODEL), x_prompt.dtype),
                              meta_tokens.astype(x_prompt.dtype)], axis=0)[None]
    cos_p, sin_p = _rope_tables(jnp.arange(PREFIX) - (PREFIX - N_META))
    _, kr_p, vr_p, _, _, ka_p, va_p = _proj(prefix, shared["norm_mix"], shared["w_in"], cos_p, sin_p, PREFIX)
    shared["kr_pre"], shared["vr_pre"] = kr_p[0], vr_p[0]
    pad_meta = lambda t: jnp.pad(t[0, PREFIX - N_META:], ((0, ATT_BLOCK - N_META), (0, 0)))
    shared["k_meta"], shared["v_meta"] = pad_meta(ka_p), pad_meta(va_p)
    return (_encode(x_prompt, shared), _encode(x_sample, shared))
```

```python
import functools
import math

import jax
import jax.numpy as jnp
from jax import lax
from jax.experimental import pallas as pl
from jax.experimental.pallas import tpu as pltpu

F32 = jnp.float32
BF16 = jnp.bfloat16

D_MODEL = 1024
N_META = 16
PREFIX = 128
RET_HEADS = 4
RET_D = 128
ATT_HEADS = 8
ATT_KV = 2
ATT_GROUP = ATT_HEADS // ATT_KV
ATT_HD = 64
ATT_BLOCK = 128
N_BUCKETS = 32
MAX_DISTANCE = 128
ROPE_BASE = 10000.0
PEER_HEADS = 8
PEER_NKEYS = 128
PEER_EXPERTS = PEER_NKEYS * PEER_NKEYS
PEER_TOPK = 16
PEER_SLOTS = PEER_HEADS * PEER_TOPK
RMS_EPS = 1e-6
GN_EPS = 1e-5
NEG = -1e30
IN_COLS = (0, 512, 1024, 1536, 2048, 2560, 2688, 2816)

LANES = 128
SUBLANES = 8
VMEM_LIMIT = 56 * 1024 * 1024
ROWS_PER_EXPERT = D_MODEL // 2 // LANES
TABLE_PAD = ROWS_PER_EXPERT
PEER_TOKENS = 512
ROUTE_TOKENS = 256
SLOT_ORDER = (0, 2, 1, 3)
SLOT_GROUP = 1
V_TOKENS_PER_ITER = 8
NORM_TOKENS = 32


def _cparams(sem):
    return pltpu.CompilerParams(dimension_semantics=sem, vmem_limit_bytes=VMEM_LIMIT)


def _proj_kernel(x_ref, nw_ref, w_ref, cos_ref, sin_ref,
                 qr_ref, kr_ref, vr_ref, gr_ref, qa_ref, ka_ref, va_ref):
    x = x_ref[0]
    ms = jnp.mean(x * x, axis=-1, keepdims=True)
    xn = (x * lax.rsqrt(ms + RMS_EPS) * nw_ref[...]).astype(BF16)
    cosf = cos_ref[...]
    sinf = sin_ref[...]

    def mm(i):
        return jnp.dot(xn, w_ref[:, IN_COLS[i]:IN_COLS[i + 1]], preferred_element_type=F32)

    def rotary(t, scale):
        for h in range(RET_HEADS):
            th = t[:, h * RET_D:(h + 1) * RET_D]
            yield h, (th * cosf + pltpu.roll(th, RET_D // 2, 1) * sinf) * scale

    for h, r in rotary(mm(0), 1.0):
        qr_ref[0, :, h * RET_D:(h + 1) * RET_D] = r.astype(BF16)
    for h, r in rotary(mm(1), RET_D ** -0.5):
        kr_ref[0, :, h * RET_D:(h + 1) * RET_D] = r.astype(BF16)
    vr_ref[0] = mm(2).astype(BF16)
    gr_ref[0] = mm(3)
    qa_ref[0] = (mm(4) * (ATT_HD ** -0.5)).astype(BF16)
    ka_ref[0] = mm(5).astype(BF16)
    va_ref[0] = mm(6).astype(BF16)


def _proj(x, norm_w, w_in_bf, cos_t, sin_t, tm):
    b, s, d = x.shape
    widths = (512, 512, 512, 512, 512, 128, 128)
    dtypes = (BF16, BF16, BF16, F32, BF16, BF16, BF16)
    return pl.pallas_call(
        _proj_kernel,
        out_shape=[jax.ShapeDtypeStruct((b, s, w), dt) for w, dt in zip(widths, dtypes)],
        grid=(b, s // tm),
        in_specs=[
            pl.BlockSpec((1, tm, d), lambda i, j: (i, j, 0)),
            pl.BlockSpec((1, d), lambda i, j: (0, 0)),
            pl.BlockSpec(w_in_bf.shape, lambda i, j: (0, 0)),
            pl.BlockSpec((tm, LANES), lambda i, j: (j, 0)),
            pl.BlockSpec((tm, LANES), lambda i, j: (j, 0)),
        ],
        out_specs=[pl.BlockSpec((1, tm, w), lambda i, j: (i, j, 0)) for w in widths],
        compiler_params=_cparams(("arbitrary", "arbitrary")),
        name="proj",
    )(x, norm_w, w_in_bf, cos_t, sin_t)


def _ret_kernel(q_ref, k_ref, v_ref, g_ref, kpre_ref, vpre_ref, dmat_ref, qwf_ref, qwb_ref,
                kwf_ref, kwb_ref, kwp_ref, gdec_ref, gnw_ref, o_ref, sf_ref, sb_ref, sball_ref):
    ph = pl.program_id(1)
    c = pl.program_id(2)
    nc = pl.num_programs(2)
    contract0 = (((0,), (0,)), ((), ()))
    contract1 = (((1,), (1,)), ((), ()))

    def head(ref, h):
        return ref[0, :, h * RET_D:(h + 1) * RET_D]

    def kv_update(kh, vh, kw):
        kw = (kh.astype(F32) * kw).astype(BF16)
        return lax.dot_general(kw, vh, contract0, preferred_element_type=F32)

    @pl.when(ph == 0)
    def _backward_states():
        @pl.when(c == 0)
        def _():
            sb_ref[...] = jnp.zeros_like(sb_ref)

        j = nc - 1 - c
        for h in range(RET_HEADS):
            sball_ref[j, h] = sb_ref[h].astype(BF16)
            sb_ref[h] = gdec_ref[RET_HEADS + h] * sb_ref[h] + kv_update(head(k_ref, h), head(v_ref, h), kwb_ref[h])

    @pl.when(ph == 1)
    def _outputs():
        @pl.when(c == 0)
        def _():
            for h in range(RET_HEADS):
                kp = kpre_ref[:, h * RET_D:(h + 1) * RET_D]
                vp = vpre_ref[:, h * RET_D:(h + 1) * RET_D]
                sf_ref[h] = kv_update(kp, vp, kwp_ref[h])

        for h in range(RET_HEADS):
            qh, kh, vh = head(q_ref, h), head(k_ref, h), head(v_ref, h)
            s = lax.dot_general(qh, kh, contract1, preferred_element_type=F32) * dmat_ref[h]
            o = jnp.dot(s.astype(BF16), vh, preferred_element_type=F32)
            o = o + jnp.dot(qh, sf_ref[h].astype(BF16), preferred_element_type=F32) * qwf_ref[h]
            o = o + jnp.dot(qh, sball_ref[c, h], preferred_element_type=F32) * qwb_ref[h]
            mu = jnp.mean(o, axis=-1, keepdims=True)
            oc = o - mu
            var = jnp.mean(oc * oc, axis=-1, keepdims=True)
            on = oc * lax.rsqrt(var + GN_EPS) * gnw_ref[:, h * RET_D:(h + 1) * RET_D]
            g = head(g_ref, h)
            o_ref[0, :, h * RET_D:(h + 1) * RET_D] = (g * jax.nn.sigmoid(g) * on).astype(BF16)
            sf_ref[h] = gdec_ref[h] * sf_ref[h] + kv_update(kh, vh, kwf_ref[h])


def _retention(qr, kr, vr, gr, kpre, vpre, dmat, qwf, qwb, kwf, kwb, kwp, gdec, gnw, chunk):
    b, s, w = qr.shape
    nc = s // chunk

    def seq_map(i, ph, c):
        return (i, jnp.where(ph == 0, nc - 1 - c, c), 0)

    def fwd_only(i, ph, c):
        return (i, c * ph, 0)

    const2 = lambda i, ph, c: (0, 0)
    const3 = lambda i, ph, c: (0, 0, 0)
    return pl.pallas_call(
        _ret_kernel,
        out_shape=jax.ShapeDtypeStruct((b, s, w), BF16),
        grid=(b, 2, nc),
        in_specs=[
            pl.BlockSpec((1, chunk, w), fwd_only),
            pl.BlockSpec((1, chunk, w), seq_map),
            pl.BlockSpec((1, chunk, w), seq_map),
            pl.BlockSpec((1, chunk, w), fwd_only),
            pl.BlockSpec(kpre.shape, const2),
            pl.BlockSpec(vpre.shape, const2),
            pl.BlockSpec(dmat.shape, const3),
            pl.BlockSpec(qwf.shape, const3),
            pl.BlockSpec(qwb.shape, const3),
            pl.BlockSpec(kwf.shape, const3),
            pl.BlockSpec(kwb.shape, const3),
            pl.BlockSpec(kwp.shape, const3),
            pl.BlockSpec(memory_space=pltpu.SMEM),
            pl.BlockSpec(gnw.shape, const2),
        ],
        out_specs=pl.BlockSpec((1, chunk, w), fwd_only),
        scratch_shapes=[
            pltpu.VMEM((RET_HEADS, RET_D, RET_D), F32),
            pltpu.VMEM((RET_HEADS, RET_D, RET_D), F32),
            pltpu.VMEM((nc, RET_HEADS, RET_D, RET_D), BF16),
        ],
        compiler_params=_cparams(("arbitrary", "arbitrary", "arbitrary")),
        name="retention",
    )(qr, kr, vr, gr, kpre, vpre, dmat, qwf, qwb, kwf, kwb, kwp, gdec, gnw)


def _attn_kernel(q_ref, kp_ref, kc_ref, kn_ref, vp_ref, vc_ref, vn_ref, km_ref, vm_ref,
                 bias_ref, sink_ref, o_ref):
    j = pl.program_id(1)
    nb = pl.num_programs(1)
    lane = lax.broadcasted_iota(jnp.int32, (1, 4 * ATT_BLOCK), 1)
    first = jnp.logical_and(lane < ATT_BLOCK, j == 0)
    last = jnp.logical_and(jnp.logical_and(lane >= 2 * ATT_BLOCK, lane < 3 * ATT_BLOCK), j == nb - 1)
    pen = jnp.where(jnp.logical_or(first, last), NEG, 0.0).astype(F32)
    contract1 = (((1,), (1,)), ((), ()))
    rows = lax.broadcasted_iota(jnp.int32, (ATT_GROUP * ATT_BLOCK, 1), 0)
    for g in range(ATT_KV):
        sl = slice(g * ATT_HD, (g + 1) * ATT_HD)
        kall = jnp.concatenate([kp_ref[0][:, sl], kc_ref[0][:, sl], kn_ref[0][:, sl], km_ref[:, sl]], axis=0)
        vall = jnp.concatenate([vp_ref[0][:, sl], vc_ref[0][:, sl], vn_ref[0][:, sl], vm_ref[:, sl]], axis=0)
        heads = range(g * ATT_GROUP, (g + 1) * ATT_GROUP)
        q = jnp.concatenate([q_ref[0, :, h * ATT_HD:(h + 1) * ATT_HD] for h in heads], axis=0)
        bias = bias_ref[0, g * ATT_GROUP:(g + 1) * ATT_GROUP].reshape(ATT_GROUP * ATT_BLOCK, 4 * ATT_BLOCK)
        snk = jnp.zeros((ATT_GROUP * ATT_BLOCK, 1), F32)
        for hh, h in enumerate(heads):
            snk = jnp.where(rows // ATT_BLOCK == hh, sink_ref[h], snk)
        s = lax.dot_general(q, kall, contract1, preferred_element_type=F32) + bias + pen
        m = jnp.maximum(jnp.max(s, axis=-1, keepdims=True), snk)
        p = jnp.exp(s - m)
        den = jnp.sum(p, axis=-1, keepdims=True) + jnp.exp(snk - m)
        o = jnp.dot(p.astype(BF16), vall, preferred_element_type=F32) / den
        for hh in range(0, ATT_GROUP, 2):
            lanes = slice((g * ATT_GROUP + hh) * ATT_HD, (g * ATT_GROUP + hh + 2) * ATT_HD)
            o_ref[0, :, lanes] = jnp.concatenate(
                [o[hh * ATT_BLOCK:(hh + 1) * ATT_BLOCK], o[(hh + 1) * ATT_BLOCK:(hh + 2) * ATT_BLOCK]],
                axis=-1).astype(BF16)


def _attention(qa, ka, va, kmeta, vmeta, bias, sink):
    b, s, w = qa.shape
    nb = s // ATT_BLOCK
    kvw = ka.shape[-1]
    prev = lambda i, j: (i, jnp.maximum(j - 1, 0), 0)
    cur = lambda i, j: (i, j, 0)
    nxt = lambda i, j: (i, jnp.minimum(j + 1, nb - 1), 0)
    const2 = lambda i, j: (0, 0)
    kv = lambda m: pl.BlockSpec((1, ATT_BLOCK, kvw), m)
    return pl.pallas_call(
        _attn_kernel,
        out_shape=jax.ShapeDtypeStruct((b, s, w), BF16),
        grid=(b, nb),
        in_specs=[
            pl.BlockSpec((1, ATT_BLOCK, w), cur),
            kv(prev), kv(cur), kv(nxt), kv(prev), kv(cur), kv(nxt),
            pl.BlockSpec(kmeta.shape, const2),
            pl.BlockSpec(vmeta.shape, const2),
            pl.BlockSpec((1,) + bias.shape[1:], lambda i, j: (jnp.minimum(j, 1), 0, 0, 0)),
            pl.BlockSpec(memory_space=pltpu.SMEM),
        ],
        out_specs=pl.BlockSpec((1, ATT_BLOCK, w), cur),
        compiler_params=_cparams(("arbitrary", "arbitrary")),
        name="attention",
    )(qa, ka, ka, ka, va, va, va, kmeta, vmeta, bias, sink)


def _store_token_rows(ref, v):
    for c in range(D_MODEL // LANES):
        ref[pl.ds(c, v.shape[0], stride=SUBLANES), :] = v[:, c * LANES:(c + 1) * LANES]


def _outproj_kernel(or_ref, oa_ref, x_ref, w_ref, nw_ref, h_ref, xn_ref, xnb_ref):
    half = or_ref.shape[-1]
    h = x_ref[0]
    h = h + jnp.dot(or_ref[0], w_ref[:half], preferred_element_type=F32)
    h = h + jnp.dot(oa_ref[0], w_ref[half:], preferred_element_type=F32)
    _store_token_rows(h_ref, h)
    ms = jnp.mean(h * h, axis=-1, keepdims=True)
    xn = h * lax.rsqrt(ms + RMS_EPS) * nw_ref[...]
    _store_token_rows(xn_ref, xn)
    xnb_ref[0] = xn.astype(BF16)


def _outproj(o_r, o_a, x, w_out_bf, norm_w, tm):
    b, s, d = x.shape
    half = o_r.shape[-1]
    row = lambda i, j: (i, j, 0)
    const2 = lambda i, j: (0, 0)
    token_rows = pl.BlockSpec((tm * SUBLANES, LANES), lambda i, j: (i * (s // tm) + j, 0))
    return pl.pallas_call(
        _outproj_kernel,
        out_shape=[jax.ShapeDtypeStruct((b * s * SUBLANES, LANES), F32)] * 2 + [jax.ShapeDtypeStruct((b, s, d), BF16)],
        grid=(b, s // tm),
        in_specs=[
            pl.BlockSpec((1, tm, half), row),
            pl.BlockSpec((1, tm, half), row),
            pl.BlockSpec((1, tm, d), row),
            pl.BlockSpec(w_out_bf.shape, const2),
            pl.BlockSpec((1, d), const2),
        ],
        out_specs=[token_rows, token_rows, pl.BlockSpec((1, tm, d), row)],
        compiler_params=_cparams(("arbitrary", "arbitrary")),
        name="outproj",
    )(o_r, o_a, x, w_out_bf, norm_w)


def _top16_rows(s, iota):
    nrows = float(s.shape[0])
    vals, ids = [], []
    for _ in range(PEER_TOPK):
        m = jnp.max(s, axis=0, keepdims=True)
        am = jnp.min(jnp.where(s == m, iota, nrows), axis=0, keepdims=True)
        vals.append(m)
        ids.append(am)
        s = jnp.where(iota == am, -jnp.inf, s)
    return jnp.concatenate(vals, axis=0), jnp.concatenate(ids, axis=0)


_PAIR_GROUPS = ((0, 0, 8), (0, 8, 8), (1, 0, 8), (2, 0, 5), (3, 0, 4), (4, 0, 3), (5, 0, 2), (6, 0, 2), (7, 0, 2))


def _route_kernel(x_ref, wq_ref, pk_ref, idx_ref, gate_ref, sc_ref):
    tm = x_ref.shape[1]
    q = jnp.dot(x_ref[0], wq_ref[...], preferred_element_type=F32).astype(BF16)
    contract1 = (((1,), (1,)), ((), ()))
    for hp in range(2 * PEER_HEADS):
        sc_ref[hp] = lax.dot_general(pk_ref[hp], q[:, hp * PEER_NKEYS:(hp + 1) * PEER_NKEYS], contract1,
                                     preferred_element_type=F32)
    iota = lax.broadcasted_iota(jnp.int32, (PEER_NKEYS, LANES), 0).astype(F32)
    sub = lax.broadcasted_iota(jnp.int32, (SUBLANES, LANES), 0)
    subf = sub.astype(F32)
    far = float(PEER_TOPK * PEER_TOPK)

    def head_body(h, carry):
        for lt in range(tm // LANES):
            lanes = slice(lt * LANES, (lt + 1) * LANES)
            s1, i1 = _top16_rows(sc_ref[2 * h, :, lanes], iota)
            s2, i2 = _top16_rows(sc_ref[2 * h + 1, :, lanes], iota)
            i1 = i1 * float(PEER_NKEYS)
            sc, ex, pos = [], [], []
            for i, j0, cnt in _PAIR_GROUPS:
                ok = sub < cnt
                sc.append(jnp.where(ok, s1[i:i + 1] + s2[j0:j0 + SUBLANES], -jnp.inf))
                ex.append(i1[i:i + 1] + i2[j0:j0 + SUBLANES])
                pos.append(jnp.where(ok, float(i * PEER_TOPK + j0) + subf, far))
            sc.append(s1[SUBLANES:] + s2[0:1])
            ex.append(i1[SUBLANES:] + i2[0:1])
            pos.append((float(SUBLANES) + subf) * float(PEER_TOPK))
            sc = jnp.concatenate(sc, axis=0)
            ex = jnp.concatenate(ex, axis=0)
            pos = jnp.concatenate(pos, axis=0)
            top, experts = [], []
            for _ in range(PEER_TOPK):
                m = jnp.max(sc, axis=0, keepdims=True)
                pm = jnp.min(jnp.where(sc == m, pos, far), axis=0, keepdims=True)
                hit = pos == pm
                experts.append(jnp.max(jnp.where(hit, ex, -1.0), axis=0, keepdims=True))
                top.append(m)
                sc = jnp.where(hit, -jnp.inf, sc)
            top = jnp.concatenate(top, axis=0)
            e = jnp.exp(top - top[0:1])
            rows = pl.ds(pl.multiple_of(h * PEER_TOPK, PEER_TOPK), PEER_TOPK)
            gate_ref[0, rows, lanes] = e / jnp.sum(e, axis=0, keepdims=True)
            idx_ref[0, rows, lanes] = jnp.concatenate(experts, axis=0).astype(jnp.int32)
        return carry

    lax.fori_loop(0, PEER_HEADS, head_body, 0)


def _route(xn, wq_bf, pk_bf, tm):
    b, s, d = xn.shape
    const2 = lambda i, j: (0, 0)
    slot = lambda i, j: (i, 0, j)
    return pl.pallas_call(
        _route_kernel,
        out_shape=[jax.ShapeDtypeStruct((b, PEER_SLOTS, s), jnp.int32),
                   jax.ShapeDtypeStruct((b, PEER_SLOTS, s), F32)],
        grid=(b, s // tm),
        in_specs=[
            pl.BlockSpec((1, tm, d), lambda i, j: (i, j, 0)),
            pl.BlockSpec(wq_bf.shape, const2),
            pl.BlockSpec(pk_bf.shape, lambda i, j: (0, 0, 0)),
        ],
        out_specs=[pl.BlockSpec((1, PEER_SLOTS, tm), slot)] * 2,
        scratch_shapes=[pltpu.VMEM((2 * PEER_HEADS, PEER_NKEYS, tm), F32)],
        compiler_params=_cparams(("arbitrary", "arbitrary")),
        name="route",
    )(xn, wq_bf, pk_bf)


def _pair_slots():
    out = []
    for g in range(PEER_SLOTS // SUBLANES):
        for c in range(4):
            top = SUBLANES * g + SLOT_ORDER[c]
            out.append((top, top + 4))
    return out


def _gather_words(tbl_ref, top_off, bot_off, upper):
    ra = tbl_ref[pl.ds(pl.multiple_of(top_off, ROWS_PER_EXPERT), SUBLANES), :]
    rb = tbl_ref[pl.ds(pl.multiple_of(bot_off, ROWS_PER_EXPERT), SUBLANES), :]
    return jnp.where(upper, ra, rb)


def _unpack_words(words):
    lo = lax.bitcast_convert_type(words << 16, F32)
    hi = lax.bitcast_convert_type(words & jnp.uint32(0xFFFF0000), F32)
    return lo, hi


def _gather_pair(tbl_ref, top_off, bot_off, upper):
    return _unpack_words(_gather_words(tbl_ref, top_off, bot_off, upper))


def _merge_sublanes(a, b, h, sub):
    keep = (sub & h) == 0
    return jnp.where(keep, a + pltpu.roll(a, SUBLANES - h, 0), b + pltpu.roll(b, h, 0))


def _slot_specs(tb):
    return [pl.BlockSpec((None, None, SLOT_GROUP * tb), lambda i, j=j: (j, 0, i), memory_space=pltpu.SMEM,
                         pipeline_mode=pl.Buffered(1)) for j in range(PEER_SLOTS // SLOT_GROUP)]


def _group_slots(a, tb):
    n = a.shape[-1]
    a = a.reshape(PEER_SLOTS // SLOT_GROUP, SLOT_GROUP, n // tb, tb).transpose(0, 2, 1, 3)
    return a.reshape(PEER_SLOTS // SLOT_GROUP, 1, n * SLOT_GROUP)


def _slot_reader(refs, t, tb):
    at = [t + g * tb for g in range(SLOT_GROUP)]
    return lambda k: refs[k // SLOT_GROUP][at[k % SLOT_GROUP]]


def _peer_u_kernel(*refs):
    ngroups = PEER_SLOTS // SLOT_GROUP
    offs = refs[:ngroups]
    x_ref, tbl_ref, o_ref, z_ref, sel_ref = refs[ngroups:]
    tb = x_ref.shape[0] // SUBLANES
    sub = lax.broadcasted_iota(jnp.int32, (SUBLANES, LANES), 0)
    upper = sub < ROWS_PER_EXPERT
    pairs = _pair_slots()
    lane_pair = lax.broadcasted_iota(jnp.int32, (PEER_SLOTS, LANES), 1) >> 1

    row = lax.broadcasted_iota(jnp.int32, (2 * LANES, LANES), 0)
    col = lax.broadcasted_iota(jnp.int32, (2 * LANES, LANES), 1)
    sel_ref[...] = jnp.where((col & 1) == (row >= LANES).astype(jnp.int32), 1.0, 0.0).astype(BF16)
    o_ref[...] = jnp.zeros_like(o_ref)

    group_rows = 4 * PEER_SLOTS
    z_ref[...] = jnp.zeros_like(z_ref)

    def lane_partials(gi, zbase):
        for j in range(SUBLANES):
            t = gi * SUBLANES + j
            x = x_ref[pl.ds(pl.multiple_of(t * SUBLANES, SUBLANES), SUBLANES), :]
            swapped = pltpu.roll(x, ROWS_PER_EXPERT, 0)
            xa = jnp.where(upper, x, swapped)
            xb = jnp.where(upper, swapped, x)
            xa_bits = lax.bitcast_convert_type(xa.astype(BF16).astype(F32), jnp.uint32)
            xb_bits = lax.bitcast_convert_type(xb.astype(BF16).astype(F32), jnp.uint32)
            xw = pltpu.bitcast((xb_bits & jnp.uint32(0xFFFF0000)) | (xa_bits >> 16), BF16)
            off = _slot_reader(offs, t, tb)
            zs = []
            for g in range(PEER_SLOTS // SUBLANES):
                ps = []
                for c in range(4):
                    top, bot = pairs[4 * g + c]
                    words = _gather_words(tbl_ref, off(top), off(bot), upper)
                    lo, hi = _unpack_words(pltpu.bitcast(pltpu.bitcast(words, BF16) * xw, jnp.uint32))
                    ps.append(lo + hi)
                r0 = _merge_sublanes(ps[0], ps[1], 2, sub)
                r1 = _merge_sublanes(ps[2], ps[3], 2, sub)
                zs.append(_merge_sublanes(r0, r1, 1, sub))
            rows = zbase + (j // 2) * PEER_SLOTS
            cols = slice((j % 2) * LANES, (j % 2 + 1) * LANES)
            for m in range(len(zs) // 2):
                tile = jnp.concatenate([zs[2 * m], zs[2 * m + 1]], axis=0).astype(BF16)
                z_ref[pl.ds(pl.multiple_of(rows + 2 * SUBLANES * m, 2 * SUBLANES), 2 * SUBLANES), cols] = tile

    def lane_sums(gi, zbase):
        sums = jnp.dot(z_ref[pl.ds(zbase, group_rows), :], sel_ref[...], preferred_element_type=F32)
        block = pl.ds(pl.multiple_of((gi // (LANES // SUBLANES)) * LANES, LANES), LANES)
        first_pair = (gi % (LANES // SUBLANES)) * 4
        acc = o_ref[:, block]
        for p in range(4):
            acc = jnp.where(lane_pair == first_pair + p, sums[p * PEER_SLOTS:(p + 1) * PEER_SLOTS], acc)
        o_ref[:, block] = acc

    def zbase_of(gi):
        return pl.multiple_of((gi % 2) * group_rows, group_rows)

    def step(gi, carry):
        prev = jnp.maximum(gi - 1, 0)
        lane_sums(prev, zbase_of(gi + 1))
        lane_partials(gi, zbase_of(gi))
        return carry

    ngroups_tokens = tb // SUBLANES
    lax.fori_loop(0, ngroups_tokens, step, 0)
    lane_sums(ngroups_tokens - 1, (ngroups_tokens - 1) % 2 * group_rows)


def _peer_u(offs_g, x_rows, tbl, tb):
    n = x_rows.shape[0] // SUBLANES
    assert tb % LANES == 0
    return pl.pallas_call(
        _peer_u_kernel,
        out_shape=jax.ShapeDtypeStruct((PEER_SLOTS, n), F32),
        grid=(n // tb,),
        in_specs=_slot_specs(tb) + [
            pl.BlockSpec((tb * SUBLANES, LANES), lambda i: (i, 0)),
            pl.BlockSpec(tbl.shape, lambda i: (0, 0), pipeline_mode=pl.Buffered(1)),
        ],
        out_specs=pl.BlockSpec((PEER_SLOTS, tb), lambda i: (0, i)),
        scratch_shapes=[
            pltpu.VMEM((2 * 4 * PEER_SLOTS, 2 * LANES), BF16),
            pltpu.VMEM((2 * LANES, LANES), BF16),
        ],
        compiler_params=_cparams(("arbitrary",)),
        name="peer_u",
    )(*([offs_g] * (PEER_SLOTS // SLOT_GROUP)), x_rows, tbl)


def _peer_w_kernel(gate_ref, act_ref, w_ref):
    a = act_ref[...]
    w_ref[...] = gate_ref[...] * (0.5 * a * (1.0 + lax.erf(a * (2.0 ** -0.5))))


def _peer_w(gate_t, act_t, tm):
    n = gate_t.shape[-1]
    spec = pl.BlockSpec((PEER_SLOTS, tm), lambda i: (0, i))
    return pl.pallas_call(
        _peer_w_kernel,
        out_shape=jax.ShapeDtypeStruct(gate_t.shape, F32),
        grid=(n // tm,),
        in_specs=[spec, spec],
        out_specs=spec,
        compiler_params=_cparams(("arbitrary",)),
        name="peer_w",
    )(gate_t, act_t)


def _peer_v_kernel(*refs):
    ngroups = PEER_SLOTS // SLOT_GROUP
    offs = refs[:ngroups]
    w_ref, h_ref, tbl_ref, nw_ref, o_ref, acc_ref, whi_ref, wlo_ref, repa_ref, repb_ref, ones_ref = refs[ngroups:]
    tb = h_ref.shape[0] // SUBLANES
    sub = lax.broadcasted_iota(jnp.int32, (SUBLANES, LANES), 0)
    upper = sub < ROWS_PER_EXPERT
    pairs = _pair_slots()
    lane = lax.broadcasted_iota(jnp.int32, (PEER_SLOTS, LANES), 1)

    w = w_ref[...]
    whi = w.astype(BF16).astype(F32)
    whi_ref[...] = whi
    wlo_ref[...] = w - whi
    row = lax.broadcasted_iota(jnp.int32, (2 * LANES, 2 * LANES), 0)
    col = lax.broadcasted_iota(jnp.int32, (2 * LANES, 2 * LANES), 1)
    ones_ref[...] = jnp.where((row >= LANES) == (col >= LANES), 1.0, 0.0).astype(BF16)

    def replicate(t0, rep_ref):
        for r in range(V_TOKENS_PER_ITER // 2):
            t = t0 + 2 * r
            block = pl.ds(pl.multiple_of((t // LANES) * LANES, LANES), LANES)
            la = t % LANES
            out = None
            for part_ref in (whi_ref, wlo_ref):
                part = part_ref[:, block]
                lhs = jnp.concatenate([jnp.where(lane == la, part, 0.0), jnp.where(lane == la + 1, part, 0.0)],
                                      axis=1).astype(BF16)
                prod = jnp.dot(lhs, ones_ref[...], preferred_element_type=F32)
                out = prod if out is None else out + prod
            rep_ref[r] = out

    def token(t, rep_ref, slot, half):
        off = _slot_reader(offs, t, tb)
        lanes = slice(half * LANES, (half + 1) * LANES)
        acc = [jnp.zeros((SUBLANES, LANES), F32) for _ in range(4)]
        for q, (top, bot) in enumerate(pairs):
            lo, hi = _gather_pair(tbl_ref, off(top), off(bot), upper)
            wtop = jnp.broadcast_to(rep_ref[slot, top:top + 1, lanes], (SUBLANES, LANES))
            wbot = jnp.broadcast_to(rep_ref[slot, bot:bot + 1, lanes], (SUBLANES, LANES))
            wv = jnp.where(upper, wtop, wbot)
            k = 2 * (q % 2)
            acc[k] = acc[k] + lo * wv
            acc[k + 1] = acc[k + 1] + hi * wv
        lo = acc[0] + acc[2]
        hi = acc[1] + acc[3]
        lo = lo + pltpu.roll(lo, ROWS_PER_EXPERT, 0)
        hi = hi + pltpu.roll(hi, ROWS_PER_EXPERT, 0)
        rows = pl.ds(pl.multiple_of(t * SUBLANES, SUBLANES), SUBLANES)
        acc_ref[rows, :] = h_ref[rows, :] + jnp.where(upper, lo, hi)

    def tokens(t0, rep_ref):
        for r in range(V_TOKENS_PER_ITER):
            token(t0 + r, rep_ref, r // 2, r % 2)

    replicate(0, repa_ref)

    def step(i, carry):
        t0 = i * 2 * V_TOKENS_PER_ITER
        replicate(t0 + V_TOKENS_PER_ITER, repb_ref)
        tokens(t0, repa_ref)
        replicate(jnp.minimum(t0 + 2 * V_TOKENS_PER_ITER, tb - V_TOKENS_PER_ITER), repa_ref)
        tokens(t0 + V_TOKENS_PER_ITER, repb_ref)
        return carry

    lax.fori_loop(0, tb // (2 * V_TOKENS_PER_ITER), step, 0)

    def norm_step(i, carry):
        rows = pl.ds(pl.multiple_of(i * NORM_TOKENS * SUBLANES, NORM_TOKENS * SUBLANES), NORM_TOKENS * SUBLANES)
        h = acc_ref[rows, :].reshape(NORM_TOKENS, SUBLANES, LANES)
        ss = jnp.sum(jnp.sum(h * h, axis=2, keepdims=True), axis=1, keepdims=True)
        y = h * lax.rsqrt(ss * (1.0 / D_MODEL) + RMS_EPS) * nw_ref[...][None]
        acc_ref[rows, :] = y.reshape(NORM_TOKENS * SUBLANES, LANES)
        return carry

    lax.fori_loop(0, tb // NORM_TOKENS, norm_step, 0)
    for c in range(D_MODEL // LANES):
        o_ref[:, c * LANES:(c + 1) * LANES] = acc_ref[pl.ds(c, tb, stride=SUBLANES), :]


def _peer_v(offs_g, w_t, h_rows, tbl, norm_rows, tb):
    n = h_rows.shape[0] // SUBLANES
    assert tb % LANES == 0 and V_TOKENS_PER_ITER % 2 == 0 and tb % (2 * V_TOKENS_PER_ITER) == 0 and tb % NORM_TOKENS == 0
    rows = pl.BlockSpec((tb * SUBLANES, LANES), lambda i: (i, 0))
    return pl.pallas_call(
        _peer_v_kernel,
        out_shape=jax.ShapeDtypeStruct((n, D_MODEL), F32),
        grid=(n // tb,),
        in_specs=_slot_specs(tb) + [
            pl.BlockSpec((PEER_SLOTS, tb), lambda i: (0, i)),
            rows,
            pl.BlockSpec(tbl.shape, lambda i: (0, 0), pipeline_mode=pl.Buffered(1)),
            pl.BlockSpec((SUBLANES, LANES), lambda i: (0, 0)),
        ],
        out_specs=pl.BlockSpec((tb, D_MODEL), lambda i: (i, 0)),
        scratch_shapes=[
            pltpu.VMEM((tb * SUBLANES, LANES), F32),
            pltpu.VMEM((PEER_SLOTS, tb), F32),
            pltpu.VMEM((PEER_SLOTS, tb), F32),
            pltpu.VMEM((V_TOKENS_PER_ITER // 2, PEER_SLOTS, 2 * LANES), F32),
            pltpu.VMEM((V_TOKENS_PER_ITER // 2, PEER_SLOTS, 2 * LANES), F32),
            pltpu.VMEM((2 * LANES, 2 * LANES), BF16),
        ],
        compiler_params=_cparams(("arbitrary",)),
        name="peer_v",
    )(*([offs_g] * (PEER_SLOTS // SLOT_GROUP)), w_t, h_rows, tbl, norm_rows)


def _pack_table(t):
    half = D_MODEL // 2
    bits = lax.bitcast_convert_type(t.astype(BF16), jnp.uint16).astype(jnp.uint32)
    words = bits[:, :half] | (bits[:, half:] << 16)
    words = words.reshape(t.shape[0] * ROWS_PER_EXPERT, LANES)
    return jnp.pad(words, ((TABLE_PAD, TABLE_PAD), (0, 0)))


def _t5_bucket(rel):
    half = N_BUCKETS // 2
    exact = half // 2
    n = jnp.abs(rel)
    large = exact + (jnp.log(jnp.maximum(n, 1).astype(F32) / exact)
                     / math.log(MAX_DISTANCE / exact) * (half - exact)).astype(jnp.int32)
    large = jnp.minimum(large, half - 1)
    return jnp.where(rel > 0, half, 0) + jnp.where(n < exact, n, large)


def _attention_bias(rel_bias):
    rb = rel_bias.astype(F32)
    i = jnp.arange(ATT_BLOCK)
    jb = jnp.arange(3 * ATT_BLOCK)
    rel = (jb[None, :] - ATT_BLOCK) - i[:, None]
    band = jnp.where((jnp.abs(rel) <= ATT_BLOCK)[..., None], rb[_t5_bucket(rel)], NEG)
    mpos = PREFIX - N_META + jnp.arange(N_META)
    variants = []
    for blk in (0, 1):
        qpos = PREFIX + blk * ATT_BLOCK + i
        meta = rb[_t5_bucket(mpos[None, :] - qpos[:, None])]
        fill = jnp.full((ATT_BLOCK, ATT_BLOCK - N_META, ATT_HEADS), NEG, F32)
        variants.append(jnp.concatenate([band, meta, fill], axis=1).transpose(2, 0, 1))
    return jnp.stack(variants)


def _rope_tables(pos):
    half = RET_D // 2
    inv = ROPE_BASE ** (-jnp.arange(half, dtype=F32) / half)
    ang = pos.astype(F32)[:, None] * inv[None, :]
    cos, sin = jnp.cos(ang), jnp.sin(ang)
    return jnp.concatenate([cos, cos], axis=1), jnp.concatenate([-sin, sin], axis=1)


def _decay_tables(dec_f, dec_b, chunk):
    lf = jax.nn.log_sigmoid(dec_f.astype(F32))[:, None]
    lb = jax.nn.log_sigmoid(dec_b.astype(F32))[:, None]
    idx = jnp.arange(chunk, dtype=F32)
    diff = idx[:, None] - idx[None, :]
    dmat = (jnp.where(diff >= 0, jnp.exp(jnp.maximum(diff, 0.0)[None] * lf[:, :, None]), 0.0)
            + jnp.where(diff < 0, jnp.exp(jnp.maximum(-diff, 0.0)[None] * lb[:, :, None]), 0.0))
    bc = lambda v: jnp.broadcast_to(v[:, :, None], v.shape + (RET_D,))
    qwf = bc(jnp.exp((idx + 1.0)[None] * lf))
    kwf = bc(jnp.exp((chunk - 1.0 - idx)[None] * lf))
    qwb = bc(jnp.exp((chunk - idx)[None] * lb))
    kwb = bc(jnp.exp(idx[None] * lb))
    pidx = jnp.arange(PREFIX, dtype=F32)
    kwp = bc(jnp.exp((PREFIX - 1.0 - pidx)[None] * lf))
    gdec = jnp.concatenate([jnp.exp(chunk * lf[:, 0]), jnp.exp(chunk * lb[:, 0])])
    return dmat, qwf, qwb, kwf, kwb, kwp, gdec


def _divisor_tile(n, want):
    t = min(n, want)
    while n % t:
        t //= 2
    return t


def _encode(x, shared):
    b, s, d = x.shape
    n = b * s
    tm = _divisor_tile(s, 512)
    chunk = _divisor_tile(s, 256)
    cos_t, sin_t = _rope_tables(jnp.arange(s) + N_META)
    qr, kr, vr, gr, qa, ka, va = _proj(x, shared["norm_mix"], shared["w_in"], cos_t, sin_t, tm)
    o_r = _retention(qr, kr, vr, gr, shared["kr_pre"], shared["vr_pre"], *_decay_tables(
        shared["dec_f"], shared["dec_b"], chunk), shared["gn_w"], chunk)
    o_a = _attention(qa, ka, va, shared["k_meta"], shared["v_meta"], shared["bias"], shared["sink"])
    h_rows, xn_rows, xn_bf = _outproj(o_r, o_a, x, shared["w_out"], shared["norm_ffn"], tm)
    idx_t, gate_t = _route(xn_bf, shared["wq"], shared["pkeys"], _divisor_tile(s, ROUTE_TOKENS))
    idx_t = idx_t.transpose(1, 0, 2).reshape(PEER_SLOTS, n)
    gate_t = gate_t.transpose(1, 0, 2).reshape(PEER_SLOTS, n)
    top_half = (jnp.arange(PEER_SLOTS) % SUBLANES) < ROWS_PER_EXPERT
    tb = _divisor_tile(n, PEER_TOKENS)
    offs_g = _group_slots(idx_t * ROWS_PER_EXPERT + jnp.where(top_half, TABLE_PAD, 0)[:, None], tb)
    act_t = _peer_u(offs_g, xn_rows, shared["u_tbl"], tb)
    w_t = _peer_w(gate_t, act_t, _divisor_tile(n, 2048))
    y = _peer_v(offs_g, w_t, h_rows, shared["v_tbl"], shared["norm_final"], tb)
    return y.reshape(b, s, d)


def kernel(x_prompt, x_sample, meta_tokens, norm_mix_w, w_in, ret_decay_fwd, ret_decay_bwd, ret_gn_w,
           attn_sink, rel_bias, w_out, norm_ffn_w, peer_wq, peer_keys, peer_u, peer_v, norm_final_w):
    layer = 0
    shared = {
        "norm_mix": norm_mix_w[layer][None, :].astype(F32),
        "w_in": w_in[layer].astype(BF16),
        "dec_f": ret_decay_fwd[layer],
        "dec_b": ret_decay_bwd[layer],
        "gn_w": ret_gn_w[layer][None, :].astype(F32),
        "sink": attn_sink[layer].astype(F32),
        "bias": _attention_bias(rel_bias),
        "w_out": w_out[layer].astype(BF16),
        "norm_ffn": norm_ffn_w[layer][None, :].astype(F32),
        "wq": peer_wq[layer].astype(BF16),
        "pkeys": peer_keys[layer].reshape(2 * PEER_HEADS, PEER_NKEYS, PEER_NKEYS).astype(BF16),
        "u_tbl": _pack_table(peer_u[layer]),
        "v_tbl": _pack_table(peer_v[layer]),
        "norm_final": norm_final_w.reshape(SUBLANES, LANES).astype(F32),
    }
    prefix = jnp.concatenate([jnp.zeros((PREFIX - N_META, D_MODEL), x_prompt.dtype),
                              meta_tokens.astype(x_prompt.dtype)], axis=0)[None]
    cos_p, sin_p = _rope_tables(jnp.arange(PREFIX) - (PREFIX - N_META))
    _, kr_p, vr_p, _, _, ka_p, va_p = _proj(prefix, shared["norm_mix"], shared["w_in"], cos_p, sin_p, PREFIX)
    shared["kr_pre"], shared["vr_pre"] = kr_p[0], vr_p[0]
    pad_meta = lambda t: jnp.pad(t[0, PREFIX - N_META:], ((0, ATT_BLOCK - N_META), (0, 0)))
    shared["k_meta"], shared["v_meta"] = pad_meta(ka_p), pad_meta(va_p)
    return (_encode(x_prompt, shared), _encode(x_sample, shared))
```

```python
import functools
import math

import jax
import jax.numpy as jnp
from jax import lax
from jax.experimental import pallas as pl
from jax.experimental.pallas import tpu as pltpu

F32 = jnp.float32
BF16 = jnp.bfloat16

D_MODEL = 1024
N_META = 16
PREFIX = 128
RET_HEADS = 4
RET_D = 128
ATT_HEADS = 8
ATT_KV = 2
ATT_GROUP = ATT_HEADS // ATT_KV
ATT_HD = 64
ATT_BLOCK = 128
N_BUCKETS = 32
MAX_DISTANCE = 128
ROPE_BASE = 10000.0
PEER_HEADS = 8
PEER_NKEYS = 128
PEER_EXPERTS = PEER_NKEYS * PEER_NKEYS
PEER_TOPK = 16
PEER_SLOTS = PEER_HEADS * PEER_TOPK
RMS_EPS = 1e-6
GN_EPS = 1e-5
NEG = -1e30
IN_COLS = (0, 512, 1024, 1536, 2048, 2560, 2688, 2816)

LANES = 128
SUBLANES = 8
VMEM_LIMIT = 56 * 1024 * 1024
ROWS_PER_EXPERT = D_MODEL // 2 // LANES
TABLE_PAD = ROWS_PER_EXPERT
PEER_TOKENS = 512
ROUTE_TOKENS = 256
ROUTE_HEADS_PER_ITER = 2
SLOT_ORDER = (0, 2, 1, 3)
SLOT_GROUP = 1
V_TOKENS_PER_ITER = 8
NORM_TOKENS = 32


def _cparams(sem):
    return pltpu.CompilerParams(dimension_semantics=sem, vmem_limit_bytes=VMEM_LIMIT)


def _proj_kernel(x_ref, nw_ref, w_ref, cos_ref, sin_ref,
                 qr_ref, kr_ref, vr_ref, gr_ref, qa_ref, ka_ref, va_ref):
    x = x_ref[0]
    ms = jnp.mean(x * x, axis=-1, keepdims=True)
    xn = (x * lax.rsqrt(ms + RMS_EPS) * nw_ref[...]).astype(BF16)
    cosf = cos_ref[...]
    sinf = sin_ref[...]

    def mm(i):
        return jnp.dot(xn, w_ref[:, IN_COLS[i]:IN_COLS[i + 1]], preferred_element_type=F32)

    def rotary(t, scale):
        for h in range(RET_HEADS):
            th = t[:, h * RET_D:(h + 1) * RET_D]
            yield h, (th * cosf + pltpu.roll(th, RET_D // 2, 1) * sinf) * scale

    for h, r in rotary(mm(0), 1.0):
        qr_ref[0, :, h * RET_D:(h + 1) * RET_D] = r.astype(BF16)
    for h, r in rotary(mm(1), RET_D ** -0.5):
        kr_ref[0, :, h * RET_D:(h + 1) * RET_D] = r.astype(BF16)
    vr_ref[0] = mm(2).astype(BF16)
    gr_ref[0] = mm(3)
    qa_ref[0] = (mm(4) * (ATT_HD ** -0.5)).astype(BF16)
    ka_ref[0] = mm(5).astype(BF16)
    va_ref[0] = mm(6).astype(BF16)


def _proj(x, norm_w, w_in_bf, cos_t, sin_t, tm):
    b, s, d = x.shape
    widths = (512, 512, 512, 512, 512, 128, 128)
    dtypes = (BF16, BF16, BF16, F32, BF16, BF16, BF16)
    return pl.pallas_call(
        _proj_kernel,
        out_shape=[jax.ShapeDtypeStruct((b, s, w), dt) for w, dt in zip(widths, dtypes)],
        grid=(b, s // tm),
        in_specs=[
            pl.BlockSpec((1, tm, d), lambda i, j: (i, j, 0)),
            pl.BlockSpec((1, d), lambda i, j: (0, 0)),
            pl.BlockSpec(w_in_bf.shape, lambda i, j: (0, 0)),
            pl.BlockSpec((tm, LANES), lambda i, j: (j, 0)),
            pl.BlockSpec((tm, LANES), lambda i, j: (j, 0)),
        ],
        out_specs=[pl.BlockSpec((1, tm, w), lambda i, j: (i, j, 0)) for w in widths],
        compiler_params=_cparams(("arbitrary", "arbitrary")),
        name="proj",
    )(x, norm_w, w_in_bf, cos_t, sin_t)


def _ret_kernel(q_ref, k_ref, v_ref, g_ref, kpre_ref, vpre_ref, dmat_ref, qwf_ref, qwb_ref,
                kwf_ref, kwb_ref, kwp_ref, gdec_ref, gnw_ref, o_ref, sf_ref, sb_ref, sball_ref):
    ph = pl.program_id(1)
    c = pl.program_id(2)
    nc = pl.num_programs(2)
    contract0 = (((0,), (0,)), ((), ()))
    contract1 = (((1,), (1,)), ((), ()))

    def head(ref, h):
        return ref[0, :, h * RET_D:(h + 1) * RET_D]

    def kv_update(kh, vh, kw):
        kw = (kh.astype(F32) * kw).astype(BF16)
        return lax.dot_general(kw, vh, contract0, preferred_element_type=F32)

    @pl.when(ph == 0)
    def _backward_states():
        @pl.when(c == 0)
        def _():
            sb_ref[...] = jnp.zeros_like(sb_ref)

        j = nc - 1 - c
        for h in range(RET_HEADS):
            sball_ref[j, h] = sb_ref[h].astype(BF16)
            sb_ref[h] = gdec_ref[RET_HEADS + h] * sb_ref[h] + kv_update(head(k_ref, h), head(v_ref, h), kwb_ref[h])

    @pl.when(ph == 1)
    def _outputs():
        @pl.when(c == 0)
        def _():
            for h in range(RET_HEADS):
                kp = kpre_ref[:, h * RET_D:(h + 1) * RET_D]
                vp = vpre_ref[:, h * RET_D:(h + 1) * RET_D]
                sf_ref[h] = kv_update(kp, vp, kwp_ref[h])

        for h in range(RET_HEADS):
            qh, kh, vh = head(q_ref, h), head(k_ref, h), head(v_ref, h)
            s = lax.dot_general(qh, kh, contract1, preferred_element_type=F32) * dmat_ref[h]
            o = jnp.dot(s.astype(BF16), vh, preferred_element_type=F32)
            o = o + jnp.dot(qh, sf_ref[h].astype(BF16), preferred_element_type=F32) * qwf_ref[h]
            o = o + jnp.dot(qh, sball_ref[c, h], preferred_element_type=F32) * qwb_ref[h]
            mu = jnp.mean(o, axis=-1, keepdims=True)
            oc = o - mu
            var = jnp.mean(oc * oc, axis=-1, keepdims=True)
            on = oc * lax.rsqrt(var + GN_EPS) * gnw_ref[:, h * RET_D:(h + 1) * RET_D]
            g = head(g_ref, h)
            o_ref[0, :, h * RET_D:(h + 1) * RET_D] = (g * jax.nn.sigmoid(g) * on).astype(BF16)
            sf_ref[h] = gdec_ref[h] * sf_ref[h] + kv_update(kh, vh, kwf_ref[h])


def _retention(qr, kr, vr, gr, kpre, vpre, dmat, qwf, qwb, kwf, kwb, kwp, gdec, gnw, chunk):
    b, s, w = qr.shape
    nc = s // chunk

    def seq_map(i, ph, c):
        return (i, jnp.where(ph == 0, nc - 1 - c, c), 0)

    def fwd_only(i, ph, c):
        return (i, c * ph, 0)

    const2 = lambda i, ph, c: (0, 0)
    const3 = lambda i, ph, c: (0, 0, 0)
    return pl.pallas_call(
        _ret_kernel,
        out_shape=jax.ShapeDtypeStruct((b, s, w), BF16),
        grid=(b, 2, nc),
        in_specs=[
            pl.BlockSpec((1, chunk, w), fwd_only),
            pl.BlockSpec((1, chunk, w), seq_map),
            pl.BlockSpec((1, chunk, w), seq_map),
            pl.BlockSpec((1, chunk, w), fwd_only),
            pl.BlockSpec(kpre.shape, const2),
            pl.BlockSpec(vpre.shape, const2),
            pl.BlockSpec(dmat.shape, const3),
            pl.BlockSpec(qwf.shape, const3),
            pl.BlockSpec(qwb.shape, const3),
            pl.BlockSpec(kwf.shape, const3),
            pl.BlockSpec(kwb.shape, const3),
            pl.BlockSpec(kwp.shape, const3),
            pl.BlockSpec(memory_space=pltpu.SMEM),
            pl.BlockSpec(gnw.shape, const2),
        ],
        out_specs=pl.BlockSpec((1, chunk, w), fwd_only),
        scratch_shapes=[
            pltpu.VMEM((RET_HEADS, RET_D, RET_D), F32),
            pltpu.VMEM((RET_HEADS, RET_D, RET_D), F32),
            pltpu.VMEM((nc, RET_HEADS, RET_D, RET_D), BF16),
        ],
        compiler_params=_cparams(("arbitrary", "arbitrary", "arbitrary")),
        name="retention",
    )(qr, kr, vr, gr, kpre, vpre, dmat, qwf, qwb, kwf, kwb, kwp, gdec, gnw)


def _attn_kernel(q_ref, kp_ref, kc_ref, kn_ref, vp_ref, vc_ref, vn_ref, km_ref, vm_ref,
                 bias_ref, sink_ref, o_ref):
    j = pl.program_id(1)
    nb = pl.num_programs(1)
    lane = lax.broadcasted_iota(jnp.int32, (1, 4 * ATT_BLOCK), 1)
    first = jnp.logical_and(lane < ATT_BLOCK, j == 0)
    last = jnp.logical_and(jnp.logical_and(lane >= 2 * ATT_BLOCK, lane < 3 * ATT_BLOCK), j == nb - 1)
    pen = jnp.where(jnp.logical_or(first, last), NEG, 0.0).astype(F32)
    contract1 = (((1,), (1,)), ((), ()))
    rows = lax.broadcasted_iota(jnp.int32, (ATT_GROUP * ATT_BLOCK, 1), 0)
    for g in range(ATT_KV):
        sl = slice(g * ATT_HD, (g + 1) * ATT_HD)
        kall = jnp.concatenate([kp_ref[0][:, sl], kc_ref[0][:, sl], kn_ref[0][:, sl], km_ref[:, sl]], axis=0)
        vall = jnp.concatenate([vp_ref[0][:, sl], vc_ref[0][:, sl], vn_ref[0][:, sl], vm_ref[:, sl]], axis=0)
        heads = range(g * ATT_GROUP, (g + 1) * ATT_GROUP)
        q = jnp.concatenate([q_ref[0, :, h * ATT_HD:(h + 1) * ATT_HD] for h in heads], axis=0)
        bias = bias_ref[0, g * ATT_GROUP:(g + 1) * ATT_GROUP].reshape(ATT_GROUP * ATT_BLOCK, 4 * ATT_BLOCK)
        snk = jnp.zeros((ATT_GROUP * ATT_BLOCK, 1), F32)
        for hh, h in enumerate(heads):
            snk = jnp.where(rows // ATT_BLOCK == hh, sink_ref[h], snk)
        s = lax.dot_general(q, kall, contract1, preferred_element_type=F32) + bias + pen
        m = jnp.maximum(jnp.max(s, axis=-1, keepdims=True), snk)
        p = jnp.exp(s - m)
        den = jnp.sum(p, axis=-1, keepdims=True) + jnp.exp(snk - m)
        o = jnp.dot(p.astype(BF16), vall, preferred_element_type=F32) / den
        for hh in range(0, ATT_GROUP, 2):
            lanes = slice((g * ATT_GROUP + hh) * ATT_HD, (g * ATT_GROUP + hh + 2) * ATT_HD)
            o_ref[0, :, lanes] = jnp.concatenate(
                [o[hh * ATT_BLOCK:(hh + 1) * ATT_BLOCK], o[(hh + 1) * ATT_BLOCK:(hh + 2) * ATT_BLOCK]],
                axis=-1).astype(BF16)


def _attention(qa, ka, va, kmeta, vmeta, bias, sink):
    b, s, w = qa.shape
    nb = s // ATT_BLOCK
    kvw = ka.shape[-1]
    prev = lambda i, j: (i, jnp.maximum(j - 1, 0), 0)
    cur = lambda i, j: (i, j, 0)
    nxt = lambda i, j: (i, jnp.minimum(j + 1, nb - 1), 0)
    const2 = lambda i, j: (0, 0)
    kv = lambda m: pl.BlockSpec((1, ATT_BLOCK, kvw), m)
    return pl.pallas_call(
        _attn_kernel,
        out_shape=jax.ShapeDtypeStruct((b, s, w), BF16),
        grid=(b, nb),
        in_specs=[
            pl.BlockSpec((1, ATT_BLOCK, w), cur),
            kv(prev), kv(cur), kv(nxt), kv(prev), kv(cur), kv(nxt),
            pl.BlockSpec(kmeta.shape, const2),
            pl.BlockSpec(vmeta.shape, const2),
            pl.BlockSpec((1,) + bias.shape[1:], lambda i, j: (jnp.minimum(j, 1), 0, 0, 0)),
            pl.BlockSpec(memory_space=pltpu.SMEM),
        ],
        out_specs=pl.BlockSpec((1, ATT_BLOCK, w), cur),
        compiler_params=_cparams(("arbitrary", "arbitrary")),
        name="attention",
    )(qa, ka, ka, ka, va, va, va, kmeta, vmeta, bias, sink)


def _store_token_rows(ref, v):
    for c in range(D_MODEL // LANES):
        ref[pl.ds(c, v.shape[0], stride=SUBLANES), :] = v[:, c * LANES:(c + 1) * LANES]


def _outproj_kernel(or_ref, oa_ref, x_ref, w_ref, nw_ref, h_ref, xn_ref, xnb_ref):
    half = or_ref.shape[-1]
    h = x_ref[0]
    h = h + jnp.dot(or_ref[0], w_ref[:half], preferred_element_type=F32)
    h = h + jnp.dot(oa_ref[0], w_ref[half:], preferred_element_type=F32)
    _store_token_rows(h_ref, h)
    ms = jnp.mean(h * h, axis=-1, keepdims=True)
    xn = h * lax.rsqrt(ms + RMS_EPS) * nw_ref[...]
    _store_token_rows(xn_ref, xn)
    xnb_ref[0] = xn.astype(BF16)


def _outproj(o_r, o_a, x, w_out_bf, norm_w, tm):
    b, s, d = x.shape
    half = o_r.shape[-1]
    row = lambda i, j: (i, j, 0)
    const2 = lambda i, j: (0, 0)
    token_rows = pl.BlockSpec((tm * SUBLANES, LANES), lambda i, j: (i * (s // tm) + j, 0))
    return pl.pallas_call(
        _outproj_kernel,
        out_shape=[jax.ShapeDtypeStruct((b * s * SUBLANES, LANES), F32)] * 2 + [jax.ShapeDtypeStruct((b, s, d), BF16)],
        grid=(b, s // tm),
        in_specs=[
            pl.BlockSpec((1, tm, half), row),
            pl.BlockSpec((1, tm, half), row),
            pl.BlockSpec((1, tm, d), row),
            pl.BlockSpec(w_out_bf.shape, const2),
            pl.BlockSpec((1, d), const2),
        ],
        out_specs=[token_rows, token_rows, pl.BlockSpec((1, tm, d), row)],
        compiler_params=_cparams(("arbitrary", "arbitrary")),
        name="outproj",
    )(o_r, o_a, x, w_out_bf, norm_w)


def _top16_rows(s, iota):
    nrows = float(s.shape[0])
    vals, ids = [], []
    for _ in range(PEER_TOPK):
        m = jnp.max(s, axis=0, keepdims=True)
        am = jnp.min(jnp.where(s == m, iota, nrows), axis=0, keepdims=True)
        vals.append(m)
        ids.append(am)
        s = jnp.where(iota == am, -jnp.inf, s)
    return jnp.concatenate(vals, axis=0), jnp.concatenate(ids, axis=0)


_PAIR_GROUPS = ((0, 0, 8), (0, 8, 8), (1, 0, 8), (2, 0, 5), (3, 0, 4), (4, 0, 3), (5, 0, 2), (6, 0, 2), (7, 0, 2))


def _route_kernel(x_ref, wq_ref, pk_ref, idx_ref, gate_ref, sc_ref):
    tm = x_ref.shape[1]
    q = jnp.dot(x_ref[0], wq_ref[...], preferred_element_type=F32).astype(BF16)
    contract1 = (((1,), (1,)), ((), ()))
    for hp in range(2 * PEER_HEADS):
        sc_ref[hp] = lax.dot_general(pk_ref[hp], q[:, hp * PEER_NKEYS:(hp + 1) * PEER_NKEYS], contract1,
                                     preferred_element_type=F32)
    iota = lax.broadcasted_iota(jnp.int32, (PEER_NKEYS, LANES), 0).astype(F32)
    sub = lax.broadcasted_iota(jnp.int32, (SUBLANES, LANES), 0)
    subf = sub.astype(F32)
    far = float(PEER_TOPK * PEER_TOPK)

    def one_head(h):
        for lt in range(tm // LANES):
            lanes = slice(lt * LANES, (lt + 1) * LANES)
            s1, i1 = _top16_rows(sc_ref[2 * h, :, lanes], iota)
            s2, i2 = _top16_rows(sc_ref[2 * h + 1, :, lanes], iota)
            i1 = i1 * float(PEER_NKEYS)
            sc, ex, pos = [], [], []
            for i, j0, cnt in _PAIR_GROUPS:
                ok = sub < cnt
                sc.append(jnp.where(ok, s1[i:i + 1] + s2[j0:j0 + SUBLANES], -jnp.inf))
                ex.append(i1[i:i + 1] + i2[j0:j0 + SUBLANES])
                pos.append(jnp.where(ok, float(i * PEER_TOPK + j0) + subf, far))
            sc.append(s1[SUBLANES:] + s2[0:1])
            ex.append(i1[SUBLANES:] + i2[0:1])
            pos.append((float(SUBLANES) + subf) * float(PEER_TOPK))
            sc = jnp.concatenate(sc, axis=0)
            ex = jnp.concatenate(ex, axis=0)
            pos = jnp.concatenate(pos, axis=0)
            top, experts = [], []
            for _ in range(PEER_TOPK):
                m = jnp.max(sc, axis=0, keepdims=True)
                pm = jnp.min(jnp.where(sc == m, pos, far), axis=0, keepdims=True)
                hit = pos == pm
                experts.append(jnp.max(jnp.where(hit, ex, -1.0), axis=0, keepdims=True))
                top.append(m)
                sc = jnp.where(hit, -jnp.inf, sc)
            top = jnp.concatenate(top, axis=0)
            e = jnp.exp(top - top[0:1])
            rows = pl.ds(pl.multiple_of(h * PEER_TOPK, PEER_TOPK), PEER_TOPK)
            gate_ref[0, rows, lanes] = e / jnp.sum(e, axis=0, keepdims=True)
            idx_ref[0, rows, lanes] = jnp.concatenate(experts, axis=0).astype(jnp.int32)

    def heads_step(i, carry):
        for r in range(ROUTE_HEADS_PER_ITER):
            one_head(i * ROUTE_HEADS_PER_ITER + r)
        return carry

    lax.fori_loop(0, PEER_HEADS // ROUTE_HEADS_PER_ITER, heads_step, 0)


def _route(xn, wq_bf, pk_bf, tm):
    b, s, d = xn.shape
    const2 = lambda i, j: (0, 0)
    slot = lambda i, j: (i, 0, j)
    return pl.pallas_call(
        _route_kernel,
        out_shape=[jax.ShapeDtypeStruct((b, PEER_SLOTS, s), jnp.int32),
                   jax.ShapeDtypeStruct((b, PEER_SLOTS, s), F32)],
        grid=(b, s // tm),
        in_specs=[
            pl.BlockSpec((1, tm, d), lambda i, j: (i, j, 0)),
            pl.BlockSpec(wq_bf.shape, const2),
            pl.BlockSpec(pk_bf.shape, lambda i, j: (0, 0, 0)),
        ],
        out_specs=[pl.BlockSpec((1, PEER_SLOTS, tm), slot)] * 2,
        scratch_shapes=[pltpu.VMEM((2 * PEER_HEADS, PEER_NKEYS, tm), F32)],
        compiler_params=_cparams(("arbitrary", "arbitrary")),
        name="route",
    )(xn, wq_bf, pk_bf)


def _pair_slots():
    out = []
    for g in range(PEER_SLOTS // SUBLANES):
        for c in range(4):
            top = SUBLANES * g + SLOT_ORDER[c]
            out.append((top, top + 4))
    return out


def _gather_words(tbl_ref, top_off, bot_off, upper):
    ra = tbl_ref[pl.ds(pl.multiple_of(top_off, ROWS_PER_EXPERT), SUBLANES), :]
    rb = tbl_ref[pl.ds(pl.multiple_of(bot_off, ROWS_PER_EXPERT), SUBLANES), :]
    return jnp.where(upper, ra, rb)


def _unpack_words(words):
    lo = lax.bitcast_convert_type(words << 16, F32)
    hi = lax.bitcast_convert_type(words & jnp.uint32(0xFFFF0000), F32)
    return lo, hi


def _gather_pair(tbl_ref, top_off, bot_off, upper):
    return _unpack_words(_gather_words(tbl_ref, top_off, bot_off, upper))


def _merge_sublanes(a, b, h, sub):
    keep = (sub & h) == 0
    return jnp.where(keep, a + pltpu.roll(a, SUBLANES - h, 0), b + pltpu.roll(b, h, 0))


def _slot_specs(tb):
    return [pl.BlockSpec((None, None, SLOT_GROUP * tb), lambda i, j=j: (j, 0, i), memory_space=pltpu.SMEM,
                         pipeline_mode=pl.Buffered(1)) for j in range(PEER_SLOTS // SLOT_GROUP)]


def _group_slots(a, tb):
    n = a.shape[-1]
    a = a.reshape(PEER_SLOTS // SLOT_GROUP, SLOT_GROUP, n // tb, tb).transpose(0, 2, 1, 3)
    return a.reshape(PEER_SLOTS // SLOT_GROUP, 1, n * SLOT_GROUP)


def _slot_reader(refs, t, tb):
    at = [t + g * tb for g in range(SLOT_GROUP)]
    return lambda k: refs[k // SLOT_GROUP][at[k % SLOT_GROUP]]


def _peer_u_kernel(*refs):
    ngroups = PEER_SLOTS // SLOT_GROUP
    offs = refs[:ngroups]
    x_ref, tbl_ref, o_ref, z_ref, sel_ref = refs[ngroups:]
    tb = x_ref.shape[0] // SUBLANES
    sub = lax.broadcasted_iota(jnp.int32, (SUBLANES, LANES), 0)
    upper = sub < ROWS_PER_EXPERT
    pairs = _pair_slots()
    lane_pair = lax.broadcasted_iota(jnp.int32, (PEER_SLOTS, LANES), 1) >> 1

    row = lax.broadcasted_iota(jnp.int32, (2 * LANES, LANES), 0)
    col = lax.broadcasted_iota(jnp.int32, (2 * LANES, LANES), 1)
    sel_ref[...] = jnp.where((col & 1) == (row >= LANES).astype(jnp.int32), 1.0, 0.0).astype(BF16)
    o_ref[...] = jnp.zeros_like(o_ref)

    group_rows = 4 * PEER_SLOTS
    z_ref[...] = jnp.zeros_like(z_ref)

    def lane_partials(gi, zbase):
        for j in range(SUBLANES):
            t = gi * SUBLANES + j
            x = x_ref[pl.ds(pl.multiple_of(t * SUBLANES, SUBLANES), SUBLANES), :]
            swapped = pltpu.roll(x, ROWS_PER_EXPERT, 0)
            xa = jnp.where(upper, x, swapped)
            xb = jnp.where(upper, swapped, x)
            xa_bits = lax.bitcast_convert_type(xa.astype(BF16).astype(F32), jnp.uint32)
            xb_bits = lax.bitcast_convert_type(xb.astype(BF16).astype(F32), jnp.uint32)
            xw = pltpu.bitcast((xb_bits & jnp.uint32(0xFFFF0000)) | (xa_bits >> 16), BF16)
            off = _slot_reader(offs, t, tb)
            zs = []
            for g in range(PEER_SLOTS // SUBLANES):
                ps = []
                for c in range(4):
                    top, bot = pairs[4 * g + c]
                    words = _gather_words(tbl_ref, off(top), off(bot), upper)
                    lo, hi = _unpack_words(pltpu.bitcast(pltpu.bitcast(words, BF16) * xw, jnp.uint32))
                    ps.append(lo + hi)
                r0 = _merge_sublanes(ps[0], ps[1], 2, sub)
                r1 = _merge_sublanes(ps[2], ps[3], 2, sub)
                zs.append(_merge_sublanes(r0, r1, 1, sub))
            rows = zbase + (j // 2) * PEER_SLOTS
            cols = slice((j % 2) * LANES, (j % 2 + 1) * LANES)
            for m in range(len(zs) // 2):
                tile = jnp.concatenate([zs[2 * m], zs[2 * m + 1]], axis=0).astype(BF16)
                z_ref[pl.ds(pl.multiple_of(rows + 2 * SUBLANES * m, 2 * SUBLANES), 2 * SUBLANES), cols] = tile

    def lane_sums(gi, zbase):
        sums = jnp.dot(z_ref[pl.ds(zbase, group_rows), :], sel_ref[...], preferred_element_type=F32)
        block = pl.ds(pl.multiple_of((gi // (LANES // SUBLANES)) * LANES, LANES), LANES)
        first_pair = (gi % (LANES // SUBLANES)) * 4
        acc = o_ref[:, block]
        for p in range(4):
            acc = jnp.where(lane_pair == first_pair + p, sums[p * PEER_SLOTS:(p + 1) * PEER_SLOTS], acc)
        o_ref[:, block] = acc

    def zbase_of(gi):
        return pl.multiple_of((gi % 2) * group_rows, group_rows)

    def step(gi, carry):
        prev = jnp.maximum(gi - 1, 0)
        lane_sums(prev, zbase_of(gi + 1))
        lane_partials(gi, zbase_of(gi))
        return carry

    ngroups_tokens = tb // SUBLANES
    lax.fori_loop(0, ngroups_tokens, step, 0)
    lane_sums(ngroups_tokens - 1, (ngroups_tokens - 1) % 2 * group_rows)


def _peer_u(offs_g, x_rows, tbl, tb):
    n = x_rows.shape[0] // SUBLANES
    assert tb % LANES == 0
    return pl.pallas_call(
        _peer_u_kernel,
        out_shape=jax.ShapeDtypeStruct((PEER_SLOTS, n), F32),
        grid=(n // tb,),
        in_specs=_slot_specs(tb) + [
            pl.BlockSpec((tb * SUBLANES, LANES), lambda i: (i, 0)),
            pl.BlockSpec(tbl.shape, lambda i: (0, 0), pipeline_mode=pl.Buffered(1)),
        ],
        out_specs=pl.BlockSpec((PEER_SLOTS, tb), lambda i: (0, i)),
        scratch_shapes=[
            pltpu.VMEM((2 * 4 * PEER_SLOTS, 2 * LANES), BF16),
            pltpu.VMEM((2 * LANES, LANES), BF16),
        ],
        compiler_params=_cparams(("arbitrary",)),
        name="peer_u",
    )(*([offs_g] * (PEER_SLOTS // SLOT_GROUP)), x_rows, tbl)


def _peer_w_kernel(gate_ref, act_ref, w_ref):
    a = act_ref[...]
    w_ref[...] = gate_ref[...] * (0.5 * a * (1.0 + lax.erf(a * (2.0 ** -0.5))))


def _peer_w(gate_t, act_t, tm):
    n = gate_t.shape[-1]
    spec = pl.BlockSpec((PEER_SLOTS, tm), lambda i: (0, i))
    return pl.pallas_call(
        _peer_w_kernel,
        out_shape=jax.ShapeDtypeStruct(gate_t.shape, F32),
        grid=(n // tm,),
        in_specs=[spec, spec],
        out_specs=spec,
        compiler_params=_cparams(("arbitrary",)),
        name="peer_w",
    )(gate_t, act_t)


def _peer_v_kernel(*refs):
    ngroups = PEER_SLOTS // SLOT_GROUP
    offs = refs[:ngroups]
    w_ref, h_ref, tbl_ref, nw_ref, o_ref, acc_ref, whi_ref, wlo_ref, repa_ref, repb_ref, ones_ref = refs[ngroups:]
    tb = h_ref.shape[0] // SUBLANES
    sub = lax.broadcasted_iota(jnp.int32, (SUBLANES, LANES), 0)
    upper = sub < ROWS_PER_EXPERT
    pairs = _pair_slots()
    lane = lax.broadcasted_iota(jnp.int32, (PEER_SLOTS, LANES), 1)

    w = w_ref[...]
    whi = w.astype(BF16).astype(F32)
    whi_ref[...] = whi
    wlo_ref[...] = w - whi
    row = lax.broadcasted_iota(jnp.int32, (2 * LANES, 2 * LANES), 0)
    col = lax.broadcasted_iota(jnp.int32, (2 * LANES, 2 * LANES), 1)
    ones_ref[...] = jnp.where((row >= LANES) == (col >= LANES), 1.0, 0.0).astype(BF16)

    def replicate(t0, rep_ref):
        for r in range(V_TOKENS_PER_ITER // 2):
            t = t0 + 2 * r
            block = pl.ds(pl.multiple_of((t // LANES) * LANES, LANES), LANES)
            la = t % LANES
            out = None
            for part_ref in (whi_ref, wlo_ref):
                part = part_ref[:, block]
                lhs = jnp.concatenate([jnp.where(lane == la, part, 0.0), jnp.where(lane == la + 1, part, 0.0)],
                                      axis=1).astype(BF16)
                prod = jnp.dot(lhs, ones_ref[...], preferred_element_type=F32)
                out = prod if out is None else out + prod
            rep_ref[r] = out

    def token(t, rep_ref, slot, half):
        off = _slot_reader(offs, t, tb)
        lanes = slice(half * LANES, (half + 1) * LANES)
        acc = [jnp.zeros((SUBLANES, LANES), F32) for _ in range(4)]
        for q, (top, bot) in enumerate(pairs):
            lo, hi = _gather_pair(tbl_ref, off(top), off(bot), upper)
            wtop = jnp.broadcast_to(rep_ref[slot, top:top + 1, lanes], (SUBLANES, LANES))
            wbot = jnp.broadcast_to(rep_ref[slot, bot:bot + 1, lanes], (SUBLANES, LANES))
            wv = jnp.where(upper, wtop, wbot)
            k = 2 * (q % 2)
            acc[k] = acc[k] + lo * wv
            acc[k + 1] = acc[k + 1] + hi * wv
        lo = acc[0] + acc[2]
        hi = acc[1] + acc[3]
        lo = lo + pltpu.roll(lo, ROWS_PER_EXPERT, 0)
        hi = hi + pltpu.roll(hi, ROWS_PER_EXPERT, 0)
        rows = pl.ds(pl.multiple_of(t * SUBLANES, SUBLANES), SUBLANES)
        acc_ref[rows, :] = h_ref[rows, :] + jnp.where(upper, lo, hi)

    def tokens(t0, rep_ref):
        for r in range(V_TOKENS_PER_ITER):
            token(t0 + r, rep_ref, r // 2, r % 2)

    replicate(0, repa_ref)

    def step(i, carry):
        t0 = i * 2 * V_TOKENS_PER_ITER
        replicate(t0 + V_TOKENS_PER_ITER, repb_ref)
        tokens(t0, repa_ref)
        replicate(jnp.minimum(t0 + 2 * V_TOKENS_PER_ITER, tb - V_TOKENS_PER_ITER), repa_ref)
        tokens(t0 + V_TOKENS_PER_ITER, repb_ref)
        return carry

    lax.fori_loop(0, tb // (2 * V_TOKENS_PER_ITER), step, 0)

    def norm_step(i, carry):
        rows = pl.ds(pl.multiple_of(i * NORM_TOKENS * SUBLANES, NORM_TOKENS * SUBLANES), NORM_TOKENS * SUBLANES)
        h = acc_ref[rows, :].reshape(NORM_TOKENS, SUBLANES, LANES)
        ss = jnp.sum(jnp.sum(h * h, axis=2, keepdims=True), axis=1, keepdims=True)
        y = h * lax.rsqrt(ss * (1.0 / D_MODEL) + RMS_EPS) * nw_ref[...][None]
        acc_ref[rows, :] = y.reshape(NORM_TOKENS * SUBLANES, LANES)
        return carry

    lax.fori_loop(0, tb // NORM_TOKENS, norm_step, 0)
    for c in range(D_MODEL // LANES):
        o_ref[:, c * LANES:(c + 1) * LANES] = acc_ref[pl.ds(c, tb, stride=SUBLANES), :]


def _peer_v(offs_g, w_t, h_rows, tbl, norm_rows, tb):
    n = h_rows.shape[0] // SUBLANES
    assert tb % LANES == 0 and V_TOKENS_PER_ITER % 2 == 0 and tb % (2 * V_TOKENS_PER_ITER) == 0 and tb % NORM_TOKENS == 0
    rows = pl.BlockSpec((tb * SUBLANES, LANES), lambda i: (i, 0))
    return pl.pallas_call(
        _peer_v_kernel,
        out_shape=jax.ShapeDtypeStruct((n, D_MODEL), F32),
        grid=(n // tb,),
        in_specs=_slot_specs(tb) + [
            pl.BlockSpec((PEER_SLOTS, tb), lambda i: (0, i)),
            rows,
            pl.BlockSpec(tbl.shape, lambda i: (0, 0), pipeline_mode=pl.Buffered(1)),
            pl.BlockSpec((SUBLANES, LANES), lambda i: (0, 0)),
        ],
        out_specs=pl.BlockSpec((tb, D_MODEL), lambda i: (i, 0)),
        scratch_shapes=[
            pltpu.VMEM((tb * SUBLANES, LANES), F32),
            pltpu.VMEM((PEER_SLOTS, tb), F32),
            pltpu.VMEM((PEER_SLOTS, tb), F32),
            pltpu.VMEM((V_TOKENS_PER_ITER // 2, PEER_SLOTS, 2 * LANES), F32),
            pltpu.VMEM((V_TOKENS_PER_ITER // 2, PEER_SLOTS, 2 * LANES), F32),
            pltpu.VMEM((2 * LANES, 2 * LANES), BF16),
        ],
        compiler_params=_cparams(("arbitrary",)),
        name="peer_v",
    )(*([offs_g] * (PEER_SLOTS // SLOT_GROUP)), w_t, h_rows, tbl, norm_rows)


def _pack_table(t):
    half = D_MODEL // 2
    bits = lax.bitcast_convert_type(t.astype(BF16), jnp.uint16).astype(jnp.uint32)
    words = bits[:, :half] | (bits[:, half:] << 16)
    words = words.reshape(t.shape[0] * ROWS_PER_EXPERT, LANES)
    return jnp.pad(words, ((TABLE_PAD, TABLE_PAD), (0, 0)))


def _t5_bucket(rel):
    half = N_BUCKETS // 2
    exact = half // 2
    n = jnp.abs(rel)
    large = exact + (jnp.log(jnp.maximum(n, 1).astype(F32) / exact)
                     / math.log(MAX_DISTANCE / exact) * (half - exact)).astype(jnp.int32)
    large = jnp.minimum(large, half - 1)
    return jnp.where(rel > 0, half, 0) + jnp.where(n < exact, n, large)


def _attention_bias(rel_bias):
    rb = rel_bias.astype(F32)
    i = jnp.arange(ATT_BLOCK)
    jb = jnp.arange(3 * ATT_BLOCK)
    rel = (jb[None, :] - ATT_BLOCK) - i[:, None]
    band = jnp.where((jnp.abs(rel) <= ATT_BLOCK)[..., None], rb[_t5_bucket(rel)], NEG)
    mpos = PREFIX - N_META + jnp.arange(N_META)
    variants = []
    for blk in (0, 1):
        qpos = PREFIX + blk * ATT_BLOCK + i
        meta = rb[_t5_bucket(mpos[None, :] - qpos[:, None])]
        fill = jnp.full((ATT_BLOCK, ATT_BLOCK - N_META, ATT_HEADS), NEG, F32)
        variants.append(jnp.concatenate([band, meta, fill], axis=1).transpose(2, 0, 1))
    return jnp.stack(variants)


def _rope_tables(pos):
    half = RET_D // 2
    inv = ROPE_BASE ** (-jnp.arange(half, dtype=F32) / half)
    ang = pos.astype(F32)[:, None] * inv[None, :]
    cos, sin = jnp.cos(ang), jnp.sin(ang)
    return jnp.concatenate([cos, cos], axis=1), jnp.concatenate([-sin, sin], axis=1)


def _decay_tables(dec_f, dec_b, chunk):
    lf = jax.nn.log_sigmoid(dec_f.astype(F32))[:, None]
    lb = jax.nn.log_sigmoid(dec_b.astype(F32))[:, None]
    idx = jnp.arange(chunk, dtype=F32)
    diff = idx[:, None] - idx[None, :]
    dmat = (jnp.where(diff >= 0, jnp.exp(jnp.maximum(diff, 0.0)[None] * lf[:, :, None]), 0.0)
            + jnp.where(diff < 0, jnp.exp(jnp.maximum(-diff, 0.0)[None] * lb[:, :, None]), 0.0))
    bc = lambda v: jnp.broadcast_to(v[:, :, None], v.shape + (RET_D,))
    qwf = bc(jnp.exp((idx + 1.0)[None] * lf))
    kwf = bc(jnp.exp((chunk - 1.0 - idx)[None] * lf))
    qwb = bc(jnp.exp((chunk - idx)[None] * lb))
    kwb = bc(jnp.exp(idx[None] * lb))
    pidx = jnp.arange(PREFIX, dtype=F32)
    kwp = bc(jnp.exp((PREFIX - 1.0 - pidx)[None] * lf))
    gdec = jnp.concatenate([jnp.exp(chunk * lf[:, 0]), jnp.exp(chunk * lb[:, 0])])
    return dmat, qwf, qwb, kwf, kwb, kwp, gdec


def _divisor_tile(n, want):
    t = min(n, want)
    while n % t:
        t //= 2
    return t


def _encode(x, shared):
    b, s, d = x.shape
    n = b * s
    tm = _divisor_tile(s, 512)
    chunk = _divisor_tile(s, 256)
    cos_t, sin_t = _rope_tables(jnp.arange(s) + N_META)
    qr, kr, vr, gr, qa, ka, va = _proj(x, shared["norm_mix"], shared["w_in"], cos_t, sin_t, tm)
    o_r = _retention(qr, kr, vr, gr, shared["kr_pre"], shared["vr_pre"], *_decay_tables(
        shared["dec_f"], shared["dec_b"], chunk), shared["gn_w"], chunk)
    o_a = _attention(qa, ka, va, shared["k_meta"], shared["v_meta"], shared["bias"], shared["sink"])
    h_rows, xn_rows, xn_bf = _outproj(o_r, o_a, x, shared["w_out"], shared["norm_ffn"], tm)
    idx_t, gate_t = _route(xn_bf, shared["wq"], shared["pkeys"], _divisor_tile(s, ROUTE_TOKENS))
    idx_t = idx_t.transpose(1, 0, 2).reshape(PEER_SLOTS, n)
    gate_t = gate_t.transpose(1, 0, 2).reshape(PEER_SLOTS, n)
    top_half = (jnp.arange(PEER_SLOTS) % SUBLANES) < ROWS_PER_EXPERT
    tb = _divisor_tile(n, PEER_TOKENS)
    offs_g = _group_slots(idx_t * ROWS_PER_EXPERT + jnp.where(top_half, TABLE_PAD, 0)[:, None], tb)
    act_t = _peer_u(offs_g, xn_rows, shared["u_tbl"], tb)
    w_t = _peer_w(gate_t, act_t, _divisor_tile(n, 2048))
    y = _peer_v(offs_g, w_t, h_rows, shared["v_tbl"], shared["norm_final"], tb)
    return y.reshape(b, s, d)


def kernel(x_prompt, x_sample, meta_tokens, norm_mix_w, w_in, ret_decay_fwd, ret_decay_bwd, ret_gn_w,
           attn_sink, rel_bias, w_out, norm_ffn_w, peer_wq, peer_keys, peer_u, peer_v, norm_final_w):
    layer = 0
    shared = {
        "norm_mix": norm_mix_w[layer][None, :].astype(F32),
        "w_in": w_in[layer].astype(BF16),
        "dec_f": ret_decay_fwd[layer],
        "dec_b": ret_decay_bwd[layer],
        "gn_w": ret_gn_w[layer][None, :].astype(F32),
        "sink": attn_sink[layer].astype(F32),
        "bias": _attention_bias(rel_bias),
        "w_out": w_out[layer].astype(BF16),
        "norm_ffn": norm_ffn_w[layer][None, :].astype(F32),
        "wq": peer_wq[layer].astype(BF16),
        "pkeys": peer_keys[layer].reshape(2 * PEER_HEADS, PEER_NKEYS, PEER_NKEYS).astype(BF16),
        "u_tbl": _pack_table(peer_u[layer]),
        "v_tbl": _pack_table(peer_v[layer]),
        "norm_final": norm_final_w.reshape(SUBLANES, LANES).astype(F32),
    }
    prefix = jnp.concatenate([jnp.zeros((PREFIX - N_META, D_MODEL), x_prompt.dtype),
                              meta_tokens.astype(x_prompt.dtype)], axis=0)[None]
    cos_p, sin_p = _rope_tables(jnp.arange(PREFIX) - (PREFIX - N_META))
    _, kr_p, vr_p, _, _, ka_p, va_p = _proj(prefix, shared["norm_mix"], shared["w_in"], cos_p, sin_p, PREFIX)
    shared["kr_pre"], shared["vr_pre"] = kr_p[0], vr_p[0]
    pad_meta = lambda t: jnp.pad(t[0, PREFIX - N_META:], ((0, ATT_BLOCK - N_META), (0, 0)))
    shared["k_meta"], shared["v_meta"] = pad_meta(ka_p), pad_meta(va_p)
    return (_encode(x_prompt, shared), _encode(x_sample, shared))
```

```python
import functools
import math

import jax
import jax.numpy as jnp
from jax import lax
from jax.experimental import pallas as pl
from jax.experimental.pallas import tpu as pltpu

F32 = jnp.float32
BF16 = jnp.bfloat16

D_MODEL = 1024
N_META = 16
PREFIX = 128
RET_HEADS = 4
RET_D = 128
ATT_HEADS = 8
ATT_KV = 2
ATT_GROUP = ATT_HEADS // ATT_KV
ATT_HD = 64
ATT_BLOCK = 128
N_BUCKETS = 32
MAX_DISTANCE = 128
ROPE_BASE = 10000.0
PEER_HEADS = 8
PEER_NKEYS = 128
PEER_EXPERTS = PEER_NKEYS * PEER_NKEYS
PEER_TOPK = 16
PEER_SLOTS = PEER_HEADS * PEER_TOPK
RMS_EPS = 1e-6
GN_EPS = 1e-5
NEG = -1e30
IN_COLS = (0, 512, 1024, 1536, 2048, 2560, 2688, 2816)

LANES = 128
SUBLANES = 8
VMEM_LIMIT = 56 * 1024 * 1024
ROWS_PER_EXPERT = D_MODEL // 2 // LANES
TABLE_PAD = ROWS_PER_EXPERT
PEER_TOKENS = 512
ROUTE_TOKENS = 512
ROUTE_HEADS_PER_ITER = 2
SLOT_ORDER = (0, 2, 1, 3)
SLOT_GROUP = 1
V_TOKENS_PER_ITER = 8
NORM_TOKENS = 32


def _cparams(sem):
    return pltpu.CompilerParams(dimension_semantics=sem, vmem_limit_bytes=VMEM_LIMIT)


def _proj_kernel(x_ref, nw_ref, w_ref, cos_ref, sin_ref,
                 qr_ref, kr_ref, vr_ref, gr_ref, qa_ref, ka_ref, va_ref):
    x = x_ref[0]
    ms = jnp.mean(x * x, axis=-1, keepdims=True)
    xn = (x * lax.rsqrt(ms + RMS_EPS) * nw_ref[...]).astype(BF16)
    cosf = cos_ref[...]
    sinf = sin_ref[...]

    def mm(i):
        return jnp.dot(xn, w_ref[:, IN_COLS[i]:IN_COLS[i + 1]], preferred_element_type=F32)

    def rotary(t, scale):
        for h in range(RET_HEADS):
            th = t[:, h * RET_D:(h + 1) * RET_D]
            yield h, (th * cosf + pltpu.roll(th, RET_D // 2, 1) * sinf) * scale

    for h, r in rotary(mm(0), 1.0):
        qr_ref[0, :, h * RET_D:(h + 1) * RET_D] = r.astype(BF16)
    for h, r in rotary(mm(1), RET_D ** -0.5):
        kr_ref[0, :, h * RET_D:(h + 1) * RET_D] = r.astype(BF16)
    vr_ref[0] = mm(2).astype(BF16)
    gr_ref[0] = mm(3)
    qa_ref[0] = (mm(4) * (ATT_HD ** -0.5)).astype(BF16)
    ka_ref[0] = mm(5).astype(BF16)
    va_ref[0] = mm(6).astype(BF16)


def _proj(x, norm_w, w_in_bf, cos_t, sin_t, tm):
    b, s, d = x.shape
    widths = (512, 512, 512, 512, 512, 128, 128)
    dtypes = (BF16, BF16, BF16, F32, BF16, BF16, BF16)
    return pl.pallas_call(
        _proj_kernel,
        out_shape=[jax.ShapeDtypeStruct((b, s, w), dt) for w, dt in zip(widths, dtypes)],
        grid=(b, s // tm),
        in_specs=[
            pl.BlockSpec((1, tm, d), lambda i, j: (i, j, 0)),
            pl.BlockSpec((1, d), lambda i, j: (0, 0)),
            pl.BlockSpec(w_in_bf.shape, lambda i, j: (0, 0)),
            pl.BlockSpec((tm, LANES), lambda i, j: (j, 0)),
            pl.BlockSpec((tm, LANES), lambda i, j: (j, 0)),
        ],
        out_specs=[pl.BlockSpec((1, tm, w), lambda i, j: (i, j, 0)) for w in widths],
        compiler_params=_cparams(("arbitrary", "arbitrary")),
        name="proj",
    )(x, norm_w, w_in_bf, cos_t, sin_t)


def _ret_kernel(q_ref, k_ref, v_ref, g_ref, kpre_ref, vpre_ref, dmat_ref, qwf_ref, qwb_ref,
                kwf_ref, kwb_ref, kwp_ref, gdec_ref, gnw_ref, o_ref, sf_ref, sb_ref, sball_ref):
    ph = pl.program_id(1)
    c = pl.program_id(2)
    nc = pl.num_programs(2)
    contract0 = (((0,), (0,)), ((), ()))
    contract1 = (((1,), (1,)), ((), ()))

    def head(ref, h):
        return ref[0, :, h * RET_D:(h + 1) * RET_D]

    def kv_update(kh, vh, kw):
        kw = (kh.astype(F32) * kw).astype(BF16)
        return lax.dot_general(kw, vh, contract0, preferred_element_type=F32)

    @pl.when(ph == 0)
    def _backward_states():
        @pl.when(c == 0)
        def _():
            sb_ref[...] = jnp.zeros_like(sb_ref)

        j = nc - 1 - c
        for h in range(RET_HEADS):
            sball_ref[j, h] = sb_ref[h].astype(BF16)
            sb_ref[h] = gdec_ref[RET_HEADS + h] * sb_ref[h] + kv_update(head(k_ref, h), head(v_ref, h), kwb_ref[h])

    @pl.when(ph == 1)
    def _outputs():
        @pl.when(c == 0)
        def _():
            for h in range(RET_HEADS):
                kp = kpre_ref[:, h * RET_D:(h + 1) * RET_D]
                vp = vpre_ref[:, h * RET_D:(h + 1) * RET_D]
                sf_ref[h] = kv_update(kp, vp, kwp_ref[h])

        for h in range(RET_HEADS):
            qh, kh, vh = head(q_ref, h), head(k_ref, h), head(v_ref, h)
            s = lax.dot_general(qh, kh, contract1, preferred_element_type=F32) * dmat_ref[h]
            o = jnp.dot(s.astype(BF16), vh, preferred_element_type=F32)
            o = o + jnp.dot(qh, sf_ref[h].astype(BF16), preferred_element_type=F32) * qwf_ref[h]
            o = o + jnp.dot(qh, sball_ref[c, h], preferred_element_type=F32) * qwb_ref[h]
            mu = jnp.mean(o, axis=-1, keepdims=True)
            oc = o - mu
            var = jnp.mean(oc * oc, axis=-1, keepdims=True)
            on = oc * lax.rsqrt(var + GN_EPS) * gnw_ref[:, h * RET_D:(h + 1) * RET_D]
            g = head(g_ref, h)
            o_ref[0, :, h * RET_D:(h + 1) * RET_D] = (g * jax.nn.sigmoid(g) * on).astype(BF16)
            sf_ref[h] = gdec_ref[h] * sf_ref[h] + kv_update(kh, vh, kwf_ref[h])


def _retention(qr, kr, vr, gr, kpre, vpre, dmat, qwf, qwb, kwf, kwb, kwp, gdec, gnw, chunk):
    b, s, w = qr.shape
    nc = s // chunk

    def seq_map(i, ph, c):
        return (i, jnp.where(ph == 0, nc - 1 - c, c), 0)

    def fwd_only(i, ph, c):
        return (i, c * ph, 0)

    const2 = lambda i, ph, c: (0, 0)
    const3 = lambda i, ph, c: (0, 0, 0)
    return pl.pallas_call(
        _ret_kernel,
        out_shape=jax.ShapeDtypeStruct((b, s, w), BF16),
        grid=(b, 2, nc),
        in_specs=[
            pl.BlockSpec((1, chunk, w), fwd_only),
            pl.BlockSpec((1, chunk, w), seq_map),
            pl.BlockSpec((1, chunk, w), seq_map),
            pl.BlockSpec((1, chunk, w), fwd_only),
            pl.BlockSpec(kpre.shape, const2),
            pl.BlockSpec(vpre.shape, const2),
            pl.BlockSpec(dmat.shape, const3),
            pl.BlockSpec(qwf.shape, const3),
            pl.BlockSpec(qwb.shape, const3),
            pl.BlockSpec(kwf.shape, const3),
            pl.BlockSpec(kwb.shape, const3),
            pl.BlockSpec(kwp.shape, const3),
            pl.BlockSpec(memory_space=pltpu.SMEM),
            pl.BlockSpec(gnw.shape, const2),
        ],
        out_specs=pl.BlockSpec((1, chunk, w), fwd_only),
        scratch_shapes=[
            pltpu.VMEM((RET_HEADS, RET_D, RET_D), F32),
            pltpu.VMEM((RET_HEADS, RET_D, RET_D), F32),
            pltpu.VMEM((nc, RET_HEADS, RET_D, RET_D), BF16),
        ],
        compiler_params=_cparams(("arbitrary", "arbitrary", "arbitrary")),
        name="retention",
    )(qr, kr, vr, gr, kpre, vpre, dmat, qwf, qwb, kwf, kwb, kwp, gdec, gnw)


def _attn_kernel(q_ref, kp_ref, kc_ref, kn_ref, vp_ref, vc_ref, vn_ref, km_ref, vm_ref,
                 bias_ref, sink_ref, o_ref):
    j = pl.program_id(1)
    nb = pl.num_programs(1)
    lane = lax.broadcasted_iota(jnp.int32, (1, 4 * ATT_BLOCK), 1)
    first = jnp.logical_and(lane < ATT_BLOCK, j == 0)
    last = jnp.logical_and(jnp.logical_and(lane >= 2 * ATT_BLOCK, lane < 3 * ATT_BLOCK), j == nb - 1)
    pen = jnp.where(jnp.logical_or(first, last), NEG, 0.0).astype(F32)
    contract1 = (((1,), (1,)), ((), ()))
    rows = lax.broadcasted_iota(jnp.int32, (ATT_GROUP * ATT_BLOCK, 1), 0)
    for g in range(ATT_KV):
        sl = slice(g * ATT_HD, (g + 1) * ATT_HD)
        kall = jnp.concatenate([kp_ref[0][:, sl], kc_ref[0][:, sl], kn_ref[0][:, sl], km_ref[:, sl]], axis=0)
        vall = jnp.concatenate([vp_ref[0][:, sl], vc_ref[0][:, sl], vn_ref[0][:, sl], vm_ref[:, sl]], axis=0)
        heads = range(g * ATT_GROUP, (g + 1) * ATT_GROUP)
        q = jnp.concatenate([q_ref[0, :, h * ATT_HD:(h + 1) * ATT_HD] for h in heads], axis=0)
        bias = bias_ref[0, g * ATT_GROUP:(g + 1) * ATT_GROUP].reshape(ATT_GROUP * ATT_BLOCK, 4 * ATT_BLOCK)
        snk = jnp.zeros((ATT_GROUP * ATT_BLOCK, 1), F32)
        for hh, h in enumerate(heads):
            snk = jnp.where(rows // ATT_BLOCK == hh, sink_ref[h], snk)
        s = lax.dot_general(q, kall, contract1, preferred_element_type=F32) + bias + pen
        m = jnp.maximum(jnp.max(s, axis=-1, keepdims=True), snk)
        p = jnp.exp(s - m)
        den = jnp.sum(p, axis=-1, keepdims=True) + jnp.exp(snk - m)
        o = jnp.dot(p.astype(BF16), vall, preferred_element_type=F32) / den
        for hh in range(0, ATT_GROUP, 2):
            lanes = slice((g * ATT_GROUP + hh) * ATT_HD, (g * ATT_GROUP + hh + 2) * ATT_HD)
            o_ref[0, :, lanes] = jnp.concatenate(
                [o[hh * ATT_BLOCK:(hh + 1) * ATT_BLOCK], o[(hh + 1) * ATT_BLOCK:(hh + 2) * ATT_BLOCK]],
                axis=-1).astype(BF16)


def _attention(qa, ka, va, kmeta, vmeta, bias, sink):
    b, s, w = qa.shape
    nb = s // ATT_BLOCK
    kvw = ka.shape[-1]
    prev = lambda i, j: (i, jnp.maximum(j - 1, 0), 0)
    cur = lambda i, j: (i, j, 0)
    nxt = lambda i, j: (i, jnp.minimum(j + 1, nb - 1), 0)
    const2 = lambda i, j: (0, 0)
    kv = lambda m: pl.BlockSpec((1, ATT_BLOCK, kvw), m)
    return pl.pallas_call(
        _attn_kernel,
        out_shape=jax.ShapeDtypeStruct((b, s, w), BF16),
        grid=(b, nb),
        in_specs=[
            pl.BlockSpec((1, ATT_BLOCK, w), cur),
            kv(prev), kv(cur), kv(nxt), kv(prev), kv(cur), kv(nxt),
            pl.BlockSpec(kmeta.shape, const2),
            pl.BlockSpec(vmeta.shape, const2),
            pl.BlockSpec((1,) + bias.shape[1:], lambda i, j: (jnp.minimum(j, 1), 0, 0, 0)),
            pl.BlockSpec(memory_space=pltpu.SMEM),
        ],
        out_specs=pl.BlockSpec((1, ATT_BLOCK, w), cur),
        compiler_params=_cparams(("arbitrary", "arbitrary")),
        name="attention",
    )(qa, ka, ka, ka, va, va, va, kmeta, vmeta, bias, sink)


def _store_token_rows(ref, v):
    for c in range(D_MODEL // LANES):
        ref[pl.ds(c, v.shape[0], stride=SUBLANES), :] = v[:, c * LANES:(c + 1) * LANES]


def _outproj_kernel(or_ref, oa_ref, x_ref, w_ref, nw_ref, h_ref, xn_ref, xnb_ref):
    half = or_ref.shape[-1]
    h = x_ref[0]
    h = h + jnp.dot(or_ref[0], w_ref[:half], preferred_element_type=F32)
    h = h + jnp.dot(oa_ref[0], w_ref[half:], preferred_element_type=F32)
    _store_token_rows(h_ref, h)
    ms = jnp.mean(h * h, axis=-1, keepdims=True)
    xn = h * lax.rsqrt(ms + RMS_EPS) * nw_ref[...]
    _store_token_rows(xn_ref, xn)
    xnb_ref[0] = xn.astype(BF16)


def _outproj(o_r, o_a, x, w_out_bf, norm_w, tm):
    b, s, d = x.shape
    half = o_r.shape[-1]
    row = lambda i, j: (i, j, 0)
    const2 = lambda i, j: (0, 0)
    token_rows = pl.BlockSpec((tm * SUBLANES, LANES), lambda i, j: (i * (s // tm) + j, 0))
    return pl.pallas_call(
        _outproj_kernel,
        out_shape=[jax.ShapeDtypeStruct((b * s * SUBLANES, LANES), F32)] * 2 + [jax.ShapeDtypeStruct((b, s, d), BF16)],
        grid=(b, s // tm),
        in_specs=[
            pl.BlockSpec((1, tm, half), row),
            pl.BlockSpec((1, tm, half), row),
            pl.BlockSpec((1, tm, d), row),
            pl.BlockSpec(w_out_bf.shape, const2),
            pl.BlockSpec((1, d), const2),
        ],
        out_specs=[token_rows, token_rows, pl.BlockSpec((1, tm, d), row)],
        compiler_params=_cparams(("arbitrary", "arbitrary")),
        name="outproj",
    )(o_r, o_a, x, w_out_bf, norm_w)


def _top16_rows(s, iota):
    nrows = float(s.shape[0])
    vals, ids = [], []
    for _ in range(PEER_TOPK):
        m = jnp.max(s, axis=0, keepdims=True)
        am = jnp.min(jnp.where(s == m, iota, nrows), axis=0, keepdims=True)
        vals.append(m)
        ids.append(am)
        s = jnp.where(iota == am, -jnp.inf, s)
    return jnp.concatenate(vals, axis=0), jnp.concatenate(ids, axis=0)


_PAIR_GROUPS = ((0, 0, 8), (0, 8, 8), (1, 0, 8), (2, 0, 5), (3, 0, 4), (4, 0, 3), (5, 0, 2), (6, 0, 2), (7, 0, 2))


def _route_kernel(x_ref, wq_ref, pk_ref, idx_ref, gate_ref, sc_ref):
    tm = x_ref.shape[1]
    q = jnp.dot(x_ref[0], wq_ref[...], preferred_element_type=F32).astype(BF16)
    contract1 = (((1,), (1,)), ((), ()))
    for hp in range(2 * PEER_HEADS):
        sc_ref[hp] = lax.dot_general(pk_ref[hp], q[:, hp * PEER_NKEYS:(hp + 1) * PEER_NKEYS], contract1,
                                     preferred_element_type=F32)
    iota = lax.broadcasted_iota(jnp.int32, (PEER_NKEYS, LANES), 0).astype(F32)
    sub = lax.broadcasted_iota(jnp.int32, (SUBLANES, LANES), 0)
    subf = sub.astype(F32)
    far = float(PEER_TOPK * PEER_TOPK)

    def one_head(h):
        for lt in range(tm // LANES):
            lanes = slice(lt * LANES, (lt + 1) * LANES)
            s1, i1 = _top16_rows(sc_ref[2 * h, :, lanes], iota)
            s2, i2 = _top16_rows(sc_ref[2 * h + 1, :, lanes], iota)
            i1 = i1 * float(PEER_NKEYS)
            sc, ex, pos = [], [], []
            for i, j0, cnt in _PAIR_GROUPS:
                ok = sub < cnt
                sc.append(jnp.where(ok, s1[i:i + 1] + s2[j0:j0 + SUBLANES], -jnp.inf))
                ex.append(i1[i:i + 1] + i2[j0:j0 + SUBLANES])
                pos.append(jnp.where(ok, float(i * PEER_TOPK + j0) + subf, far))
            sc.append(s1[SUBLANES:] + s2[0:1])
            ex.append(i1[SUBLANES:] + i2[0:1])
            pos.append((float(SUBLANES) + subf) * float(PEER_TOPK))
            sc = jnp.concatenate(sc, axis=0)
            ex = jnp.concatenate(ex, axis=0)
            pos = jnp.concatenate(pos, axis=0)
            top, experts = [], []
            for _ in range(PEER_TOPK):
                m = jnp.max(sc, axis=0, keepdims=True)
                pm = jnp.min(jnp.where(sc == m, pos, far), axis=0, keepdims=True)
                hit = pos == pm
                experts.append(jnp.max(jnp.where(hit, ex, -1.0), axis=0, keepdims=True))
                top.append(m)
                sc = jnp.where(hit, -jnp.inf, sc)
            top = jnp.concatenate(top, axis=0)
            e = jnp.exp(top - top[0:1])
            rows = pl.ds(pl.multiple_of(h * PEER_TOPK, PEER_TOPK), PEER_TOPK)
            gate_ref[0, rows, lanes] = e / jnp.sum(e, axis=0, keepdims=True)
            idx_ref[0, rows, lanes] = jnp.concatenate(experts, axis=0).astype(jnp.int32)

    def heads_step(i, carry):
        for r in range(ROUTE_HEADS_PER_ITER):
            one_head(i * ROUTE_HEADS_PER_ITER + r)
        return carry

    lax.fori_loop(0, PEER_HEADS // ROUTE_HEADS_PER_ITER, heads_step, 0)


def _route(xn, wq_bf, pk_bf, tm):
    b, s, d = xn.shape
    const2 = lambda i, j: (0, 0)
    slot = lambda i, j: (i, 0, j)
    return pl.pallas_call(
        _route_kernel,
        out_shape=[jax.ShapeDtypeStruct((b, PEER_SLOTS, s), jnp.int32),
                   jax.ShapeDtypeStruct((b, PEER_SLOTS, s), F32)],
        grid=(b, s // tm),
        in_specs=[
            pl.BlockSpec((1, tm, d), lambda i, j: (i, j, 0)),
            pl.BlockSpec(wq_bf.shape, const2),
            pl.BlockSpec(pk_bf.shape, lambda i, j: (0, 0, 0)),
        ],
        out_specs=[pl.BlockSpec((1, PEER_SLOTS, tm), slot)] * 2,
        scratch_shapes=[pltpu.VMEM((2 * PEER_HEADS, PEER_NKEYS, tm), F32)],
        compiler_params=_cparams(("arbitrary", "arbitrary")),
        name="route",
    )(xn, wq_bf, pk_bf)


def _pair_slots():
    out = []
    for g in range(PEER_SLOTS // SUBLANES):
        for c in range(4):
            top = SUBLANES * g + SLOT_ORDER[c]
            out.append((top, top + 4))
    return out


def _gather_words(tbl_ref, top_off, bot_off, upper):
    ra = tbl_ref[pl.ds(pl.multiple_of(top_off, ROWS_PER_EXPERT), SUBLANES), :]
    rb = tbl_ref[pl.ds(pl.multiple_of(bot_off, ROWS_PER_EXPERT), SUBLANES), :]
    return jnp.where(upper, ra, rb)


def _unpack_words(words):
    lo = lax.bitcast_convert_type(words << 16, F32)
    hi = lax.bitcast_convert_type(words & jnp.uint32(0xFFFF0000), F32)
    return lo, hi


def _gather_pair(tbl_ref, top_off, bot_off, upper):
    return _unpack_words(_gather_words(tbl_ref, top_off, bot_off, upper))


def _merge_sublanes(a, b, h, sub):
    keep = (sub & h) == 0
    return jnp.where(keep, a + pltpu.roll(a, SUBLANES - h, 0), b + pltpu.roll(b, h, 0))


def _slot_specs(tb):
    return [pl.BlockSpec((None, None, SLOT_GROUP * tb), lambda i, j=j: (j, 0, i), memory_space=pltpu.SMEM,
                         pipeline_mode=pl.Buffered(1)) for j in range(PEER_SLOTS // SLOT_GROUP)]


def _group_slots(a, tb):
    n = a.shape[-1]
    a = a.reshape(PEER_SLOTS // SLOT_GROUP, SLOT_GROUP, n // tb, tb).transpose(0, 2, 1, 3)
    return a.reshape(PEER_SLOTS // SLOT_GROUP, 1, n * SLOT_GROUP)


def _slot_reader(refs, t, tb):
    at = [t + g * tb for g in range(SLOT_GROUP)]
    return lambda k: refs[k // SLOT_GROUP][at[k % SLOT_GROUP]]


def _peer_u_kernel(*refs):
    ngroups = PEER_SLOTS // SLOT_GROUP
    offs = refs[:ngroups]
    x_ref, tbl_ref, o_ref, z_ref, sel_ref = refs[ngroups:]
    tb = x_ref.shape[0] // SUBLANES
    sub = lax.broadcasted_iota(jnp.int32, (SUBLANES, LANES), 0)
    upper = sub < ROWS_PER_EXPERT
    pairs = _pair_slots()
    lane_pair = lax.broadcasted_iota(jnp.int32, (PEER_SLOTS, LANES), 1) >> 1

    row = lax.broadcasted_iota(jnp.int32, (2 * LANES, LANES), 0)
    col = lax.broadcasted_iota(jnp.int32, (2 * LANES, LANES), 1)
    sel_ref[...] = jnp.where((col & 1) == (row >= LANES).astype(jnp.int32), 1.0, 0.0).astype(BF16)
    o_ref[...] = jnp.zeros_like(o_ref)

    group_rows = 4 * PEER_SLOTS
    z_ref[...] = jnp.zeros_like(z_ref)

    def lane_partials(gi, zbase):
        for j in range(SUBLANES):
            t = gi * SUBLANES + j
            x = x_ref[pl.ds(pl.multiple_of(t * SUBLANES, SUBLANES), SUBLANES), :]
            swapped = pltpu.roll(x, ROWS_PER_EXPERT, 0)
            xa = jnp.where(upper, x, swapped)
            xb = jnp.where(upper, swapped, x)
            xa_bits = lax.bitcast_convert_type(xa.astype(BF16).astype(F32), jnp.uint32)
            xb_bits = lax.bitcast_convert_type(xb.astype(BF16).astype(F32), jnp.uint32)
            xw = pltpu.bitcast((xb_bits & jnp.uint32(0xFFFF0000)) | (xa_bits >> 16), BF16)
            off = _slot_reader(offs, t, tb)
            zs = []
            for g in range(PEER_SLOTS // SUBLANES):
                ps = []
                for c in range(4):
                    top, bot = pairs[4 * g + c]
                    words = _gather_words(tbl_ref, off(top), off(bot), upper)
                    lo, hi = _unpack_words(pltpu.bitcast(pltpu.bitcast(words, BF16) * xw, jnp.uint32))
                    ps.append(lo + hi)
                r0 = _merge_sublanes(ps[0], ps[1], 2, sub)
                r1 = _merge_sublanes(ps[2], ps[3], 2, sub)
                zs.append(_merge_sublanes(r0, r1, 1, sub))
            rows = zbase + (j // 2) * PEER_SLOTS
            cols = slice((j % 2) * LANES, (j % 2 + 1) * LANES)
            for m in range(len(zs) // 2):
                tile = jnp.concatenate([zs[2 * m], zs[2 * m + 1]], axis=0).astype(BF16)
                z_ref[pl.ds(pl.multiple_of(rows + 2 * SUBLANES * m, 2 * SUBLANES), 2 * SUBLANES), cols] = tile

    def lane_sums(gi, zbase):
        sums = jnp.dot(z_ref[pl.ds(zbase, group_rows), :], sel_ref[...], preferred_element_type=F32)
        block = pl.ds(pl.multiple_of((gi // (LANES // SUBLANES)) * LANES, LANES), LANES)
        first_pair = (gi % (LANES // SUBLANES)) * 4
        acc = o_ref[:, block]
        for p in range(4):
            acc = jnp.where(lane_pair == first_pair + p, sums[p * PEER_SLOTS:(p + 1) * PEER_SLOTS], acc)
        o_ref[:, block] = acc

    def zbase_of(gi):
        return pl.multiple_of((gi % 2) * group_rows, group_rows)

    def step(gi, carry):
        prev = jnp.maximum(gi - 1, 0)
        lane_sums(prev, zbase_of(gi + 1))
        lane_partials(gi, zbase_of(gi))
        return carry

    ngroups_tokens = tb // SUBLANES
    lax.fori_loop(0, ngroups_tokens, step, 0)
    lane_sums(ngroups_tokens - 1, (ngroups_tokens - 1) % 2 * group_rows)


def _peer_u(offs_g, x_rows, tbl, tb):
    n = x_rows.shape[0] // SUBLANES
    assert tb % LANES == 0
    return pl.pallas_call(
        _peer_u_kernel,
        out_shape=jax.ShapeDtypeStruct((PEER_SLOTS, n), F32),
        grid=(n // tb,),
        in_specs=_slot_specs(tb) + [
            pl.BlockSpec((tb * SUBLANES, LANES), lambda i: (i, 0)),
            pl.BlockSpec(tbl.shape, lambda i: (0, 0), pipeline_mode=pl.Buffered(1)),
        ],
        out_specs=pl.BlockSpec((PEER_SLOTS, tb), lambda i: (0, i)),
        scratch_shapes=[
            pltpu.VMEM((2 * 4 * PEER_SLOTS, 2 * LANES), BF16),
            pltpu.VMEM((2 * LANES, LANES), BF16),
        ],
        compiler_params=_cparams(("arbitrary",)),
        name="peer_u",
    )(*([offs_g] * (PEER_SLOTS // SLOT_GROUP)), x_rows, tbl)


def _peer_w_kernel(gate_ref, act_ref, w_ref):
    a = act_ref[...]
    w_ref[...] = gate_ref[...] * (0.5 * a * (1.0 + lax.erf(a * (2.0 ** -0.5))))


def _peer_w(gate_t, act_t, tm):
    n = gate_t.shape[-1]
    spec = pl.BlockSpec((PEER_SLOTS, tm), lambda i: (0, i))
    return pl.pallas_call(
        _peer_w_kernel,
        out_shape=jax.ShapeDtypeStruct(gate_t.shape, F32),
        grid=(n // tm,),
        in_specs=[spec, spec],
        out_specs=spec,
        compiler_params=_cparams(("arbitrary",)),
        name="peer_w",
    )(gate_t, act_t)


def _peer_v_kernel(*refs):
    ngroups = PEER_SLOTS // SLOT_GROUP
    offs = refs[:ngroups]
    w_ref, h_ref, tbl_ref, nw_ref, o_ref, acc_ref, whi_ref, wlo_ref, repa_ref, repb_ref, ones_ref = refs[ngroups:]
    tb = h_ref.shape[0] // SUBLANES
    sub = lax.broadcasted_iota(jnp.int32, (SUBLANES, LANES), 0)
    upper = sub < ROWS_PER_EXPERT
    pairs = _pair_slots()
    lane = lax.broadcasted_iota(jnp.int32, (PEER_SLOTS, LANES), 1)

    w = w_ref[...]
    whi = w.astype(BF16).astype(F32)
    whi_ref[...] = whi
    wlo_ref[...] = w - whi
    row = lax.broadcasted_iota(jnp.int32, (2 * LANES, 2 * LANES), 0)
    col = lax.broadcasted_iota(jnp.int32, (2 * LANES, 2 * LANES), 1)
    ones_ref[...] = jnp.where((row >= LANES) == (col >= LANES), 1.0, 0.0).astype(BF16)

    def replicate(t0, rep_ref):
        for r in range(V_TOKENS_PER_ITER // 2):
            t = t0 + 2 * r
            block = pl.ds(pl.multiple_of((t // LANES) * LANES, LANES), LANES)
            la = t % LANES
            out = None
            for part_ref in (whi_ref, wlo_ref):
                part = part_ref[:, block]
                lhs = jnp.concatenate([jnp.where(lane == la, part, 0.0), jnp.where(lane == la + 1, part, 0.0)],
                                      axis=1).astype(BF16)
                prod = jnp.dot(lhs, ones_ref[...], preferred_element_type=F32)
                out = prod if out is None else out + prod
            rep_ref[r] = out

    def token(t, rep_ref, slot, half):
        off = _slot_reader(offs, t, tb)
        lanes = slice(half * LANES, (half + 1) * LANES)
        acc = [jnp.zeros((SUBLANES, LANES), F32) for _ in range(4)]
        for q, (top, bot) in enumerate(pairs):
            lo, hi = _gather_pair(tbl_ref, off(top), off(bot), upper)
            wtop = jnp.broadcast_to(rep_ref[slot, top:top + 1, lanes], (SUBLANES, LANES))
            wbot = jnp.broadcast_to(rep_ref[slot, bot:bot + 1, lanes], (SUBLANES, LANES))
            wv = jnp.where(upper, wtop, wbot)
            k = 2 * (q % 2)
            acc[k] = acc[k] + lo * wv
            acc[k + 1] = acc[k + 1] + hi * wv
        lo = acc[0] + acc[2]
        hi = acc[1] + acc[3]
        lo = lo + pltpu.roll(lo, ROWS_PER_EXPERT, 0)
        hi = hi + pltpu.roll(hi, ROWS_PER_EXPERT, 0)
        rows = pl.ds(pl.multiple_of(t * SUBLANES, SUBLANES), SUBLANES)
        acc_ref[rows, :] = h_ref[rows, :] + jnp.where(upper, lo, hi)

    def tokens(t0, rep_ref):
        for r in range(V_TOKENS_PER_ITER):
            token(t0 + r, rep_ref, r // 2, r % 2)

    replicate(0, repa_ref)

    def step(i, carry):
        t0 = i * 2 * V_TOKENS_PER_ITER
        replicate(t0 + V_TOKENS_PER_ITER, repb_ref)
        tokens(t0, repa_ref)
        replicate(jnp.minimum(t0 + 2 * V_TOKENS_PER_ITER, tb - V_TOKENS_PER_ITER), repa_ref)
        tokens(t0 + V_TOKENS_PER_ITER, repb_ref)
        return carry

    lax.fori_loop(0, tb // (2 * V_TOKENS_PER_ITER), step, 0)

    def norm_step(i, carry):
        rows = pl.ds(pl.multiple_of(i * NORM_TOKENS * SUBLANES, NORM_TOKENS * SUBLANES), NORM_TOKENS * SUBLANES)
        h = acc_ref[rows, :].reshape(NORM_TOKENS, SUBLANES, LANES)
        ss = jnp.sum(jnp.sum(h * h, axis=2, keepdims=True), axis=1, keepdims=True)
        y = h * lax.rsqrt(ss * (1.0 / D_MODEL) + RMS_EPS) * nw_ref[...][None]
        acc_ref[rows, :] = y.reshape(NORM_TOKENS * SUBLANES, LANES)
        return carry

    lax.fori_loop(0, tb // NORM_TOKENS, norm_step, 0)
    for c in range(D_MODEL // LANES):
        o_ref[:, c * LANES:(c + 1) * LANES] = acc_ref[pl.ds(c, tb, stride=SUBLANES), :]


def _peer_v(offs_g, w_t, h_rows, tbl, norm_rows, tb):
    n = h_rows.shape[0] // SUBLANES
    assert tb % LANES == 0 and V_TOKENS_PER_ITER % 2 == 0 and tb % (2 * V_TOKENS_PER_ITER) == 0 and tb % NORM_TOKENS == 0
    rows = pl.BlockSpec((tb * SUBLANES, LANES), lambda i: (i, 0))
    return pl.pallas_call(
        _peer_v_kernel,
        out_shape=jax.ShapeDtypeStruct((n, D_MODEL), F32),
        grid=(n // tb,),
        in_specs=_slot_specs(tb) + [
            pl.BlockSpec((PEER_SLOTS, tb), lambda i: (0, i)),
            rows,
            pl.BlockSpec(tbl.shape, lambda i: (0, 0), pipeline_mode=pl.Buffered(1)),
            pl.BlockSpec((SUBLANES, LANES), lambda i: (0, 0)),
        ],
        out_specs=pl.BlockSpec((tb, D_MODEL), lambda i: (i, 0)),
        scratch_shapes=[
            pltpu.VMEM((tb * SUBLANES, LANES), F32),
            pltpu.VMEM((PEER_SLOTS, tb), F32),
            pltpu.VMEM((PEER_SLOTS, tb), F32),
            pltpu.VMEM((V_TOKENS_PER_ITER // 2, PEER_SLOTS, 2 * LANES), F32),
            pltpu.VMEM((V_TOKENS_PER_ITER // 2, PEER_SLOTS, 2 * LANES), F32),
            pltpu.VMEM((2 * LANES, 2 * LANES), BF16),
        ],
        compiler_params=_cparams(("arbitrary",)),
        name="peer_v",
    )(*([offs_g] * (PEER_SLOTS // SLOT_GROUP)), w_t, h_rows, tbl, norm_rows)


def _pack_table(t):
    half = D_MODEL // 2
    bits = lax.bitcast_convert_type(t.astype(BF16), jnp.uint16).astype(jnp.uint32)
    words = bits[:, :half] | (bits[:, half:] << 16)
    words = words.reshape(t.shape[0] * ROWS_PER_EXPERT, LANES)
    return jnp.pad(words, ((TABLE_PAD, TABLE_PAD), (0, 0)))


def _t5_bucket(rel):
    half = N_BUCKETS // 2
    exact = half // 2
    n = jnp.abs(rel)
    large = exact + (jnp.log(jnp.maximum(n, 1).astype(F32) / exact)
                     / math.log(MAX_DISTANCE / exact) * (half - exact)).astype(jnp.int32)
    large = jnp.minimum(large, half - 1)
    return jnp.where(rel > 0, half, 0) + jnp.where(n < exact, n, large)


def _attention_bias(rel_bias):
    rb = rel_bias.astype(F32)
    i = jnp.arange(ATT_BLOCK)
    jb = jnp.arange(3 * ATT_BLOCK)
    rel = (jb[None, :] - ATT_BLOCK) - i[:, None]
    band = jnp.where((jnp.abs(rel) <= ATT_BLOCK)[..., None], rb[_t5_bucket(rel)], NEG)
    mpos = PREFIX - N_META + jnp.arange(N_META)
    variants = []
    for blk in (0, 1):
        qpos = PREFIX + blk * ATT_BLOCK + i
        meta = rb[_t5_bucket(mpos[None, :] - qpos[:, None])]
        fill = jnp.full((ATT_BLOCK, ATT_BLOCK - N_META, ATT_HEADS), NEG, F32)
        variants.append(jnp.concatenate([band, meta, fill], axis=1).transpose(2, 0, 1))
    return jnp.stack(variants)


def _rope_tables(pos):
    half = RET_D // 2
    inv = ROPE_BASE ** (-jnp.arange(half, dtype=F32) / half)
    ang = pos.astype(F32)[:, None] * inv[None, :]
    cos, sin = jnp.cos(ang), jnp.sin(ang)
    return jnp.concatenate([cos, cos], axis=1), jnp.concatenate([-sin, sin], axis=1)


def _decay_tables(dec_f, dec_b, chunk):
    lf = jax.nn.log_sigmoid(dec_f.astype(F32))[:, None]
    lb = jax.nn.log_sigmoid(dec_b.astype(F32))[:, None]
    idx = jnp.arange(chunk, dtype=F32)
    diff = idx[:, None] - idx[None, :]
    dmat = (jnp.where(diff >= 0, jnp.exp(jnp.maximum(diff, 0.0)[None] * lf[:, :, None]), 0.0)
            + jnp.where(diff < 0, jnp.exp(jnp.maximum(-diff, 0.0)[None] * lb[:, :, None]), 0.0))
    bc = lambda v: jnp.broadcast_to(v[:, :, None], v.shape + (RET_D,))
    qwf = bc(jnp.exp((idx + 1.0)[None] * lf))
    kwf = bc(jnp.exp((chunk - 1.0 - idx)[None] * lf))
    qwb = bc(jnp.exp((chunk - idx)[None] * lb))
    kwb = bc(jnp.exp(idx[None] * lb))
    pidx = jnp.arange(PREFIX, dtype=F32)
    kwp = bc(jnp.exp((PREFIX - 1.0 - pidx)[None] * lf))
    gdec = jnp.concatenate([jnp.exp(chunk * lf[:, 0]), jnp.exp(chunk * lb[:, 0])])
    return dmat, qwf, qwb, kwf, kwb, kwp, gdec


def _divisor_tile(n, want):
    t = min(n, want)
    while n % t:
        t //= 2
    return t


def _encode(x, shared):
    b, s, d = x.shape
    n = b * s
    tm = _divisor_tile(s, 512)
    chunk = _divisor_tile(s, 256)
    cos_t, sin_t = _rope_tables(jnp.arange(s) + N_META)
    qr, kr, vr, gr, qa, ka, va = _proj(x, shared["norm_mix"], shared["w_in"], cos_t, sin_t, tm)
    o_r = _retention(qr, kr, vr, gr, shared["kr_pre"], shared["vr_pre"], *_decay_tables(
        shared["dec_f"], shared["dec_b"], chunk), shared["gn_w"], chunk)
    o_a = _attention(qa, ka, va, shared["k_meta"], shared["v_meta"], shared["bias"], shared["sink"])
    h_rows, xn_rows, xn_bf = _outproj(o_r, o_a, x, shared["w_out"], shared["norm_ffn"], tm)
    idx_t, gate_t = _route(xn_bf, shared["wq"], shared["pkeys"], _divisor_tile(s, ROUTE_TOKENS))
    idx_t = idx_t.transpose(1, 0, 2).reshape(PEER_SLOTS, n)
    gate_t = gate_t.transpose(1, 0, 2).reshape(PEER_SLOTS, n)
    top_half = (jnp.arange(PEER_SLOTS) % SUBLANES) < ROWS_PER_EXPERT
    tb = _divisor_tile(n, PEER_TOKENS)
    offs_g = _group_slots(idx_t * ROWS_PER_EXPERT + jnp.where(top_half, TABLE_PAD, 0)[:, None], tb)
    act_t = _peer_u(offs_g, xn_rows, shared["u_tbl"], tb)
    w_t = _peer_w(gate_t, act_t, _divisor_tile(n, 2048))
    y = _peer_v(offs_g, w_t, h_rows, shared["v_tbl"], shared["norm_final"], tb)
    return y.reshape(b, s, d)


def kernel(x_prompt, x_sample, meta_tokens, norm_mix_w, w_in, ret_decay_fwd, ret_decay_bwd, ret_gn_w,
           attn_sink, rel_bias, w_out, norm_ffn_w, peer_wq, peer_keys, peer_u, peer_v, norm_final_w):
    layer = 0
    shared = {
        "norm_mix": norm_mix_w[layer][None, :].astype(F32),
        "w_in": w_in[layer].astype(BF16),
        "dec_f": ret_decay_fwd[layer],
        "dec_b": ret_decay_bwd[layer],
        "gn_w": ret_gn_w[layer][None, :].astype(F32),
        "sink": attn_sink[layer].astype(F32),
        "bias": _attention_bias(rel_bias),
        "w_out": w_out[layer].astype(BF16),
        "norm_ffn": norm_ffn_w[layer][None, :].astype(F32),
        "wq": peer_wq[layer].astype(BF16),
        "pkeys": peer_keys[layer].reshape(2 * PEER_HEADS, PEER_NKEYS, PEER_NKEYS).astype(BF16),
        "u_tbl": _pack_table(peer_u[layer]),
        "v_tbl": _pack_table(peer_v[layer]),
        "norm_final": norm_final_w.reshape(SUBLANES, LANES).astype(F32),
    }
    prefix = jnp.concatenate([jnp.zeros((PREFIX - N_META, D_MODEL), x_prompt.dtype),
                              meta_tokens.astype(x_prompt.dtype)], axis=0)[None]
    cos_p, sin_p = _rope_tables(jnp.arange(PREFIX) - (PREFIX - N_META))
    _, kr_p, vr_p, _, _, ka_p, va_p = _proj(prefix, shared["norm_mix"], shared["w_in"], cos_p, sin_p, PREFIX)
    shared["kr_pre"], shared["vr_pre"] = kr_p[0], vr_p[0]
    pad_meta = lambda t: jnp.pad(t[0, PREFIX - N_META:], ((0, ATT_BLOCK - N_META), (0, 0)))
    shared["k_meta"], shared["v_meta"] = pad_meta(ka_p), pad_meta(va_p)
    return (_encode(x_prompt, shared), _encode(x_sample, shared))
```

```python
import functools
import math

import jax
import jax.numpy as jnp
from jax import lax
from jax.experimental import pallas as pl
from jax.experimental.pallas import tpu as pltpu

F32 = jnp.float32
BF16 = jnp.bfloat16

D_MODEL = 1024
N_META = 16
PREFIX = 128
RET_HEADS = 4
RET_D = 128
ATT_HEADS = 8
ATT_KV = 2
ATT_GROUP = ATT_HEADS // ATT_KV
ATT_HD = 64
ATT_BLOCK = 128
N_BUCKETS = 32
MAX_DISTANCE = 128
ROPE_BASE = 10000.0
PEER_HEADS = 8
PEER_NKEYS = 128
PEER_EXPERTS = PEER_NKEYS * PEER_NKEYS
PEER_TOPK = 16
PEER_SLOTS = PEER_HEADS * PEER_TOPK
RMS_EPS = 1e-6
GN_EPS = 1e-5
NEG = -1e30
IN_COLS = (0, 512, 1024, 1536, 2048, 2560, 2688, 2816)

LANES = 128
SUBLANES = 8
VMEM_LIMIT = 56 * 1024 * 1024
ROWS_PER_EXPERT = D_MODEL // 2 // LANES
TABLE_PAD = ROWS_PER_EXPERT
PEER_TOKENS = 512
ROUTE_TOKENS = 256
ROUTE_HEADS_PER_ITER = 2
SLOT_ORDER = (0, 2, 1, 3)
SLOT_GROUP = 1
V_TOKENS_PER_ITER = 8
NORM_TOKENS = 32


def _cparams(sem):
    return pltpu.CompilerParams(dimension_semantics=sem, vmem_limit_bytes=VMEM_LIMIT)


def _proj_kernel(x_ref, nw_ref, w_ref, cos_ref, sin_ref,
                 qr_ref, kr_ref, vr_ref, gr_ref, qa_ref, ka_ref, va_ref):
    x = x_ref[0]
    ms = jnp.mean(x * x, axis=-1, keepdims=True)
    xn = (x * lax.rsqrt(ms + RMS_EPS) * nw_ref[...]).astype(BF16)
    cosf = cos_ref[...]
    sinf = sin_ref[...]

    def mm(i):
        return jnp.dot(xn, w_ref[:, IN_COLS[i]:IN_COLS[i + 1]], preferred_element_type=F32)

    def rotary(t, scale):
        for h in range(RET_HEADS):
            th = t[:, h * RET_D:(h + 1) * RET_D]
            yield h, (th * cosf + pltpu.roll(th, RET_D // 2, 1) * sinf) * scale

    for h, r in rotary(mm(0), 1.0):
        qr_ref[0, :, h * RET_D:(h + 1) * RET_D] = r.astype(BF16)
    for h, r in rotary(mm(1), RET_D ** -0.5):
        kr_ref[0, :, h * RET_D:(h + 1) * RET_D] = r.astype(BF16)
    vr_ref[0] = mm(2).astype(BF16)
    gr_ref[0] = mm(3)
    qa_ref[0] = (mm(4) * (ATT_HD ** -0.5)).astype(BF16)
    ka_ref[0] = mm(5).astype(BF16)
    va_ref[0] = mm(6).astype(BF16)


def _proj(x, norm_w, w_in_bf, cos_t, sin_t, tm):
    b, s, d = x.shape
    widths = (512, 512, 512, 512, 512, 128, 128)
    dtypes = (BF16, BF16, BF16, F32, BF16, BF16, BF16)
    return pl.pallas_call(
        _proj_kernel,
        out_shape=[jax.ShapeDtypeStruct((b, s, w), dt) for w, dt in zip(widths, dtypes)],
        grid=(b, s // tm),
        in_specs=[
            pl.BlockSpec((1, tm, d), lambda i, j: (i, j, 0)),
            pl.BlockSpec((1, d), lambda i, j: (0, 0)),
            pl.BlockSpec(w_in_bf.shape, lambda i, j: (0, 0)),
            pl.BlockSpec((tm, LANES), lambda i, j: (j, 0)),
            pl.BlockSpec((tm, LANES), lambda i, j: (j, 0)),
        ],
        out_specs=[pl.BlockSpec((1, tm, w), lambda i, j: (i, j, 0)) for w in widths],
        compiler_params=_cparams(("arbitrary", "arbitrary")),
        name="proj",
    )(x, norm_w, w_in_bf, cos_t, sin_t)


def _ret_kernel(q_ref, k_ref, v_ref, g_ref, kpre_ref, vpre_ref, dmat_ref, qwf_ref, qwb_ref,
                kwf_ref, kwb_ref, kwp_ref, gdec_ref, gnw_ref, o_ref, sf_ref, sb_ref, sball_ref):
    ph = pl.program_id(1)
    c = pl.program_id(2)
    nc = pl.num_programs(2)
    contract0 = (((0,), (0,)), ((), ()))
    contract1 = (((1,), (1,)), ((), ()))

    def head(ref, h):
        return ref[0, :, h * RET_D:(h + 1) * RET_D]

    def kv_update(kh, vh, kw):
        kw = (kh.astype(F32) * kw).astype(BF16)
        return lax.dot_general(kw, vh, contract0, preferred_element_type=F32)

    @pl.when(ph == 0)
    def _backward_states():
        @pl.when(c == 0)
        def _():
            sb_ref[...] = jnp.zeros_like(sb_ref)

        j = nc - 1 - c
        for h in range(RET_HEADS):
            sball_ref[j, h] = sb_ref[h].astype(BF16)
            sb_ref[h] = gdec_ref[RET_HEADS + h] * sb_ref[h] + kv_update(head(k_ref, h), head(v_ref, h), kwb_ref[h])

    @pl.when(ph == 1)
    def _outputs():
        @pl.when(c == 0)
        def _():
            for h in range(RET_HEADS):
                kp = kpre_ref[:, h * RET_D:(h + 1) * RET_D]
                vp = vpre_ref[:, h * RET_D:(h + 1) * RET_D]
                sf_ref[h] = kv_update(kp, vp, kwp_ref[h])

        for h in range(RET_HEADS):
            qh, kh, vh = head(q_ref, h), head(k_ref, h), head(v_ref, h)
            s = lax.dot_general(qh, kh, contract1, preferred_element_type=F32) * dmat_ref[h]
            o = jnp.dot(s.astype(BF16), vh, preferred_element_type=F32)
            o = o + jnp.dot(qh, sf_ref[h].astype(BF16), preferred_element_type=F32) * qwf_ref[h]
            o = o + jnp.dot(qh, sball_ref[c, h], preferred_element_type=F32) * qwb_ref[h]
            mu = jnp.mean(o, axis=-1, keepdims=True)
            oc = o - mu
            var = jnp.mean(oc * oc, axis=-1, keepdims=True)
            on = oc * lax.rsqrt(var + GN_EPS) * gnw_ref[:, h * RET_D:(h + 1) * RET_D]
            g = head(g_ref, h)
            o_ref[0, :, h * RET_D:(h + 1) * RET_D] = (g * jax.nn.sigmoid(g) * on).astype(BF16)
            sf_ref[h] = gdec_ref[h] * sf_ref[h] + kv_update(kh, vh, kwf_ref[h])


def _retention(qr, kr, vr, gr, kpre, vpre, dmat, qwf, qwb, kwf, kwb, kwp, gdec, gnw, chunk):
    b, s, w = qr.shape
    nc = s // chunk

    def seq_map(i, ph, c):
        return (i, jnp.where(ph == 0, nc - 1 - c, c), 0)

    def fwd_only(i, ph, c):
        return (i, c * ph, 0)

    const2 = lambda i, ph, c: (0, 0)
    const3 = lambda i, ph, c: (0, 0, 0)
    return pl.pallas_call(
        _ret_kernel,
        out_shape=jax.ShapeDtypeStruct((b, s, w), BF16),
        grid=(b, 2, nc),
        in_specs=[
            pl.BlockSpec((1, chunk, w), fwd_only),
            pl.BlockSpec((1, chunk, w), seq_map),
            pl.BlockSpec((1, chunk, w), seq_map),
            pl.BlockSpec((1, chunk, w), fwd_only),
            pl.BlockSpec(kpre.shape, const2),
            pl.BlockSpec(vpre.shape, const2),
            pl.BlockSpec(dmat.shape, const3),
            pl.BlockSpec(qwf.shape, const3),
            pl.BlockSpec(qwb.shape, const3),
            pl.BlockSpec(kwf.shape, const3),
            pl.BlockSpec(kwb.shape, const3),
            pl.BlockSpec(kwp.shape, const3),
            pl.BlockSpec(memory_space=pltpu.SMEM),
            pl.BlockSpec(gnw.shape, const2),
        ],
        out_specs=pl.BlockSpec((1, chunk, w), fwd_only),
        scratch_shapes=[
            pltpu.VMEM((RET_HEADS, RET_D, RET_D), F32),
            pltpu.VMEM((RET_HEADS, RET_D, RET_D), F32),
            pltpu.VMEM((nc, RET_HEADS, RET_D, RET_D), BF16),
        ],
        compiler_params=_cparams(("arbitrary", "arbitrary", "arbitrary")),
        name="retention",
    )(qr, kr, vr, gr, kpre, vpre, dmat, qwf, qwb, kwf, kwb, kwp, gdec, gnw)


def _attn_kernel(q_ref, kp_ref, kc_ref, kn_ref, vp_ref, vc_ref, vn_ref, km_ref, vm_ref,
                 bias_ref, sink_ref, o_ref):
    j = pl.program_id(1)
    nb = pl.num_programs(1)
    lane = lax.broadcasted_iota(jnp.int32, (1, 4 * ATT_BLOCK), 1)
    first = jnp.logical_and(lane < ATT_BLOCK, j == 0)
    last = jnp.logical_and(jnp.logical_and(lane >= 2 * ATT_BLOCK, lane < 3 * ATT_BLOCK), j == nb - 1)
    pen = jnp.where(jnp.logical_or(first, last), NEG, 0.0).astype(F32)
    contract1 = (((1,), (1,)), ((), ()))
    rows = lax.broadcasted_iota(jnp.int32, (ATT_GROUP * ATT_BLOCK, 1), 0)
    for g in range(ATT_KV):
        sl = slice(g * ATT_HD, (g + 1) * ATT_HD)
        kall = jnp.concatenate([kp_ref[0][:, sl], kc_ref[0][:, sl], kn_ref[0][:, sl], km_ref[:, sl]], axis=0)
        vall = jnp.concatenate([vp_ref[0][:, sl], vc_ref[0][:, sl], vn_ref[0][:, sl], vm_ref[:, sl]], axis=0)
        heads = range(g * ATT_GROUP, (g + 1) * ATT_GROUP)
        q = jnp.concatenate([q_ref[0, :, h * ATT_HD:(h + 1) * ATT_HD] for h in heads], axis=0)
        bias = bias_ref[0, g * ATT_GROUP:(g + 1) * ATT_GROUP].reshape(ATT_GROUP * ATT_BLOCK, 4 * ATT_BLOCK)
        snk = jnp.zeros((ATT_GROUP * ATT_BLOCK, 1), F32)
        for hh, h in enumerate(heads):
            snk = jnp.where(rows // ATT_BLOCK == hh, sink_ref[h], snk)
        s = lax.dot_general(q, kall, contract1, preferred_element_type=F32) + bias + pen
        m = jnp.maximum(jnp.max(s, axis=-1, keepdims=True), snk)
        p = jnp.exp(s - m)
        den = jnp.sum(p, axis=-1, keepdims=True) + jnp.exp(snk - m)
        o = jnp.dot(p.astype(BF16), vall, preferred_element_type=F32) / den
        for hh in range(0, ATT_GROUP, 2):
            lanes = slice((g * ATT_GROUP + hh) * ATT_HD, (g * ATT_GROUP + hh + 2) * ATT_HD)
            o_ref[0, :, lanes] = jnp.concatenate(
                [o[hh * ATT_BLOCK:(hh + 1) * ATT_BLOCK], o[(hh + 1) * ATT_BLOCK:(hh + 2) * ATT_BLOCK]],
                axis=-1).astype(BF16)


def _attention(qa, ka, va, kmeta, vmeta, bias, sink):
    b, s, w = qa.shape
    nb = s // ATT_BLOCK
    kvw = ka.shape[-1]
    prev = lambda i, j: (i, jnp.maximum(j - 1, 0), 0)
    cur = lambda i, j: (i, j, 0)
    nxt = lambda i, j: (i, jnp.minimum(j + 1, nb - 1), 0)
    const2 = lambda i, j: (0, 0)
    kv = lambda m: pl.BlockSpec((1, ATT_BLOCK, kvw), m)
    return pl.pallas_call(
        _attn_kernel,
        out_shape=jax.ShapeDtypeStruct((b, s, w), BF16),
        grid=(b, nb),
        in_specs=[
            pl.BlockSpec((1, ATT_BLOCK, w), cur),
            kv(prev), kv(cur), kv(nxt), kv(prev), kv(cur), kv(nxt),
            pl.BlockSpec(kmeta.shape, const2),
            pl.BlockSpec(vmeta.shape, const2),
            pl.BlockSpec((1,) + bias.shape[1:], lambda i, j: (jnp.minimum(j, 1), 0, 0, 0)),
            pl.BlockSpec(memory_space=pltpu.SMEM),
        ],
        out_specs=pl.BlockSpec((1, ATT_BLOCK, w), cur),
        compiler_params=_cparams(("arbitrary", "arbitrary")),
        name="attention",
    )(qa, ka, ka, ka, va, va, va, kmeta, vmeta, bias, sink)


def _store_token_rows(ref, v):
    for c in range(D_MODEL // LANES):
        ref[pl.ds(c, v.shape[0], stride=SUBLANES), :] = v[:, c * LANES:(c + 1) * LANES]


def _outproj_kernel(or_ref, oa_ref, x_ref, w_ref, nw_ref, h_ref, xn_ref, xnb_ref):
    half = or_ref.shape[-1]
    h = x_ref[0]
    h = h + jnp.dot(or_ref[0], w_ref[:half], preferred_element_type=F32)
    h = h + jnp.dot(oa_ref[0], w_ref[half:], preferred_element_type=F32)
    _store_token_rows(h_ref, h)
    ms = jnp.mean(h * h, axis=-1, keepdims=True)
    xn = h * lax.rsqrt(ms + RMS_EPS) * nw_ref[...]
    _store_token_rows(xn_ref, xn)
    xnb_ref[0] = xn.astype(BF16)


def _outproj(o_r, o_a, x, w_out_bf, norm_w, tm):
    b, s, d = x.shape
    half = o_r.shape[-1]
    row = lambda i, j: (i, j, 0)
    const2 = lambda i, j: (0, 0)
    token_rows = pl.BlockSpec((tm * SUBLANES, LANES), lambda i, j: (i * (s // tm) + j, 0))
    return pl.pallas_call(
        _outproj_kernel,
        out_shape=[jax.ShapeDtypeStruct((b * s * SUBLANES, LANES), F32)] * 2 + [jax.ShapeDtypeStruct((b, s, d), BF16)],
        grid=(b, s // tm),
        in_specs=[
            pl.BlockSpec((1, tm, half), row),
            pl.BlockSpec((1, tm, half), row),
            pl.BlockSpec((1, tm, d), row),
            pl.BlockSpec(w_out_bf.shape, const2),
            pl.BlockSpec((1, d), const2),
        ],
        out_specs=[token_rows, token_rows, pl.BlockSpec((1, tm, d), row)],
        compiler_params=_cparams(("arbitrary", "arbitrary")),
        name="outproj",
    )(o_r, o_a, x, w_out_bf, norm_w)


def _top16_rows(s, iota):
    nrows = float(s.shape[0])
    vals, ids = [], []
    for _ in range(PEER_TOPK):
        m = jnp.max(s, axis=0, keepdims=True)
        am = jnp.min(jnp.where(s == m, iota, nrows), axis=0, keepdims=True)
        vals.append(m)
        ids.append(am)
        s = jnp.where(iota == am, -jnp.inf, s)
    return jnp.concatenate(vals, axis=0), jnp.concatenate(ids, axis=0)


_PAIR_GROUPS = ((0, 0, 8), (0, 8, 8), (1, 0, 8), (2, 0, 5), (3, 0, 4), (4, 0, 3), (5, 0, 2), (6, 0, 2), (7, 0, 2))


def _route_kernel(x_ref, wq_ref, pk_ref, idx_ref, gate_ref, sc_ref):
    tm = x_ref.shape[1]
    q = jnp.dot(x_ref[0], wq_ref[...], preferred_element_type=F32).astype(BF16)
    contract1 = (((1,), (1,)), ((), ()))
    for hp in range(2 * PEER_HEADS):
        sc_ref[hp] = lax.dot_general(pk_ref[hp], q[:, hp * PEER_NKEYS:(hp + 1) * PEER_NKEYS], contract1,
                                     preferred_element_type=F32)
    iota = lax.broadcasted_iota(jnp.int32, (PEER_NKEYS, LANES), 0).astype(F32)
    sub = lax.broadcasted_iota(jnp.int32, (SUBLANES, LANES), 0)
    subf = sub.astype(F32)
    far = float(PEER_TOPK * PEER_TOPK)

    def one_head(h):
        for lt in range(tm // LANES):
            lanes = slice(lt * LANES, (lt + 1) * LANES)
            s1, i1 = _top16_rows(sc_ref[2 * h, :, lanes], iota)
            s2, i2 = _top16_rows(sc_ref[2 * h + 1, :, lanes], iota)
            i1 = i1 * float(PEER_NKEYS)
            sc, ex, pos = [], [], []
            for i, j0, cnt in _PAIR_GROUPS:
                ok = sub < cnt
                sc.append(jnp.where(ok, s1[i:i + 1] + s2[j0:j0 + SUBLANES], -jnp.inf))
                ex.append(i1[i:i + 1] + i2[j0:j0 + SUBLANES])
                pos.append(jnp.where(ok, float(i * PEER_TOPK + j0) + subf, far))
            sc.append(s1[SUBLANES:] + s2[0:1])
            ex.append(i1[SUBLANES:] + i2[0:1])
            pos.append((float(SUBLANES) + subf) * float(PEER_TOPK))
            sc = jnp.concatenate(sc, axis=0)
            ex = jnp.concatenate(ex, axis=0)
            pos = jnp.concatenate(pos, axis=0)
            top, experts = [], []
            for _ in range(PEER_TOPK):
                m = jnp.max(sc, axis=0, keepdims=True)
                pm = jnp.min(jnp.where(sc == m, pos, far), axis=0, keepdims=True)
                hit = pos == pm
                experts.append(jnp.max(jnp.where(hit, ex, -1.0), axis=0, keepdims=True))
                top.append(m)
                sc = jnp.where(hit, -jnp.inf, sc)
            top = jnp.concatenate(top, axis=0)
            e = jnp.exp(top - top[0:1])
            rows = pl.ds(pl.multiple_of(h * PEER_TOPK, PEER_TOPK), PEER_TOPK)
            gate_ref[0, rows, lanes] = e / jnp.sum(e, axis=0, keepdims=True)
            idx_ref[0, rows, lanes] = jnp.concatenate(experts, axis=0).astype(jnp.int32)

    def heads_step(i, carry):
        for r in range(ROUTE_HEADS_PER_ITER):
            one_head(i * ROUTE_HEADS_PER_ITER + r)
        return carry

    lax.fori_loop(0, PEER_HEADS // ROUTE_HEADS_PER_ITER, heads_step, 0)


def _route(xn, wq_bf, pk_bf, tm):
    b, s, d = xn.shape
    const2 = lambda i, j: (0, 0)
    slot = lambda i, j: (i, 0, j)
    return pl.pallas_call(
        _route_kernel,
        out_shape=[jax.ShapeDtypeStruct((b, PEER_SLOTS, s), jnp.int32),
                   jax.ShapeDtypeStruct((b, PEER_SLOTS, s), F32)],
        grid=(b, s // tm),
        in_specs=[
            pl.BlockSpec((1, tm, d), lambda i, j: (i, j, 0)),
            pl.BlockSpec(wq_bf.shape, const2),
            pl.BlockSpec(pk_bf.shape, lambda i, j: (0, 0, 0)),
        ],
        out_specs=[pl.BlockSpec((1, PEER_SLOTS, tm), slot)] * 2,
        scratch_shapes=[pltpu.VMEM((2 * PEER_HEADS, PEER_NKEYS, tm), F32)],
        compiler_params=_cparams(("arbitrary", "arbitrary")),
        name="route",
    )(xn, wq_bf, pk_bf)


def _pair_slots():
    out = []
    for g in range(PEER_SLOTS // SUBLANES):
        for c in range(4):
            top = SUBLANES * g + SLOT_ORDER[c]
            out.append((top, top + 4))
    return out


def _gather_words(tbl_ref, top_off, bot_off, upper):
    ra = tbl_ref[pl.ds(pl.multiple_of(top_off, ROWS_PER_EXPERT), SUBLANES), :]
    rb = tbl_ref[pl.ds(pl.multiple_of(bot_off, ROWS_PER_EXPERT), SUBLANES), :]
    return jnp.where(upper, ra, rb)


def _unpack_words(words):
    lo = lax.bitcast_convert_type(words << 16, F32)
    hi = lax.bitcast_convert_type(words & jnp.uint32(0xFFFF0000), F32)
    return lo, hi


def _gather_pair(tbl_ref, top_off, bot_off, upper):
    return _unpack_words(_gather_words(tbl_ref, top_off, bot_off, upper))


def _merge_sublanes(a, b, h, sub):
    keep = (sub & h) == 0
    return jnp.where(keep, a + pltpu.roll(a, SUBLANES - h, 0), b + pltpu.roll(b, h, 0))


def _slot_specs(tb):
    return [pl.BlockSpec((None, None, SLOT_GROUP * tb), lambda i, j=j: (j, 0, i), memory_space=pltpu.SMEM,
                         pipeline_mode=pl.Buffered(1)) for j in range(PEER_SLOTS // SLOT_GROUP)]


def _group_slots(a, tb):
    n = a.shape[-1]
    a = a.reshape(PEER_SLOTS // SLOT_GROUP, SLOT_GROUP, n // tb, tb).transpose(0, 2, 1, 3)
    return a.reshape(PEER_SLOTS // SLOT_GROUP, 1, n * SLOT_GROUP)


def _slot_reader(refs, t, tb):
    at = [t + g * tb for g in range(SLOT_GROUP)]
    return lambda k: refs[k // SLOT_GROUP][at[k % SLOT_GROUP]]


def _peer_u_kernel(*refs):
    ngroups = PEER_SLOTS // SLOT_GROUP
    offs = refs[:ngroups]
    x_ref, tbl_ref, o_ref, z_ref, sel_ref = refs[ngroups:]
    tb = x_ref.shape[0] // SUBLANES
    sub = lax.broadcasted_iota(jnp.int32, (SUBLANES, LANES), 0)
    upper = sub < ROWS_PER_EXPERT
    pairs = _pair_slots()
    lane_pair = lax.broadcasted_iota(jnp.int32, (PEER_SLOTS, LANES), 1) >> 1

    row = lax.broadcasted_iota(jnp.int32, (2 * LANES, LANES), 0)
    col = lax.broadcasted_iota(jnp.int32, (2 * LANES, LANES), 1)
    sel_ref[...] = jnp.where((col & 1) == (row >= LANES).astype(jnp.int32), 1.0, 0.0).astype(BF16)
    o_ref[...] = jnp.zeros_like(o_ref)

    group_rows = 4 * PEER_SLOTS
    z_ref[...] = jnp.zeros_like(z_ref)

    def lane_partials(gi, zbase):
        for j in range(SUBLANES):
            t = gi * SUBLANES + j
            x = x_ref[pl.ds(pl.multiple_of(t * SUBLANES, SUBLANES), SUBLANES), :]
            swapped = pltpu.roll(x, ROWS_PER_EXPERT, 0)
            xa = jnp.where(upper, x, swapped)
            xb = jnp.where(upper, swapped, x)
            xa_bits = lax.bitcast_convert_type(xa.astype(BF16).astype(F32), jnp.uint32)
            xb_bits = lax.bitcast_convert_type(xb.astype(BF16).astype(F32), jnp.uint32)
            xw = pltpu.bitcast((xb_bits & jnp.uint32(0xFFFF0000)) | (xa_bits >> 16), BF16)
            off = _slot_reader(offs, t, tb)
            zs = []
            for g in range(PEER_SLOTS // SUBLANES):
                ps = []
                for c in range(4):
                    top, bot = pairs[4 * g + c]
                    words = _gather_words(tbl_ref, off(top), off(bot), upper)
                    lo, hi = _unpack_words(pltpu.bitcast(pltpu.bitcast(words, BF16) * xw, jnp.uint32))
                    ps.append(lo + hi)
                r0 = _merge_sublanes(ps[0], ps[1], 2, sub)
                r1 = _merge_sublanes(ps[2], ps[3], 2, sub)
                zs.append(_merge_sublanes(r0, r1, 1, sub))
            rows = zbase + (j // 2) * PEER_SLOTS
            cols = slice((j % 2) * LANES, (j % 2 + 1) * LANES)
            for m in range(len(zs) // 2):
                tile = jnp.concatenate([zs[2 * m], zs[2 * m + 1]], axis=0).astype(BF16)
                z_ref[pl.ds(pl.multiple_of(rows + 2 * SUBLANES * m, 2 * SUBLANES), 2 * SUBLANES), cols] = tile

    def lane_sums(gi, zbase):
        sums = jnp.dot(z_ref[pl.ds(zbase, group_rows), :], sel_ref[...], preferred_element_type=F32)
        block = pl.ds(pl.multiple_of((gi // (LANES // SUBLANES)) * LANES, LANES), LANES)
        first_pair = (gi % (LANES // SUBLANES)) * 4
        acc = o_ref[:, block]
        for p in range(4):
            acc = jnp.where(lane_pair == first_pair + p, sums[p * PEER_SLOTS:(p + 1) * PEER_SLOTS], acc)
        o_ref[:, block] = acc

    def zbase_of(gi):
        return pl.multiple_of((gi % 2) * group_rows, group_rows)

    def step(gi, carry):
        prev = jnp.maximum(gi - 1, 0)
        lane_sums(prev, zbase_of(gi + 1))
        lane_partials(gi, zbase_of(gi))
        return carry

    ngroups_tokens = tb // SUBLANES
    lax.fori_loop(0, ngroups_tokens, step, 0)
    lane_sums(ngroups_tokens - 1, (ngroups_tokens - 1) % 2 * group_rows)


def _peer_u(offs_g, x_rows, tbl, tb):
    n = x_rows.shape[0] // SUBLANES
    assert tb % LANES == 0
    return pl.pallas_call(
        _peer_u_kernel,
        out_shape=jax.ShapeDtypeStruct((PEER_SLOTS, n), F32),
        grid=(n // tb,),
        in_specs=_slot_specs(tb) + [
            pl.BlockSpec((tb * SUBLANES, LANES), lambda i: (i, 0)),
            pl.BlockSpec(tbl.shape, lambda i: (0, 0), pipeline_mode=pl.Buffered(1)),
        ],
        out_specs=pl.BlockSpec((PEER_SLOTS, tb), lambda i: (0, i)),
        scratch_shapes=[
            pltpu.VMEM((2 * 4 * PEER_SLOTS, 2 * LANES), BF16),
            pltpu.VMEM((2 * LANES, LANES), BF16),
        ],
        compiler_params=_cparams(("arbitrary",)),
        name="peer_u",
    )(*([offs_g] * (PEER_SLOTS // SLOT_GROUP)), x_rows, tbl)


def _peer_v_kernel(*refs):
    ngroups = PEER_SLOTS // SLOT_GROUP
    offs = refs[:ngroups]
    (gate_ref, act_ref, h_ref, tbl_ref, nw_ref, o_ref, acc_ref, whi_ref, wlo_ref, repa_ref, repb_ref,
     ones_ref) = refs[ngroups:]
    tb = h_ref.shape[0] // SUBLANES
    sub = lax.broadcasted_iota(jnp.int32, (SUBLANES, LANES), 0)
    upper = sub < ROWS_PER_EXPERT
    pairs = _pair_slots()
    lane = lax.broadcasted_iota(jnp.int32, (PEER_SLOTS, LANES), 1)

    a = act_ref[...]
    w = gate_ref[...] * (0.5 * a * (1.0 + lax.erf(a * (2.0 ** -0.5))))
    whi = w.astype(BF16).astype(F32)
    whi_ref[...] = whi
    wlo_ref[...] = w - whi
    row = lax.broadcasted_iota(jnp.int32, (2 * LANES, 2 * LANES), 0)
    col = lax.broadcasted_iota(jnp.int32, (2 * LANES, 2 * LANES), 1)
    ones_ref[...] = jnp.where((row >= LANES) == (col >= LANES), 1.0, 0.0).astype(BF16)

    def replicate(t0, rep_ref):
        for r in range(V_TOKENS_PER_ITER // 2):
            t = t0 + 2 * r
            block = pl.ds(pl.multiple_of((t // LANES) * LANES, LANES), LANES)
            la = t % LANES
            out = None
            for part_ref in (whi_ref, wlo_ref):
                part = part_ref[:, block]
                lhs = jnp.concatenate([jnp.where(lane == la, part, 0.0), jnp.where(lane == la + 1, part, 0.0)],
                                      axis=1).astype(BF16)
                prod = jnp.dot(lhs, ones_ref[...], preferred_element_type=F32)
                out = prod if out is None else out + prod
            rep_ref[r] = out

    def token(t, rep_ref, slot, half):
        off = _slot_reader(offs, t, tb)
        lanes = slice(half * LANES, (half + 1) * LANES)
        acc = [jnp.zeros((SUBLANES, LANES), F32) for _ in range(4)]
        for q, (top, bot) in enumerate(pairs):
            lo, hi = _gather_pair(tbl_ref, off(top), off(bot), upper)
            wtop = jnp.broadcast_to(rep_ref[slot, top:top + 1, lanes], (SUBLANES, LANES))
            wbot = jnp.broadcast_to(rep_ref[slot, bot:bot + 1, lanes], (SUBLANES, LANES))
            wv = jnp.where(upper, wtop, wbot)
            k = 2 * (q % 2)
            acc[k] = acc[k] + lo * wv
            acc[k + 1] = acc[k + 1] + hi * wv
        lo = acc[0] + acc[2]
        hi = acc[1] + acc[3]
        lo = lo + pltpu.roll(lo, ROWS_PER_EXPERT, 0)
        hi = hi + pltpu.roll(hi, ROWS_PER_EXPERT, 0)
        rows = pl.ds(pl.multiple_of(t * SUBLANES, SUBLANES), SUBLANES)
        acc_ref[rows, :] = h_ref[rows, :] + jnp.where(upper, lo, hi)

    def tokens(t0, rep_ref):
        for r in range(V_TOKENS_PER_ITER):
            token(t0 + r, rep_ref, r // 2, r % 2)

    replicate(0, repa_ref)

    def step(i, carry):
        t0 = i * 2 * V_TOKENS_PER_ITER
        replicate(t0 + V_TOKENS_PER_ITER, repb_ref)
        tokens(t0, repa_ref)
        replicate(jnp.minimum(t0 + 2 * V_TOKENS_PER_ITER, tb - V_TOKENS_PER_ITER), repa_ref)
        tokens(t0 + V_TOKENS_PER_ITER, repb_ref)
        return carry

    lax.fori_loop(0, tb // (2 * V_TOKENS_PER_ITER), step, 0)

    def norm_step(i, carry):
        rows = pl.ds(pl.multiple_of(i * NORM_TOKENS * SUBLANES, NORM_TOKENS * SUBLANES), NORM_TOKENS * SUBLANES)
        h = acc_ref[rows, :].reshape(NORM_TOKENS, SUBLANES, LANES)
        ss = jnp.sum(jnp.sum(h * h, axis=2, keepdims=True), axis=1, keepdims=True)
        y = h * lax.rsqrt(ss * (1.0 / D_MODEL) + RMS_EPS) * nw_ref[...][None]
        acc_ref[rows, :] = y.reshape(NORM_TOKENS * SUBLANES, LANES)
        return carry

    lax.fori_loop(0, tb // NORM_TOKENS, norm_step, 0)
    for c in range(D_MODEL // LANES):
        o_ref[:, c * LANES:(c + 1) * LANES] = acc_ref[pl.ds(c, tb, stride=SUBLANES), :]


def _peer_v(offs_g, gate_t, act_t, h_rows, tbl, norm_rows, tb):
    n = h_rows.shape[0] // SUBLANES
    assert tb % LANES == 0 and V_TOKENS_PER_ITER % 2 == 0 and tb % (2 * V_TOKENS_PER_ITER) == 0 and tb % NORM_TOKENS == 0
    rows = pl.BlockSpec((tb * SUBLANES, LANES), lambda i: (i, 0))
    return pl.pallas_call(
        _peer_v_kernel,
        out_shape=jax.ShapeDtypeStruct((n, D_MODEL), F32),
        grid=(n // tb,),
        in_specs=_slot_specs(tb) + [
            pl.BlockSpec((PEER_SLOTS, tb), lambda i: (0, i)),
            pl.BlockSpec((PEER_SLOTS, tb), lambda i: (0, i)),
            rows,
            pl.BlockSpec(tbl.shape, lambda i: (0, 0), pipeline_mode=pl.Buffered(1)),
            pl.BlockSpec((SUBLANES, LANES), lambda i: (0, 0)),
        ],
        out_specs=pl.BlockSpec((tb, D_MODEL), lambda i: (i, 0)),
        scratch_shapes=[
            pltpu.VMEM((tb * SUBLANES, LANES), F32),
            pltpu.VMEM((PEER_SLOTS, tb), F32),
            pltpu.VMEM((PEER_SLOTS, tb), F32),
            pltpu.VMEM((V_TOKENS_PER_ITER // 2, PEER_SLOTS, 2 * LANES), F32),
            pltpu.VMEM((V_TOKENS_PER_ITER // 2, PEER_SLOTS, 2 * LANES), F32),
            pltpu.VMEM((2 * LANES, 2 * LANES), BF16),
        ],
        compiler_params=_cparams(("arbitrary",)),
        name="peer_v",
    )(*([offs_g] * (PEER_SLOTS // SLOT_GROUP)), gate_t, act_t, h_rows, tbl, norm_rows)


def _pack_table(t):
    half = D_MODEL // 2
    bits = lax.bitcast_convert_type(t.astype(BF16), jnp.uint16).astype(jnp.uint32)
    words = bits[:, :half] | (bits[:, half:] << 16)
    words = words.reshape(t.shape[0] * ROWS_PER_EXPERT, LANES)
    return jnp.pad(words, ((TABLE_PAD, TABLE_PAD), (0, 0)))


def _t5_bucket(rel):
    half = N_BUCKETS // 2
    exact = half // 2
    n = jnp.abs(rel)
    large = exact + (jnp.log(jnp.maximum(n, 1).astype(F32) / exact)
                     / math.log(MAX_DISTANCE / exact) * (half - exact)).astype(jnp.int32)
    large = jnp.minimum(large, half - 1)
    return jnp.where(rel > 0, half, 0) + jnp.where(n < exact, n, large)


def _attention_bias(rel_bias):
    rb = rel_bias.astype(F32)
    i = jnp.arange(ATT_BLOCK)
    jb = jnp.arange(3 * ATT_BLOCK)
    rel = (jb[None, :] - ATT_BLOCK) - i[:, None]
    band = jnp.where((jnp.abs(rel) <= ATT_BLOCK)[..., None], rb[_t5_bucket(rel)], NEG)
    mpos = PREFIX - N_META + jnp.arange(N_META)
    variants = []
    for blk in (0, 1):
        qpos = PREFIX + blk * ATT_BLOCK + i
        meta = rb[_t5_bucket(mpos[None, :] - qpos[:, None])]
        fill = jnp.full((ATT_BLOCK, ATT_BLOCK - N_META, ATT_HEADS), NEG, F32)
        variants.append(jnp.concatenate([band, meta, fill], axis=1).transpose(2, 0, 1))
    return jnp.stack(variants)


def _rope_tables(pos):
    half = RET_D // 2
    inv = ROPE_BASE ** (-jnp.arange(half, dtype=F32) / half)
    ang = pos.astype(F32)[:, None] * inv[None, :]
    cos, sin = jnp.cos(ang), jnp.sin(ang)
    return jnp.concatenate([cos, cos], axis=1), jnp.concatenate([-sin, sin], axis=1)


def _decay_tables(dec_f, dec_b, chunk):
    lf = jax.nn.log_sigmoid(dec_f.astype(F32))[:, None]
    lb = jax.nn.log_sigmoid(dec_b.astype(F32))[:, None]
    idx = jnp.arange(chunk, dtype=F32)
    diff = idx[:, None] - idx[None, :]
    dmat = (jnp.where(diff >= 0, jnp.exp(jnp.maximum(diff, 0.0)[None] * lf[:, :, None]), 0.0)
            + jnp.where(diff < 0, jnp.exp(jnp.maximum(-diff, 0.0)[None] * lb[:, :, None]), 0.0))
    bc = lambda v: jnp.broadcast_to(v[:, :, None], v.shape + (RET_D,))
    qwf = bc(jnp.exp((idx + 1.0)[None] * lf))
    kwf = bc(jnp.exp((chunk - 1.0 - idx)[None] * lf))
    qwb = bc(jnp.exp((chunk - idx)[None] * lb))
    kwb = bc(jnp.exp(idx[None] * lb))
    pidx = jnp.arange(PREFIX, dtype=F32)
    kwp = bc(jnp.exp((PREFIX - 1.0 - pidx)[None] * lf))
    gdec = jnp.concatenate([jnp.exp(chunk * lf[:, 0]), jnp.exp(chunk * lb[:, 0])])
    return dmat, qwf, qwb, kwf, kwb, kwp, gdec


def _divisor_tile(n, want):
    t = min(n, want)
    while n % t:
        t //= 2
    return t


def _encode(x, shared):
    b, s, d = x.shape
    n = b * s
    tm = _divisor_tile(s, 512)
    chunk = _divisor_tile(s, 256)
    cos_t, sin_t = _rope_tables(jnp.arange(s) + N_META)
    qr, kr, vr, gr, qa, ka, va = _proj(x, shared["norm_mix"], shared["w_in"], cos_t, sin_t, tm)
    o_r = _retention(qr, kr, vr, gr, shared["kr_pre"], shared["vr_pre"], *_decay_tables(
        shared["dec_f"], shared["dec_b"], chunk), shared["gn_w"], chunk)
    o_a = _attention(qa, ka, va, shared["k_meta"], shared["v_meta"], shared["bias"], shared["sink"])
    h_rows, xn_rows, xn_bf = _outproj(o_r, o_a, x, shared["w_out"], shared["norm_ffn"], tm)
    idx_t, gate_t = _route(xn_bf, shared["wq"], shared["pkeys"], _divisor_tile(s, ROUTE_TOKENS))
    idx_t = idx_t.transpose(1, 0, 2).reshape(PEER_SLOTS, n)
    gate_t = gate_t.transpose(1, 0, 2).reshape(PEER_SLOTS, n)
    top_half = (jnp.arange(PEER_SLOTS) % SUBLANES) < ROWS_PER_EXPERT
    tb = _divisor_tile(n, PEER_TOKENS)
    offs_g = _group_slots(idx_t * ROWS_PER_EXPERT + jnp.where(top_half, TABLE_PAD, 0)[:, None], tb)
    act_t = _peer_u(offs_g, xn_rows, shared["u_tbl"], tb)
    y = _peer_v(offs_g, gate_t, act_t, h_rows, shared["v_tbl"], shared["norm_final"], tb)
    return y.reshape(b, s, d)


def kernel(x_prompt, x_sample, meta_tokens, norm_mix_w, w_in, ret_decay_fwd, ret_decay_bwd, ret_gn_w,
           attn_sink, rel_bias, w_out, norm_ffn_w, peer_wq, peer_keys, peer_u, peer_v, norm_final_w):
    layer = 0
    shared = {
        "norm_mix": norm_mix_w[layer][None, :].astype(F32),
        "w_in": w_in[layer].astype(BF16),
        "dec_f": ret_decay_fwd[layer],
        "dec_b": ret_decay_bwd[layer],
        "gn_w": ret_gn_w[layer][None, :].astype(F32),
        "sink": attn_sink[layer].astype(F32),
        "bias": _attention_bias(rel_bias),
        "w_out": w_out[layer].astype(BF16),
        "norm_ffn": norm_ffn_w[layer][None, :].astype(F32),
        "wq": peer_wq[layer].astype(BF16),
        "pkeys": peer_keys[layer].reshape(2 * PEER_HEADS, PEER_NKEYS, PEER_NKEYS).astype(BF16),
        "u_tbl": _pack_table(peer_u[layer]),
        "v_tbl": _pack_table(peer_v[layer]),
        "norm_final": norm_final_w.reshape(SUBLANES, LANES).astype(F32),
    }
    prefix = jnp.concatenate([jnp.zeros((PREFIX - N_META, D_MODEL), x_prompt.dtype),
                              meta_tokens.astype(x_prompt.dtype)], axis=0)[None]
    cos_p, sin_p = _rope_tables(jnp.arange(PREFIX) - (PREFIX - N_META))
    _, kr_p, vr_p, _, _, ka_p, va_p = _proj(prefix, shared["norm_mix"], shared["w_in"], cos_p, sin_p, PREFIX)
    shared["kr_pre"], shared["vr_pre"] = kr_p[0], vr_p[0]
    pad_meta = lambda t: jnp.pad(t[0, PREFIX - N_META:], ((0, ATT_BLOCK - N_META), (0, 0)))
    shared["k_meta"], shared["v_meta"] = pad_meta(ka_p), pad_meta(va_p)
    return (_encode(x_prompt, shared), _encode(x_sample, shared))
```
